```python
import jax
import jax.numpy as jnp
from jax import lax
import numpy as np

D_MODEL = 1024
BATCH = 32
SEQ = 256
DEPTH = 2
DEC_BATCH = 2
DEC_SEQ = 4096
PAST_LEN = 512

GRID_W = 64
H_A = 4
Q_RANK = 192
KV_RANK = 128
NOPE_A = 64
ROPE_A = 32
V_A = 64
H_B = 4
P_B = 64
G_B = 2
N_B = 64
D_CONV = 5
SSD_CHUNK = 128
D_B = H_B * P_B
CONV_CH = D_B + 2 * G_B * N_B
H_C = 4
K_C = 64
V_C = 64
HGRN_CHUNK = 16
D_C = H_C * V_C
H_D = 4
KV_D = 2
G_D = H_D // KV_D
HD_D = 64
WINDOW = 128
ATTN_BLOCK = 128
ROPE_BASE = 10000.0
D_MIX = H_A * V_A + D_B + D_C + H_D * HD_D
D_FF = 2816
N_SUB = 3
ALPHA = (2 * DEPTH) ** 0.25
BETA = (8 * DEPTH) ** -0.25
EPS = 1e-6
F_MIN = 1e-6
IN_SIZES = (Q_RANK, KV_RANK, ROPE_A,
            D_B, CONV_CH, 2 * H_B,
            H_C * K_C, 2 * H_C * K_C, D_C, D_C,
            H_D * HD_D, KV_D * HD_D, KV_D * HD_D)
D_IN = sum(IN_SIZES)
IN_SPLITS = tuple(int(v) for v in np.cumsum(IN_SIZES)[:-1])
F32 = jnp.float32
NEG = -1e30

kernel_name = 'hybrid_diffusion_prefix_trunk_step'


def rms_norm(x, g):
    xf = x.astype(F32)
    y = xf * lax.rsqrt(jnp.mean(xf * xf, axis=-1, keepdims=True) + EPS)
    return (y * g.astype(F32)).astype(x.dtype)


def layer_norm(x, g, b):
    xf = x.astype(F32)
    xc = xf - jnp.mean(xf, axis=-1, keepdims=True)
    var = jnp.mean(xc * xc, axis=-1, keepdims=True)
    return (xc * lax.rsqrt(var + EPS) * g.astype(F32) + b.astype(F32)).astype(x.dtype)


def rope_1d(x, pos):
    half = x.shape[-1] // 2
    inv = ROPE_BASE ** (-jnp.arange(half, dtype=F32) / half)
    ang = pos[:, None] * inv[None, :]
    cos = jnp.cos(ang)[None, :, None, :]
    sin = jnp.sin(ang)[None, :, None, :]
    x1 = x[..., :half].astype(F32)
    x2 = x[..., half:].astype(F32)
    return jnp.concatenate([x1 * cos - x2 * sin, x1 * sin + x2 * cos], axis=-1)


def axial_rope(x):
    t_len = x.shape[1]
    rows = t_len // GRID_W
    row = jnp.repeat(jnp.arange(rows, dtype=F32), GRID_W)
    col = (jnp.arange(t_len) % GRID_W).astype(F32)
    r = x.shape[-1] // 2
    return jnp.concatenate([rope_1d(x[..., :r], row), rope_1d(x[..., r:], col)], axis=-1).astype(x.dtype)


def sink_softmax(s, sink):
    if sink is None:
        return jax.nn.softmax(s, axis=-1)
    sk = sink.astype(F32)[:, :, None]
    m = jnp.maximum(jnp.max(s, axis=-1), sk)
    e = jnp.exp(s - m[..., None])
    den = jnp.sum(e, axis=-1) + jnp.exp(sk - m)
    return e / den[..., None]


def dense_attention(q, k, v, sink):
    B, Tq, Hk, G, Dq = q.shape
    scale = Dq ** -0.5
    qb = jnp.moveaxis(q.reshape(B, Tq // ATTN_BLOCK, ATTN_BLOCK, Hk, G, Dq), 1, 0)

    def one_block(q_blk):
        s = jnp.einsum('bqhgd,bkhd->bhgqk', q_blk, k, preferred_element_type=F32) * scale
        p = sink_softmax(s, sink)
        return jnp.einsum('bhgqk,bkhd->bqhgd', p.astype(v.dtype), v)

    o = lax.map(one_block, qb)
    return jnp.moveaxis(o, 0, 1).reshape(B, Tq, Hk, G, v.shape[-1])


def banded_attention(q, k, v, kc, vc, sink):
    B, T, Hk, G, D = q.shape
    W = WINDOW
    nb = T // W
    scale = D ** -0.5
    qb = q.reshape(B, nb, W, Hk, G, D)
    pad = ((0, 0), (W, W), (0, 0), (0, 0))
    kp = jnp.pad(k, pad).reshape(B, nb + 2, W, Hk, D)
    vp = jnp.pad(v, pad).reshape(B, nb + 2, W, Hk, D)
    kb = jnp.concatenate([kp[:, :-2], kp[:, 1:-1], kp[:, 2:]], axis=2)
    vb = jnp.concatenate([vp[:, :-2], vp[:, 1:-1], vp[:, 2:]], axis=2)
    qi = jnp.arange(W)[:, None]
    kj = jnp.arange(3 * W)[None, :]
    band = jnp.abs(kj - W - qi) <= WINDOW
    seg_blk = jnp.arange(nb)[:, None] + jnp.arange(3)[None, :] - 1
    valid = jnp.repeat((seg_blk >= 0) & (seg_blk < nb), W, axis=1)
    mask = band[None] & valid[:, None, :]
    s_lat = jnp.einsum('bnqhgd,bnkhd->bnhgqk', qb, kb, preferred_element_type=F32) * scale
    s_lat = jnp.where(mask[None, :, None, None], s_lat, NEG)
    s_ctx = jnp.einsum('bnqhgd,bkhd->bnhgqk', qb, kc, preferred_element_type=F32) * scale
    p = sink_softmax(jnp.concatenate([s_lat, s_ctx], axis=-1), sink)
    o = (jnp.einsum('bnhgqk,bnkhd->bnqhgd', p[..., :3 * W].astype(vb.dtype), vb)
         + jnp.einsum('bnhgqk,bkhd->bnqhgd', p[..., 3 * W:].astype(vc.dtype), vc))
    return o.reshape(B, T, Hk, G, D)


def centred_dwconv(x, w, b):
    y = lax.conv_general_dilated(x, w[:, None, :].astype(x.dtype), window_strides=(1,),
                                 padding=((D_CONV // 2, D_CONV // 2),),
                                 dimension_numbers=('NWC', 'WIO', 'NWC'),
                                 feature_group_count=x.shape[-1])
    return y + b.astype(x.dtype)


def masked_decay(diff, causal):
    return jnp.where(causal, jnp.exp(jnp.where(causal, diff, 0.0)), 0.0)


def ssd_scan(x, dt, a, bm, cm, d_skip, s0):
    B, T, H, P = x.shape
    Q = SSD_CHUNK
    nc = T // Q
    ch = lambda t: t.astype(F32).reshape(B, nc, Q, *t.shape[2:])
    xc, dtc, bc, cc = ch(x), ch(dt), ch(bm), ch(cm)
    cum = jnp.cumsum(dtc * a.astype(F32), axis=2)
    xdt = xc * dtc[..., None]
    causal = jnp.tril(jnp.ones((Q, Q), bool))[:, :, None]
    diff = cum[:, :, :, None] - cum[:, :, None, :]
    seg = masked_decay(diff, causal)
    scores = jnp.einsum('bcihn,bcjhn->bcijh', cc, bc) * seg
    y_intra = jnp.einsum('bcijh,bcjhp->bcihp', scores, xdt)
    last = cum[:, :, -1:]
    chunk_state = jnp.einsum('bcjhn,bcjhp->bchpn', bc * jnp.exp(last - cum)[..., None], xdt)
    chunk_decay = jnp.exp(last[:, :, 0])

    def step(s, inp):
        cs, cd = inp
        return cd[..., None, None] * s + cs, s

    s_fin, s_in = lax.scan(step, s0.astype(F32),
                           (jnp.moveaxis(chunk_state, 1, 0), jnp.moveaxis(chunk_decay, 1, 0)))
    s_in = jnp.moveaxis(s_in, 0, 1)
    y_inter = jnp.einsum('bcihn,bchpn->bcihp', cc * jnp.exp(cum)[..., None], s_in)
    y = (y_intra + y_inter).reshape(B, T, H, P) + x.astype(F32) * d_skip.astype(F32)[:, None]
    return y.astype(x.dtype), s_fin.astype(x.dtype)


def gla_scan(q, k, v, log_f, s0):
    B, T, H, K = q.shape
    Q = HGRN_CHUNK
    nc = T // Q
    ch = lambda t: t.astype(F32).reshape(B, nc, Q, *t.shape[2:])
    qc, kc, vc, lf = ch(q), ch(k), ch(v), ch(log_f)
    cum = jnp.cumsum(lf, axis=2)
    causal = jnp.tril(jnp.ones((Q, Q), bool))[:, :, None, None]
    diff = cum[:, :, :, None] - cum[:, :, None, :]
    dec = masked_decay(diff, causal)
    att = jnp.einsum('bcijhk,bcjhk->bcijh', dec * qc[:, :, :, None], kc)
    o_intra = jnp.einsum('bcijh,bcjhv->bcihv', att, vc)
    last = cum[:, :, -1:]
    chunk_state = jnp.einsum('bcjhk,bcjhv->bchkv', kc * jnp.exp(last - cum), vc)
    chunk_decay = jnp.exp(last[:, :, 0])

    def step(s, inp):
        cs, cd = inp
        return cd[..., None] * s + cs, s

    s_fin, s_in = lax.scan(step, s0.astype(F32),
                           (jnp.moveaxis(chunk_state, 1, 0), jnp.moveaxis(chunk_decay, 1, 0)))
    s_in = jnp.moveaxis(s_in, 0, 1)
    o_inter = jnp.einsum('bcihk,bchkv->bcihv', qc * jnp.exp(cum), s_in)
    o = (o_intra + o_inter).reshape(B, T, H, v.shape[-1])
    return o.astype(v.dtype), s_fin.astype(v.dtype)


def flip_t(t):
    return jnp.flip(t, axis=1)


def mla_expand(ckv_n, kr, w_ukv):
    B, T, _ = ckv_n.shape
    kv = (ckv_n @ w_ukv).reshape(B, T, H_A, NOPE_A + V_A)
    k = jnp.concatenate([kv[..., :NOPE_A], jnp.broadcast_to(kr, (B, T, H_A, ROPE_A)).astype(kv.dtype)], axis=-1)
    return k, kv[..., NOPE_A:]


def ssd_mixer(z, xbc, dt_raw, p, s0):
    B, T, _ = z.shape
    xbc = jax.nn.silu(centred_dwconv(xbc, p['ssd_conv_w'], p['ssd_conv_b']))
    xs, bm, cm = jnp.split(xbc, [D_B, D_B + G_B * N_B], axis=-1)
    xs = xs.reshape(B, T, H_B, P_B)
    rep = H_B // G_B
    bm = jnp.repeat(bm.reshape(B, T, G_B, N_B), rep, axis=2)
    cm = jnp.repeat(cm.reshape(B, T, G_B, N_B), rep, axis=2)
    dt = jax.nn.softplus(dt_raw.reshape(B, T, 2, H_B).astype(F32) + p['ssd_dt_bias'].astype(F32))
    a = -jnp.exp(p['ssd_a_log'].astype(F32))
    d = p['ssd_d']
    y_f, s_f = ssd_scan(xs, dt[:, :, 0], a[0], bm, cm, d[0], s0[:, 0])
    y_b, s_b = ssd_scan(flip_t(xs), flip_t(dt[:, :, 1]), a[1], flip_t(bm), flip_t(cm), d[1], s0[:, 1])
    y = (y_f + flip_t(y_b)) * jax.nn.silu(z.reshape(B, T, H_B, P_B))
    y = rms_norm(y, p['ssd_norm'].reshape(H_B, P_B))
    return y.reshape(B, T, D_B), jnp.stack([s_f, s_b], axis=1)


def hgrn_mixer(q, f_raw, i_in, g, p, s0):
    B, T, _ = q.shape
    fr = f_raw.reshape(B, T, 2, H_C, K_C).astype(F32)
    lb = p['hgrn_lb'].astype(F32).reshape(2, H_C, K_C)
    f = lb + (1.0 - lb) * jax.nn.sigmoid(fr)
    log_f = jnp.log(jnp.maximum(f, F_MIN))
    key = (1.0 - lb) * jax.nn.sigmoid(-fr)
    qh = q.reshape(B, T, H_C, K_C)
    vh = i_in.reshape(B, T, H_C, V_C)
    o_f, s_f = gla_scan(qh, key[:, :, 0], vh, log_f[:, :, 0], s0[:, 0])
    o_b, s_b = gla_scan(flip_t(qh), flip_t(key[:, :, 1]), flip_t(vh), flip_t(log_f[:, :, 1]), s0[:, 1])
    o = rms_norm(o_f + flip_t(o_b), p['hgrn_norm'].reshape(H_C, V_C)) * jax.nn.silu(g.reshape(B, T, H_C, V_C))
    return o.reshape(B, T, D_C), jnp.stack([s_f, s_b], axis=1)


def token_mixing(h, p, ctx):
    B, T, _ = h.shape
    latent = ctx is not None
    (a_cq, a_ckv, a_kr, b_z, b_xbc, b_dt, c_q, c_f, c_i, c_g, d_q, d_k, d_v) = jnp.split(
        h @ p['w_in'], IN_SPLITS, axis=-1)

    q_a = (rms_norm(a_cq, p['mla_q_norm']) @ p['mla_w_uq']).reshape(B, T, H_A, NOPE_A + ROPE_A)
    ckv_n = rms_norm(a_ckv, p['mla_kv_norm'])
    kr = a_kr[:, :, None, :]
    if latent:
        q_a = jnp.concatenate([q_a[..., :NOPE_A], axial_rope(q_a[..., NOPE_A:])], axis=-1)
        kr = axial_rope(kr)
    k_a, v_a = mla_expand(ckv_n, kr, p['mla_w_ukv'])
    if latent:
        k_c, v_c = mla_expand(ctx[0], ctx[1][:, :, None, :], p['mla_w_ukv'])
        k_a = jnp.concatenate([k_a, k_c], axis=1)
        v_a = jnp.concatenate([v_a, v_c], axis=1)
    o_a = dense_attention(q_a[:, :, :, None, :], k_a, v_a, None).reshape(B, T, H_A * V_A)

    s0_b = ctx[2] if latent else jnp.zeros((B, 2, H_B, P_B, N_B), h.dtype)
    o_b, st_b = ssd_mixer(b_z, b_xbc, b_dt, p, s0_b)

    s0_c = ctx[3] if latent else jnp.zeros((B, 2, H_C, K_C, V_C), h.dtype)
    o_c, st_c = hgrn_mixer(c_q, c_f, c_i, c_g, p, s0_c)

    q_d = d_q.reshape(B, T, H_D, HD_D)
    k_d = d_k.reshape(B, T, KV_D, HD_D)
    v_d = d_v.reshape(B, T, KV_D, HD_D)
    sink = p['gqa_sink'].reshape(KV_D, G_D)
    if latent:
        o_d = banded_attention(axial_rope(q_d).reshape(B, T, KV_D, G_D, HD_D), axial_rope(k_d), v_d,
                               ctx[4], ctx[5], sink)
    else:
        o_d = dense_attention(q_d.reshape(B, T, KV_D, G_D, HD_D), k_d, v_d, sink)
    o_d = o_d.reshape(B, T, H_D * HD_D)

    out = jnp.concatenate([o_a, o_b, o_c, o_d], axis=-1) @ p['w_out']
    state = None if latent else (ckv_n, a_kr, st_b, st_c, k_d, v_d)
    return out, state


def swiglu(h, w_gu, w_down):
    gate, up = jnp.split(h @ w_gu, 2, axis=-1)
    return (jax.nn.silu(gate) * up) @ w_down


def modulate(x, shift, scale):
    return x * (1 + scale) + shift


def trunk_layer(x, mod, p, ctx):
    shift, scale, gate = mod[:, :, :, 0], mod[:, :, :, 1], mod[:, :, :, 2]
    h = modulate(x, shift[:, :, 0], scale[:, :, 0])
    x = layer_norm(ALPHA * x + 0.5 * gate[:, :, 0] * swiglu(h, p['ffn_w_gu'][0], p['ffn_w_down'][0]),
                   p['ln_g'][0], p['ln_b'][0])
    h = modulate(x, shift[:, :, 1], scale[:, :, 1])
    u, state = token_mixing(h, p, ctx)
    x = layer_norm(ALPHA * x + gate[:, :, 1] * u, p['ln_g'][1], p['ln_b'][1])
    h = modulate(x, shift[:, :, 2], scale[:, :, 2])
    x = layer_norm(ALPHA * x + 0.5 * gate[:, :, 2] * swiglu(h, p['ffn_w_gu'][1], p['ffn_w_down'][1]),
                   p['ln_g'][2], p['ln_b'][2])
    return x, state


def setup_inputs(seed: int = 0) -> dict:
    key = jax.random.key(seed)
    ks = iter(jax.random.split(key, 48))

    def nrm(shape, scale):
        return scale * jax.random.normal(next(ks), shape, F32)

    dt0 = jnp.exp(jax.random.uniform(next(ks), (DEPTH, 2, H_B), F32, np.log(1e-3), np.log(1e-1)))
    dt_bias = dt0 + jnp.log(-jnp.expm1(-dt0))
    a_log = jnp.log(jax.random.uniform(next(ks), (DEPTH, 2, H_B), F32, 1.0, 16.0))
    return {
        'x_prompt': nrm((BATCH, SEQ, D_MODEL), 1.0),
        'x_sample': nrm((DEC_BATCH, DEC_SEQ, D_MODEL), 1.0),
        'cache_a_ckv': nrm((DEC_BATCH, DEPTH, PAST_LEN, KV_RANK), 1.0),
        'cache_a_krope': nrm((DEC_BATCH, DEPTH, PAST_LEN, ROPE_A), 1.0),
        'state_b_ssm': nrm((DEC_BATCH, DEPTH, 2, H_B, P_B, N_B), 0.5),
        'state_c_hgrn': nrm((DEC_BATCH, DEPTH, 2, H_C, K_C, V_C), 0.5),
        'cache_d_k': nrm((DEC_BATCH, DEPTH, PAST_LEN, KV_D, HD_D), 1.0),
        'cache_d_v': nrm((DEC_BATCH, DEPTH, PAST_LEN, KV_D, HD_D), 1.0),
        'c': nrm((DEC_BATCH, D_MODEL), 1.0),
        'c_ctx': nrm((D_MODEL,), 1.0),
        'w_mod': nrm((DEPTH, D_MODEL, N_SUB * 3 * D_MODEL), 0.5 * D_MODEL ** -0.5),
        'b_mod': nrm((DEPTH, N_SUB * 3 * D_MODEL), 0.02),
        'ln_g': 1.0 + nrm((DEPTH, N_SUB, D_MODEL), 0.02),
        'ln_b': nrm((DEPTH, N_SUB, D_MODEL), 0.02),
        'ffn_w_gu': nrm((DEPTH, 2, D_MODEL, 2 * D_FF), BETA * D_MODEL ** -0.5),
        'ffn_w_down': nrm((DEPTH, 2, D_FF, D_MODEL), BETA * D_FF ** -0.5),
        'w_in': nrm((DEPTH, D_MODEL, D_IN), D_MODEL ** -0.5),
        'w_out': nrm((DEPTH, D_MIX, D_MODEL), BETA * D_MIX ** -0.5),
        'mla_q_norm': 1.0 + nrm((DEPTH, Q_RANK), 0.02),
        'mla_kv_norm': 1.0 + nrm((DEPTH, KV_RANK), 0.02),
        'mla_w_uq': nrm((DEPTH, Q_RANK, H_A * (NOPE_A + ROPE_A)), Q_RANK ** -0.5),
        'mla_w_ukv': nrm((DEPTH, KV_RANK, H_A * (NOPE_A + V_A)), KV_RANK ** -0.5),
        'ssd_conv_w': nrm((DEPTH, D_CONV, CONV_CH), D_CONV ** -0.5),
        'ssd_conv_b': nrm((DEPTH, CONV_CH), 0.02),
        'ssd_a_log': a_log,
        'ssd_dt_bias': dt_bias,
        'ssd_d': 1.0 + nrm((DEPTH, 2, H_B), 0.02),
        'ssd_norm': 1.0 + nrm((DEPTH, D_B), 0.02),
        'hgrn_lb_logits': nrm((DEPTH, 2, H_C * K_C), 0.1),
        'hgrn_norm': 1.0 + nrm((DEPTH, D_C), 0.02),
        'gqa_sink': nrm((DEPTH, H_D), 0.5),
    }


def reference(x_prompt, x_sample, cache_a_ckv, cache_a_krope, state_b_ssm, state_c_hgrn,
              cache_d_k, cache_d_v, c, c_ctx, w_mod, b_mod, ln_g, ln_b, ffn_w_gu, ffn_w_down,
              w_in, w_out, mla_q_norm, mla_kv_norm, mla_w_uq, mla_w_ukv, ssd_conv_w, ssd_conv_b,
              ssd_a_log, ssd_dt_bias, ssd_d, ssd_norm, hgrn_lb_logits, hgrn_norm, gqa_sink):
    lb_p = jax.nn.softmax(hgrn_lb_logits.astype(F32), axis=0)
    hgrn_lb = jnp.cumsum(lb_p, axis=0) - lb_p[:1]
    y_p = x_prompt
    y_s = x_sample
    ctx_states = []
    for l in range(DEPTH):
        p = {'w_in': w_in[l], 'w_out': w_out[l], 'ln_g': ln_g[l], 'ln_b': ln_b[l],
             'ffn_w_gu': ffn_w_gu[l], 'ffn_w_down': ffn_w_down[l],
             'mla_q_norm': mla_q_norm[l], 'mla_kv_norm': mla_kv_norm[l],
             'mla_w_uq': mla_w_uq[l], 'mla_w_ukv': mla_w_ukv[l],
             'ssd_conv_w': ssd_conv_w[l], 'ssd_conv_b': ssd_conv_b[l], 'ssd_a_log': ssd_a_log[l],
             'ssd_dt_bias': ssd_dt_bias[l], 'ssd_d': ssd_d[l], 'ssd_norm': ssd_norm[l],
             'hgrn_lb': hgrn_lb[l], 'hgrn_norm': hgrn_norm[l], 'gqa_sink': gqa_sink[l]}
        mod_ctx = (jax.nn.silu(c_ctx) @ w_mod[l] + b_mod[l]).reshape(1, 1, N_SUB, 3, D_MODEL)
        mod_lat = (jax.nn.silu(c) @ w_mod[l] + b_mod[l]).reshape(c.shape[0], 1, N_SUB, 3, D_MODEL)
        y_p, st = trunk_layer(y_p, mod_ctx, p, None)
        ctx_states.append(st)
        ctx = (cache_a_ckv[:, l], cache_a_krope[:, l], state_b_ssm[:, l], state_c_hgrn[:, l],
               cache_d_k[:, l], cache_d_v[:, l])
        y_s, _ = trunk_layer(y_s, mod_lat, p, ctx)
    new_a_ckv = jnp.stack([s[0] for s in ctx_states], axis=1)
    new_a_krope = jnp.stack([s[1] for s in ctx_states], axis=1)
    new_b_ssm = jnp.stack([s[2] for s in ctx_states], axis=1)
    new_c_hgrn = jnp.stack([s[3] for s in ctx_states], axis=1)
    new_d_k = jnp.stack([s[4] for s in ctx_states], axis=1)
    new_d_v = jnp.stack([s[5] for s in ctx_states], axis=1)
    return (y_p, y_s, new_a_ckv, new_a_krope, new_b_ssm, new_c_hgrn, new_d_k, new_d_v)
```

```python
import functools

import numpy as np
import jax
import jax.numpy as jnp
from jax import lax
from jax.experimental import pallas as pl
from jax.experimental.pallas import tpu as pltpu

F32 = jnp.float32
BF16 = jnp.bfloat16

D_MODEL = 1024
BATCH = 32
SEQ = 256
DEPTH = 2
DEC_BATCH = 2
DEC_SEQ = 4096
PAST_LEN = 512
GRID_W = 64
H_A, Q_RANK, KV_RANK, NOPE_A, ROPE_A, V_A = 4, 192, 128, 64, 32, 64
H_B, P_B, G_B, N_B, D_CONV, SSD_CHUNK = 4, 64, 2, 64, 5, 128
H_C, K_C, V_C, HGRN_CHUNK = 4, 64, 64, 16
H_D, KV_D, HD_D, WINDOW = 4, 2, 64, 128
G_D = H_D // KV_D
ROPE_BASE = 10000.0
D_FF = 2816
N_SUB = 3
ALPHA = (2 * DEPTH) ** 0.25
EPS = 1e-6
F_MIN = 1e-6
NEG = -1e30
D_IN = 2920
N_MOD = N_SUB * 3 * D_MODEL

LANE = 128
SUBLANE = 8
VMEM_LIMIT = 56 * 1024 * 1024

TM = 512
FF_CHUNK = 256
TQ_A = 256
GLA_TILE = 256
CONV_TILE = 256
MOD_TN = 1536

C_ACQ, C_ACKV, C_AKR = 0, 256, 384
C_BZ, C_BXBC, C_BDT = 512, 1024, 2048
C_CQ, C_CF, C_CI, C_CG = 2176, 2432, 2944, 3200
C_DQ, C_DK, C_DV = 3456, 3968, 4224
W_IN_P = 4480
W_XBC_P = 1024
HB = 128


def _dot(a, b, precision=None):
    return jnp.dot(a, b, preferred_element_type=F32, precision=precision)


def _dot_nt(a, b):
    return lax.dot_general(a, b, (((1,), (1,)), ((), ())), preferred_element_type=F32)


def _params(*sem):
    return pltpu.CompilerParams(dimension_semantics=sem, vmem_limit_bytes=VMEM_LIMIT)


def _resident(shape):
    nd = len(shape)
    return pl.BlockSpec(shape, lambda *_: (0,) * nd, pipeline_mode=pl.Buffered(1))


def _silu(x):
    return x * jax.nn.sigmoid(x)


def _layer_norm(y, g, b):
    mu = jnp.mean(y, axis=-1, keepdims=True)
    yc = y - mu
    var = jnp.mean(yc * yc, axis=-1, keepdims=True)
    return yc * lax.rsqrt(var + EPS) * g + b


def _index_map(width, pieces):
    idx = np.full((width,), -1, np.int32)
    for dst, src, w in pieces:
        idx[dst:dst + w] = np.arange(src, src + w)
    return idx


def _gather_pad(arr, idx, axis):
    valid = jnp.asarray(idx >= 0)
    shape = [1] * arr.ndim
    shape[axis] = idx.shape[0]
    out = jnp.take(arr, jnp.asarray(np.maximum(idx, 0)), axis=axis)
    return jnp.where(valid.reshape(shape), out, jnp.zeros((), arr.dtype))


def _heads(dst0, src0, n):
    return [(dst0 + HB * h, src0 + 64 * h, 64) for h in range(n)]


_IDX_W_IN = _index_map(W_IN_P, [
    (C_ACQ, 0, Q_RANK), (C_ACKV, 192, KV_RANK), (C_AKR, 320, ROPE_A),
    *_heads(C_BZ, 352, 4),
    *_heads(C_BXBC, 608, 4), *_heads(C_BXBC + 512, 864, 2), *_heads(C_BXBC + 768, 992, 2),
    (C_BDT, 1120, 2 * H_B),
    (C_CQ, 1128, 256), (C_CF, 1384, 512), (C_CI, 1896, 256), (C_CG, 2152, 256),
    *_heads(C_DQ, 2408, 4), *_heads(C_DK, 2664, 2), *_heads(C_DV, 2792, 2)])
_IDX_UQ_ROWS = _index_map(256, [(0, 0, Q_RANK)])
_IDX_UQ_COLS = _index_map(4 * HB, [(HB * h, 96 * h, 96) for h in range(H_A)])
_IDX_UKV_K = _index_map(4 * HB, [(HB * h, 128 * h, NOPE_A) for h in range(H_A)])
_IDX_UKV_V = _index_map(4 * V_A, [(V_A * h, 128 * h + NOPE_A, V_A) for h in range(H_A)])
_IDX_CONV = _index_map(W_XBC_P, [*_heads(0, 0, 4), *_heads(512, 256, 2), *_heads(768, 384, 2)])
_IDX_HEAD4 = _index_map(4 * HB, _heads(0, 0, 4))


def _rope_tables(half, lane0):
    t = np.arange(DEC_SEQ)
    pos = np.stack([t // GRID_W, t % GRID_W], 0).astype(np.float64)
    inv = ROPE_BASE ** (-np.arange(half, dtype=np.float64) / half)
    cos = np.ones((DEC_SEQ, LANE))
    sin = np.zeros((DEC_SEQ, LANE))
    for axis in range(2):
        ang = pos[axis][:, None] * inv[None, :]
        base = lane0 + axis * 2 * half
        cos[:, base:base + half] = np.cos(ang)
        cos[:, base + half:base + 2 * half] = np.cos(ang)
        sin[:, base:base + half] = -np.sin(ang)
        sin[:, base + half:base + 2 * half] = np.sin(ang)
    ident_c = np.ones((TM, LANE))
    ident_s = np.zeros((TM, LANE))
    return (jnp.asarray(np.concatenate([ident_c, cos], 0), F32),
            jnp.asarray(np.concatenate([ident_s, sin], 0), F32))


def _rope(x, cos, sin, first, half):
    partner = jnp.where(first, pltpu.roll(x, LANE - half, 1), pltpu.roll(x, half, 1))
    return x * cos + partner * sin


def _mod_kernel(c_ref, w_ref, b_ref, o_ref):
    c = c_ref[...]
    s = _silu(c).astype(BF16)
    o_ref[0] = _dot(s, w_ref[0].astype(BF16)) + b_ref[0]


def _modulation(cvec, w_mod, b_mod):
    return pl.pallas_call(
        _mod_kernel,
        grid=(DEPTH, N_MOD // MOD_TN),
        in_specs=[pl.BlockSpec((SUBLANE, D_MODEL), lambda l, j: (0, 0)),
                  pl.BlockSpec((1, D_MODEL, MOD_TN), lambda l, j: (l, 0, j)),
                  pl.BlockSpec((1, 1, MOD_TN), lambda l, j: (l, 0, j))],
        out_specs=pl.BlockSpec((1, SUBLANE, MOD_TN), lambda l, j: (l, 0, j)),
        out_shape=jax.ShapeDtypeStruct((DEPTH, SUBLANE, N_MOD), F32),
        compiler_params=_params("arbitrary", "arbitrary"),
        name="modulation",
    )(cvec, w_mod, b_mod.reshape(DEPTH, 1, N_MOD))


def _mod_spec(seq_len):
    return pl.BlockSpec((1, N_SUB * 3, D_MODEL), lambda i: (i * TM // seq_len, 0, 0))


def _ffn_kernel(x_ref, mod_ref, wg_ref, wu_ref, wd_ref, g_ref, b_ref, o_ref, *, sub):
    x = x_ref[...]
    shift = mod_ref[0, 3 * sub:3 * sub + 1, :]
    scale = mod_ref[0, 3 * sub + 1:3 * sub + 2, :]
    gate = mod_ref[0, 3 * sub + 2:3 * sub + 3, :]
    h = (x * (1.0 + scale) + shift).astype(BF16)
    acc = jnp.zeros((TM, D_MODEL), F32)
    for j in range(D_FF // FF_CHUNK):
        cols = slice(j * FF_CHUNK, (j + 1) * FF_CHUNK)
        gt = _dot(h, wg_ref[:, cols])
        up = _dot(h, wu_ref[:, cols])
        acc = acc + _dot((_silu(gt) * up).astype(BF16), wd_ref[cols, :])
    y = ALPHA * x + 0.5 * gate * acc
    o_ref[...] = _layer_norm(y, g_ref[...], b_ref[...])


def _ffn(x, mod, seq_len, wg, wu, wd, ln_g, ln_b, sub):
    n = x.shape[0]
    row = pl.BlockSpec((TM, D_MODEL), lambda i: (i, 0))
    return pl.pallas_call(
        functools.partial(_ffn_kernel, sub=sub),
        grid=(n // TM,),
        in_specs=[row, _mod_spec(seq_len), _resident((D_MODEL, D_FF)), _resident((D_MODEL, D_FF)),
                  _resident((D_FF, D_MODEL)), _resident((1, D_MODEL)), _resident((1, D_MODEL))],
        out_specs=row,
        out_shape=jax.ShapeDtypeStruct((n, D_MODEL), F32),
        compiler_params=_params("arbitrary"),
        name="ffn",
    )(x, mod, wg, wu, wd, ln_g, ln_b)


def _in_kernel(x_ref, mod_ref, w_ref, wuq_ref, wk_ref, wv_ref, gq_ref, gkv_ref, dtb_ref,
               cosq_ref, sinq_ref, cosd_ref, sind_ref,
               qa_ref, ka_ref, va_ref, ckv_ref, kr_ref, bz_ref, bxbc_ref, bdt_ref,
               cq_ref, cf_ref, ci_ref, cg_ref, dq_ref, dk_ref, dv_ref, *, latent):
    x = x_ref[...]
    h = (x * (1.0 + mod_ref[0, 4:5, :]) + mod_ref[0, 3:4, :]).astype(BF16)

    def proj(start, width):
        return _dot(h, w_ref[:, start:start + width])

    lane = lax.broadcasted_iota(jnp.int32, (TM, LANE), 1)
    first_a = (lane % 16) < 8
    first_d = (lane % 32) < 16

    def rope_a(blk):
        return _rope(blk, cosq_ref[...], sinq_ref[...], first_a, 8) if latent else blk

    def rope_d(blk):
        return _rope(blk, cosd_ref[...], sind_ref[...], first_d, 16) if latent else blk

    acq = proj(C_ACQ, 256)
    ms = jnp.sum(acq * acq, axis=-1, keepdims=True) * (1.0 / Q_RANK)
    qn = (acq * lax.rsqrt(ms + EPS) * gq_ref[...]).astype(BF16)
    q = _dot(qn, wuq_ref[...])
    scale_a = (NOPE_A + ROPE_A) ** -0.5
    for hh in range(H_A):
        blk = slice(HB * hh, HB * (hh + 1))
        qa_ref[:, blk] = (rope_a(q[:, blk]) * scale_a).astype(BF16)
    ackv = proj(C_ACKV, KV_RANK)
    ms = jnp.mean(ackv * ackv, axis=-1, keepdims=True)
    ckv = ackv * lax.rsqrt(ms + EPS) * gkv_ref[...]
    ckv_ref[...] = ckv
    ckv_b = ckv.astype(BF16)
    kk = _dot(ckv_b, wk_ref[...])
    akr = proj(C_AKR, LANE)
    kr_ref[...] = akr
    krp = rope_a(pltpu.roll(akr, NOPE_A, 1))
    for hh in range(H_A):
        blk = slice(HB * hh, HB * (hh + 1))
        ka_ref[:, blk] = (kk[:, blk] + krp).astype(BF16)
    va_ref[...] = _dot(ckv_b, wv_ref[...]).astype(BF16)

    bz_ref[...] = proj(C_BZ, 4 * HB)
    bxbc_ref[...] = proj(C_BXBC, W_XBC_P)
    dtr = proj(C_BDT, LANE) + dtb_ref[...]
    bdt_ref[...] = jnp.maximum(dtr, 0.0) + jnp.log(1.0 + jnp.exp(-jnp.abs(dtr)))

    cq_ref[...] = proj(C_CQ, 256)
    cf_ref[...] = proj(C_CF, 512)
    ci_ref[...] = proj(C_CI, 256)
    cg_ref[...] = proj(C_CG, 256)

    dq = proj(C_DQ, 4 * HB)
    scale_d = HD_D ** -0.5
    for hh in range(H_D):
        blk = slice(HB * hh, HB * (hh + 1))
        dq_ref[:, blk] = (rope_d(dq[:, blk]) * scale_d).astype(BF16)
    dk = proj(C_DK, 2 * HB)
    for hh in range(KV_D):
        blk = slice(HB * hh, HB * (hh + 1))
        dk_ref[:, blk] = rope_d(dk[:, blk])
    dv_ref[...] = proj(C_DV, 2 * HB)


_IN_OUT_WIDTHS = (("qa", 512, BF16), ("ka", 512, BF16), ("va", 256, BF16), ("ckv", 128, F32), ("kr", 128, F32),
                  ("bz", 512, F32), ("bxbc", W_XBC_P, F32), ("bdt", 128, F32),
                  ("cq", 256, F32), ("cf", 512, F32), ("ci", 256, F32), ("cg", 256, F32),
                  ("dq", 512, BF16), ("dk", 256, F32), ("dv", 256, F32))


def _in_proj(x, mod, group_len, latent, wp, tabs):
    n = x.shape[0]
    row = lambda w: pl.BlockSpec((TM, w), lambda i: (i, 0))
    tab = pl.BlockSpec((TM, LANE), (lambda i: (1 + i % (DEC_SEQ // TM), 0)) if latent else (lambda i: (0, 0)))
    outs = pl.pallas_call(
        functools.partial(_in_kernel, latent=latent),
        grid=(n // TM,),
        in_specs=[row(D_MODEL), _mod_spec(group_len), _resident((D_MODEL, W_IN_P)), _resident((256, 512)),
                  _resident((KV_RANK, 512)), _resident((KV_RANK, 256)), _resident((1, 256)),
                  _resident((1, KV_RANK)), _resident((1, LANE)), tab, tab, tab, tab],
        out_specs=[row(w) for _, w, _ in _IN_OUT_WIDTHS],
        out_shape=[jax.ShapeDtypeStruct((n, w), dt) for _, w, dt in _IN_OUT_WIDTHS],
        compiler_params=_params("arbitrary"),
        name="in_proj",
    )(x, mod, wp["w_in"], wp["w_uq"], wp["w_uk"], wp["w_uv"], wp["g_q"], wp["g_kv"], wp["dt_bias"],
      tabs[0], tabs[1], tabs[2], tabs[3])
    return dict(zip([k for k, _, _ in _IN_OUT_WIDTHS], outs))


def _mla_cache_kernel(ckv_ref, krp_ref, wk_ref, wv_ref, kc_ref, vc_ref):
    ckv_b = ckv_ref[0].astype(BF16)
    kk = _dot(ckv_b, wk_ref[...])
    krp = krp_ref[0]
    for hh in range(H_A):
        blk = slice(HB * hh, HB * (hh + 1))
        kc_ref[0, :, blk] = (kk[:, blk] + krp).astype(BF16)
    vc_ref[0] = _dot(ckv_b, wv_ref[...]).astype(BF16)


def _mla_cache(ckv, krope_placed, wp):
    nb = ckv.shape[0]
    return pl.pallas_call(
        _mla_cache_kernel,
        grid=(nb,),
        in_specs=[pl.BlockSpec((1, PAST_LEN, KV_RANK), lambda b: (b, 0, 0)),
                  pl.BlockSpec((1, PAST_LEN, LANE), lambda b: (b, 0, 0)),
                  _resident((KV_RANK, 512)), _resident((KV_RANK, 256))],
        out_specs=[pl.BlockSpec((1, PAST_LEN, 512), lambda b: (b, 0, 0)),
                   pl.BlockSpec((1, PAST_LEN, 256), lambda b: (b, 0, 0))],
        out_shape=[jax.ShapeDtypeStruct((nb, PAST_LEN, 512), BF16),
                   jax.ShapeDtypeStruct((nb, PAST_LEN, 256), BF16)],
        compiler_params=_params("arbitrary"),
        name="mla_cache",
    )(ckv, krope_placed, wp["w_uk"], wp["w_uv"])


def _mla_kernel(*refs, has_cache):
    if has_cache:
        q_ref, k_ref, v_ref, kc_ref, vc_ref, o_ref = refs
    else:
        q_ref, k_ref, v_ref, o_ref = refs
    v = v_ref[...]
    head_of_lane = lax.broadcasted_iota(jnp.int32, (1, H_A * V_A), 1) // V_A
    acc = jnp.zeros((TQ_A, H_A * V_A), F32)
    for hh in range(H_A):
        blk = slice(HB * hh, HB * (hh + 1))
        qh = q_ref[:, blk]
        s = _dot_nt(qh, k_ref[:, blk])
        m = jnp.max(s, axis=-1, keepdims=True)
        if has_cache:
            sc = _dot_nt(qh, kc_ref[0, :, blk])
            m = jnp.maximum(m, jnp.max(sc, axis=-1, keepdims=True))
        e = jnp.exp(s - m)
        den = jnp.sum(e, axis=-1, keepdims=True)
        pv = _dot(e.astype(BF16), v)
        if has_cache:
            ec = jnp.exp(sc - m)
            den = den + jnp.sum(ec, axis=-1, keepdims=True)
            pv = pv + _dot(ec.astype(BF16), vc_ref[0])
        acc = jnp.where(head_of_lane == hh, pv / den, acc)
    o_ref[...] = acc


def _mla(q, k, v, n_seq, seq_len, cache=None):
    nq = seq_len // TQ_A
    in_specs = [pl.BlockSpec((TQ_A, 512), lambda b, i: (b * nq + i, 0)),
                pl.BlockSpec((seq_len, 512), lambda b, i: (b, 0)),
                pl.BlockSpec((seq_len, 256), lambda b, i: (b, 0))]
    args = [q, k, v]
    if cache is not None:
        in_specs += [pl.BlockSpec((1, PAST_LEN, 512), lambda b, i: (b, 0, 0)),
                     pl.BlockSpec((1, PAST_LEN, 256), lambda b, i: (b, 0, 0))]
        args += list(cache)
    return pl.pallas_call(
        functools.partial(_mla_kernel, has_cache=cache is not None),
        grid=(n_seq, nq),
        in_specs=in_specs,
        out_specs=pl.BlockSpec((TQ_A, H_A * V_A), lambda b, i: (b * nq + i, 0)),
        out_shape=jax.ShapeDtypeStruct((n_seq * seq_len, H_A * V_A), F32),
        compiler_params=_params("arbitrary", "arbitrary"),
        name="mla_attention",
    )(*args)


def _sink_softmax_pv(parts, sink):
    m = sink
    for s, _ in parts:
        m = jnp.maximum(m, jnp.max(s, axis=-1, keepdims=True))
    den = jnp.exp(sink - m)
    pv = None
    for s, v in parts:
        e = jnp.exp(s - m)
        den = den + jnp.sum(e, axis=-1, keepdims=True)
        t = _dot(e.astype(BF16), v)
        pv = t if pv is None else pv + t
    return pv / den


def _gqa_ctx_kernel(q_ref, k_ref, v_ref, sink_ref, o_ref):
    for hq in range(H_D):
        kv = slice(HB * (hq // G_D), HB * (hq // G_D + 1))
        blk = slice(HB * hq, HB * (hq + 1))
        s = _dot_nt(q_ref[:, blk], k_ref[:, kv].astype(BF16))
        o_ref[:, blk] = _sink_softmax_pv([(s, v_ref[:, kv].astype(BF16))], sink_ref[hq:hq + 1, 0:1])


def _gqa_ctx(q, k, v, sink, n_seq):
    seq = lambda w: pl.BlockSpec((SEQ, w), lambda b: (b, 0))
    return pl.pallas_call(
        _gqa_ctx_kernel,
        grid=(n_seq,),
        in_specs=[seq(512), seq(256), seq(256), _resident((H_D, LANE))],
        out_specs=seq(512),
        out_shape=jax.ShapeDtypeStruct((n_seq * SEQ, 512), F32),
        compiler_params=_params("arbitrary"),
        name="gqa_context",
    )(q, k, v, sink)


def _gqa_lat_kernel(q_ref, k_ref, v_ref, kc_ref, vc_ref, sink_ref, o_ref):
    n = pl.program_id(1)
    span = 3 * WINDOW
    start = pl.multiple_of(jnp.clip((n - 1) * WINDOW, 0, DEC_SEQ - span), WINDOW)
    qpos = n * WINDOW + lax.broadcasted_iota(jnp.int32, (WINDOW, span), 0)
    kpos = start + lax.broadcasted_iota(jnp.int32, (WINDOW, span), 1)
    band = jnp.abs(kpos - qpos) <= WINDOW
    for hq in range(H_D):
        kv = slice(HB * (hq // G_D), HB * (hq // G_D + 1))
        blk = slice(HB * hq, HB * (hq + 1))
        qh = q_ref[:, blk]
        kw = k_ref[pl.ds(start, span), kv].astype(BF16)
        vw = v_ref[pl.ds(start, span), kv].astype(BF16)
        s_lat = jnp.where(band, _dot_nt(qh, kw), NEG)
        s_ctx = _dot_nt(qh, kc_ref[0, :, kv])
        o_ref[:, blk] = _sink_softmax_pv([(s_lat, vw), (s_ctx, vc_ref[0, :, kv])], sink_ref[hq:hq + 1, 0:1])


def _gqa_lat(q, k, v, kc, vc, sink, n_seq):
    nb = DEC_SEQ // WINDOW
    return pl.pallas_call(
        _gqa_lat_kernel,
        grid=(n_seq, nb),
        in_specs=[pl.BlockSpec((WINDOW, 512), lambda b, n: (b * nb + n, 0)),
                  pl.BlockSpec((DEC_SEQ, 256), lambda b, n: (b, 0)),
                  pl.BlockSpec((DEC_SEQ, 256), lambda b, n: (b, 0)),
                  pl.BlockSpec((1, PAST_LEN, 256), lambda b, n: (b, 0, 0)),
                  pl.BlockSpec((1, PAST_LEN, 256), lambda b, n: (b, 0, 0)),
                  _resident((H_D, LANE))],
        out_specs=pl.BlockSpec((WINDOW, 512), lambda b, n: (b * nb + n, 0)),
        out_shape=jax.ShapeDtypeStruct((n_seq * DEC_SEQ, 512), F32),
        compiler_params=_params("arbitrary", "arbitrary"),
        name="gqa_latent",
    )(q, k, v, kc, vc, sink)


def _conv_kernel(cur_ref, prev_ref, next_ref, w_ref, b_ref, o_ref, pad_ref, *, tiles_per_seq):
    i = pl.program_id(0)
    has_prev = (i % tiles_per_seq) != 0
    has_next = (i % tiles_per_seq) != tiles_per_seq - 1
    pad_ref[0:SUBLANE, :] = jnp.where(has_prev, prev_ref[...], 0.0)
    pad_ref[SUBLANE:SUBLANE + CONV_TILE, :] = cur_ref[...]
    pad_ref[SUBLANE + CONV_TILE:, :] = jnp.where(has_next, next_ref[...], 0.0)
    y = jnp.zeros((CONV_TILE, W_XBC_P), F32) + b_ref[...]
    for k in range(D_CONV):
        off = SUBLANE - D_CONV // 2 + k
        y = y + w_ref[k:k + 1, :] * pad_ref[off:off + CONV_TILE, :]
    o_ref[...] = _silu(y)


def _conv(xbc, seq_len, w, b):
    n = xbc.shape[0]
    per = CONV_TILE // SUBLANE
    last = n // SUBLANE - 1
    return pl.pallas_call(
        functools.partial(_conv_kernel, tiles_per_seq=seq_len // CONV_TILE),
        grid=(n // CONV_TILE,),
        in_specs=[pl.BlockSpec((CONV_TILE, W_XBC_P), lambda i: (i, 0)),
                  pl.BlockSpec((SUBLANE, W_XBC_P), lambda i: (jnp.maximum(i * per - 1, 0), 0)),
                  pl.BlockSpec((SUBLANE, W_XBC_P), lambda i: (jnp.minimum((i + 1) * per, last), 0)),
                  _resident((SUBLANE, W_XBC_P)), _resident((1, W_XBC_P))],
        out_specs=pl.BlockSpec((CONV_TILE, W_XBC_P), lambda i: (i, 0)),
        out_shape=jax.ShapeDtypeStruct((n, W_XBC_P), F32),
        scratch_shapes=[pltpu.VMEM((CONV_TILE + 2 * SUBLANE, W_XBC_P), F32)],
        compiler_params=_params("arbitrary"),
        name="ssd_conv",
    )(xbc, xbc, xbc, w, b)


def _ssd_kernel(*refs, has_s0):
    if has_s0:
        xf_ref, xb_ref, dtf_ref, dtb_ref, alog_ref, dsk_ref, s0_ref, yf_ref, yb_ref, st_ref, s_scr = refs
    else:
        xf_ref, xb_ref, dtf_ref, dtb_ref, alog_ref, dsk_ref, yf_ref, yb_ref, st_ref, s_scr = refs
    c = pl.program_id(1)
    q = SSD_CHUNK

    @pl.when(c == 0)
    def _():
        s_scr[...] = s0_ref[0] if has_s0 else jnp.zeros(s_scr.shape, F32)

    row = lax.broadcasted_iota(jnp.int32, (q, q), 0)
    col = lax.broadcasted_iota(jnp.int32, (q, q), 1)
    a_coef = -jnp.exp(alog_ref[...])
    for d, (x_ref, dt_ref, y_ref) in enumerate(((xf_ref, dtf_ref, yf_ref), (xb_ref, dtb_ref, yb_ref))):
        tri = (row >= col) if d == 0 else (row <= col)
        dt = dt_ref[...]
        cum = _dot(tri.astype(F32), dt * a_coef, precision=lax.Precision.HIGHEST)
        cum_t = cum.T
        total = cum[q - 1:q, :] if d == 0 else cum[0:1, :]
        for g in range(G_B):
            bg = x_ref[:, 512 + HB * g:512 + HB * (g + 1)]
            cg = x_ref[:, 768 + HB * g:768 + HB * (g + 1)]
            cb = _dot_nt(cg.astype(BF16), bg.astype(BF16))
            for hh in range(g * (H_B // G_B), (g + 1) * (H_B // G_B)):
                k = d * H_B + hh
                cum_c = cum[:, k:k + 1]
                diff = cum_c - cum_t[k:k + 1, :]
                seg = jnp.where(tri, jnp.exp(jnp.where(tri, diff, 0.0)), 0.0)
                xh = x_ref[:, HB * hh:HB * (hh + 1)]
                xdt = xh * dt[:, k:k + 1]
                xdt_b = xdt.astype(BF16)
                s_in = s_scr[d, hh]
                y = _dot((cb * seg).astype(BF16), xdt_b)
                y = y + _dot_nt((cg * jnp.exp(cum_c)).astype(BF16), s_in.astype(BF16))
                y_ref[:, HB * hh:HB * (hh + 1)] = y + xh * dsk_ref[k:k + 1, :]
                tot = total[:, k:k + 1]
                bdec = bg * jnp.exp(tot - cum_c)
                cs = _dot(xdt.T.astype(BF16), bdec.astype(BF16))
                s_scr[d, hh] = jnp.exp(tot) * s_in + cs

    @pl.when(c == pl.num_programs(1) - 1)
    def _():
        st_ref[0] = s_scr[...]


def _ssd(xbc, dt, n_seq, seq_len, a_log, dskip, s0=None):
    nc = seq_len // SSD_CHUNK
    fwd = lambda w: pl.BlockSpec((SSD_CHUNK, w), lambda b, c: (b * nc + c, 0))
    bwd = lambda w: pl.BlockSpec((SSD_CHUNK, w), lambda b, c: (b * nc + nc - 1 - c, 0))
    state = pl.BlockSpec((1, 2, H_B, HB, HB), lambda b, c: (b, 0, 0, 0, 0))
    in_specs = [fwd(W_XBC_P), bwd(W_XBC_P), fwd(LANE), bwd(LANE), _resident((1, LANE)), _resident((2 * H_B, LANE))]
    args = [xbc, xbc, dt, dt, a_log, dskip]
    if s0 is not None:
        in_specs.append(state)
        args.append(s0)
    n = n_seq * seq_len
    return pl.pallas_call(
        functools.partial(_ssd_kernel, has_s0=s0 is not None),
        grid=(n_seq, nc),
        in_specs=in_specs,
        out_specs=[fwd(4 * HB), bwd(4 * HB), state],
        out_shape=[jax.ShapeDtypeStruct((n, 4 * HB), F32), jax.ShapeDtypeStruct((n, 4 * HB), F32),
                   jax.ShapeDtypeStruct((n_seq, 2, H_B, HB, HB), F32)],
        scratch_shapes=[pltpu.VMEM((2, H_B, HB, HB), F32)],
        compiler_params=_params("arbitrary", "arbitrary"),
        name="ssd_scan",
    )(*args)


def _gla_kernel(*refs, has_s0):
    if has_s0:
        (qf_ref, qb_ref, ff_ref, fb_ref, vf_ref, vb_ref, lb_ref, s0_ref, of_ref, ob_ref, st_ref, s_scr) = refs
    else:
        (qf_ref, qb_ref, ff_ref, fb_ref, vf_ref, vb_ref, lb_ref, of_ref, ob_ref, st_ref, s_scr) = refs
    c = pl.program_id(1)
    t = GLA_TILE
    ch = HGRN_CHUNK
    nch = t // ch
    w = H_C * K_C

    @pl.when(c == 0)
    def _():
        s_scr[...] = s0_ref[0] if has_s0 else jnp.zeros(s_scr.shape, F32)

    row = lax.broadcasted_iota(jnp.int32, (t, t), 0)
    col = lax.broadcasted_iota(jnp.int32, (t, t), 1)
    same_chunk = (row // ch) == (col // ch)
    same_head = (row // K_C) == (col // K_C)
    head_ones = same_head.astype(BF16)
    i_in_chunk = lax.broadcasted_iota(jnp.int32, (nch, ch, w), 1)
    tok = lax.broadcasted_iota(jnp.int32, (t, w), 0)
    for d, (q_ref, f_ref, v_ref, o_ref) in enumerate(((qf_ref, ff_ref, vf_ref, of_ref),
                                                      (qb_ref, fb_ref, vb_ref, ob_ref))):
        qv = q_ref[...]
        fr = f_ref[...]
        vv = v_ref[...]
        lb = lb_ref[d:d + 1, :]
        f = lb + (1.0 - lb) * jax.nn.sigmoid(fr)
        log_f = jnp.log(jnp.maximum(f, F_MIN))
        key = (1.0 - lb) * jax.nn.sigmoid(-fr)
        tri = same_chunk & ((col <= row) if d == 0 else (col >= row))
        cum = _dot(tri.astype(F32), log_f, precision=lax.Precision.HIGHEST)
        cum3 = cum.reshape(nch, ch, w)
        q3 = qv.reshape(nch, ch, w)
        k3 = key.reshape(nch, ch, w)
        v3 = vv.reshape(nch, ch, w)
        o3 = jnp.zeros((nch, ch, w), F32)
        for j in range(ch):
            live = (i_in_chunk >= j) if d == 0 else (i_in_chunk <= j)
            e = jnp.exp(jnp.where(live, cum3 - cum3[:, j:j + 1, :], 0.0))
            term = jnp.where(live, q3 * e * k3[:, j:j + 1, :], 0.0)
            att = _dot(term.reshape(t, w).astype(BF16), head_ones)
            o3 = o3 + att.reshape(nch, ch, w) * v3[:, j:j + 1, :]
        o_intra = o3.reshape(t, w)
        edge = ch - 1 if d == 0 else 0
        last3 = jnp.broadcast_to(cum3[:, edge:edge + 1, :], (nch, ch, w))
        k_dec = (k3 * jnp.exp(last3 - cum3)).reshape(t, w)
        q_dec = (qv * jnp.exp(cum)).astype(BF16)
        v_t = vv.T.astype(BF16)
        order = range(nch) if d == 0 else range(nch - 1, -1, -1)
        for cc in order:
            rows = slice(cc * ch, (cc + 1) * ch)
            st = s_scr[d]
            o_ref[rows, :] = o_intra[rows, :] + _dot_nt(q_dec[rows, :], st.astype(BF16))
            in_chunk = (tok >= cc * ch) & (tok < (cc + 1) * ch)
            cs_t = _dot(v_t, jnp.where(in_chunk, k_dec, 0.0).astype(BF16))
            decay = jnp.exp(cum[cc * ch + edge:cc * ch + edge + 1, :])
            s_scr[d] = decay * st + jnp.where(same_head, cs_t, 0.0)

    @pl.when(c == pl.num_programs(1) - 1)
    def _():
        st_ref[0] = s_scr[...]


def _gla(cq, cf, ci, n_seq, seq_len, lb, s0=None):
    nt = seq_len // GLA_TILE
    w = H_C * K_C
    fwd = lambda j: pl.BlockSpec((GLA_TILE, w), lambda b, c: (b * nt + c, j))
    bwd = lambda j: pl.BlockSpec((GLA_TILE, w), lambda b, c: (b * nt + nt - 1 - c, j))
    state = pl.BlockSpec((1, 2, w, w), lambda b, c: (b, 0, 0, 0))
    in_specs = [fwd(0), bwd(0), fwd(0), bwd(1), fwd(0), bwd(0), _resident((2, w))]
    args = [cq, cq, cf, cf, ci, ci, lb]
    if s0 is not None:
        in_specs.append(state)
        args.append(s0)
    n = n_seq * seq_len
    return pl.pallas_call(
        functools.partial(_gla_kernel, has_s0=s0 is not None),
        grid=(n_seq, nt),
        in_specs=in_specs,
        out_specs=[fwd(0), bwd(0), state],
        out_shape=[jax.ShapeDtypeStruct((n, w), F32), jax.ShapeDtypeStruct((n, w), F32),
                   jax.ShapeDtypeStruct((n_seq, 2, w, w), F32)],
        scratch_shapes=[pltpu.VMEM((2, w, w), F32)],
        compiler_params=_params("arbitrary", "arbitrary"),
        name="hgrn_scan",
    )(*args)


def _out_kernel(x_ref, mod_ref, oa_ref, yf_ref, yb_ref, bz_ref, of_ref, ob_ref, cg_ref, od_ref,
                woa_ref, wob_ref, woc_ref, wod_ref, nb_ref, nc_ref, g_ref, b_ref, o_ref):
    x = x_ref[...]
    gate = mod_ref[0, 5:6, :]
    yb = (yf_ref[...] + yb_ref[...]) * _silu(bz_ref[...])
    parts = []
    for hh in range(H_B):
        blk = yb[:, HB * hh:HB * (hh + 1)]
        ms = jnp.sum(blk * blk, axis=-1, keepdims=True) * (1.0 / P_B)
        parts.append((blk * lax.rsqrt(ms + EPS), hh))
    u = _dot(oa_ref[...].astype(BF16), woa_ref[...])
    for blk, hh in parts:
        nb = nb_ref[:, HB * hh:HB * (hh + 1)]
        u = u + _dot((blk * nb).astype(BF16), wob_ref[HB * hh:HB * (hh + 1), :])
    oc = of_ref[...] + ob_ref[...]
    lane = lax.broadcasted_iota(jnp.int32, (1, LANE), 1)
    low = lane < V_C
    halves = []
    for t in range(H_C * V_C // LANE):
        blk = oc[:, LANE * t:LANE * (t + 1)]
        sq = blk * blk
        s_all = jnp.sum(sq, axis=-1, keepdims=True)
        s_low = jnp.sum(jnp.where(low, sq, 0.0), axis=-1, keepdims=True)
        ms = jnp.where(low, s_low, s_all - s_low) * (1.0 / V_C)
        halves.append(blk * lax.rsqrt(ms + EPS))
    ocn = jnp.concatenate(halves, axis=-1) * nc_ref[...] * _silu(cg_ref[...])
    u = u + _dot(ocn.astype(BF16), woc_ref[...])
    u = u + _dot(od_ref[...].astype(BF16), wod_ref[...])
    o_ref[...] = _layer_norm(ALPHA * x + gate * u, g_ref[...], b_ref[...])


def _out_proj(x, mod, seq_len, mix, wp, ln_g, ln_b):
    n = x.shape[0]
    row = lambda w: pl.BlockSpec((TM, w), lambda i: (i, 0))
    return pl.pallas_call(
        _out_kernel,
        grid=(n // TM,),
        in_specs=[row(D_MODEL), _mod_spec(seq_len), row(256), row(512), row(512), row(512), row(256), row(256),
                  row(256), row(512),
                  _resident((256, D_MODEL)), _resident((512, D_MODEL)), _resident((256, D_MODEL)),
                  _resident((512, D_MODEL)), _resident((1, 512)), _resident((1, 256)),
                  _resident((1, D_MODEL)), _resident((1, D_MODEL))],
        out_specs=row(D_MODEL),
        out_shape=jax.ShapeDtypeStruct((n, D_MODEL), F32),
        compiler_params=_params("arbitrary"),
        name="out_proj",
    )(x, mod, mix["oa"], mix["yf"], mix["yb"], mix["bz"], mix["of"], mix["ob"], mix["cg"], mix["od"],
      wp["w_oa"], wp["w_ob"], wp["w_oc"], wp["w_od"], wp["ssd_norm"], wp["hgrn_norm"], ln_g, ln_b)


def _prep_layer(l, w_in, w_out, mla_q_norm, mla_kv_norm, mla_w_uq, mla_w_ukv, ssd_conv_w, ssd_conv_b,
                ssd_a_log, ssd_dt_bias, ssd_d, ssd_norm, hgrn_lb, hgrn_norm, gqa_sink):
    wo = w_out[l]
    conv_w = _gather_pad(ssd_conv_w[l], _IDX_CONV, 1)
    return {
        "w_in": _gather_pad(w_in[l], _IDX_W_IN, 1).astype(BF16),
        "w_uq": _gather_pad(_gather_pad(mla_w_uq[l], _IDX_UQ_ROWS, 0), _IDX_UQ_COLS, 1).astype(BF16),
        "w_uk": _gather_pad(mla_w_ukv[l], _IDX_UKV_K, 1).astype(BF16),
        "w_uv": _gather_pad(mla_w_ukv[l], _IDX_UKV_V, 1).astype(BF16),
        "g_q": _gather_pad(mla_q_norm[l], _IDX_UQ_ROWS, 0).reshape(1, 256),
        "g_kv": mla_kv_norm[l].reshape(1, KV_RANK),
        "dt_bias": jnp.pad(ssd_dt_bias[l].reshape(1, 2 * H_B), ((0, 0), (0, LANE - 2 * H_B))),
        "conv_w": jnp.pad(conv_w, ((0, SUBLANE - D_CONV), (0, 0))),
        "conv_b": _gather_pad(ssd_conv_b[l], _IDX_CONV, 0).reshape(1, W_XBC_P),
        "a_log": jnp.pad(ssd_a_log[l].reshape(1, 2 * H_B), ((0, 0), (0, LANE - 2 * H_B))),
        "d_skip": jnp.broadcast_to(ssd_d[l].reshape(2 * H_B, 1), (2 * H_B, LANE)),
        "ssd_norm": _gather_pad(ssd_norm[l], _IDX_HEAD4, 0).reshape(1, 4 * HB),
        "hgrn_lb": hgrn_lb[l],
        "hgrn_norm": hgrn_norm[l].reshape(1, H_C * V_C),
        "sink": jnp.broadcast_to(gqa_sink[l].reshape(H_D, 1), (H_D, LANE)),
        "w_oa": wo[0:256].astype(BF16),
        "w_ob": _gather_pad(wo[256:512], _IDX_HEAD4, 0).astype(BF16),
        "w_oc": wo[512:768].astype(BF16),
        "w_od": _gather_pad(wo[768:1024], _IDX_HEAD4, 0).astype(BF16),
    }


def _pad_heads(t):
    return jnp.pad(t, [(0, 0)] * (t.ndim - 1) + [(0, HB - t.shape[-1])])


def _mixer(x, mod, group_len, wp, tabs, n_seq, seq_len, ctx):
    latent = ctx is not None
    p = _in_proj(x, mod, group_len, latent, wp, tabs)
    mix = {"bz": p["bz"], "cg": p["cg"]}
    cache = _mla_cache(ctx["ckv"], ctx["krope"], wp) if latent else None
    mix["oa"] = _mla(p["qa"], p["ka"], p["va"], n_seq, seq_len, cache)
    xbc = _conv(p["bxbc"], seq_len, wp["conv_w"], wp["conv_b"])
    mix["yf"], mix["yb"], st_b = _ssd(xbc, p["bdt"], n_seq, seq_len, wp["a_log"], wp["d_skip"],
                                      ctx["ssm"] if latent else None)
    mix["of"], mix["ob"], st_c = _gla(p["cq"], p["cf"], p["ci"], n_seq, seq_len, wp["hgrn_lb"],
                                      ctx["hgrn"] if latent else None)
    if latent:
        mix["od"] = _gqa_lat(p["dq"], p["dk"], p["dv"], ctx["dk"], ctx["dv"], wp["sink"], n_seq)
    else:
        mix["od"] = _gqa_ctx(p["dq"], p["dk"], p["dv"], wp["sink"], n_seq)
    state = None if latent else (p["ckv"], p["kr"], st_b, st_c, p["dk"], p["dv"])
    return mix, state


def _run_stream(x, mod, n_seq, seq_len, ctx, wp, ffn_w, lng, lnb, tabs):
    group_len = x.shape[0] if ctx is None else seq_len
    x = _ffn(x, mod, group_len, *ffn_w[0], lng[0], lnb[0], sub=0)
    mix, st = _mixer(x, mod, group_len, wp, tabs, n_seq, seq_len, ctx)
    x = _out_proj(x, mod, group_len, mix, wp, lng[1], lnb[1])
    x = _ffn(x, mod, group_len, *ffn_w[1], lng[2], lnb[2], sub=2)
    return x, st


def _layer_inputs(l, ctx_tensors, weights, hgrn_lb):
    (cache_a_ckv, cache_a_krope, state_b_ssm, state_c_hgrn, cache_d_k, cache_d_v) = ctx_tensors
    (ln_g, ln_b, ffn_w_gu, ffn_w_down, w_in, w_out, mla_q_norm, mla_kv_norm, mla_w_uq, mla_w_ukv, ssd_conv_w,
     ssd_conv_b, ssd_a_log, ssd_dt_bias, ssd_d, ssd_norm, hgrn_norm, gqa_sink) = weights
    wp = _prep_layer(l, w_in, w_out, mla_q_norm, mla_kv_norm, mla_w_uq, mla_w_ukv, ssd_conv_w, ssd_conv_b,
                     ssd_a_log, ssd_dt_bias, ssd_d, ssd_norm, hgrn_lb, hgrn_norm, gqa_sink)
    ffn_w = []
    for s in range(2):
        gu = ffn_w_gu[l, s].astype(BF16)
        ffn_w.append((gu[:, :D_FF], gu[:, D_FF:], ffn_w_down[l, s].astype(BF16)))
    lng = [ln_g[l, s].reshape(1, D_MODEL) for s in range(N_SUB)]
    lnb = [ln_b[l, s].reshape(1, D_MODEL) for s in range(N_SUB)]
    nb = cache_a_ckv.shape[0]
    ctx = {
        "ckv": cache_a_ckv[:, l],
        "krope": jnp.pad(cache_a_krope[:, l], ((0, 0), (0, 0), (NOPE_A, LANE - NOPE_A - ROPE_A))),
        "ssm": jnp.pad(state_b_ssm[:, l], ((0, 0),) * 3 + ((0, HB - P_B), (0, HB - N_B))),
        "hgrn": jnp.einsum("bdhkv,hg->bdhvgk", state_c_hgrn[:, l], jnp.eye(H_C, dtype=F32)).reshape(
            nb, 2, H_C * V_C, H_C * K_C),
        "dk": _pad_heads(cache_d_k[:, l]).reshape(nb, PAST_LEN, KV_D * HB).astype(BF16),
        "dv": _pad_heads(cache_d_v[:, l]).reshape(nb, PAST_LEN, KV_D * HB).astype(BF16),
    }
    return wp, ffn_w, lng, lnb, ctx


def kernel(x_prompt, x_sample, cache_a_ckv, cache_a_krope, state_b_ssm, state_c_hgrn, cache_d_k, cache_d_v,
           c, c_ctx, w_mod, b_mod, ln_g, ln_b, ffn_w_gu, ffn_w_down, w_in, w_out, mla_q_norm, mla_kv_norm,
           mla_w_uq, mla_w_ukv, ssd_conv_w, ssd_conv_b, ssd_a_log, ssd_dt_bias, ssd_d, ssd_norm,
           hgrn_lb_logits, hgrn_norm, gqa_sink):
    lb_p = jax.nn.softmax(hgrn_lb_logits.astype(F32), axis=0)
    hgrn_lb = jnp.cumsum(lb_p, axis=0) - lb_p[:1]

    cvec = jnp.concatenate([c_ctx[None], c, jnp.zeros((SUBLANE - 1 - DEC_BATCH, D_MODEL), F32)], axis=0)
    mod_all = _modulation(cvec, w_mod, b_mod)
    tabs = _rope_tables(8, NOPE_A) + _rope_tables(16, 0)
    ctx_tensors = (cache_a_ckv, cache_a_krope, state_b_ssm, state_c_hgrn, cache_d_k, cache_d_v)
    weights = (ln_g, ln_b, ffn_w_gu, ffn_w_down, w_in, w_out, mla_q_norm, mla_kv_norm, mla_w_uq, mla_w_ukv,
               ssd_conv_w, ssd_conv_b, ssd_a_log, ssd_dt_bias, ssd_d, ssd_norm, hgrn_norm, gqa_sink)

    y_p = x_prompt.reshape(BATCH * SEQ, D_MODEL)
    y_s = x_sample.reshape(DEC_BATCH * DEC_SEQ, D_MODEL)
    states = []
    for l in range(DEPTH):
        wp, ffn_w, lng, lnb, ctx = _layer_inputs(l, ctx_tensors, weights, hgrn_lb)
        mod_ctx = mod_all[l, 0:1].reshape(1, N_SUB * 3, D_MODEL)
        mod_lat = mod_all[l, 1:1 + DEC_BATCH].reshape(DEC_BATCH, N_SUB * 3, D_MODEL)
        y_p, st = _run_stream(y_p, mod_ctx, BATCH, SEQ, None, wp, ffn_w, lng, lnb, tabs)
        y_s, _ = _run_stream(y_s, mod_lat, DEC_BATCH, DEC_SEQ, ctx, wp, ffn_w, lng, lnb, tabs)
        states.append(st)

    def stack(i, f):
        return jnp.stack([f(s[i]) for s in states], axis=1)

    new_a_ckv = stack(0, lambda t: t.reshape(BATCH, SEQ, KV_RANK))
    new_a_krope = stack(1, lambda t: t.reshape(BATCH, SEQ, LANE)[..., :ROPE_A])
    new_b_ssm = stack(2, lambda t: t[..., :P_B, :N_B])
    new_c_hgrn = stack(3, lambda t: jnp.stack(
        [t.reshape(BATCH, 2, H_C, V_C, H_C, K_C)[:, :, h, :, h, :] for h in range(H_C)], axis=2).swapaxes(-1, -2))
    new_d_k = stack(4, lambda t: t.reshape(BATCH, SEQ, KV_D, HB)[..., :HD_D])
    new_d_v = stack(5, lambda t: t.reshape(BATCH, SEQ, KV_D, HB)[..., :HD_D])
    return (y_p.reshape(BATCH, SEQ, D_MODEL), y_s.reshape(DEC_BATCH, DEC_SEQ, D_MODEL),
            new_a_ckv, new_a_krope, new_b_ssm, new_c_hgrn, new_d_k, new_d_v)
```

```python
import functools

import numpy as np
import jax
import jax.numpy as jnp
from jax import lax
from jax.experimental import pallas as pl
from jax.experimental.pallas import tpu as pltpu

F32 = jnp.float32
BF16 = jnp.bfloat16

D_MODEL = 1024
BATCH = 32
SEQ = 256
DEPTH = 2
DEC_BATCH = 2
DEC_SEQ = 4096
PAST_LEN = 512
GRID_W = 64
H_A, Q_RANK, KV_RANK, NOPE_A, ROPE_A, V_A = 4, 192, 128, 64, 32, 64
H_B, P_B, G_B, N_B, D_CONV, SSD_CHUNK = 4, 64, 2, 64, 5, 128
H_C, K_C, V_C, HGRN_CHUNK = 4, 64, 64, 16
H_D, KV_D, HD_D, WINDOW = 4, 2, 64, 128
G_D = H_D // KV_D
ROPE_BASE = 10000.0
D_FF = 2816
N_SUB = 3
ALPHA = (2 * DEPTH) ** 0.25
EPS = 1e-6
F_MIN = 1e-6
NEG = -1e30
D_IN = 2920
N_MOD = N_SUB * 3 * D_MODEL

LANE = 128
SUBLANE = 8
VMEM_LIMIT = 56 * 1024 * 1024

TM = 512
FF_CHUNK = 256
TQ_A = 256
GLA_TILE = 256
CONV_TILE = 256
MOD_TN = 1536

C_ACQ, C_ACKV, C_AKR = 0, 256, 384
C_BZ, C_BXBC, C_BDT = 512, 1024, 2048
C_CQ, C_CF, C_CI, C_CG = 2176, 2432, 2944, 3200
C_DQ, C_DK, C_DV = 3456, 3968, 4224
W_IN_P = 4480
W_XBC_P = 1024
HB = 128


def _dot(a, b, precision=None):
    return jnp.dot(a, b, preferred_element_type=F32, precision=precision)


def _dot_nt(a, b):
    return lax.dot_general(a, b, (((1,), (1,)), ((), ())), preferred_element_type=F32)


def _params(*sem):
    return pltpu.CompilerParams(dimension_semantics=sem, vmem_limit_bytes=VMEM_LIMIT)


def _resident(shape, index=None):
    index = (0,) * len(shape) if index is None else index
    return pl.BlockSpec(shape, lambda *_: index, pipeline_mode=pl.Buffered(1))


def _silu(x):
    return x * jax.nn.sigmoid(x)


def _layer_norm(y, g, b):
    mu = jnp.mean(y, axis=-1, keepdims=True)
    yc = y - mu
    var = jnp.mean(yc * yc, axis=-1, keepdims=True)
    return yc * lax.rsqrt(var + EPS) * g + b


def _index_map(width, pieces):
    idx = np.full((width,), -1, np.int32)
    for dst, src, w in pieces:
        idx[dst:dst + w] = np.arange(src, src + w)
    return idx


def _gather_pad(arr, idx, axis):
    parts = []
    i = 0
    n = idx.shape[0]
    while i < n:
        j = i
        if idx[i] < 0:
            while j < n and idx[j] < 0:
                j += 1
            shape = list(arr.shape)
            shape[axis] = j - i
            parts.append(jnp.zeros(shape, arr.dtype))
        else:
            while j + 1 < n and idx[j + 1] == idx[j] + 1:
                j += 1
            j += 1
            parts.append(lax.slice_in_dim(arr, int(idx[i]), int(idx[i]) + j - i, axis=axis))
        i = j
    return jnp.concatenate(parts, axis=axis)


def _heads(dst0, src0, n):
    return [(dst0 + HB * h, src0 + 64 * h, 64) for h in range(n)]


_IDX_W_IN = _index_map(W_IN_P, [
    (C_ACQ, 0, Q_RANK), (C_ACKV, 192, KV_RANK), (C_AKR, 320, ROPE_A),
    *_heads(C_BZ, 352, 4),
    *_heads(C_BXBC, 608, 4), *_heads(C_BXBC + 512, 864, 2), *_heads(C_BXBC + 768, 992, 2),
    (C_BDT, 1120, 2 * H_B),
    (C_CQ, 1128, 256), (C_CF, 1384, 512), (C_CI, 1896, 256), (C_CG, 2152, 256),
    *_heads(C_DQ, 2408, 4), *_heads(C_DK, 2664, 2), *_heads(C_DV, 2792, 2)])
_IDX_UQ_ROWS = _index_map(256, [(0, 0, Q_RANK)])
_IDX_UQ_COLS = _index_map(4 * HB, [(HB * h, 96 * h, 96) for h in range(H_A)])
_IDX_UKV_K = _index_map(4 * HB, [(HB * h, 128 * h, NOPE_A) for h in range(H_A)])
_IDX_UKV_V = _index_map(4 * V_A, [(V_A * h, 128 * h + NOPE_A, V_A) for h in range(H_A)])
_IDX_CONV = _index_map(W_XBC_P, [*_heads(0, 0, 4), *_heads(512, 256, 2), *_heads(768, 384, 2)])
_IDX_HEAD4 = _index_map(4 * HB, _heads(0, 0, 4))


def _rope_tables(half, lane0):
    t = np.arange(DEC_SEQ)
    pos = np.stack([t // GRID_W, t % GRID_W], 0).astype(np.float64)
    inv = ROPE_BASE ** (-np.arange(half, dtype=np.float64) / half)
    cos = np.ones((DEC_SEQ, LANE))
    sin = np.zeros((DEC_SEQ, LANE))
    for axis in range(2):
        ang = pos[axis][:, None] * inv[None, :]
        base = lane0 + axis * 2 * half
        cos[:, base:base + half] = np.cos(ang)
        cos[:, base + half:base + 2 * half] = np.cos(ang)
        sin[:, base:base + half] = -np.sin(ang)
        sin[:, base + half:base + 2 * half] = np.sin(ang)
    ident_c = np.ones((TM, LANE))
    ident_s = np.zeros((TM, LANE))
    return (jnp.asarray(np.concatenate([ident_c, cos], 0), F32),
            jnp.asarray(np.concatenate([ident_s, sin], 0), F32))


def _rope(x, cos, sin, first, half):
    partner = jnp.where(first, pltpu.roll(x, LANE - half, 1), pltpu.roll(x, half, 1))
    return x * cos + partner * sin


def _mod_kernel(c_ref, w_ref, b_ref, o_ref):
    c = c_ref[...]
    s = _silu(c).astype(BF16)
    o_ref[0] = _dot(s, w_ref[0].astype(BF16)) + b_ref[0]


def _modulation(cvec, w_mod, b_mod):
    return pl.pallas_call(
        _mod_kernel,
        grid=(DEPTH, N_MOD // MOD_TN),
        in_specs=[pl.BlockSpec((SUBLANE, D_MODEL), lambda l, j: (0, 0)),
                  pl.BlockSpec((1, D_MODEL, MOD_TN), lambda l, j: (l, 0, j)),
                  pl.BlockSpec((1, 1, MOD_TN), lambda l, j: (l, 0, j))],
        out_specs=pl.BlockSpec((1, SUBLANE, MOD_TN), lambda l, j: (l, 0, j)),
        out_shape=jax.ShapeDtypeStruct((DEPTH, SUBLANE, N_MOD), F32),
        compiler_params=_params("arbitrary", "arbitrary"),
        name="modulation",
    )(cvec, w_mod, b_mod.reshape(DEPTH, 1, N_MOD))


def _mod_spec(seq_len):
    return pl.BlockSpec((1, N_SUB * 3, D_MODEL), lambda i: (i * TM // seq_len, 0, 0))


def _ffn_kernel(x_ref, mod_ref, wg_ref, wu_ref, wd_ref, g_ref, b_ref, o_ref, *, sub):
    x = x_ref[...]
    shift = mod_ref[0, 3 * sub:3 * sub + 1, :]
    scale = mod_ref[0, 3 * sub + 1:3 * sub + 2, :]
    gate = mod_ref[0, 3 * sub + 2:3 * sub + 3, :]
    h = (x * (1.0 + scale) + shift).astype(BF16)
    acc = jnp.zeros((TM, D_MODEL), F32)
    for j in range(D_FF // FF_CHUNK):
        cols = slice(j * FF_CHUNK, (j + 1) * FF_CHUNK)
        gt = _dot(h, wg_ref[:, cols])
        up = _dot(h, wu_ref[:, cols])
        acc = acc + _dot((_silu(gt) * up).astype(BF16), wd_ref[cols, :])
    y = ALPHA * x + 0.5 * gate * acc
    o_ref[...] = _layer_norm(y, g_ref[...], b_ref[...])


def _ffn(x, mod, seq_len, w_gu, wd, ln_g, ln_b, sub):
    n = x.shape[0]
    row = pl.BlockSpec((TM, D_MODEL), lambda i: (i, 0))
    return pl.pallas_call(
        functools.partial(_ffn_kernel, sub=sub),
        grid=(n // TM,),
        in_specs=[row, _mod_spec(seq_len), _resident((D_MODEL, D_FF), (0, 0)), _resident((D_MODEL, D_FF), (0, 1)),
                  _resident((D_FF, D_MODEL)), _resident((1, D_MODEL)), _resident((1, D_MODEL))],
        out_specs=row,
        out_shape=jax.ShapeDtypeStruct((n, D_MODEL), F32),
        compiler_params=_params("arbitrary"),
        name="ffn",
    )(x, mod, w_gu, w_gu, wd, ln_g, ln_b)


def _in_kernel(x_ref, mod_ref, w_ref, wuq_ref, wk_ref, wv_ref, gq_ref, gkv_ref, dtb_ref,
               cosq_ref, sinq_ref, cosd_ref, sind_ref, *out_refs, latent):
    out = dict(zip([name for name, _, _ in _in_outputs(latent)], out_refs))
    qa_ref, ka_ref, va_ref = out["qa"], out["ka"], out["va"]
    bz_ref, bxbc_ref, bdt_ref = out["bz"], out["bxbc"], out["bdt"]
    cq_ref, cf_ref, ci_ref, cg_ref = out["cq"], out["cf"], out["ci"], out["cg"]
    dq_ref, dk_ref, dv_ref = out["dq"], out["dk"], out["dv"]
    x = x_ref[...]
    h = (x * (1.0 + mod_ref[0, 4:5, :]) + mod_ref[0, 3:4, :]).astype(BF16)

    def proj(start, width):
        return _dot(h, w_ref[:, start:start + width])

    lane = lax.broadcasted_iota(jnp.int32, (TM, LANE), 1)
    first_a = (lane % 16) < 8
    first_d = (lane % 32) < 16

    def rope_a(blk):
        return _rope(blk, cosq_ref[...], sinq_ref[...], first_a, 8) if latent else blk

    def rope_d(blk):
        return _rope(blk, cosd_ref[...], sind_ref[...], first_d, 16) if latent else blk

    acq = proj(C_ACQ, 256)
    ms = jnp.sum(acq * acq, axis=-1, keepdims=True) * (1.0 / Q_RANK)
    qn = (acq * lax.rsqrt(ms + EPS) * gq_ref[...]).astype(BF16)
    q = _dot(qn, wuq_ref[...])
    scale_a = (NOPE_A + ROPE_A) ** -0.5
    for hh in range(H_A):
        blk = slice(HB * hh, HB * (hh + 1))
        qa_ref[:, blk] = (rope_a(q[:, blk]) * scale_a).astype(BF16)
    ackv = proj(C_ACKV, KV_RANK)
    ms = jnp.mean(ackv * ackv, axis=-1, keepdims=True)
    ckv = ackv * lax.rsqrt(ms + EPS) * gkv_ref[...]
    ckv_b = ckv.astype(BF16)
    kk = _dot(ckv_b, wk_ref[...])
    akr = proj(C_AKR, LANE)
    if not latent:
        out["ckv"][...] = ckv
        out["kr"][...] = akr
    krp = rope_a(pltpu.roll(akr, NOPE_A, 1))
    for hh in range(H_A):
        blk = slice(HB * hh, HB * (hh + 1))
        ka_ref[:, blk] = (kk[:, blk] + krp).astype(BF16)
    va_ref[...] = _dot(ckv_b, wv_ref[...]).astype(BF16)

    bz_ref[...] = proj(C_BZ, 4 * HB)
    bxbc_ref[...] = proj(C_BXBC, W_XBC_P)
    dtr = proj(C_BDT, LANE) + dtb_ref[...]
    bdt_ref[...] = jnp.maximum(dtr, 0.0) + jnp.log(1.0 + jnp.exp(-jnp.abs(dtr)))

    cq_ref[...] = proj(C_CQ, 256)
    cf_ref[...] = proj(C_CF, 512)
    ci_ref[...] = proj(C_CI, 256)
    cg_ref[...] = proj(C_CG, 256)

    dq = proj(C_DQ, 4 * HB)
    scale_d = HD_D ** -0.5
    for hh in range(H_D):
        blk = slice(HB * hh, HB * (hh + 1))
        dq_ref[:, blk] = (rope_d(dq[:, blk]) * scale_d).astype(BF16)
    dk = proj(C_DK, 2 * HB)
    for hh in range(KV_D):
        blk = slice(HB * hh, HB * (hh + 1))
        dk_ref[:, blk] = rope_d(dk[:, blk])
    dv = proj(C_DV, 2 * HB)
    dv_ref[...] = dv
    if not latent:
        low = lane < HD_D
        out["dkc"][...] = jnp.where(low, dk[:, 0:HB], pltpu.roll(dk[:, HB:2 * HB], HD_D, 1))
        out["dvc"][...] = jnp.where(low, dv[:, 0:HB], pltpu.roll(dv[:, HB:2 * HB], HD_D, 1))


def _in_outputs(latent):
    outs = [("qa", 512, BF16), ("ka", 512, BF16), ("va", 256, BF16),
            ("bz", 512, F32), ("bxbc", W_XBC_P, F32), ("bdt", 128, F32),
            ("cq", 256, F32), ("cf", 512, F32), ("ci", 256, F32), ("cg", 256, F32),
            ("dq", 512, BF16), ("dk", 256, F32), ("dv", 256, F32)]
    if not latent:
        outs += [("ckv", KV_RANK, F32), ("kr", LANE, F32), ("dkc", KV_D * HD_D, F32), ("dvc", KV_D * HD_D, F32)]
    return outs


def _in_proj(x, mod, group_len, latent, wp, tabs):
    n = x.shape[0]
    row = lambda w: pl.BlockSpec((TM, w), lambda i: (i, 0))
    tab = pl.BlockSpec((TM, LANE), (lambda i: (1 + i % (DEC_SEQ // TM), 0)) if latent else (lambda i: (0, 0)))
    outs = pl.pallas_call(
        functools.partial(_in_kernel, latent=latent),
        grid=(n // TM,),
        in_specs=[row(D_MODEL), _mod_spec(group_len), _resident((D_MODEL, W_IN_P)), _resident((256, 512)),
                  _resident((KV_RANK, 512)), _resident((KV_RANK, 256)), _resident((1, 256)),
                  _resident((1, KV_RANK)), _resident((1, LANE)), tab, tab, tab, tab],
        out_specs=[row(w) for _, w, _ in _in_outputs(latent)],
        out_shape=[jax.ShapeDtypeStruct((n, w), dt) for _, w, dt in _in_outputs(latent)],
        compiler_params=_params("arbitrary"),
        name="in_proj",
    )(x, mod, wp["w_in"], wp["w_uq"], wp["w_uk"], wp["w_uv"], wp["g_q"], wp["g_kv"], wp["dt_bias"],
      tabs[0], tabs[1], tabs[2], tabs[3])
    return dict(zip([k for k, _, _ in _in_outputs(latent)], outs))


def _mla_cache_kernel(ckv_ref, krp_ref, wk_ref, wv_ref, kc_ref, vc_ref):
    ckv_b = ckv_ref[0].astype(BF16)
    kk = _dot(ckv_b, wk_ref[...])
    krp = krp_ref[0]
    for hh in range(H_A):
        blk = slice(HB * hh, HB * (hh + 1))
        kc_ref[0, :, blk] = (kk[:, blk] + krp).astype(BF16)
    vc_ref[0] = _dot(ckv_b, wv_ref[...]).astype(BF16)


def _mla_cache(ckv, krope_placed, wp):
    nb = ckv.shape[0]
    return pl.pallas_call(
        _mla_cache_kernel,
        grid=(nb,),
        in_specs=[pl.BlockSpec((1, PAST_LEN, KV_RANK), lambda b: (b, 0, 0)),
                  pl.BlockSpec((1, PAST_LEN, LANE), lambda b: (b, 0, 0)),
                  _resident((KV_RANK, 512)), _resident((KV_RANK, 256))],
        out_specs=[pl.BlockSpec((1, PAST_LEN, 512), lambda b: (b, 0, 0)),
                   pl.BlockSpec((1, PAST_LEN, 256), lambda b: (b, 0, 0))],
        out_shape=[jax.ShapeDtypeStruct((nb, PAST_LEN, 512), BF16),
                   jax.ShapeDtypeStruct((nb, PAST_LEN, 256), BF16)],
        compiler_params=_params("arbitrary"),
        name="mla_cache",
    )(ckv, krope_placed, wp["w_uk"], wp["w_uv"])


def _mla_kernel(*refs, has_cache):
    if has_cache:
        q_ref, k_ref, v_ref, kc_ref, vc_ref, o_ref = refs
    else:
        q_ref, k_ref, v_ref, o_ref = refs
    v = v_ref[...]
    head_of_lane = lax.broadcasted_iota(jnp.int32, (1, H_A * V_A), 1) // V_A
    acc = jnp.zeros((TQ_A, H_A * V_A), F32)
    for hh in range(H_A):
        blk = slice(HB * hh, HB * (hh + 1))
        qh = q_ref[:, blk]
        s = _dot_nt(qh, k_ref[:, blk])
        m = jnp.max(s, axis=-1, keepdims=True)
        if has_cache:
            sc = _dot_nt(qh, kc_ref[0, :, blk])
            m = jnp.maximum(m, jnp.max(sc, axis=-1, keepdims=True))
        e = jnp.exp(s - m)
        den = jnp.sum(e, axis=-1, keepdims=True)
        pv = _dot(e.astype(BF16), v)
        if has_cache:
            ec = jnp.exp(sc - m)
            den = den + jnp.sum(ec, axis=-1, keepdims=True)
            pv = pv + _dot(ec.astype(BF16), vc_ref[0])
        acc = jnp.where(head_of_lane == hh, pv / den, acc)
    o_ref[...] = acc


def _mla(q, k, v, n_seq, seq_len, cache=None):
    nq = seq_len // TQ_A
    in_specs = [pl.BlockSpec((TQ_A, 512), lambda b, i: (b * nq + i, 0)),
                pl.BlockSpec((seq_len, 512), lambda b, i: (b, 0)),
                pl.BlockSpec((seq_len, 256), lambda b, i: (b, 0))]
    args = [q, k, v]
    if cache is not None:
        in_specs += [pl.BlockSpec((1, PAST_LEN, 512), lambda b, i: (b, 0, 0)),
                     pl.BlockSpec((1, PAST_LEN, 256), lambda b, i: (b, 0, 0))]
        args += list(cache)
    return pl.pallas_call(
        functools.partial(_mla_kernel, has_cache=cache is not None),
        grid=(n_seq, nq),
        in_specs=in_specs,
        out_specs=pl.BlockSpec((TQ_A, H_A * V_A), lambda b, i: (b * nq + i, 0)),
        out_shape=jax.ShapeDtypeStruct((n_seq * seq_len, H_A * V_A), F32),
        compiler_params=_params("arbitrary", "arbitrary"),
        name="mla_attention",
    )(*args)


def _sink_softmax_pv(parts, sink):
    m = sink
    for s, _ in parts:
        m = jnp.maximum(m, jnp.max(s, axis=-1, keepdims=True))
    den = jnp.exp(sink - m)
    pv = None
    for s, v in parts:
        e = jnp.exp(s - m)
        den = den + jnp.sum(e, axis=-1, keepdims=True)
        t = _dot(e.astype(BF16), v)
        pv = t if pv is None else pv + t
    return pv / den


def _gqa_ctx_kernel(q_ref, k_ref, v_ref, sink_ref, o_ref):
    for hq in range(H_D):
        kv = slice(HB * (hq // G_D), HB * (hq // G_D + 1))
        blk = slice(HB * hq, HB * (hq + 1))
        s = _dot_nt(q_ref[:, blk], k_ref[:, kv].astype(BF16))
        o_ref[:, blk] = _sink_softmax_pv([(s, v_ref[:, kv].astype(BF16))], sink_ref[hq:hq + 1, 0:1])


def _gqa_ctx(q, k, v, sink, n_seq):
    seq = lambda w: pl.BlockSpec((SEQ, w), lambda b: (b, 0))
    return pl.pallas_call(
        _gqa_ctx_kernel,
        grid=(n_seq,),
        in_specs=[seq(512), seq(256), seq(256), _resident((H_D, LANE))],
        out_specs=seq(512),
        out_shape=jax.ShapeDtypeStruct((n_seq * SEQ, 512), F32),
        compiler_params=_params("arbitrary"),
        name="gqa_context",
    )(q, k, v, sink)


def _gqa_lat_kernel(q_ref, k_ref, v_ref, kc_ref, vc_ref, sink_ref, o_ref):
    n = pl.program_id(1)
    span = 3 * WINDOW
    start = pl.multiple_of(jnp.clip((n - 1) * WINDOW, 0, DEC_SEQ - span), WINDOW)
    qpos = n * WINDOW + lax.broadcasted_iota(jnp.int32, (WINDOW, span), 0)
    kpos = start + lax.broadcasted_iota(jnp.int32, (WINDOW, span), 1)
    band = jnp.abs(kpos - qpos) <= WINDOW
    for hq in range(H_D):
        kv = slice(HB * (hq // G_D), HB * (hq // G_D + 1))
        blk = slice(HB * hq, HB * (hq + 1))
        qh = q_ref[:, blk]
        kw = k_ref[pl.ds(start, span), kv].astype(BF16)
        vw = v_ref[pl.ds(start, span), kv].astype(BF16)
        s_lat = jnp.where(band, _dot_nt(qh, kw), NEG)
        s_ctx = _dot_nt(qh, kc_ref[0, :, kv])
        o_ref[:, blk] = _sink_softmax_pv([(s_lat, vw), (s_ctx, vc_ref[0, :, kv])], sink_ref[hq:hq + 1, 0:1])


def _gqa_lat(q, k, v, kc, vc, sink, n_seq):
    nb = DEC_SEQ // WINDOW
    return pl.pallas_call(
        _gqa_lat_kernel,
        grid=(n_seq, nb),
        in_specs=[pl.BlockSpec((WINDOW, 512), lambda b, n: (b * nb + n, 0)),
                  pl.BlockSpec((DEC_SEQ, 256), lambda b, n: (b, 0)),
                  pl.BlockSpec((DEC_SEQ, 256), lambda b, n: (b, 0)),
                  pl.BlockSpec((1, PAST_LEN, 256), lambda b, n: (b, 0, 0)),
                  pl.BlockSpec((1, PAST_LEN, 256), lambda b, n: (b, 0, 0)),
                  _resident((H_D, LANE))],
        out_specs=pl.BlockSpec((WINDOW, 512), lambda b, n: (b * nb + n, 0)),
        out_shape=jax.ShapeDtypeStruct((n_seq * DEC_SEQ, 512), F32),
        compiler_params=_params("arbitrary", "arbitrary"),
        name="gqa_latent",
    )(q, k, v, kc, vc, sink)


def _conv_kernel(cur_ref, prev_ref, next_ref, w_ref, b_ref, o_ref, pad_ref, *, tiles_per_seq):
    i = pl.program_id(0)
    has_prev = (i % tiles_per_seq) != 0
    has_next = (i % tiles_per_seq) != tiles_per_seq - 1
    pad_ref[0:SUBLANE, :] = jnp.where(has_prev, prev_ref[...], 0.0)
    pad_ref[SUBLANE:SUBLANE + CONV_TILE, :] = cur_ref[...]
    pad_ref[SUBLANE + CONV_TILE:, :] = jnp.where(has_next, next_ref[...], 0.0)
    y = jnp.zeros((CONV_TILE, W_XBC_P), F32) + b_ref[...]
    for k in range(D_CONV):
        off = SUBLANE - D_CONV // 2 + k
        y = y + w_ref[k:k + 1, :] * pad_ref[off:off + CONV_TILE, :]
    o_ref[...] = _silu(y)


def _conv(xbc, seq_len, w, b):
    n = xbc.shape[0]
    per = CONV_TILE // SUBLANE
    last = n // SUBLANE - 1
    return pl.pallas_call(
        functools.partial(_conv_kernel, tiles_per_seq=seq_len // CONV_TILE),
        grid=(n // CONV_TILE,),
        in_specs=[pl.BlockSpec((CONV_TILE, W_XBC_P), lambda i: (i, 0)),
                  pl.BlockSpec((SUBLANE, W_XBC_P), lambda i: (jnp.maximum(i * per - 1, 0), 0)),
                  pl.BlockSpec((SUBLANE, W_XBC_P), lambda i: (jnp.minimum((i + 1) * per, last), 0)),
                  _resident((SUBLANE, W_XBC_P)), _resident((1, W_XBC_P))],
        out_specs=pl.BlockSpec((CONV_TILE, W_XBC_P), lambda i: (i, 0)),
        out_shape=jax.ShapeDtypeStruct((n, W_XBC_P), F32),
        scratch_shapes=[pltpu.VMEM((CONV_TILE + 2 * SUBLANE, W_XBC_P), F32)],
        compiler_params=_params("arbitrary"),
        name="ssd_conv",
    )(xbc, xbc, xbc, w, b)


def _ssd_kernel(*refs, has_s0):
    if has_s0:
        xf_ref, xb_ref, dtf_ref, dtb_ref, alog_ref, dsk_ref, s0_ref, yf_ref, yb_ref, st_ref, s_scr = refs
    else:
        xf_ref, xb_ref, dtf_ref, dtb_ref, alog_ref, dsk_ref, yf_ref, yb_ref, st_ref, s_scr = refs
    c = pl.program_id(1)
    q = SSD_CHUNK

    @pl.when(c == 0)
    def _():
        s_scr[...] = s0_ref[0] if has_s0 else jnp.zeros(s_scr.shape, F32)

    row = lax.broadcasted_iota(jnp.int32, (q, q), 0)
    col = lax.broadcasted_iota(jnp.int32, (q, q), 1)
    a_coef = -jnp.exp(alog_ref[...])
    for d, (x_ref, dt_ref, y_ref) in enumerate(((xf_ref, dtf_ref, yf_ref), (xb_ref, dtb_ref, yb_ref))):
        tri = (row >= col) if d == 0 else (row <= col)
        dt = dt_ref[...]
        cum = _dot(tri.astype(F32), dt * a_coef, precision=lax.Precision.HIGHEST)
        cum_t = cum.T
        total = cum[q - 1:q, :] if d == 0 else cum[0:1, :]
        for g in range(G_B):
            bg = x_ref[:, 512 + HB * g:512 + HB * (g + 1)]
            cg = x_ref[:, 768 + HB * g:768 + HB * (g + 1)]
            cb = _dot_nt(cg.astype(BF16), bg.astype(BF16))
            for hh in range(g * (H_B // G_B), (g + 1) * (H_B // G_B)):
                k = d * H_B + hh
                cum_c = cum[:, k:k + 1]
                diff = cum_c - cum_t[k:k + 1, :]
                seg = jnp.where(tri, jnp.exp(jnp.where(tri, diff, 0.0)), 0.0)
                xh = x_ref[:, HB * hh:HB * (hh + 1)]
                xdt = xh * dt[:, k:k + 1]
                xdt_b = xdt.astype(BF16)
                s_in = s_scr[d, hh]
                y = _dot((cb * seg).astype(BF16), xdt_b)
                y = y + _dot_nt((cg * jnp.exp(cum_c)).astype(BF16), s_in.astype(BF16))
                y_ref[:, HB * hh:HB * (hh + 1)] = y + xh * dsk_ref[k:k + 1, :]
                tot = total[:, k:k + 1]
                bdec = bg * jnp.exp(tot - cum_c)
                cs = _dot(xdt.T.astype(BF16), bdec.astype(BF16))
                s_scr[d, hh] = jnp.exp(tot) * s_in + cs

    @pl.when(c == pl.num_programs(1) - 1)
    def _():
        for d in range(2):
            for hh in range(H_B):
                st_ref[0, d, hh] = s_scr[d, hh, 0:P_B, 0:N_B]


def _ssd(xbc, dt, n_seq, seq_len, a_log, dskip, s0=None):
    nc = seq_len // SSD_CHUNK
    fwd = lambda w: pl.BlockSpec((SSD_CHUNK, w), lambda b, c: (b * nc + c, 0))
    bwd = lambda w: pl.BlockSpec((SSD_CHUNK, w), lambda b, c: (b * nc + nc - 1 - c, 0))
    state = pl.BlockSpec((1, 2, H_B, HB, HB), lambda b, c: (b, 0, 0, 0, 0))
    in_specs = [fwd(W_XBC_P), bwd(W_XBC_P), fwd(LANE), bwd(LANE), _resident((1, LANE)), _resident((2 * H_B, LANE))]
    args = [xbc, xbc, dt, dt, a_log, dskip]
    if s0 is not None:
        in_specs.append(state)
        args.append(s0)
    n = n_seq * seq_len
    return pl.pallas_call(
        functools.partial(_ssd_kernel, has_s0=s0 is not None),
        grid=(n_seq, nc),
        in_specs=in_specs,
        out_specs=[fwd(4 * HB), bwd(4 * HB), pl.BlockSpec((1, 2, H_B, P_B, N_B), lambda b, c: (b, 0, 0, 0, 0))],
        out_shape=[jax.ShapeDtypeStruct((n, 4 * HB), F32), jax.ShapeDtypeStruct((n, 4 * HB), F32),
                   jax.ShapeDtypeStruct((n_seq, 2, H_B, P_B, N_B), F32)],
        scratch_shapes=[pltpu.VMEM((2, H_B, HB, HB), F32)],
        compiler_params=_params("arbitrary", "arbitrary"),
        name="ssd_scan",
    )(*args)


def _gla_kernel(*refs, has_s0):
    if has_s0:
        (qf_ref, qb_ref, ff_ref, fb_ref, vf_ref, vb_ref, lb_ref, s0_ref, of_ref, ob_ref, st_ref, s_scr) = refs
    else:
        (qf_ref, qb_ref, ff_ref, fb_ref, vf_ref, vb_ref, lb_ref, of_ref, ob_ref, st_ref, s_scr) = refs
    c = pl.program_id(1)
    t = GLA_TILE
    ch = HGRN_CHUNK
    nch = t // ch
    w = H_C * K_C

    @pl.when(c == 0)
    def _():
        s_scr[...] = s0_ref[0] if has_s0 else jnp.zeros(s_scr.shape, F32)

    row = lax.broadcasted_iota(jnp.int32, (t, t), 0)
    col = lax.broadcasted_iota(jnp.int32, (t, t), 1)
    same_chunk = (row // ch) == (col // ch)
    same_head = (row // K_C) == (col // K_C)
    head_ones = same_head.astype(BF16)
    i_in_chunk = lax.broadcasted_iota(jnp.int32, (nch, ch, w), 1)
    tok = lax.broadcasted_iota(jnp.int32, (t, w), 0)
    for d, (q_ref, f_ref, v_ref, o_ref) in enumerate(((qf_ref, ff_ref, vf_ref, of_ref),
                                                      (qb_ref, fb_ref, vb_ref, ob_ref))):
        qv = q_ref[...]
        fr = f_ref[...]
        vv = v_ref[...]
        lb = lb_ref[d:d + 1, :]
        f = lb + (1.0 - lb) * jax.nn.sigmoid(fr)
        log_f = jnp.log(jnp.maximum(f, F_MIN))
        key = (1.0 - lb) * jax.nn.sigmoid(-fr)
        tri = same_chunk & ((col <= row) if d == 0 else (col >= row))
        cum = _dot(tri.astype(F32), log_f, precision=lax.Precision.HIGHEST)
        cum3 = cum.reshape(nch, ch, w)
        q3 = qv.reshape(nch, ch, w)
        k3 = key.reshape(nch, ch, w)
        v3 = vv.reshape(nch, ch, w)
        o3 = jnp.zeros((nch, ch, w), F32)
        for j in range(ch):
            live = (i_in_chunk >= j) if d == 0 else (i_in_chunk <= j)
            e = jnp.exp(jnp.where(live, cum3 - cum3[:, j:j + 1, :], 0.0))
            term = jnp.where(live, q3 * e * k3[:, j:j + 1, :], 0.0)
            att = _dot(term.reshape(t, w).astype(BF16), head_ones)
            o3 = o3 + att.reshape(nch, ch, w) * v3[:, j:j + 1, :]
        o_intra = o3.reshape(t, w)
        edge = ch - 1 if d == 0 else 0
        last3 = jnp.broadcast_to(cum3[:, edge:edge + 1, :], (nch, ch, w))
        k_dec = (k3 * jnp.exp(last3 - cum3)).reshape(t, w)
        q_dec = (qv * jnp.exp(cum)).astype(BF16)
        v_t = vv.T.astype(BF16)
        order = range(nch) if d == 0 else range(nch - 1, -1, -1)
        for cc in order:
            rows = slice(cc * ch, (cc + 1) * ch)
            st = s_scr[d]
            o_ref[rows, :] = o_intra[rows, :] + _dot_nt(q_dec[rows, :], st.astype(BF16))
            in_chunk = (tok >= cc * ch) & (tok < (cc + 1) * ch)
            cs_t = _dot(v_t, jnp.where(in_chunk, k_dec, 0.0).astype(BF16))
            decay = jnp.exp(cum[cc * ch + edge:cc * ch + edge + 1, :])
            s_scr[d] = decay * st + jnp.where(same_head, cs_t, 0.0)

    @pl.when(c == pl.num_programs(1) - 1)
    def _():
        k_idx = lax.broadcasted_iota(jnp.int32, (K_C, w), 0)
        lane_idx = lax.broadcasted_iota(jnp.int32, (K_C, w), 1)
        for d in range(2):
            for hh in range(H_C):
                sel = (lane_idx == hh * K_C + k_idx).astype(F32)
                rows = s_scr[d, hh * V_C:(hh + 1) * V_C, :]
                st_ref[0, d, hh] = lax.dot_general(sel, rows, (((1,), (1,)), ((), ())),
                                                   preferred_element_type=F32, precision=lax.Precision.HIGHEST)


def _gla(cq, cf, ci, n_seq, seq_len, lb, s0=None):
    nt = seq_len // GLA_TILE
    w = H_C * K_C
    fwd = lambda j: pl.BlockSpec((GLA_TILE, w), lambda b, c: (b * nt + c, j))
    bwd = lambda j: pl.BlockSpec((GLA_TILE, w), lambda b, c: (b * nt + nt - 1 - c, j))
    state = pl.BlockSpec((1, 2, w, w), lambda b, c: (b, 0, 0, 0))
    in_specs = [fwd(0), bwd(0), fwd(0), bwd(1), fwd(0), bwd(0), _resident((2, w))]
    args = [cq, cq, cf, cf, ci, ci, lb]
    if s0 is not None:
        in_specs.append(state)
        args.append(s0)
    n = n_seq * seq_len
    return pl.pallas_call(
        functools.partial(_gla_kernel, has_s0=s0 is not None),
        grid=(n_seq, nt),
        in_specs=in_specs,
        out_specs=[fwd(0), bwd(0), pl.BlockSpec((1, 2, H_C, K_C, V_C), lambda b, c: (b, 0, 0, 0, 0))],
        out_shape=[jax.ShapeDtypeStruct((n, w), F32), jax.ShapeDtypeStruct((n, w), F32),
                   jax.ShapeDtypeStruct((n_seq, 2, H_C, K_C, V_C), F32)],
        scratch_shapes=[pltpu.VMEM((2, w, w), F32)],
        compiler_params=_params("arbitrary", "arbitrary"),
        name="hgrn_scan",
    )(*args)


def _out_kernel(x_ref, mod_ref, oa_ref, yf_ref, yb_ref, bz_ref, of_ref, ob_ref, cg_ref, od_ref,
                woa_ref, wob_ref, woc_ref, wod_ref, nb_ref, nc_ref, g_ref, b_ref, o_ref):
    x = x_ref[...]
    gate = mod_ref[0, 5:6, :]
    yb = (yf_ref[...] + yb_ref[...]) * _silu(bz_ref[...])
    parts = []
    for hh in range(H_B):
        blk = yb[:, HB * hh:HB * (hh + 1)]
        ms = jnp.sum(blk * blk, axis=-1, keepdims=True) * (1.0 / P_B)
        parts.append((blk * lax.rsqrt(ms + EPS), hh))
    u = _dot(oa_ref[...].astype(BF16), woa_ref[...])
    for blk, hh in parts:
        nb = nb_ref[:, HB * hh:HB * (hh + 1)]
        u = u + _dot((blk * nb).astype(BF16), wob_ref[HB * hh:HB * (hh + 1), :])
    oc = of_ref[...] + ob_ref[...]
    lane = lax.broadcasted_iota(jnp.int32, (1, LANE), 1)
    low = lane < V_C
    halves = []
    for t in range(H_C * V_C // LANE):
        blk = oc[:, LANE * t:LANE * (t + 1)]
        sq = blk * blk
        s_all = jnp.sum(sq, axis=-1, keepdims=True)
        s_low = jnp.sum(jnp.where(low, sq, 0.0), axis=-1, keepdims=True)
        ms = jnp.where(low, s_low, s_all - s_low) * (1.0 / V_C)
        halves.append(blk * lax.rsqrt(ms + EPS))
    ocn = jnp.concatenate(halves, axis=-1) * nc_ref[...] * _silu(cg_ref[...])
    u = u + _dot(ocn.astype(BF16), woc_ref[...])
    u = u + _dot(od_ref[...].astype(BF16), wod_ref[...])
    o_ref[...] = _layer_norm(ALPHA * x + gate * u, g_ref[...], b_ref[...])


def _out_proj(x, mod, seq_len, mix, wp, ln_g, ln_b):
    n = x.shape[0]
    row = lambda w: pl.BlockSpec((TM, w), lambda i: (i, 0))
    return pl.pallas_call(
        _out_kernel,
        grid=(n // TM,),
        in_specs=[row(D_MODEL), _mod_spec(seq_len), row(256), row(512), row(512), row(512), row(256), row(256),
                  row(256), row(512),
                  _resident((256, D_MODEL)), _resident((512, D_MODEL)), _resident((256, D_MODEL)),
                  _resident((512, D_MODEL)), _resident((1, 512)), _resident((1, 256)),
                  _resident((1, D_MODEL)), _resident((1, D_MODEL))],
        out_specs=row(D_MODEL),
        out_shape=jax.ShapeDtypeStruct((n, D_MODEL), F32),
        compiler_params=_params("arbitrary"),
        name="out_proj",
    )(x, mod, mix["oa"], mix["yf"], mix["yb"], mix["bz"], mix["of"], mix["ob"], mix["cg"], mix["od"],
      wp["w_oa"], wp["w_ob"], wp["w_oc"], wp["w_od"], wp["ssd_norm"], wp["hgrn_norm"], ln_g, ln_b)


def _prep_layer(l, w_in, w_out, mla_q_norm, mla_kv_norm, mla_w_uq, mla_w_ukv, ssd_conv_w, ssd_conv_b,
                ssd_a_log, ssd_dt_bias, ssd_d, ssd_norm, hgrn_lb, hgrn_norm, gqa_sink):
    wo = w_out[l]
    conv_w = _gather_pad(ssd_conv_w[l], _IDX_CONV, 1)
    return {
        "w_in": _gather_pad(w_in[l], _IDX_W_IN, 1).astype(BF16),
        "w_uq": _gather_pad(_gather_pad(mla_w_uq[l], _IDX_UQ_ROWS, 0), _IDX_UQ_COLS, 1).astype(BF16),
        "w_uk": _gather_pad(mla_w_ukv[l], _IDX_UKV_K, 1).astype(BF16),
        "w_uv": _gather_pad(mla_w_ukv[l], _IDX_UKV_V, 1).astype(BF16),
        "g_q": _gather_pad(mla_q_norm[l], _IDX_UQ_ROWS, 0).reshape(1, 256),
        "g_kv": mla_kv_norm[l].reshape(1, KV_RANK),
        "dt_bias": jnp.pad(ssd_dt_bias[l].reshape(1, 2 * H_B), ((0, 0), (0, LANE - 2 * H_B))),
        "conv_w": jnp.pad(conv_w, ((0, SUBLANE - D_CONV), (0, 0))),
        "conv_b": _gather_pad(ssd_conv_b[l], _IDX_CONV, 0).reshape(1, W_XBC_P),
        "a_log": jnp.pad(ssd_a_log[l].reshape(1, 2 * H_B), ((0, 0), (0, LANE - 2 * H_B))),
        "d_skip": jnp.broadcast_to(ssd_d[l].reshape(2 * H_B, 1), (2 * H_B, LANE)),
        "ssd_norm": _gather_pad(ssd_norm[l], _IDX_HEAD4, 0).reshape(1, 4 * HB),
        "hgrn_lb": hgrn_lb[l],
        "hgrn_norm": hgrn_norm[l].reshape(1, H_C * V_C),
        "sink": jnp.broadcast_to(gqa_sink[l].reshape(H_D, 1), (H_D, LANE)),
        "w_oa": wo[0:256].astype(BF16),
        "w_ob": _gather_pad(wo[256:512], _IDX_HEAD4, 0).astype(BF16),
        "w_oc": wo[512:768].astype(BF16),
        "w_od": _gather_pad(wo[768:1024], _IDX_HEAD4, 0).astype(BF16),
    }


def _pad_heads(t):
    return jnp.pad(t, [(0, 0)] * (t.ndim - 1) + [(0, HB - t.shape[-1])])


def _mixer(x, mod, group_len, wp, tabs, n_seq, seq_len, ctx):
    latent = ctx is not None
    p = _in_proj(x, mod, group_len, latent, wp, tabs)
    mix = {"bz": p["bz"], "cg": p["cg"]}
    cache = _mla_cache(ctx["ckv"], ctx["krope"], wp) if latent else None
    mix["oa"] = _mla(p["qa"], p["ka"], p["va"], n_seq, seq_len, cache)
    xbc = _conv(p["bxbc"], seq_len, wp["conv_w"], wp["conv_b"])
    mix["yf"], mix["yb"], st_b = _ssd(xbc, p["bdt"], n_seq, seq_len, wp["a_log"], wp["d_skip"],
                                      ctx["ssm"] if latent else None)
    mix["of"], mix["ob"], st_c = _gla(p["cq"], p["cf"], p["ci"], n_seq, seq_len, wp["hgrn_lb"],
                                      ctx["hgrn"] if latent else None)
    if latent:
        mix["od"] = _gqa_lat(p["dq"], p["dk"], p["dv"], ctx["dk"], ctx["dv"], wp["sink"], n_seq)
    else:
        mix["od"] = _gqa_ctx(p["dq"], p["dk"], p["dv"], wp["sink"], n_seq)
    state = None if latent else (p["ckv"], p["kr"], st_b, st_c, p["dkc"], p["dvc"])
    return mix, state


def _run_stream(x, mod, n_seq, seq_len, ctx, wp, ffn_w, lng, lnb, tabs):
    group_len = x.shape[0] if ctx is None else seq_len
    x = _ffn(x, mod, group_len, *ffn_w[0], lng[0], lnb[0], sub=0)
    mix, st = _mixer(x, mod, group_len, wp, tabs, n_seq, seq_len, ctx)
    x = _out_proj(x, mod, group_len, mix, wp, lng[1], lnb[1])
    x = _ffn(x, mod, group_len, *ffn_w[1], lng[2], lnb[2], sub=2)
    return x, st


def _layer_inputs(l, ctx_tensors, weights, hgrn_lb):
    (cache_a_ckv, cache_a_krope, state_b_ssm, state_c_hgrn, cache_d_k, cache_d_v) = ctx_tensors
    (ln_g, ln_b, ffn_w_gu, ffn_w_down, w_in, w_out, mla_q_norm, mla_kv_norm, mla_w_uq, mla_w_ukv, ssd_conv_w,
     ssd_conv_b, ssd_a_log, ssd_dt_bias, ssd_d, ssd_norm, hgrn_norm, gqa_sink) = weights
    wp = _prep_layer(l, w_in, w_out, mla_q_norm, mla_kv_norm, mla_w_uq, mla_w_ukv, ssd_conv_w, ssd_conv_b,
                     ssd_a_log, ssd_dt_bias, ssd_d, ssd_norm, hgrn_lb, hgrn_norm, gqa_sink)
    ffn_w = [(ffn_w_gu[l, s].astype(BF16), ffn_w_down[l, s].astype(BF16)) for s in range(2)]
    lng = [ln_g[l, s].reshape(1, D_MODEL) for s in range(N_SUB)]
    lnb = [ln_b[l, s].reshape(1, D_MODEL) for s in range(N_SUB)]
    nb = cache_a_ckv.shape[0]
    ctx = {
        "ckv": cache_a_ckv[:, l],
        "krope": jnp.pad(cache_a_krope[:, l], ((0, 0), (0, 0), (NOPE_A, LANE - NOPE_A - ROPE_A))),
        "ssm": jnp.pad(state_b_ssm[:, l], ((0, 0),) * 3 + ((0, HB - P_B), (0, HB - N_B))),
        "hgrn": jnp.einsum("bdhkv,hg->bdhvgk", state_c_hgrn[:, l], jnp.eye(H_C, dtype=F32)).reshape(
            nb, 2, H_C * V_C, H_C * K_C),
        "dk": _pad_heads(cache_d_k[:, l]).reshape(nb, PAST_LEN, KV_D * HB).astype(BF16),
        "dv": _pad_heads(cache_d_v[:, l]).reshape(nb, PAST_LEN, KV_D * HB).astype(BF16),
    }
    return wp, ffn_w, lng, lnb, ctx


def kernel(x_prompt, x_sample, cache_a_ckv, cache_a_krope, state_b_ssm, state_c_hgrn, cache_d_k, cache_d_v,
           c, c_ctx, w_mod, b_mod, ln_g, ln_b, ffn_w_gu, ffn_w_down, w_in, w_out, mla_q_norm, mla_kv_norm,
           mla_w_uq, mla_w_ukv, ssd_conv_w, ssd_conv_b, ssd_a_log, ssd_dt_bias, ssd_d, ssd_norm,
           hgrn_lb_logits, hgrn_norm, gqa_sink):
    lb_p = jax.nn.softmax(hgrn_lb_logits.astype(F32), axis=0)
    hgrn_lb = jnp.cumsum(lb_p, axis=0) - lb_p[:1]

    cvec = jnp.concatenate([c_ctx[None], c, jnp.zeros((SUBLANE - 1 - DEC_BATCH, D_MODEL), F32)], axis=0)
    mod_all = _modulation(cvec, w_mod, b_mod)
    tabs = _rope_tables(8, NOPE_A) + _rope_tables(16, 0)
    ctx_tensors = (cache_a_ckv, cache_a_krope, state_b_ssm, state_c_hgrn, cache_d_k, cache_d_v)
    weights = (ln_g, ln_b, ffn_w_gu, ffn_w_down, w_in, w_out, mla_q_norm, mla_kv_norm, mla_w_uq, mla_w_ukv,
               ssd_conv_w, ssd_conv_b, ssd_a_log, ssd_dt_bias, ssd_d, ssd_norm, hgrn_norm, gqa_sink)

    y_p = x_prompt.reshape(BATCH * SEQ, D_MODEL)
    y_s = x_sample.reshape(DEC_BATCH * DEC_SEQ, D_MODEL)
    states = []
    for l in range(DEPTH):
        wp, ffn_w, lng, lnb, ctx = _layer_inputs(l, ctx_tensors, weights, hgrn_lb)
        mod_ctx = mod_all[l, 0:1].reshape(1, N_SUB * 3, D_MODEL)
        mod_lat = mod_all[l, 1:1 + DEC_BATCH].reshape(DEC_BATCH, N_SUB * 3, D_MODEL)
        y_p, st = _run_stream(y_p, mod_ctx, BATCH, SEQ, None, wp, ffn_w, lng, lnb, tabs)
        y_s, _ = _run_stream(y_s, mod_lat, DEC_BATCH, DEC_SEQ, ctx, wp, ffn_w, lng, lnb, tabs)
        states.append(st)

    def stack(i, f):
        return jnp.stack([f(s[i]) for s in states], axis=1)

    new_a_ckv = stack(0, lambda t: t.reshape(BATCH, SEQ, KV_RANK))
    new_a_krope = stack(1, lambda t: t.reshape(BATCH, SEQ, LANE)[..., :ROPE_A])
    new_b_ssm = stack(2, lambda t: t)
    new_c_hgrn = stack(3, lambda t: t)
    new_d_k = stack(4, lambda t: t.reshape(BATCH, SEQ, KV_D, HD_D))
    new_d_v = stack(5, lambda t: t.reshape(BATCH, SEQ, KV_D, HD_D))
    return (y_p.reshape(BATCH, SEQ, D_MODEL), y_s.reshape(DEC_BATCH, DEC_SEQ, D_MODEL),
            new_a_ckv, new_a_krope, new_b_ssm, new_c_hgrn, new_d_k, new_d_v)
```

```python
import functools

import numpy as np
import jax
import jax.numpy as jnp
from jax import lax
from jax.experimental import pallas as pl
from jax.experimental.pallas import tpu as pltpu

F32 = jnp.float32
BF16 = jnp.bfloat16

D_MODEL = 1024
BATCH = 32
SEQ = 256
DEPTH = 2
DEC_BATCH = 2
DEC_SEQ = 4096
PAST_LEN = 512
GRID_W = 64
H_A, Q_RANK, KV_RANK, NOPE_A, ROPE_A, V_A = 4, 192, 128, 64, 32, 64
H_B, P_B, G_B, N_B, D_CONV, SSD_CHUNK = 4, 64, 2, 64, 5, 128
H_C, K_C, V_C, HGRN_CHUNK = 4, 64, 64, 16
H_D, KV_D, HD_D, WINDOW = 4, 2, 64, 128
G_D = H_D // KV_D
ROPE_BASE = 10000.0
D_FF = 2816
N_SUB = 3
ALPHA = (2 * DEPTH) ** 0.25
EPS = 1e-6
F_MIN = 1e-6
NEG = -1e30
D_IN = 2920
N_MOD = N_SUB * 3 * D_MODEL

LANE = 128
SUBLANE = 8
VMEM_LIMIT = 56 * 1024 * 1024

TM = 512
FF_CHUNK = 256
TQ_A = 256
GLA_TILE = 256
CONV_TILE = 256
MOD_TN = 1536
GLA_SAFE_LOG_DECAY = 60.0

C_ACQ, C_ACKV, C_AKR = 0, 256, 384
C_BZ, C_BXBC, C_BDT = 512, 1024, 2048
C_CQ, C_CF, C_CI, C_CG = 2176, 2432, 2944, 3200
C_DQ, C_DK, C_DV = 3456, 3968, 4224
W_IN_P = 4480
W_XBC_P = 1024
HB = 128


def _dot(a, b, precision=None):
    return jnp.dot(a, b, preferred_element_type=F32, precision=precision)


def _dot_nt(a, b):
    return lax.dot_general(a, b, (((1,), (1,)), ((), ())), preferred_element_type=F32)


def _prefix_dot(tri, x):
    t = tri.astype(BF16)
    hi = x.astype(BF16)
    rest = x - hi.astype(F32)
    mid = rest.astype(BF16)
    lo = (rest - mid.astype(F32)).astype(BF16)
    return _dot(t, hi) + _dot(t, mid) + _dot(t, lo)


def _params(*sem):
    return pltpu.CompilerParams(dimension_semantics=sem, vmem_limit_bytes=VMEM_LIMIT)


def _resident(shape, index=None):
    index = (0,) * len(shape) if index is None else index
    return pl.BlockSpec(shape, lambda *_: index, pipeline_mode=pl.Buffered(1))


def _silu(x):
    return x * jax.nn.sigmoid(x)


def _layer_norm(y, g, b):
    mu = jnp.mean(y, axis=-1, keepdims=True)
    yc = y - mu
    var = jnp.mean(yc * yc, axis=-1, keepdims=True)
    return yc * lax.rsqrt(var + EPS) * g + b


def _index_map(width, pieces):
    idx = np.full((width,), -1, np.int32)
    for dst, src, w in pieces:
        idx[dst:dst + w] = np.arange(src, src + w)
    return idx


def _gather_pad(arr, idx, axis):
    parts = []
    i = 0
    n = idx.shape[0]
    while i < n:
        j = i
        if idx[i] < 0:
            while j < n and idx[j] < 0:
                j += 1
            shape = list(arr.shape)
            shape[axis] = j - i
            parts.append(jnp.zeros(shape, arr.dtype))
        else:
            while j + 1 < n and idx[j + 1] == idx[j] + 1:
                j += 1
            j += 1
            parts.append(lax.slice_in_dim(arr, int(idx[i]), int(idx[i]) + j - i, axis=axis))
        i = j
    return jnp.concatenate(parts, axis=axis)


def _heads(dst0, src0, n):
    return [(dst0 + HB * h, src0 + 64 * h, 64) for h in range(n)]


_IDX_W_IN = _index_map(W_IN_P, [
    (C_ACQ, 0, Q_RANK), (C_ACKV, 192, KV_RANK), (C_AKR, 320, ROPE_A),
    *_heads(C_BZ, 352, 4),
    *_heads(C_BXBC, 608, 4), *_heads(C_BXBC + 512, 864, 2), *_heads(C_BXBC + 768, 992, 2),
    (C_BDT, 1120, 2 * H_B),
    (C_CQ, 1128, 256), (C_CF, 1384, 512), (C_CI, 1896, 256), (C_CG, 2152, 256),
    *_heads(C_DQ, 2408, 4), *_heads(C_DK, 2664, 2), *_heads(C_DV, 2792, 2)])
_IDX_UQ_ROWS = _index_map(256, [(0, 0, Q_RANK)])
_IDX_UQ_COLS = _index_map(4 * HB, [(HB * h, 96 * h, 96) for h in range(H_A)])
_IDX_UKV_K = _index_map(4 * HB, [(HB * h, 128 * h, NOPE_A) for h in range(H_A)])
_IDX_UKV_V = _index_map(4 * V_A, [(V_A * h, 128 * h + NOPE_A, V_A) for h in range(H_A)])
_IDX_CONV = _index_map(W_XBC_P, [*_heads(0, 0, 4), *_heads(512, 256, 2), *_heads(768, 384, 2)])
_IDX_HEAD4 = _index_map(4 * HB, _heads(0, 0, 4))


def _rope_tables(half, lane0):
    t = np.arange(DEC_SEQ)
    pos = np.stack([t // GRID_W, t % GRID_W], 0).astype(np.float64)
    inv = ROPE_BASE ** (-np.arange(half, dtype=np.float64) / half)
    cos = np.ones((DEC_SEQ, LANE))
    sin = np.zeros((DEC_SEQ, LANE))
    for axis in range(2):
        ang = pos[axis][:, None] * inv[None, :]
        base = lane0 + axis * 2 * half
        cos[:, base:base + half] = np.cos(ang)
        cos[:, base + half:base + 2 * half] = np.cos(ang)
        sin[:, base:base + half] = -np.sin(ang)
        sin[:, base + half:base + 2 * half] = np.sin(ang)
    ident_c = np.ones((TM, LANE))
    ident_s = np.zeros((TM, LANE))
    return (jnp.asarray(np.concatenate([ident_c, cos], 0), F32),
            jnp.asarray(np.concatenate([ident_s, sin], 0), F32))


def _rope(x, cos, sin, first, half):
    partner = jnp.where(first, pltpu.roll(x, LANE - half, 1), pltpu.roll(x, half, 1))
    return x * cos + partner * sin


def _mod_kernel(c_ref, w_ref, b_ref, o_ref):
    c = c_ref[...]
    s = _silu(c).astype(BF16)
    o_ref[0] = _dot(s, w_ref[0].astype(BF16)) + b_ref[0]


def _modulation(cvec, w_mod, b_mod):
    return pl.pallas_call(
        _mod_kernel,
        grid=(DEPTH, N_MOD // MOD_TN),
        in_specs=[pl.BlockSpec((SUBLANE, D_MODEL), lambda l, j: (0, 0)),
                  pl.BlockSpec((1, D_MODEL, MOD_TN), lambda l, j: (l, 0, j)),
                  pl.BlockSpec((1, 1, MOD_TN), lambda l, j: (l, 0, j))],
        out_specs=pl.BlockSpec((1, SUBLANE, MOD_TN), lambda l, j: (l, 0, j)),
        out_shape=jax.ShapeDtypeStruct((DEPTH, SUBLANE, N_MOD), F32),
        compiler_params=_params("arbitrary", "arbitrary"),
        name="modulation",
    )(cvec, w_mod, b_mod.reshape(DEPTH, 1, N_MOD))


def _mod_spec(seq_len):
    return pl.BlockSpec((1, N_SUB * 3, D_MODEL), lambda i: (i * TM // seq_len, 0, 0))


def _ffn_kernel(x_ref, mod_ref, wg_ref, wu_ref, wd_ref, g_ref, b_ref, o_ref, *, sub):
    x = x_ref[...]
    shift = mod_ref[0, 3 * sub:3 * sub + 1, :]
    scale = mod_ref[0, 3 * sub + 1:3 * sub + 2, :]
    gate = mod_ref[0, 3 * sub + 2:3 * sub + 3, :]
    h = (x * (1.0 + scale) + shift).astype(BF16)
    acc = jnp.zeros((TM, D_MODEL), F32)
    for j in range(D_FF // FF_CHUNK):
        cols = slice(j * FF_CHUNK, (j + 1) * FF_CHUNK)
        gt = _dot(h, wg_ref[:, cols])
        up = _dot(h, wu_ref[:, cols])
        acc = acc + _dot((_silu(gt) * up).astype(BF16), wd_ref[cols, :])
    y = ALPHA * x + 0.5 * gate * acc
    o_ref[...] = _layer_norm(y, g_ref[...], b_ref[...])


def _ffn(x, mod, seq_len, w_gu, wd, ln_g, ln_b, sub):
    n = x.shape[0]
    row = pl.BlockSpec((TM, D_MODEL), lambda i: (i, 0))
    return pl.pallas_call(
        functools.partial(_ffn_kernel, sub=sub),
        grid=(n // TM,),
        in_specs=[row, _mod_spec(seq_len), _resident((D_MODEL, D_FF), (0, 0)), _resident((D_MODEL, D_FF), (0, 1)),
                  _resident((D_FF, D_MODEL)), _resident((1, D_MODEL)), _resident((1, D_MODEL))],
        out_specs=row,
        out_shape=jax.ShapeDtypeStruct((n, D_MODEL), F32),
        compiler_params=_params("arbitrary"),
        name="ffn",
    )(x, mod, w_gu, w_gu, wd, ln_g, ln_b)


def _in_kernel(x_ref, mod_ref, w_ref, wuq_ref, wk_ref, wv_ref, gq_ref, gkv_ref, dtb_ref,
               cosq_ref, sinq_ref, cosd_ref, sind_ref, *out_refs, latent):
    out = dict(zip([name for name, _, _ in _in_outputs(latent)], out_refs))
    qa_ref, ka_ref, va_ref = out["qa"], out["ka"], out["va"]
    bz_ref, bxbc_ref, bdt_ref = out["bz"], out["bxbc"], out["bdt"]
    cq_ref, cf_ref, ci_ref, cg_ref = out["cq"], out["cf"], out["ci"], out["cg"]
    dq_ref, dk_ref, dv_ref = out["dq"], out["dk"], out["dv"]
    x = x_ref[...]
    h = (x * (1.0 + mod_ref[0, 4:5, :]) + mod_ref[0, 3:4, :]).astype(BF16)

    def proj(start, width):
        return _dot(h, w_ref[:, start:start + width])

    lane = lax.broadcasted_iota(jnp.int32, (TM, LANE), 1)
    first_a = (lane % 16) < 8
    first_d = (lane % 32) < 16

    def rope_a(blk):
        return _rope(blk, cosq_ref[...], sinq_ref[...], first_a, 8) if latent else blk

    def rope_d(blk):
        return _rope(blk, cosd_ref[...], sind_ref[...], first_d, 16) if latent else blk

    acq = proj(C_ACQ, 256)
    ms = jnp.sum(acq * acq, axis=-1, keepdims=True) * (1.0 / Q_RANK)
    qn = (acq * lax.rsqrt(ms + EPS) * gq_ref[...]).astype(BF16)
    q = _dot(qn, wuq_ref[...])
    scale_a = (NOPE_A + ROPE_A) ** -0.5
    for hh in range(H_A):
        blk = slice(HB * hh, HB * (hh + 1))
        qa_ref[:, blk] = (rope_a(q[:, blk]) * scale_a).astype(BF16)
    ackv = proj(C_ACKV, KV_RANK)
    ms = jnp.mean(ackv * ackv, axis=-1, keepdims=True)
    ckv = ackv * lax.rsqrt(ms + EPS) * gkv_ref[...]
    ckv_b = ckv.astype(BF16)
    kk = _dot(ckv_b, wk_ref[...])
    akr = proj(C_AKR, LANE)
    if not latent:
        out["ckv"][...] = ckv
        out["kr"][...] = akr
    krp = rope_a(pltpu.roll(akr, NOPE_A, 1))
    for hh in range(H_A):
        blk = slice(HB * hh, HB * (hh + 1))
        ka_ref[:, blk] = (kk[:, blk] + krp).astype(BF16)
    va_ref[...] = _dot(ckv_b, wv_ref[...]).astype(BF16)

    bz_ref[...] = proj(C_BZ, 4 * HB)
    bxbc_ref[...] = proj(C_BXBC, W_XBC_P)
    dtr = proj(C_BDT, LANE) + dtb_ref[...]
    bdt_ref[...] = jnp.maximum(dtr, 0.0) + jnp.log(1.0 + jnp.exp(-jnp.abs(dtr)))

    cq_ref[...] = proj(C_CQ, 256)
    cf_ref[...] = proj(C_CF, 512)
    ci_ref[...] = proj(C_CI, 256)
    cg_ref[...] = proj(C_CG, 256)

    dq = proj(C_DQ, 4 * HB)
    scale_d = HD_D ** -0.5
    for hh in range(H_D):
        blk = slice(HB * hh, HB * (hh + 1))
        dq_ref[:, blk] = (rope_d(dq[:, blk]) * scale_d).astype(BF16)
    dk = proj(C_DK, 2 * HB)
    for hh in range(KV_D):
        blk = slice(HB * hh, HB * (hh + 1))
        dk_ref[:, blk] = rope_d(dk[:, blk])
    dv = proj(C_DV, 2 * HB)
    dv_ref[...] = dv
    if not latent:
        low = lane < HD_D
        out["dkc"][...] = jnp.where(low, dk[:, 0:HB], pltpu.roll(dk[:, HB:2 * HB], HD_D, 1))
        out["dvc"][...] = jnp.where(low, dv[:, 0:HB], pltpu.roll(dv[:, HB:2 * HB], HD_D, 1))


def _in_outputs(latent):
    outs = [("qa", 512, BF16), ("ka", 512, BF16), ("va", 256, BF16),
            ("bz", 512, F32), ("bxbc", W_XBC_P, F32), ("bdt", 128, F32),
            ("cq", 256, F32), ("cf", 512, F32), ("ci", 256, F32), ("cg", 256, F32),
            ("dq", 512, BF16), ("dk", 256, F32), ("dv", 256, F32)]
    if not latent:
        outs += [("ckv", KV_RANK, F32), ("kr", LANE, F32), ("dkc", KV_D * HD_D, F32), ("dvc", KV_D * HD_D, F32)]
    return outs


def _in_proj(x, mod, group_len, latent, wp, tabs):
    n = x.shape[0]
    row = lambda w: pl.BlockSpec((TM, w), lambda i: (i, 0))
    tab = pl.BlockSpec((TM, LANE), (lambda i: (1 + i % (DEC_SEQ // TM), 0)) if latent else (lambda i: (0, 0)))
    outs = pl.pallas_call(
        functools.partial(_in_kernel, latent=latent),
        grid=(n // TM,),
        in_specs=[row(D_MODEL), _mod_spec(group_len), _resident((D_MODEL, W_IN_P)), _resident((256, 512)),
                  _resident((KV_RANK, 512)), _resident((KV_RANK, 256)), _resident((1, 256)),
                  _resident((1, KV_RANK)), _resident((1, LANE)), tab, tab, tab, tab],
        out_specs=[row(w) for _, w, _ in _in_outputs(latent)],
        out_shape=[jax.ShapeDtypeStruct((n, w), dt) for _, w, dt in _in_outputs(latent)],
        compiler_params=_params("arbitrary"),
        name="in_proj",
    )(x, mod, wp["w_in"], wp["w_uq"], wp["w_uk"], wp["w_uv"], wp["g_q"], wp["g_kv"], wp["dt_bias"],
      tabs[0], tabs[1], tabs[2], tabs[3])
    return dict(zip([k for k, _, _ in _in_outputs(latent)], outs))


def _mla_cache_kernel(ckv_ref, krp_ref, wk_ref, wv_ref, kc_ref, vc_ref):
    ckv_b = ckv_ref[0].astype(BF16)
    kk = _dot(ckv_b, wk_ref[...])
    krp = krp_ref[0]
    for hh in range(H_A):
        blk = slice(HB * hh, HB * (hh + 1))
        kc_ref[0, :, blk] = (kk[:, blk] + krp).astype(BF16)
    vc_ref[0] = _dot(ckv_b, wv_ref[...]).astype(BF16)


def _mla_cache(ckv, krope_placed, wp):
    nb = ckv.shape[0]
    return pl.pallas_call(
        _mla_cache_kernel,
        grid=(nb,),
        in_specs=[pl.BlockSpec((1, PAST_LEN, KV_RANK), lambda b: (b, 0, 0)),
                  pl.BlockSpec((1, PAST_LEN, LANE), lambda b: (b, 0, 0)),
                  _resident((KV_RANK, 512)), _resident((KV_RANK, 256))],
        out_specs=[pl.BlockSpec((1, PAST_LEN, 512), lambda b: (b, 0, 0)),
                   pl.BlockSpec((1, PAST_LEN, 256), lambda b: (b, 0, 0))],
        out_shape=[jax.ShapeDtypeStruct((nb, PAST_LEN, 512), BF16),
                   jax.ShapeDtypeStruct((nb, PAST_LEN, 256), BF16)],
        compiler_params=_params("arbitrary"),
        name="mla_cache",
    )(ckv, krope_placed, wp["w_uk"], wp["w_uv"])


def _mla_kernel(*refs, has_cache):
    if has_cache:
        q_ref, k_ref, v_ref, kc_ref, vc_ref, o_ref = refs
    else:
        q_ref, k_ref, v_ref, o_ref = refs
    v = v_ref[...]
    head_of_lane = lax.broadcasted_iota(jnp.int32, (1, H_A * V_A), 1) // V_A
    acc = jnp.zeros((TQ_A, H_A * V_A), F32)
    for hh in range(H_A):
        blk = slice(HB * hh, HB * (hh + 1))
        qh = q_ref[:, blk]
        s = _dot_nt(qh, k_ref[:, blk])
        m = jnp.max(s, axis=-1, keepdims=True)
        if has_cache:
            sc = _dot_nt(qh, kc_ref[0, :, blk])
            m = jnp.maximum(m, jnp.max(sc, axis=-1, keepdims=True))
        e = jnp.exp(s - m)
        den = jnp.sum(e, axis=-1, keepdims=True)
        pv = _dot(e.astype(BF16), v)
        if has_cache:
            ec = jnp.exp(sc - m)
            den = den + jnp.sum(ec, axis=-1, keepdims=True)
            pv = pv + _dot(ec.astype(BF16), vc_ref[0])
        acc = jnp.where(head_of_lane == hh, pv / den, acc)
    o_ref[...] = acc


def _mla(q, k, v, n_seq, seq_len, cache=None):
    nq = seq_len // TQ_A
    in_specs = [pl.BlockSpec((TQ_A, 512), lambda b, i: (b * nq + i, 0)),
                pl.BlockSpec((seq_len, 512), lambda b, i: (b, 0)),
                pl.BlockSpec((seq_len, 256), lambda b, i: (b, 0))]
    args = [q, k, v]
    if cache is not None:
        in_specs += [pl.BlockSpec((1, PAST_LEN, 512), lambda b, i: (b, 0, 0)),
                     pl.BlockSpec((1, PAST_LEN, 256), lambda b, i: (b, 0, 0))]
        args += list(cache)
    return pl.pallas_call(
        functools.partial(_mla_kernel, has_cache=cache is not None),
        grid=(n_seq, nq),
        in_specs=in_specs,
        out_specs=pl.BlockSpec((TQ_A, H_A * V_A), lambda b, i: (b * nq + i, 0)),
        out_shape=jax.ShapeDtypeStruct((n_seq * seq_len, H_A * V_A), F32),
        compiler_params=_params("arbitrary", "arbitrary"),
        name="mla_attention",
    )(*args)


def _sink_softmax_pv(parts, sink):
    m = sink
    for s, _ in parts:
        m = jnp.maximum(m, jnp.max(s, axis=-1, keepdims=True))
    den = jnp.exp(sink - m)
    pv = None
    for s, v in parts:
        e = jnp.exp(s - m)
        den = den + jnp.sum(e, axis=-1, keepdims=True)
        t = _dot(e.astype(BF16), v)
        pv = t if pv is None else pv + t
    return pv / den


def _gqa_ctx_kernel(q_ref, k_ref, v_ref, sink_ref, o_ref):
    for hq in range(H_D):
        kv = slice(HB * (hq // G_D), HB * (hq // G_D + 1))
        blk = slice(HB * hq, HB * (hq + 1))
        s = _dot_nt(q_ref[:, blk], k_ref[:, kv].astype(BF16))
        o_ref[:, blk] = _sink_softmax_pv([(s, v_ref[:, kv].astype(BF16))], sink_ref[hq:hq + 1, 0:1])


def _gqa_ctx(q, k, v, sink, n_seq):
    seq = lambda w: pl.BlockSpec((SEQ, w), lambda b: (b, 0))
    return pl.pallas_call(
        _gqa_ctx_kernel,
        grid=(n_seq,),
        in_specs=[seq(512), seq(256), seq(256), _resident((H_D, LANE))],
        out_specs=seq(512),
        out_shape=jax.ShapeDtypeStruct((n_seq * SEQ, 512), F32),
        compiler_params=_params("arbitrary"),
        name="gqa_context",
    )(q, k, v, sink)


def _gqa_lat_kernel(q_ref, k_ref, v_ref, kc_ref, vc_ref, sink_ref, o_ref):
    n = pl.program_id(1)
    span = 3 * WINDOW
    start = pl.multiple_of(jnp.clip((n - 1) * WINDOW, 0, DEC_SEQ - span), WINDOW)
    qpos = n * WINDOW + lax.broadcasted_iota(jnp.int32, (WINDOW, span), 0)
    kpos = start + lax.broadcasted_iota(jnp.int32, (WINDOW, span), 1)
    band = jnp.abs(kpos - qpos) <= WINDOW
    for hq in range(H_D):
        kv = slice(HB * (hq // G_D), HB * (hq // G_D + 1))
        blk = slice(HB * hq, HB * (hq + 1))
        qh = q_ref[:, blk]
        kw = k_ref[pl.ds(start, span), kv].astype(BF16)
        vw = v_ref[pl.ds(start, span), kv].astype(BF16)
        s_lat = jnp.where(band, _dot_nt(qh, kw), NEG)
        s_ctx = _dot_nt(qh, kc_ref[0, :, kv])
        o_ref[:, blk] = _sink_softmax_pv([(s_lat, vw), (s_ctx, vc_ref[0, :, kv])], sink_ref[hq:hq + 1, 0:1])


def _gqa_lat(q, k, v, kc, vc, sink, n_seq):
    nb = DEC_SEQ // WINDOW
    return pl.pallas_call(
        _gqa_lat_kernel,
        grid=(n_seq, nb),
        in_specs=[pl.BlockSpec((WINDOW, 512), lambda b, n: (b * nb + n, 0)),
                  pl.BlockSpec((DEC_SEQ, 256), lambda b, n: (b, 0)),
                  pl.BlockSpec((DEC_SEQ, 256), lambda b, n: (b, 0)),
                  pl.BlockSpec((1, PAST_LEN, 256), lambda b, n: (b, 0, 0)),
                  pl.BlockSpec((1, PAST_LEN, 256), lambda b, n: (b, 0, 0)),
                  _resident((H_D, LANE))],
        out_specs=pl.BlockSpec((WINDOW, 512), lambda b, n: (b * nb + n, 0)),
        out_shape=jax.ShapeDtypeStruct((n_seq * DEC_SEQ, 512), F32),
        compiler_params=_params("arbitrary", "arbitrary"),
        name="gqa_latent",
    )(q, k, v, kc, vc, sink)


def _conv_kernel(cur_ref, prev_ref, next_ref, w_ref, b_ref, o_ref, pad_ref, *, tiles_per_seq):
    i = pl.program_id(0)
    has_prev = (i % tiles_per_seq) != 0
    has_next = (i % tiles_per_seq) != tiles_per_seq - 1
    pad_ref[0:SUBLANE, :] = jnp.where(has_prev, prev_ref[...], 0.0)
    pad_ref[SUBLANE:SUBLANE + CONV_TILE, :] = cur_ref[...]
    pad_ref[SUBLANE + CONV_TILE:, :] = jnp.where(has_next, next_ref[...], 0.0)
    y = jnp.zeros((CONV_TILE, W_XBC_P), F32) + b_ref[...]
    for k in range(D_CONV):
        off = SUBLANE - D_CONV // 2 + k
        y = y + w_ref[k:k + 1, :] * pad_ref[off:off + CONV_TILE, :]
    o_ref[...] = _silu(y)


def _conv(xbc, seq_len, w, b):
    n = xbc.shape[0]
    per = CONV_TILE // SUBLANE
    last = n // SUBLANE - 1
    return pl.pallas_call(
        functools.partial(_conv_kernel, tiles_per_seq=seq_len // CONV_TILE),
        grid=(n // CONV_TILE,),
        in_specs=[pl.BlockSpec((CONV_TILE, W_XBC_P), lambda i: (i, 0)),
                  pl.BlockSpec((SUBLANE, W_XBC_P), lambda i: (jnp.maximum(i * per - 1, 0), 0)),
                  pl.BlockSpec((SUBLANE, W_XBC_P), lambda i: (jnp.minimum((i + 1) * per, last), 0)),
                  _resident((SUBLANE, W_XBC_P)), _resident((1, W_XBC_P))],
        out_specs=pl.BlockSpec((CONV_TILE, W_XBC_P), lambda i: (i, 0)),
        out_shape=jax.ShapeDtypeStruct((n, W_XBC_P), F32),
        scratch_shapes=[pltpu.VMEM((CONV_TILE + 2 * SUBLANE, W_XBC_P), F32)],
        compiler_params=_params("arbitrary"),
        name="ssd_conv",
    )(xbc, xbc, xbc, w, b)


def _ssd_kernel(*refs, has_s0):
    if has_s0:
        xf_ref, xb_ref, dtf_ref, dtb_ref, alog_ref, dsk_ref, s0_ref, yf_ref, yb_ref, st_ref, s_scr = refs
    else:
        xf_ref, xb_ref, dtf_ref, dtb_ref, alog_ref, dsk_ref, yf_ref, yb_ref, st_ref, s_scr = refs
    c = pl.program_id(1)
    q = SSD_CHUNK

    @pl.when(c == 0)
    def _():
        s_scr[...] = s0_ref[0] if has_s0 else jnp.zeros(s_scr.shape, F32)

    row = lax.broadcasted_iota(jnp.int32, (q, q), 0)
    col = lax.broadcasted_iota(jnp.int32, (q, q), 1)
    a_coef = -jnp.exp(alog_ref[...])
    for d, (x_ref, dt_ref, y_ref) in enumerate(((xf_ref, dtf_ref, yf_ref), (xb_ref, dtb_ref, yb_ref))):
        tri = (row >= col) if d == 0 else (row <= col)
        dt = dt_ref[...]
        cum = _prefix_dot(tri, dt * a_coef)
        cum_t = cum.T
        total = cum[q - 1:q, :] if d == 0 else cum[0:1, :]
        for g in range(G_B):
            bg = x_ref[:, 512 + HB * g:512 + HB * (g + 1)]
            cg = x_ref[:, 768 + HB * g:768 + HB * (g + 1)]
            cb = _dot_nt(cg.astype(BF16), bg.astype(BF16))
            for hh in range(g * (H_B // G_B), (g + 1) * (H_B // G_B)):
                k = d * H_B + hh
                cum_c = cum[:, k:k + 1]
                diff = cum_c - cum_t[k:k + 1, :]
                seg = jnp.where(tri, jnp.exp(jnp.where(tri, diff, 0.0)), 0.0)
                xh = x_ref[:, HB * hh:HB * (hh + 1)]
                xdt = xh * dt[:, k:k + 1]
                xdt_b = xdt.astype(BF16)
                s_in = s_scr[d, hh]
                y = _dot((cb * seg).astype(BF16), xdt_b)
                y = y + _dot_nt((cg * jnp.exp(cum_c)).astype(BF16), s_in.astype(BF16))
                y_ref[:, HB * hh:HB * (hh + 1)] = y + xh * dsk_ref[k:k + 1, :]
                tot = total[:, k:k + 1]
                bdec = bg * jnp.exp(tot - cum_c)
                cs = _dot(xdt.T.astype(BF16), bdec.astype(BF16))
                s_scr[d, hh] = jnp.exp(tot) * s_in + cs

    @pl.when(c == pl.num_programs(1) - 1)
    def _():
        for d in range(2):
            for hh in range(H_B):
                st_ref[0, d, hh] = s_scr[d, hh, 0:P_B, 0:N_B]


def _ssd(xbc, dt, n_seq, seq_len, a_log, dskip, s0=None):
    nc = seq_len // SSD_CHUNK
    fwd = lambda w: pl.BlockSpec((SSD_CHUNK, w), lambda b, c: (b * nc + c, 0))
    bwd = lambda w: pl.BlockSpec((SSD_CHUNK, w), lambda b, c: (b * nc + nc - 1 - c, 0))
    state = pl.BlockSpec((1, 2, H_B, HB, HB), lambda b, c: (b, 0, 0, 0, 0))
    in_specs = [fwd(W_XBC_P), bwd(W_XBC_P), fwd(LANE), bwd(LANE), _resident((1, LANE)), _resident((2 * H_B, LANE))]
    args = [xbc, xbc, dt, dt, a_log, dskip]
    if s0 is not None:
        in_specs.append(state)
        args.append(s0)
    n = n_seq * seq_len
    return pl.pallas_call(
        functools.partial(_ssd_kernel, has_s0=s0 is not None),
        grid=(n_seq, nc),
        in_specs=in_specs,
        out_specs=[fwd(4 * HB), bwd(4 * HB), pl.BlockSpec((1, 2, H_B, P_B, N_B), lambda b, c: (b, 0, 0, 0, 0))],
        out_shape=[jax.ShapeDtypeStruct((n, 4 * HB), F32), jax.ShapeDtypeStruct((n, 4 * HB), F32),
                   jax.ShapeDtypeStruct((n_seq, 2, H_B, P_B, N_B), F32)],
        scratch_shapes=[pltpu.VMEM((2, H_B, HB, HB), F32)],
        compiler_params=_params("arbitrary", "arbitrary"),
        name="ssd_scan",
    )(*args)


def _gla_kernel(*refs, has_s0):
    if has_s0:
        (qf_ref, qb_ref, ff_ref, fb_ref, vf_ref, vb_ref, lb_ref, s0_ref, of_ref, ob_ref, st_ref, s_scr,
         o_scr) = refs
    else:
        (qf_ref, qb_ref, ff_ref, fb_ref, vf_ref, vb_ref, lb_ref, of_ref, ob_ref, st_ref, s_scr, o_scr) = refs
    c = pl.program_id(1)
    t = GLA_TILE
    ch = HGRN_CHUNK
    nch = t // ch
    w = H_C * K_C

    @pl.when(c == 0)
    def _():
        s_scr[...] = s0_ref[0] if has_s0 else jnp.zeros(s_scr.shape, F32)

    row = lax.broadcasted_iota(jnp.int32, (t, t), 0)
    col = lax.broadcasted_iota(jnp.int32, (t, t), 1)
    same_chunk = (row // ch) == (col // ch)
    head_of_lane = lax.broadcasted_iota(jnp.int32, (1, w), 1) // K_C
    tok_row = lax.broadcasted_iota(jnp.int32, (t, 1), 0)
    zero_b = jnp.zeros((), BF16)
    for d, (q_ref, f_ref, v_ref, o_ref) in enumerate(((qf_ref, ff_ref, vf_ref, of_ref),
                                                      (qb_ref, fb_ref, vb_ref, ob_ref))):
        qv = q_ref[...]
        fr = f_ref[...]
        vv = v_ref[...]
        lb = lb_ref[d:d + 1, :]
        f = lb + (1.0 - lb) * jax.nn.sigmoid(fr)
        log_f = jnp.log(jnp.maximum(f, F_MIN))
        key = (1.0 - lb) * jax.nn.sigmoid(-fr)
        tri = same_chunk & ((col <= row) if d == 0 else (col >= row))
        cum = _prefix_dot(tri, log_f)
        cum3 = cum.reshape(nch, ch, w)
        k3 = key.reshape(nch, ch, w)
        q_dec = (qv * jnp.exp(cum)).astype(BF16)
        q_heads = [jnp.where(head_of_lane == hh, q_dec, zero_b) for hh in range(H_C)]
        v_b = vv.astype(BF16)

        k_inv = (key * jnp.exp(-cum)).astype(BF16)
        o = jnp.zeros((t, w), F32)
        for hh in range(H_C):
            att = jnp.where(tri, _dot_nt(q_heads[hh], k_inv), 0.0).astype(BF16)
            o = jnp.where(head_of_lane == hh, _dot(att, v_b), o)
        o_scr[...] = o

        @pl.when(jnp.min(cum) < -GLA_SAFE_LOG_DECAY)
        def _():
            head_ones = ((row // K_C) == (col // K_C)).astype(F32)
            i_in_chunk = lax.broadcasted_iota(jnp.int32, (nch, ch, w), 1)
            q3 = qv.reshape(nch, ch, w)
            v3 = vv.reshape(nch, ch, w)
            o3 = jnp.zeros((nch, ch, w), F32)
            for j in range(ch):
                live = (i_in_chunk >= j) if d == 0 else (i_in_chunk <= j)
                e = jnp.exp(jnp.where(live, cum3 - cum3[:, j:j + 1, :], 0.0))
                term = jnp.where(live, q3 * e * k3[:, j:j + 1, :], 0.0)
                att = _dot(term.reshape(t, w), head_ones)
                o3 = o3 + att.reshape(nch, ch, w) * v3[:, j:j + 1, :]
            o_scr[...] = o3.reshape(t, w)

        edge = ch - 1 if d == 0 else 0
        last3 = jnp.broadcast_to(cum3[:, edge:edge + 1, :], (nch, ch, w))
        k_dec = (k3 * jnp.exp(last3 - cum3)).reshape(t, w).astype(BF16)
        v_t = vv.T.astype(BF16)
        order = range(nch) if d == 0 else range(nch - 1, -1, -1)
        for cc in order:
            rows = slice(cc * ch, (cc + 1) * ch)
            st = s_scr[d]
            q4 = jnp.concatenate([qh[rows, :] for qh in q_heads], axis=0)
            r = _dot_nt(q4, st.astype(BF16))
            o_inter = r[0:ch, :]
            for hh in range(1, H_C):
                o_inter = jnp.where(head_of_lane == hh, r[hh * ch:(hh + 1) * ch, :], o_inter)
            o_ref[rows, :] = o_scr[rows, :] + o_inter
            in_chunk = (tok_row >= cc * ch) & (tok_row < (cc + 1) * ch)
            cs_t = _dot(v_t, jnp.where(in_chunk, k_dec, zero_b))
            decay = jnp.exp(cum[cc * ch + edge:cc * ch + edge + 1, :])
            s_scr[d] = decay * st + cs_t

    @pl.when(c == pl.num_programs(1) - 1)
    def _():
        k_idx = lax.broadcasted_iota(jnp.int32, (K_C, w), 0)
        lane_idx = lax.broadcasted_iota(jnp.int32, (K_C, w), 1)
        for d in range(2):
            for hh in range(H_C):
                sel = (lane_idx == hh * K_C + k_idx).astype(F32)
                rows = s_scr[d, hh * V_C:(hh + 1) * V_C, :]
                st_ref[0, d, hh] = lax.dot_general(sel, rows, (((1,), (1,)), ((), ())),
                                                   preferred_element_type=F32, precision=lax.Precision.HIGHEST)


def _gla(cq, cf, ci, n_seq, seq_len, lb, s0=None):
    nt = seq_len // GLA_TILE
    w = H_C * K_C
    fwd = lambda j: pl.BlockSpec((GLA_TILE, w), lambda b, c: (b * nt + c, j))
    bwd = lambda j: pl.BlockSpec((GLA_TILE, w), lambda b, c: (b * nt + nt - 1 - c, j))
    state = pl.BlockSpec((1, 2, w, w), lambda b, c: (b, 0, 0, 0))
    in_specs = [fwd(0), bwd(0), fwd(0), bwd(1), fwd(0), bwd(0), _resident((2, w))]
    args = [cq, cq, cf, cf, ci, ci, lb]
    if s0 is not None:
        in_specs.append(state)
        args.append(s0)
    n = n_seq * seq_len
    return pl.pallas_call(
        functools.partial(_gla_kernel, has_s0=s0 is not None),
        grid=(n_seq, nt),
        in_specs=in_specs,
        out_specs=[fwd(0), bwd(0), pl.BlockSpec((1, 2, H_C, K_C, V_C), lambda b, c: (b, 0, 0, 0, 0))],
        out_shape=[jax.ShapeDtypeStruct((n, w), F32), jax.ShapeDtypeStruct((n, w), F32),
                   jax.ShapeDtypeStruct((n_seq, 2, H_C, K_C, V_C), F32)],
        scratch_shapes=[pltpu.VMEM((2, w, w), F32), pltpu.VMEM((GLA_TILE, w), F32)],
        compiler_params=_params("arbitrary", "arbitrary"),
        name="hgrn_scan",
    )(*args)


def _out_kernel(x_ref, mod_ref, oa_ref, yf_ref, yb_ref, bz_ref, of_ref, ob_ref, cg_ref, od_ref,
                woa_ref, wob_ref, woc_ref, wod_ref, nb_ref, nc_ref, g_ref, b_ref, o_ref):
    x = x_ref[...]
    gate = mod_ref[0, 5:6, :]
    yb = (yf_ref[...] + yb_ref[...]) * _silu(bz_ref[...])
    parts = []
    for hh in range(H_B):
        blk = yb[:, HB * hh:HB * (hh + 1)]
        ms = jnp.sum(blk * blk, axis=-1, keepdims=True) * (1.0 / P_B)
        parts.append((blk * lax.rsqrt(ms + EPS), hh))
    u = _dot(oa_ref[...].astype(BF16), woa_ref[...])
    for blk, hh in parts:
        nb = nb_ref[:, HB * hh:HB * (hh + 1)]
        u = u + _dot((blk * nb).astype(BF16), wob_ref[HB * hh:HB * (hh + 1), :])
    oc = of_ref[...] + ob_ref[...]
    lane = lax.broadcasted_iota(jnp.int32, (1, LANE), 1)
    low = lane < V_C
    halves = []
    for t in range(H_C * V_C // LANE):
        blk = oc[:, LANE * t:LANE * (t + 1)]
        sq = blk * blk
        s_all = jnp.sum(sq, axis=-1, keepdims=True)
        s_low = jnp.sum(jnp.where(low, sq, 0.0), axis=-1, keepdims=True)
        ms = jnp.where(low, s_low, s_all - s_low) * (1.0 / V_C)
        halves.append(blk * lax.rsqrt(ms + EPS))
    ocn = jnp.concatenate(halves, axis=-1) * nc_ref[...] * _silu(cg_ref[...])
    u = u + _dot(ocn.astype(BF16), woc_ref[...])
    u = u + _dot(od_ref[...].astype(BF16), wod_ref[...])
    o_ref[...] = _layer_norm(ALPHA * x + gate * u, g_ref[...], b_ref[...])


def _out_proj(x, mod, seq_len, mix, wp, ln_g, ln_b):
    n = x.shape[0]
    row = lambda w: pl.BlockSpec((TM, w), lambda i: (i, 0))
    return pl.pallas_call(
        _out_kernel,
        grid=(n // TM,),
        in_specs=[row(D_MODEL), _mod_spec(seq_len), row(256), row(512), row(512), row(512), row(256), row(256),
                  row(256), row(512),
                  _resident((256, D_MODEL)), _resident((512, D_MODEL)), _resident((256, D_MODEL)),
                  _resident((512, D_MODEL)), _resident((1, 512)), _resident((1, 256)),
                  _resident((1, D_MODEL)), _resident((1, D_MODEL))],
        out_specs=row(D_MODEL),
        out_shape=jax.ShapeDtypeStruct((n, D_MODEL), F32),
        compiler_params=_params("arbitrary"),
        name="out_proj",
    )(x, mod, mix["oa"], mix["yf"], mix["yb"], mix["bz"], mix["of"], mix["ob"], mix["cg"], mix["od"],
      wp["w_oa"], wp["w_ob"], wp["w_oc"], wp["w_od"], wp["ssd_norm"], wp["hgrn_norm"], ln_g, ln_b)


def _prep_layer(l, w_in, w_out, mla_q_norm, mla_kv_norm, mla_w_uq, mla_w_ukv, ssd_conv_w, ssd_conv_b,
                ssd_a_log, ssd_dt_bias, ssd_d, ssd_norm, hgrn_lb, hgrn_norm, gqa_sink):
    wo = w_out[l]
    conv_w = _gather_pad(ssd_conv_w[l], _IDX_CONV, 1)
    return {
        "w_in": _gather_pad(w_in[l], _IDX_W_IN, 1).astype(BF16),
        "w_uq": _gather_pad(_gather_pad(mla_w_uq[l], _IDX_UQ_ROWS, 0), _IDX_UQ_COLS, 1).astype(BF16),
        "w_uk": _gather_pad(mla_w_ukv[l], _IDX_UKV_K, 1).astype(BF16),
        "w_uv": _gather_pad(mla_w_ukv[l], _IDX_UKV_V, 1).astype(BF16),
        "g_q": _gather_pad(mla_q_norm[l], _IDX_UQ_ROWS, 0).reshape(1, 256),
        "g_kv": mla_kv_norm[l].reshape(1, KV_RANK),
        "dt_bias": jnp.pad(ssd_dt_bias[l].reshape(1, 2 * H_B), ((0, 0), (0, LANE - 2 * H_B))),
        "conv_w": jnp.pad(conv_w, ((0, SUBLANE - D_CONV), (0, 0))),
        "conv_b": _gather_pad(ssd_conv_b[l], _IDX_CONV, 0).reshape(1, W_XBC_P),
        "a_log": jnp.pad(ssd_a_log[l].reshape(1, 2 * H_B), ((0, 0), (0, LANE - 2 * H_B))),
        "d_skip": jnp.broadcast_to(ssd_d[l].reshape(2 * H_B, 1), (2 * H_B, LANE)),
        "ssd_norm": _gather_pad(ssd_norm[l], _IDX_HEAD4, 0).reshape(1, 4 * HB),
        "hgrn_lb": hgrn_lb[l],
        "hgrn_norm": hgrn_norm[l].reshape(1, H_C * V_C),
        "sink": jnp.broadcast_to(gqa_sink[l].reshape(H_D, 1), (H_D, LANE)),
        "w_oa": wo[0:256].astype(BF16),
        "w_ob": _gather_pad(wo[256:512], _IDX_HEAD4, 0).astype(BF16),
        "w_oc": wo[512:768].astype(BF16),
        "w_od": _gather_pad(wo[768:1024], _IDX_HEAD4, 0).astype(BF16),
    }


def _pad_heads(t):
    return jnp.pad(t, [(0, 0)] * (t.ndim - 1) + [(0, HB - t.shape[-1])])


def _mixer(x, mod, group_len, wp, tabs, n_seq, seq_len, ctx):
    latent = ctx is not None
    p = _in_proj(x, mod, group_len, latent, wp, tabs)
    mix = {"bz": p["bz"], "cg": p["cg"]}
    cache = _mla_cache(ctx["ckv"], ctx["krope"], wp) if latent else None
    mix["oa"] = _mla(p["qa"], p["ka"], p["va"], n_seq, seq_len, cache)
    xbc = _conv(p["bxbc"], seq_len, wp["conv_w"], wp["conv_b"])
    mix["yf"], mix["yb"], st_b = _ssd(xbc, p["bdt"], n_seq, seq_len, wp["a_log"], wp["d_skip"],
                                      ctx["ssm"] if latent else None)
    mix["of"], mix["ob"], st_c = _gla(p["cq"], p["cf"], p["ci"], n_seq, seq_len, wp["hgrn_lb"],
                                      ctx["hgrn"] if latent else None)
    if latent:
        mix["od"] = _gqa_lat(p["dq"], p["dk"], p["dv"], ctx["dk"], ctx["dv"], wp["sink"], n_seq)
    else:
        mix["od"] = _gqa_ctx(p["dq"], p["dk"], p["dv"], wp["sink"], n_seq)
    state = None if latent else (p["ckv"], p["kr"], st_b, st_c, p["dkc"], p["dvc"])
    return mix, state


def _run_stream(x, mod, n_seq, seq_len, ctx, wp, ffn_w, lng, lnb, tabs):
    group_len = x.shape[0] if ctx is None else seq_len
    x = _ffn(x, mod, group_len, *ffn_w[0], lng[0], lnb[0], sub=0)
    mix, st = _mixer(x, mod, group_len, wp, tabs, n_seq, seq_len, ctx)
    x = _out_proj(x, mod, group_len, mix, wp, lng[1], lnb[1])
    x = _ffn(x, mod, group_len, *ffn_w[1], lng[2], lnb[2], sub=2)
    return x, st


def _layer_inputs(l, ctx_tensors, weights, hgrn_lb):
    (cache_a_ckv, cache_a_krope, state_b_ssm, state_c_hgrn, cache_d_k, cache_d_v) = ctx_tensors
    (ln_g, ln_b, ffn_w_gu, ffn_w_down, w_in, w_out, mla_q_norm, mla_kv_norm, mla_w_uq, mla_w_ukv, ssd_conv_w,
     ssd_conv_b, ssd_a_log, ssd_dt_bias, ssd_d, ssd_norm, hgrn_norm, gqa_sink) = weights
    wp = _prep_layer(l, w_in, w_out, mla_q_norm, mla_kv_norm, mla_w_uq, mla_w_ukv, ssd_conv_w, ssd_conv_b,
                     ssd_a_log, ssd_dt_bias, ssd_d, ssd_norm, hgrn_lb, hgrn_norm, gqa_sink)
    ffn_w = [(ffn_w_gu[l, s].astype(BF16), ffn_w_down[l, s].astype(BF16)) for s in range(2)]
    lng = [ln_g[l, s].reshape(1, D_MODEL) for s in range(N_SUB)]
    lnb = [ln_b[l, s].reshape(1, D_MODEL) for s in range(N_SUB)]
    nb = cache_a_ckv.shape[0]
    ctx = {
        "ckv": cache_a_ckv[:, l],
        "krope": jnp.pad(cache_a_krope[:, l], ((0, 0), (0, 0), (NOPE_A, LANE - NOPE_A - ROPE_A))),
        "ssm": jnp.pad(state_b_ssm[:, l], ((0, 0),) * 3 + ((0, HB - P_B), (0, HB - N_B))),
        "hgrn": jnp.einsum("bdhkv,hg->bdhvgk", state_c_hgrn[:, l], jnp.eye(H_C, dtype=F32)).reshape(
            nb, 2, H_C * V_C, H_C * K_C),
        "dk": _pad_heads(cache_d_k[:, l]).reshape(nb, PAST_LEN, KV_D * HB).astype(BF16),
        "dv": _pad_heads(cache_d_v[:, l]).reshape(nb, PAST_LEN, KV_D * HB).astype(BF16),
    }
    return wp, ffn_w, lng, lnb, ctx


def kernel(x_prompt, x_sample, cache_a_ckv, cache_a_krope, state_b_ssm, state_c_hgrn, cache_d_k, cache_d_v,
           c, c_ctx, w_mod, b_mod, ln_g, ln_b, ffn_w_gu, ffn_w_down, w_in, w_out, mla_q_norm, mla_kv_norm,
           mla_w_uq, mla_w_ukv, ssd_conv_w, ssd_conv_b, ssd_a_log, ssd_dt_bias, ssd_d, ssd_norm,
           hgrn_lb_logits, hgrn_norm, gqa_sink):
    lb_p = jax.nn.softmax(hgrn_lb_logits.astype(F32), axis=0)
    hgrn_lb = jnp.cumsum(lb_p, axis=0) - lb_p[:1]

    cvec = jnp.concatenate([c_ctx[None], c, jnp.zeros((SUBLANE - 1 - DEC_BATCH, D_MODEL), F32)], axis=0)
    mod_all = _modulation(cvec, w_mod, b_mod)
    tabs = _rope_tables(8, NOPE_A) + _rope_tables(16, 0)
    ctx_tensors = (cache_a_ckv, cache_a_krope, state_b_ssm, state_c_hgrn, cache_d_k, cache_d_v)
    weights = (ln_g, ln_b, ffn_w_gu, ffn_w_down, w_in, w_out, mla_q_norm, mla_kv_norm, mla_w_uq, mla_w_ukv,
               ssd_conv_w, ssd_conv_b, ssd_a_log, ssd_dt_bias, ssd_d, ssd_norm, hgrn_norm, gqa_sink)

    y_p = x_prompt.reshape(BATCH * SEQ, D_MODEL)
    y_s = x_sample.reshape(DEC_BATCH * DEC_SEQ, D_MODEL)
    states = []
    for l in range(DEPTH):
        wp, ffn_w, lng, lnb, ctx = _layer_inputs(l, ctx_tensors, weights, hgrn_lb)
        mod_ctx = mod_all[l, 0:1].reshape(1, N_SUB * 3, D_MODEL)
        mod_lat = mod_all[l, 1:1 + DEC_BATCH].reshape(DEC_BATCH, N_SUB * 3, D_MODEL)
        y_p, st = _run_stream(y_p, mod_ctx, BATCH, SEQ, None, wp, ffn_w, lng, lnb, tabs)
        y_s, _ = _run_stream(y_s, mod_lat, DEC_BATCH, DEC_SEQ, ctx, wp, ffn_w, lng, lnb, tabs)
        states.append(st)

    def stack(i, f):
        return jnp.stack([f(s[i]) for s in states], axis=1)

    new_a_ckv = stack(0, lambda t: t.reshape(BATCH, SEQ, KV_RANK))
    new_a_krope = stack(1, lambda t: t.reshape(BATCH, SEQ, LANE)[..., :ROPE_A])
    new_b_ssm = stack(2, lambda t: t)
    new_c_hgrn = stack(3, lambda t: t)
    new_d_k = stack(4, lambda t: t.reshape(BATCH, SEQ, KV_D, HD_D))
    new_d_v = stack(5, lambda t: t.reshape(BATCH, SEQ, KV_D, HD_D))
    return (y_p.reshape(BATCH, SEQ, D_MODEL), y_s.reshape(DEC_BATCH, DEC_SEQ, D_MODEL),
            new_a_ckv, new_a_krope, new_b_ssm, new_c_hgrn, new_d_k, new_d_v)
```

```python
import functools

import numpy as np
import jax
import jax.numpy as jnp
from jax import lax
from jax.experimental import pallas as pl
from jax.experimental.pallas import tpu as pltpu

F32 = jnp.float32
BF16 = jnp.bfloat16

D_MODEL = 1024
BATCH = 32
SEQ = 256
DEPTH = 2
DEC_BATCH = 2
DEC_SEQ = 4096
PAST_LEN = 512
GRID_W = 64
H_A, Q_RANK, KV_RANK, NOPE_A, ROPE_A, V_A = 4, 192, 128, 64, 32, 64
H_B, P_B, G_B, N_B, D_CONV, SSD_CHUNK = 4, 64, 2, 64, 5, 128
H_C, K_C, V_C, HGRN_CHUNK = 4, 64, 64, 16
H_D, KV_D, HD_D, WINDOW = 4, 2, 64, 128
G_D = H_D // KV_D
ROPE_BASE = 10000.0
D_FF = 2816
N_SUB = 3
ALPHA = (2 * DEPTH) ** 0.25
EPS = 1e-6
F_MIN = 1e-6
NEG = -1e30
D_IN = 2920
N_MOD = N_SUB * 3 * D_MODEL

LANE = 128
SUBLANE = 8
VMEM_LIMIT = 56 * 1024 * 1024

TM = 512
FF_CHUNK = 256
TQ_A = 256
GLA_TILE = 256
SSD_STEP_CHUNKS = 4
GQA_CTX_SEQS = 4
GQA_LAT_BLOCKS = 2
CONV_TILE = 256
MOD_TN = 1536
GLA_SAFE_LOG_DECAY = 60.0

C_ACQ, C_ACKV, C_AKR = 0, 256, 384
C_BZ, C_BXBC, C_BDT = 512, 1024, 2048
C_CQ, C_CF, C_CI, C_CG = 2176, 2432, 2944, 3200
C_DQ, C_DK, C_DV = 3456, 3968, 4224
W_IN_P = 4480
W_XBC_P = 1024
HB = 128


def _dot(a, b, precision=None):
    return jnp.dot(a, b, preferred_element_type=F32, precision=precision)


def _dot_nt(a, b):
    return lax.dot_general(a, b, (((1,), (1,)), ((), ())), preferred_element_type=F32)


def _prefix_dot(tri, x):
    t = tri.astype(BF16)
    hi = x.astype(BF16)
    rest = x - hi.astype(F32)
    mid = rest.astype(BF16)
    lo = (rest - mid.astype(F32)).astype(BF16)
    return _dot(t, hi) + _dot(t, mid) + _dot(t, lo)


def _params(*sem):
    return pltpu.CompilerParams(dimension_semantics=sem, vmem_limit_bytes=VMEM_LIMIT)


def _resident(shape, index=None):
    index = (0,) * len(shape) if index is None else index
    return pl.BlockSpec(shape, lambda *_: index, pipeline_mode=pl.Buffered(1))


def _silu(x):
    return x * jax.nn.sigmoid(x)


def _layer_norm(y, g, b):
    mu = jnp.mean(y, axis=-1, keepdims=True)
    yc = y - mu
    var = jnp.mean(yc * yc, axis=-1, keepdims=True)
    return yc * lax.rsqrt(var + EPS) * g + b


def _index_map(width, pieces):
    idx = np.full((width,), -1, np.int32)
    for dst, src, w in pieces:
        idx[dst:dst + w] = np.arange(src, src + w)
    return idx


def _gather_pad(arr, idx, axis):
    parts = []
    i = 0
    n = idx.shape[0]
    while i < n:
        j = i
        if idx[i] < 0:
            while j < n and idx[j] < 0:
                j += 1
            shape = list(arr.shape)
            shape[axis] = j - i
            parts.append(jnp.zeros(shape, arr.dtype))
        else:
            while j + 1 < n and idx[j + 1] == idx[j] + 1:
                j += 1
            j += 1
            parts.append(lax.slice_in_dim(arr, int(idx[i]), int(idx[i]) + j - i, axis=axis))
        i = j
    return jnp.concatenate(parts, axis=axis)


def _heads(dst0, src0, n):
    return [(dst0 + HB * h, src0 + 64 * h, 64) for h in range(n)]


_IDX_W_IN = _index_map(W_IN_P, [
    (C_ACQ, 0, Q_RANK), (C_ACKV, 192, KV_RANK), (C_AKR, 320, ROPE_A),
    *_heads(C_BZ, 352, 4),
    *_heads(C_BXBC, 608, 4), *_heads(C_BXBC + 512, 864, 2), *_heads(C_BXBC + 768, 992, 2),
    (C_BDT, 1120, 2 * H_B),
    (C_CQ, 1128, 256), (C_CF, 1384, 512), (C_CI, 1896, 256), (C_CG, 2152, 256),
    *_heads(C_DQ, 2408, 4), *_heads(C_DK, 2664, 2), *_heads(C_DV, 2792, 2)])
_IDX_UQ_ROWS = _index_map(256, [(0, 0, Q_RANK)])
_IDX_UQ_COLS = _index_map(4 * HB, [(HB * h, 96 * h, 96) for h in range(H_A)])
_IDX_UKV_K = _index_map(4 * HB, [(HB * h, 128 * h, NOPE_A) for h in range(H_A)])
_IDX_UKV_V = _index_map(4 * V_A, [(V_A * h, 128 * h + NOPE_A, V_A) for h in range(H_A)])
_IDX_CONV = _index_map(W_XBC_P, [*_heads(0, 0, 4), *_heads(512, 256, 2), *_heads(768, 384, 2)])
_IDX_HEAD4 = _index_map(4 * HB, _heads(0, 0, 4))


def _rope_tables(half, lane0):
    t = np.arange(DEC_SEQ)
    pos = np.stack([t // GRID_W, t % GRID_W], 0).astype(np.float64)
    inv = ROPE_BASE ** (-np.arange(half, dtype=np.float64) / half)
    cos = np.ones((DEC_SEQ, LANE))
    sin = np.zeros((DEC_SEQ, LANE))
    for axis in range(2):
        ang = pos[axis][:, None] * inv[None, :]
        base = lane0 + axis * 2 * half
        cos[:, base:base + half] = np.cos(ang)
        cos[:, base + half:base + 2 * half] = np.cos(ang)
        sin[:, base:base + half] = -np.sin(ang)
        sin[:, base + half:base + 2 * half] = np.sin(ang)
    ident_c = np.ones((TM, LANE))
    ident_s = np.zeros((TM, LANE))
    return (jnp.asarray(np.concatenate([ident_c, cos], 0), F32),
            jnp.asarray(np.concatenate([ident_s, sin], 0), F32))


def _rope(x, cos, sin, first, half):
    partner = jnp.where(first, pltpu.roll(x, LANE - half, 1), pltpu.roll(x, half, 1))
    return x * cos + partner * sin


def _mod_kernel(c_ref, w_ref, b_ref, o_ref):
    c = c_ref[...]
    s = _silu(c).astype(BF16)
    o_ref[0] = _dot(s, w_ref[0].astype(BF16)) + b_ref[0]


def _modulation(cvec, w_mod, b_mod):
    return pl.pallas_call(
        _mod_kernel,
        grid=(DEPTH, N_MOD // MOD_TN),
        in_specs=[pl.BlockSpec((SUBLANE, D_MODEL), lambda l, j: (0, 0)),
                  pl.BlockSpec((1, D_MODEL, MOD_TN), lambda l, j: (l, 0, j)),
                  pl.BlockSpec((1, 1, MOD_TN), lambda l, j: (l, 0, j))],
        out_specs=pl.BlockSpec((1, SUBLANE, MOD_TN), lambda l, j: (l, 0, j)),
        out_shape=jax.ShapeDtypeStruct((DEPTH, SUBLANE, N_MOD), F32),
        compiler_params=_params("arbitrary", "arbitrary"),
        name="modulation",
    )(cvec, w_mod, b_mod.reshape(DEPTH, 1, N_MOD))


def _mod_spec(seq_len):
    return pl.BlockSpec((1, N_SUB * 3, D_MODEL), lambda i: (i * TM // seq_len, 0, 0))


def _ffn_kernel(x_ref, mod_ref, wg_ref, wu_ref, wd_ref, g_ref, b_ref, o_ref, *, sub):
    x = x_ref[...]
    shift = mod_ref[0, 3 * sub:3 * sub + 1, :]
    scale = mod_ref[0, 3 * sub + 1:3 * sub + 2, :]
    gate = mod_ref[0, 3 * sub + 2:3 * sub + 3, :]
    h = (x * (1.0 + scale) + shift).astype(BF16)
    acc = jnp.zeros((TM, D_MODEL), F32)
    for j in range(D_FF // FF_CHUNK):
        cols = slice(j * FF_CHUNK, (j + 1) * FF_CHUNK)
        gt = _dot(h, wg_ref[:, cols])
        up = _dot(h, wu_ref[:, cols])
        acc = acc + _dot((_silu(gt) * up).astype(BF16), wd_ref[cols, :])
    y = ALPHA * x + 0.5 * gate * acc
    o_ref[...] = _layer_norm(y, g_ref[...], b_ref[...])


def _ffn(x, mod, seq_len, w_gu, w_down, layer_sub, ln_g, ln_b, sub):
    n = x.shape[0]
    l, s = layer_sub
    row = pl.BlockSpec((TM, D_MODEL), lambda i: (i, 0))
    return pl.pallas_call(
        functools.partial(_ffn_kernel, sub=sub),
        grid=(n // TM,),
        in_specs=[row, _mod_spec(seq_len),
                  _resident((None, None, D_MODEL, D_FF), (l, s, 0, 0)),
                  _resident((None, None, D_MODEL, D_FF), (l, s, 0, 1)),
                  _resident((None, None, D_FF, D_MODEL), (l, s, 0, 0)),
                  _resident((1, D_MODEL)), _resident((1, D_MODEL))],
        out_specs=row,
        out_shape=jax.ShapeDtypeStruct((n, D_MODEL), F32),
        compiler_params=_params("arbitrary"),
        name="ffn",
    )(x, mod, w_gu, w_gu, w_down, ln_g, ln_b)


def _in_kernel(x_ref, mod_ref, w_ref, wuq_ref, wk_ref, wv_ref, gq_ref, gkv_ref, dtb_ref,
               cosq_ref, sinq_ref, cosd_ref, sind_ref, *out_refs, latent):
    out = dict(zip([name for name, _, _ in _in_outputs(latent)], out_refs))
    qa_ref, ka_ref, va_ref = out["qa"], out["ka"], out["va"]
    bz_ref, bxbc_ref, bdt_ref = out["bz"], out["bxbc"], out["bdt"]
    cq_ref, cf_ref, ci_ref, cg_ref = out["cq"], out["cf"], out["ci"], out["cg"]
    dq_ref, dk_ref, dv_ref = out["dq"], out["dk"], out["dv"]
    x = x_ref[...]
    h = (x * (1.0 + mod_ref[0, 4:5, :]) + mod_ref[0, 3:4, :]).astype(BF16)

    def proj(start, width):
        return _dot(h, w_ref[:, start:start + width])

    lane = lax.broadcasted_iota(jnp.int32, (TM, LANE), 1)
    first_a = (lane % 16) < 8
    first_d = (lane % 32) < 16

    def rope_a(blk):
        return _rope(blk, cosq_ref[...], sinq_ref[...], first_a, 8) if latent else blk

    def rope_d(blk):
        return _rope(blk, cosd_ref[...], sind_ref[...], first_d, 16) if latent else blk

    acq = proj(C_ACQ, 256)
    ms = jnp.sum(acq * acq, axis=-1, keepdims=True) * (1.0 / Q_RANK)
    qn = (acq * lax.rsqrt(ms + EPS) * gq_ref[...]).astype(BF16)
    q = _dot(qn, wuq_ref[...])
    scale_a = (NOPE_A + ROPE_A) ** -0.5
    for hh in range(H_A):
        blk = slice(HB * hh, HB * (hh + 1))
        qa_ref[:, blk] = (rope_a(q[:, blk]) * scale_a).astype(BF16)
    ackv = proj(C_ACKV, KV_RANK)
    ms = jnp.mean(ackv * ackv, axis=-1, keepdims=True)
    ckv = ackv * lax.rsqrt(ms + EPS) * gkv_ref[...]
    ckv_b = ckv.astype(BF16)
    kk = _dot(ckv_b, wk_ref[...])
    akr = proj(C_AKR, LANE)
    if not latent:
        out["ckv"][...] = ckv
        out["kr"][...] = akr
    krp = rope_a(pltpu.roll(akr, NOPE_A, 1))
    for hh in range(H_A):
        blk = slice(HB * hh, HB * (hh + 1))
        ka_ref[:, blk] = (kk[:, blk] + krp).astype(BF16)
    va_ref[...] = _dot(ckv_b, wv_ref[...]).astype(BF16)

    bz_ref[...] = proj(C_BZ, 4 * HB)
    bxbc_ref[...] = proj(C_BXBC, W_XBC_P)
    dtr = proj(C_BDT, LANE) + dtb_ref[...]
    bdt_ref[...] = jnp.maximum(dtr, 0.0) + jnp.log(1.0 + jnp.exp(-jnp.abs(dtr)))

    cq_ref[...] = proj(C_CQ, 256)
    cf_ref[...] = proj(C_CF, 512)
    ci_ref[...] = proj(C_CI, 256)
    cg_ref[...] = proj(C_CG, 256)

    dq = proj(C_DQ, 4 * HB)
    scale_d = HD_D ** -0.5
    for hh in range(H_D):
        blk = slice(HB * hh, HB * (hh + 1))
        dq_ref[:, blk] = (rope_d(dq[:, blk]) * scale_d).astype(BF16)
    dk = proj(C_DK, 2 * HB)
    for hh in range(KV_D):
        blk = slice(HB * hh, HB * (hh + 1))
        dk_ref[:, blk] = rope_d(dk[:, blk])
    dv = proj(C_DV, 2 * HB)
    dv_ref[...] = dv
    if not latent:
        low = lane < HD_D
        out["dkc"][...] = jnp.where(low, dk[:, 0:HB], pltpu.roll(dk[:, HB:2 * HB], HD_D, 1))
        out["dvc"][...] = jnp.where(low, dv[:, 0:HB], pltpu.roll(dv[:, HB:2 * HB], HD_D, 1))


def _in_outputs(latent):
    outs = [("qa", 512, BF16), ("ka", 512, BF16), ("va", 256, BF16),
            ("bz", 512, F32), ("bxbc", W_XBC_P, F32), ("bdt", 128, F32),
            ("cq", 256, F32), ("cf", 512, F32), ("ci", 256, F32), ("cg", 256, F32),
            ("dq", 512, BF16), ("dk", 256, F32), ("dv", 256, F32)]
    if not latent:
        outs += [("ckv", KV_RANK, F32), ("kr", LANE, F32), ("dkc", KV_D * HD_D, F32), ("dvc", KV_D * HD_D, F32)]
    return outs


def _in_proj(x, mod, group_len, latent, wp, tabs):
    n = x.shape[0]
    row = lambda w: pl.BlockSpec((TM, w), lambda i: (i, 0))
    tab = pl.BlockSpec((TM, LANE), (lambda i: (1 + i % (DEC_SEQ // TM), 0)) if latent else (lambda i: (0, 0)))
    outs = pl.pallas_call(
        functools.partial(_in_kernel, latent=latent),
        grid=(n // TM,),
        in_specs=[row(D_MODEL), _mod_spec(group_len), _resident((D_MODEL, W_IN_P)), _resident((256, 512)),
                  _resident((KV_RANK, 512)), _resident((KV_RANK, 256)), _resident((1, 256)),
                  _resident((1, KV_RANK)), _resident((1, LANE)), tab, tab, tab, tab],
        out_specs=[row(w) for _, w, _ in _in_outputs(latent)],
        out_shape=[jax.ShapeDtypeStruct((n, w), dt) for _, w, dt in _in_outputs(latent)],
        compiler_params=_params("arbitrary"),
        name="in_proj",
    )(x, mod, wp["w_in"], wp["w_uq"], wp["w_uk"], wp["w_uv"], wp["g_q"], wp["g_kv"], wp["dt_bias"],
      tabs[0], tabs[1], tabs[2], tabs[3])
    return dict(zip([k for k, _, _ in _in_outputs(latent)], outs))


def _mla_cache_kernel(ckv_ref, krp_ref, wk_ref, wv_ref, kc_ref, vc_ref):
    ckv_b = ckv_ref[0].astype(BF16)
    kk = _dot(ckv_b, wk_ref[...])
    krp = krp_ref[0]
    for hh in range(H_A):
        blk = slice(HB * hh, HB * (hh + 1))
        kc_ref[0, :, blk] = (kk[:, blk] + krp).astype(BF16)
    vc_ref[0] = _dot(ckv_b, wv_ref[...]).astype(BF16)


def _mla_cache(ckv, krope_placed, wp):
    nb = ckv.shape[0]
    return pl.pallas_call(
        _mla_cache_kernel,
        grid=(nb,),
        in_specs=[pl.BlockSpec((1, PAST_LEN, KV_RANK), lambda b: (b, 0, 0)),
                  pl.BlockSpec((1, PAST_LEN, LANE), lambda b: (b, 0, 0)),
                  _resident((KV_RANK, 512)), _resident((KV_RANK, 256))],
        out_specs=[pl.BlockSpec((1, PAST_LEN, 512), lambda b: (b, 0, 0)),
                   pl.BlockSpec((1, PAST_LEN, 256), lambda b: (b, 0, 0))],
        out_shape=[jax.ShapeDtypeStruct((nb, PAST_LEN, 512), BF16),
                   jax.ShapeDtypeStruct((nb, PAST_LEN, 256), BF16)],
        compiler_params=_params("arbitrary"),
        name="mla_cache",
    )(ckv, krope_placed, wp["w_uk"], wp["w_uv"])


def _mla_kernel(*refs, has_cache):
    if has_cache:
        q_ref, k_ref, v_ref, kc_ref, vc_ref, o_ref = refs
    else:
        q_ref, k_ref, v_ref, o_ref = refs
    v = v_ref[...]
    head_of_lane = lax.broadcasted_iota(jnp.int32, (1, H_A * V_A), 1) // V_A
    acc = jnp.zeros((TQ_A, H_A * V_A), F32)
    for hh in range(H_A):
        blk = slice(HB * hh, HB * (hh + 1))
        qh = q_ref[:, blk]
        s = _dot_nt(qh, k_ref[:, blk])
        m = jnp.max(s, axis=-1, keepdims=True)
        if has_cache:
            sc = _dot_nt(qh, kc_ref[0, :, blk])
            m = jnp.maximum(m, jnp.max(sc, axis=-1, keepdims=True))
        e = jnp.exp(s - m)
        den = jnp.sum(e, axis=-1, keepdims=True)
        pv = _dot(e.astype(BF16), v)
        if has_cache:
            ec = jnp.exp(sc - m)
            den = den + jnp.sum(ec, axis=-1, keepdims=True)
            pv = pv + _dot(ec.astype(BF16), vc_ref[0])
        acc = jnp.where(head_of_lane == hh, pv / den, acc)
    o_ref[...] = acc


def _mla(q, k, v, n_seq, seq_len, cache=None):
    nq = seq_len // TQ_A
    in_specs = [pl.BlockSpec((TQ_A, 512), lambda b, i: (b * nq + i, 0)),
                pl.BlockSpec((seq_len, 512), lambda b, i: (b, 0)),
                pl.BlockSpec((seq_len, 256), lambda b, i: (b, 0))]
    args = [q, k, v]
    if cache is not None:
        in_specs += [pl.BlockSpec((1, PAST_LEN, 512), lambda b, i: (b, 0, 0)),
                     pl.BlockSpec((1, PAST_LEN, 256), lambda b, i: (b, 0, 0))]
        args += list(cache)
    return pl.pallas_call(
        functools.partial(_mla_kernel, has_cache=cache is not None),
        grid=(n_seq, nq),
        in_specs=in_specs,
        out_specs=pl.BlockSpec((TQ_A, H_A * V_A), lambda b, i: (b * nq + i, 0)),
        out_shape=jax.ShapeDtypeStruct((n_seq * seq_len, H_A * V_A), F32),
        compiler_params=_params("arbitrary", "arbitrary"),
        name="mla_attention",
    )(*args)


def _sink_softmax_pv(parts, sink):
    m = sink
    for s, _ in parts:
        m = jnp.maximum(m, jnp.max(s, axis=-1, keepdims=True))
    den = jnp.exp(sink - m)
    pv = None
    for s, v in parts:
        e = jnp.exp(s - m)
        den = den + jnp.sum(e, axis=-1, keepdims=True)
        t = _dot(e.astype(BF16), v)
        pv = t if pv is None else pv + t
    return pv / den


def _gqa_pair(q_ref, rows, g, scores_and_values, sink_ref, o_ref):
    t = rows.stop - rows.start
    blocks = [slice(HB * (G_D * g + j), HB * (G_D * g + j + 1)) for j in range(G_D)]
    q2 = jnp.concatenate([q_ref[rows, blk] for blk in blocks], axis=0)
    head = lax.broadcasted_iota(jnp.int32, (G_D * t, 1), 0) // t
    sink = sink_ref[G_D * g:G_D * g + 1, 0:1]
    for j in range(1, G_D):
        sink = jnp.where(head == j, sink_ref[G_D * g + j:G_D * g + j + 1, 0:1], sink)
    o = _sink_softmax_pv(scores_and_values(q2), sink)
    for j, blk in enumerate(blocks):
        o_ref[rows, blk] = o[j * t:(j + 1) * t, :]


def _gqa_ctx_kernel(q_ref, k_ref, v_ref, sink_ref, o_ref):
    for b in range(GQA_CTX_SEQS):
        rows = slice(b * SEQ, (b + 1) * SEQ)
        for g in range(KV_D):
            kv = slice(HB * g, HB * (g + 1))
            k = k_ref[rows, kv].astype(BF16)
            v = v_ref[rows, kv].astype(BF16)
            _gqa_pair(q_ref, rows, g, lambda q2, k=k, v=v: [(_dot_nt(q2, k), v)], sink_ref, o_ref)


def _gqa_ctx(q, k, v, sink, n_seq):
    seq = lambda w: pl.BlockSpec((GQA_CTX_SEQS * SEQ, w), lambda b: (b, 0))
    return pl.pallas_call(
        _gqa_ctx_kernel,
        grid=(n_seq // GQA_CTX_SEQS,),
        in_specs=[seq(512), seq(256), seq(256), _resident((H_D, LANE))],
        out_specs=seq(512),
        out_shape=jax.ShapeDtypeStruct((n_seq * SEQ, 512), F32),
        compiler_params=_params("arbitrary"),
        name="gqa_context",
    )(q, k, v, sink)


def _gqa_lat_kernel(q_ref, k_ref, v_ref, kc_ref, vc_ref, sink_ref, o_ref):
    span = 3 * WINDOW
    row = lax.broadcasted_iota(jnp.int32, (G_D * WINDOW, span), 0) % WINDOW
    col = lax.broadcasted_iota(jnp.int32, (G_D * WINDOW, span), 1)
    for j in range(GQA_LAT_BLOCKS):
        n = pl.program_id(1) * GQA_LAT_BLOCKS + j
        rows = slice(j * WINDOW, (j + 1) * WINDOW)
        start = pl.multiple_of(jnp.clip((n - 1) * WINDOW, 0, DEC_SEQ - span), WINDOW)
        band = jnp.abs((start + col) - (n * WINDOW + row)) <= WINDOW
        for g in range(KV_D):
            kv = slice(HB * g, HB * (g + 1))
            kw = k_ref[pl.ds(start, span), kv].astype(BF16)
            vw = v_ref[pl.ds(start, span), kv].astype(BF16)
            kc = kc_ref[0, :, kv]
            vc = vc_ref[0, :, kv]

            def parts(q2, kw=kw, vw=vw, kc=kc, vc=vc, band=band):
                return [(jnp.where(band, _dot_nt(q2, kw), NEG), vw), (_dot_nt(q2, kc), vc)]

            _gqa_pair(q_ref, rows, g, parts, sink_ref, o_ref)


def _gqa_lat(q, k, v, kc, vc, sink, n_seq):
    nb = DEC_SEQ // (WINDOW * GQA_LAT_BLOCKS)
    qblk = pl.BlockSpec((WINDOW * GQA_LAT_BLOCKS, 512), lambda b, n: (b * nb + n, 0))
    return pl.pallas_call(
        _gqa_lat_kernel,
        grid=(n_seq, nb),
        in_specs=[qblk,
                  pl.BlockSpec((DEC_SEQ, 256), lambda b, n: (b, 0)),
                  pl.BlockSpec((DEC_SEQ, 256), lambda b, n: (b, 0)),
                  pl.BlockSpec((1, PAST_LEN, 256), lambda b, n: (b, 0, 0)),
                  pl.BlockSpec((1, PAST_LEN, 256), lambda b, n: (b, 0, 0)),
                  _resident((H_D, LANE))],
        out_specs=qblk,
        out_shape=jax.ShapeDtypeStruct((n_seq * DEC_SEQ, 512), F32),
        compiler_params=_params("arbitrary", "arbitrary"),
        name="gqa_latent",
    )(q, k, v, kc, vc, sink)


def _conv_kernel(cur_ref, prev_ref, next_ref, w_ref, b_ref, o_ref, pad_ref, *, tiles_per_seq):
    i = pl.program_id(0)
    has_prev = (i % tiles_per_seq) != 0
    has_next = (i % tiles_per_seq) != tiles_per_seq - 1
    pad_ref[0:SUBLANE, :] = jnp.where(has_prev, prev_ref[...], 0.0)
    pad_ref[SUBLANE:SUBLANE + CONV_TILE, :] = cur_ref[...]
    pad_ref[SUBLANE + CONV_TILE:, :] = jnp.where(has_next, next_ref[...], 0.0)
    y = jnp.zeros((CONV_TILE, W_XBC_P), F32) + b_ref[...]
    for k in range(D_CONV):
        off = SUBLANE - D_CONV // 2 + k
        y = y + w_ref[k:k + 1, :] * pad_ref[off:off + CONV_TILE, :]
    o_ref[...] = _silu(y)


def _conv(xbc, seq_len, w, b):
    n = xbc.shape[0]
    per = CONV_TILE // SUBLANE
    last = n // SUBLANE - 1
    return pl.pallas_call(
        functools.partial(_conv_kernel, tiles_per_seq=seq_len // CONV_TILE),
        grid=(n // CONV_TILE,),
        in_specs=[pl.BlockSpec((CONV_TILE, W_XBC_P), lambda i: (i, 0)),
                  pl.BlockSpec((SUBLANE, W_XBC_P), lambda i: (jnp.maximum(i * per - 1, 0), 0)),
                  pl.BlockSpec((SUBLANE, W_XBC_P), lambda i: (jnp.minimum((i + 1) * per, last), 0)),
                  _resident((SUBLANE, W_XBC_P)), _resident((1, W_XBC_P))],
        out_specs=pl.BlockSpec((CONV_TILE, W_XBC_P), lambda i: (i, 0)),
        out_shape=jax.ShapeDtypeStruct((n, W_XBC_P), F32),
        scratch_shapes=[pltpu.VMEM((CONV_TILE + 2 * SUBLANE, W_XBC_P), F32)],
        compiler_params=_params("arbitrary"),
        name="ssd_conv",
    )(xbc, xbc, xbc, w, b)


def _ssd_kernel(*refs, has_s0, chunks):
    if has_s0:
        xf_ref, xb_ref, dtf_ref, dtb_ref, alog_ref, dsk_ref, s0_ref, yf_ref, yb_ref, st_ref, s_scr = refs
    else:
        xf_ref, xb_ref, dtf_ref, dtb_ref, alog_ref, dsk_ref, yf_ref, yb_ref, st_ref, s_scr = refs
    c = pl.program_id(1)
    q = SSD_CHUNK

    @pl.when(c == 0)
    def _():
        s_scr[...] = s0_ref[0] if has_s0 else jnp.zeros(s_scr.shape, F32)

    row = lax.broadcasted_iota(jnp.int32, (q, q), 0)
    col = lax.broadcasted_iota(jnp.int32, (q, q), 1)
    a_coef = -jnp.exp(alog_ref[...])
    for d, (x_ref, dt_ref, y_ref) in enumerate(((xf_ref, dtf_ref, yf_ref), (xb_ref, dtb_ref, yb_ref))):
        tri = (row >= col) if d == 0 else (row <= col)
        for ci in (range(chunks) if d == 0 else range(chunks - 1, -1, -1)):
            rows = slice(ci * q, (ci + 1) * q)
            dt = dt_ref[rows, :]
            cum = _prefix_dot(tri, dt * a_coef)
            cum_t = cum.T
            total = cum[q - 1:q, :] if d == 0 else cum[0:1, :]
            for g in range(G_B):
                bg = x_ref[rows, 512 + HB * g:512 + HB * (g + 1)]
                cg = x_ref[rows, 768 + HB * g:768 + HB * (g + 1)]
                cb = _dot_nt(cg.astype(BF16), bg.astype(BF16))
                for hh in range(g * (H_B // G_B), (g + 1) * (H_B // G_B)):
                    k = d * H_B + hh
                    cum_c = cum[:, k:k + 1]
                    diff = cum_c - cum_t[k:k + 1, :]
                    seg = jnp.where(tri, jnp.exp(jnp.where(tri, diff, 0.0)), 0.0)
                    xh = x_ref[rows, HB * hh:HB * (hh + 1)]
                    xdt = xh * dt[:, k:k + 1]
                    xdt_b = xdt.astype(BF16)
                    s_in = s_scr[d, hh]
                    y = _dot((cb * seg).astype(BF16), xdt_b)
                    y = y + _dot_nt((cg * jnp.exp(cum_c)).astype(BF16), s_in.astype(BF16))
                    y_ref[rows, HB * hh:HB * (hh + 1)] = y + xh * dsk_ref[k:k + 1, :]
                    tot = total[:, k:k + 1]
                    bdec = bg * jnp.exp(tot - cum_c)
                    cs = _dot(xdt.T.astype(BF16), bdec.astype(BF16))
                    s_scr[d, hh] = jnp.exp(tot) * s_in + cs

    @pl.when(c == pl.num_programs(1) - 1)
    def _():
        for d in range(2):
            for hh in range(H_B):
                st_ref[0, d, hh] = s_scr[d, hh, 0:P_B, 0:N_B]


def _ssd(xbc, dt, n_seq, seq_len, a_log, dskip, s0=None):
    chunks = min(SSD_STEP_CHUNKS, seq_len // SSD_CHUNK)
    nc = seq_len // (SSD_CHUNK * chunks)
    fwd = lambda w: pl.BlockSpec((SSD_CHUNK * chunks, w), lambda b, c: (b * nc + c, 0))
    bwd = lambda w: pl.BlockSpec((SSD_CHUNK * chunks, w), lambda b, c: (b * nc + nc - 1 - c, 0))
    state = pl.BlockSpec((1, 2, H_B, HB, HB), lambda b, c: (b, 0, 0, 0, 0))
    in_specs = [fwd(W_XBC_P), bwd(W_XBC_P), fwd(LANE), bwd(LANE), _resident((1, LANE)), _resident((2 * H_B, LANE))]
    args = [xbc, xbc, dt, dt, a_log, dskip]
    if s0 is not None:
        in_specs.append(state)
        args.append(s0)
    n = n_seq * seq_len
    return pl.pallas_call(
        functools.partial(_ssd_kernel, has_s0=s0 is not None, chunks=chunks),
        grid=(n_seq, nc),
        in_specs=in_specs,
        out_specs=[fwd(4 * HB), bwd(4 * HB), pl.BlockSpec((1, 2, H_B, P_B, N_B), lambda b, c: (b, 0, 0, 0, 0))],
        out_shape=[jax.ShapeDtypeStruct((n, 4 * HB), F32), jax.ShapeDtypeStruct((n, 4 * HB), F32),
                   jax.ShapeDtypeStruct((n_seq, 2, H_B, P_B, N_B), F32)],
        scratch_shapes=[pltpu.VMEM((2, H_B, HB, HB), F32)],
        compiler_params=_params("arbitrary", "arbitrary"),
        name="ssd_scan",
    )(*args)


def _gla_kernel(*refs, has_s0):
    if has_s0:
        (qf_ref, qb_ref, ff_ref, fb_ref, vf_ref, vb_ref, lb_ref, s0_ref, of_ref, ob_ref, st_ref, s_scr,
         o_scr) = refs
    else:
        (qf_ref, qb_ref, ff_ref, fb_ref, vf_ref, vb_ref, lb_ref, of_ref, ob_ref, st_ref, s_scr, o_scr) = refs
    c = pl.program_id(1)
    t = GLA_TILE
    ch = HGRN_CHUNK
    nch = t // ch
    w = H_C * K_C

    @pl.when(c == 0)
    def _():
        s_scr[...] = s0_ref[0] if has_s0 else jnp.zeros(s_scr.shape, F32)

    row = lax.broadcasted_iota(jnp.int32, (t, t), 0)
    col = lax.broadcasted_iota(jnp.int32, (t, t), 1)
    same_chunk = (row // ch) == (col // ch)
    head_of_lane = lax.broadcasted_iota(jnp.int32, (1, w), 1) // K_C
    tok_row = lax.broadcasted_iota(jnp.int32, (t, 1), 0)
    zero_b = jnp.zeros((), BF16)
    for d, (q_ref, f_ref, v_ref, o_ref) in enumerate(((qf_ref, ff_ref, vf_ref, of_ref),
                                                      (qb_ref, fb_ref, vb_ref, ob_ref))):
        qv = q_ref[...]
        fr = f_ref[...]
        vv = v_ref[...]
        lb = lb_ref[d:d + 1, :]
        f = lb + (1.0 - lb) * jax.nn.sigmoid(fr)
        log_f = jnp.log(jnp.maximum(f, F_MIN))
        key = (1.0 - lb) * jax.nn.sigmoid(-fr)
        tri = same_chunk & ((col <= row) if d == 0 else (col >= row))
        cum = _prefix_dot(tri, log_f)
        cum3 = cum.reshape(nch, ch, w)
        k3 = key.reshape(nch, ch, w)
        q_dec = (qv * jnp.exp(cum)).astype(BF16)
        q_heads = [jnp.where(head_of_lane == hh, q_dec, zero_b) for hh in range(H_C)]
        v_b = vv.astype(BF16)

        k_inv = (key * jnp.exp(-cum)).astype(BF16)
        o = jnp.zeros((t, w), F32)
        for hh in range(H_C):
            att = jnp.where(tri, _dot_nt(q_heads[hh], k_inv), 0.0).astype(BF16)
            o = jnp.where(head_of_lane == hh, _dot(att, v_b), o)
        o_scr[...] = o

        @pl.when(jnp.min(cum) < -GLA_SAFE_LOG_DECAY)
        def _():
            head_ones = ((row // K_C) == (col // K_C)).astype(F32)
            i_in_chunk = lax.broadcasted_iota(jnp.int32, (nch, ch, w), 1)
            q3 = qv.reshape(nch, ch, w)
            v3 = vv.reshape(nch, ch, w)
            o3 = jnp.zeros((nch, ch, w), F32)
            for j in range(ch):
                live = (i_in_chunk >= j) if d == 0 else (i_in_chunk <= j)
                e = jnp.exp(jnp.where(live, cum3 - cum3[:, j:j + 1, :], 0.0))
                term = jnp.where(live, q3 * e * k3[:, j:j + 1, :], 0.0)
                att = _dot(term.reshape(t, w), head_ones)
                o3 = o3 + att.reshape(nch, ch, w) * v3[:, j:j + 1, :]
            o_scr[...] = o3.reshape(t, w)

        edge = ch - 1 if d == 0 else 0
        last3 = jnp.broadcast_to(cum3[:, edge:edge + 1, :], (nch, ch, w))
        k_dec = (k3 * jnp.exp(last3 - cum3)).reshape(t, w).astype(BF16)
        v_t = vv.T.astype(BF16)
        order = range(nch) if d == 0 else range(nch - 1, -1, -1)
        for cc in order:
            rows = slice(cc * ch, (cc + 1) * ch)
            st = s_scr[d]
            q4 = jnp.concatenate([qh[rows, :] for qh in q_heads], axis=0)
            r = _dot_nt(q4, st.astype(BF16))
            o_inter = r[0:ch, :]
            for hh in range(1, H_C):
                o_inter = jnp.where(head_of_lane == hh, r[hh * ch:(hh + 1) * ch, :], o_inter)
            o_ref[rows, :] = o_scr[rows, :] + o_inter
            in_chunk = (tok_row >= cc * ch) & (tok_row < (cc + 1) * ch)
            cs_t = _dot(v_t, jnp.where(in_chunk, k_dec, zero_b))
            decay = jnp.exp(cum[cc * ch + edge:cc * ch + edge + 1, :])
            s_scr[d] = decay * st + cs_t

    @pl.when(c == pl.num_programs(1) - 1)
    def _():
        k_idx = lax.broadcasted_iota(jnp.int32, (K_C, w), 0)
        lane_idx = lax.broadcasted_iota(jnp.int32, (K_C, w), 1)
        for d in range(2):
            for hh in range(H_C):
                sel = (lane_idx == hh * K_C + k_idx).astype(F32)
                rows = s_scr[d, hh * V_C:(hh + 1) * V_C, :]
                st_ref[0, d, hh] = lax.dot_general(sel, rows, (((1,), (1,)), ((), ())),
                                                   preferred_element_type=F32, precision=lax.Precision.HIGHEST)


def _gla(cq, cf, ci, n_seq, seq_len, lb, s0=None):
    nt = seq_len // GLA_TILE
    w = H_C * K_C
    fwd = lambda j: pl.BlockSpec((GLA_TILE, w), lambda b, c: (b * nt + c, j))
    bwd = lambda j: pl.BlockSpec((GLA_TILE, w), lambda b, c: (b * nt + nt - 1 - c, j))
    state = pl.BlockSpec((1, 2, w, w), lambda b, c: (b, 0, 0, 0))
    in_specs = [fwd(0), bwd(0), fwd(0), bwd(1), fwd(0), bwd(0), _resident((2, w))]
    args = [cq, cq, cf, cf, ci, ci, lb]
    if s0 is not None:
        in_specs.append(state)
        args.append(s0)
    n = n_seq * seq_len
    return pl.pallas_call(
        functools.partial(_gla_kernel, has_s0=s0 is not None),
        grid=(n_seq, nt),
        in_specs=in_specs,
        out_specs=[fwd(0), bwd(0), pl.BlockSpec((1, 2, H_C, K_C, V_C), lambda b, c: (b, 0, 0, 0, 0))],
        out_shape=[jax.ShapeDtypeStruct((n, w), F32), jax.ShapeDtypeStruct((n, w), F32),
                   jax.ShapeDtypeStruct((n_seq, 2, H_C, K_C, V_C), F32)],
        scratch_shapes=[pltpu.VMEM((2, w, w), F32), pltpu.VMEM((GLA_TILE, w), F32)],
        compiler_params=_params("arbitrary", "arbitrary"),
        name="hgrn_scan",
    )(*args)


def _out_kernel(x_ref, mod_ref, oa_ref, yf_ref, yb_ref, bz_ref, of_ref, ob_ref, cg_ref, od_ref,
                woa_ref, wob_ref, woc_ref, wod_ref, nb_ref, nc_ref, g_ref, b_ref, o_ref):
    x = x_ref[...]
    gate = mod_ref[0, 5:6, :]
    yb = (yf_ref[...] + yb_ref[...]) * _silu(bz_ref[...])
    parts = []
    for hh in range(H_B):
        blk = yb[:, HB * hh:HB * (hh + 1)]
        ms = jnp.sum(blk * blk, axis=-1, keepdims=True) * (1.0 / P_B)
        parts.append((blk * lax.rsqrt(ms + EPS), hh))
    u = _dot(oa_ref[...].astype(BF16), woa_ref[...])
    for blk, hh in parts:
        nb = nb_ref[:, HB * hh:HB * (hh + 1)]
        u = u + _dot((blk * nb).astype(BF16), wob_ref[HB * hh:HB * (hh + 1), :])
    oc = of_ref[...] + ob_ref[...]
    lane = lax.broadcasted_iota(jnp.int32, (1, LANE), 1)
    low = lane < V_C
    halves = []
    for t in range(H_C * V_C // LANE):
        blk = oc[:, LANE * t:LANE * (t + 1)]
        sq = blk * blk
        s_all = jnp.sum(sq, axis=-1, keepdims=True)
        s_low = jnp.sum(jnp.where(low, sq, 0.0), axis=-1, keepdims=True)
        ms = jnp.where(low, s_low, s_all - s_low) * (1.0 / V_C)
        halves.append(blk * lax.rsqrt(ms + EPS))
    ocn = jnp.concatenate(halves, axis=-1) * nc_ref[...] * _silu(cg_ref[...])
    u = u + _dot(ocn.astype(BF16), woc_ref[...])
    u = u + _dot(od_ref[...].astype(BF16), wod_ref[...])
    o_ref[...] = _layer_norm(ALPHA * x + gate * u, g_ref[...], b_ref[...])


def _out_proj(x, mod, seq_len, mix, wp, ln_g, ln_b):
    n = x.shape[0]
    row = lambda w: pl.BlockSpec((TM, w), lambda i: (i, 0))
    return pl.pallas_call(
        _out_kernel,
        grid=(n // TM,),
        in_specs=[row(D_MODEL), _mod_spec(seq_len), row(256), row(512), row(512), row(512), row(256), row(256),
                  row(256), row(512),
                  _resident((256, D_MODEL)), _resident((512, D_MODEL)), _resident((256, D_MODEL)),
                  _resident((512, D_MODEL)), _resident((1, 512)), _resident((1, 256)),
                  _resident((1, D_MODEL)), _resident((1, D_MODEL))],
        out_specs=row(D_MODEL),
        out_shape=jax.ShapeDtypeStruct((n, D_MODEL), F32),
        compiler_params=_params("arbitrary"),
        name="out_proj",
    )(x, mod, mix["oa"], mix["yf"], mix["yb"], mix["bz"], mix["of"], mix["ob"], mix["cg"], mix["od"],
      wp["w_oa"], wp["w_ob"], wp["w_oc"], wp["w_od"], wp["ssd_norm"], wp["hgrn_norm"], ln_g, ln_b)


def _prep_layer(l, w_in, w_out, mla_q_norm, mla_kv_norm, mla_w_uq, mla_w_ukv, ssd_conv_w, ssd_conv_b,
                ssd_a_log, ssd_dt_bias, ssd_d, ssd_norm, hgrn_lb, hgrn_norm, gqa_sink):
    wo = w_out[l]
    conv_w = _gather_pad(ssd_conv_w[l], _IDX_CONV, 1)
    return {
        "w_in": _gather_pad(w_in[l], _IDX_W_IN, 1).astype(BF16),
        "w_uq": _gather_pad(_gather_pad(mla_w_uq[l], _IDX_UQ_ROWS, 0), _IDX_UQ_COLS, 1).astype(BF16),
        "w_uk": _gather_pad(mla_w_ukv[l], _IDX_UKV_K, 1).astype(BF16),
        "w_uv": _gather_pad(mla_w_ukv[l], _IDX_UKV_V, 1).astype(BF16),
        "g_q": _gather_pad(mla_q_norm[l], _IDX_UQ_ROWS, 0).reshape(1, 256),
        "g_kv": mla_kv_norm[l].reshape(1, KV_RANK),
        "dt_bias": jnp.pad(ssd_dt_bias[l].reshape(1, 2 * H_B), ((0, 0), (0, LANE - 2 * H_B))),
        "conv_w": jnp.pad(conv_w, ((0, SUBLANE - D_CONV), (0, 0))),
        "conv_b": _gather_pad(ssd_conv_b[l], _IDX_CONV, 0).reshape(1, W_XBC_P),
        "a_log": jnp.pad(ssd_a_log[l].reshape(1, 2 * H_B), ((0, 0), (0, LANE - 2 * H_B))),
        "d_skip": jnp.broadcast_to(ssd_d[l].reshape(2 * H_B, 1), (2 * H_B, LANE)),
        "ssd_norm": _gather_pad(ssd_norm[l], _IDX_HEAD4, 0).reshape(1, 4 * HB),
        "hgrn_lb": hgrn_lb[l],
        "hgrn_norm": hgrn_norm[l].reshape(1, H_C * V_C),
        "sink": jnp.broadcast_to(gqa_sink[l].reshape(H_D, 1), (H_D, LANE)),
        "w_oa": wo[0:256].astype(BF16),
        "w_ob": _gather_pad(wo[256:512], _IDX_HEAD4, 0).astype(BF16),
        "w_oc": wo[512:768].astype(BF16),
        "w_od": _gather_pad(wo[768:1024], _IDX_HEAD4, 0).astype(BF16),
    }


def _pad_heads(t):
    return jnp.pad(t, [(0, 0)] * (t.ndim - 1) + [(0, HB - t.shape[-1])])


def _mixer(x, mod, group_len, wp, tabs, n_seq, seq_len, ctx):
    latent = ctx is not None
    p = _in_proj(x, mod, group_len, latent, wp, tabs)
    mix = {"bz": p["bz"], "cg": p["cg"]}
    cache = _mla_cache(ctx["ckv"], ctx["krope"], wp) if latent else None
    mix["oa"] = _mla(p["qa"], p["ka"], p["va"], n_seq, seq_len, cache)
    xbc = _conv(p["bxbc"], seq_len, wp["conv_w"], wp["conv_b"])
    mix["yf"], mix["yb"], st_b = _ssd(xbc, p["bdt"], n_seq, seq_len, wp["a_log"], wp["d_skip"],
                                      ctx["ssm"] if latent else None)
    mix["of"], mix["ob"], st_c = _gla(p["cq"], p["cf"], p["ci"], n_seq, seq_len, wp["hgrn_lb"],
                                      ctx["hgrn"] if latent else None)
    if latent:
        mix["od"] = _gqa_lat(p["dq"], p["dk"], p["dv"], ctx["dk"], ctx["dv"], wp["sink"], n_seq)
    else:
        mix["od"] = _gqa_ctx(p["dq"], p["dk"], p["dv"], wp["sink"], n_seq)
    state = None if latent else (p["ckv"], p["kr"], st_b, st_c, p["dkc"], p["dvc"])
    return mix, state


def _run_stream(x, mod, n_seq, seq_len, ctx, wp, ffn_w, lng, lnb, tabs):
    group_len = x.shape[0] if ctx is None else seq_len
    x = _ffn(x, mod, group_len, *ffn_w[0], lng[0], lnb[0], sub=0)
    mix, st = _mixer(x, mod, group_len, wp, tabs, n_seq, seq_len, ctx)
    x = _out_proj(x, mod, group_len, mix, wp, lng[1], lnb[1])
    x = _ffn(x, mod, group_len, *ffn_w[1], lng[2], lnb[2], sub=2)
    return x, st


def _layer_inputs(l, ctx_tensors, weights, hgrn_lb):
    (cache_a_ckv, cache_a_krope, state_b_ssm, state_c_hgrn, cache_d_k, cache_d_v) = ctx_tensors
    (ln_g, ln_b, ffn_w_gu, ffn_w_down, w_in, w_out, mla_q_norm, mla_kv_norm, mla_w_uq, mla_w_ukv, ssd_conv_w,
     ssd_conv_b, ssd_a_log, ssd_dt_bias, ssd_d, ssd_norm, hgrn_norm, gqa_sink) = weights
    wp = _prep_layer(l, w_in, w_out, mla_q_norm, mla_kv_norm, mla_w_uq, mla_w_ukv, ssd_conv_w, ssd_conv_b,
                     ssd_a_log, ssd_dt_bias, ssd_d, ssd_norm, hgrn_lb, hgrn_norm, gqa_sink)
    ffn_w = [(ffn_w_gu.astype(BF16), ffn_w_down.astype(BF16), (l, s)) for s in range(2)]
    lng = [ln_g[l, s].reshape(1, D_MODEL) for s in range(N_SUB)]
    lnb = [ln_b[l, s].reshape(1, D_MODEL) for s in range(N_SUB)]
    nb = cache_a_ckv.shape[0]
    ctx = {
        "ckv": cache_a_ckv[:, l],
        "krope": jnp.pad(cache_a_krope[:, l], ((0, 0), (0, 0), (NOPE_A, LANE - NOPE_A - ROPE_A))),
        "ssm": jnp.pad(state_b_ssm[:, l], ((0, 0),) * 3 + ((0, HB - P_B), (0, HB - N_B))),
        "hgrn": jnp.einsum("bdhkv,hg->bdhvgk", state_c_hgrn[:, l], jnp.eye(H_C, dtype=F32)).reshape(
            nb, 2, H_C * V_C, H_C * K_C),
        "dk": _pad_heads(cache_d_k[:, l]).reshape(nb, PAST_LEN, KV_D * HB).astype(BF16),
        "dv": _pad_heads(cache_d_v[:, l]).reshape(nb, PAST_LEN, KV_D * HB).astype(BF16),
    }
    return wp, ffn_w, lng, lnb, ctx


def kernel(x_prompt, x_sample, cache_a_ckv, cache_a_krope, state_b_ssm, state_c_hgrn, cache_d_k, cache_d_v,
           c, c_ctx, w_mod, b_mod, ln_g, ln_b, ffn_w_gu, ffn_w_down, w_in, w_out, mla_q_norm, mla_kv_norm,
           mla_w_uq, mla_w_ukv, ssd_conv_w, ssd_conv_b, ssd_a_log, ssd_dt_bias, ssd_d, ssd_norm,
           hgrn_lb_logits, hgrn_norm, gqa_sink):
    lb_p = jax.nn.softmax(hgrn_lb_logits.astype(F32), axis=0)
    hgrn_lb = jnp.cumsum(lb_p, axis=0) - lb_p[:1]

    cvec = jnp.concatenate([c_ctx[None], c, jnp.zeros((SUBLANE - 1 - DEC_BATCH, D_MODEL), F32)], axis=0)
    mod_all = _modulation(cvec, w_mod, b_mod)
    tabs = _rope_tables(8, NOPE_A) + _rope_tables(16, 0)
    ctx_tensors = (cache_a_ckv, cache_a_krope, state_b_ssm, state_c_hgrn, cache_d_k, cache_d_v)
    weights = (ln_g, ln_b, ffn_w_gu, ffn_w_down, w_in, w_out, mla_q_norm, mla_kv_norm, mla_w_uq, mla_w_ukv,
               ssd_conv_w, ssd_conv_b, ssd_a_log, ssd_dt_bias, ssd_d, ssd_norm, hgrn_norm, gqa_sink)

    y_p = x_prompt.reshape(BATCH * SEQ, D_MODEL)
    y_s = x_sample.reshape(DEC_BATCH * DEC_SEQ, D_MODEL)
    states = []
    for l in range(DEPTH):
        wp, ffn_w, lng, lnb, ctx = _layer_inputs(l, ctx_tensors, weights, hgrn_lb)
        mod_ctx = mod_all[l, 0:1].reshape(1, N_SUB * 3, D_MODEL)
        mod_lat = mod_all[l, 1:1 + DEC_BATCH].reshape(DEC_BATCH, N_SUB * 3, D_MODEL)
        y_p, st = _run_stream(y_p, mod_ctx, BATCH, SEQ, None, wp, ffn_w, lng, lnb, tabs)
        y_s, _ = _run_stream(y_s, mod_lat, DEC_BATCH, DEC_SEQ, ctx, wp, ffn_w, lng, lnb, tabs)
        states.append(st)

    def stack(i, f):
        return jnp.stack([f(s[i]) for s in states], axis=1)

    new_a_ckv = stack(0, lambda t: t.reshape(BATCH, SEQ, KV_RANK))
    new_a_krope = stack(1, lambda t: t.reshape(BATCH, SEQ, LANE)[..., :ROPE_A])
    new_b_ssm = stack(2, lambda t: t)
    new_c_hgrn = stack(3, lambda t: t)
    new_d_k = stack(4, lambda t: t.reshape(BATCH, SEQ, KV_D, HD_D))
    new_d_v = stack(5, lambda t: t.reshape(BATCH, SEQ, KV_D, HD_D))
    return (y_p.reshape(BATCH, SEQ, D_MODEL), y_s.reshape(DEC_BATCH, DEC_SEQ, D_MODEL),
            new_a_ckv, new_a_krope, new_b_ssm, new_c_hgrn, new_d_k, new_d_v)
```

```python
import functools

import numpy as np
import jax
import jax.numpy as jnp
from jax import lax
from jax.experimental import pallas as pl
from jax.experimental.pallas import tpu as pltpu

F32 = jnp.float32
BF16 = jnp.bfloat16

D_MODEL = 1024
BATCH = 32
SEQ = 256
DEPTH = 2
DEC_BATCH = 2
DEC_SEQ = 4096
PAST_LEN = 512
GRID_W = 64
H_A, Q_RANK, KV_RANK, NOPE_A, ROPE_A, V_A = 4, 192, 128, 64, 32, 64
H_B, P_B, G_B, N_B, D_CONV, SSD_CHUNK = 4, 64, 2, 64, 5, 128
H_C, K_C, V_C, HGRN_CHUNK = 4, 64, 64, 16
H_D, KV_D, HD_D, WINDOW = 4, 2, 64, 128
G_D = H_D // KV_D
ROPE_BASE = 10000.0
D_FF = 2816
N_SUB = 3
ALPHA = (2 * DEPTH) ** 0.25
EPS = 1e-6
F_MIN = 1e-6
NEG = -1e30
D_IN = 2920
N_MOD = N_SUB * 3 * D_MODEL

LANE = 128
SUBLANE = 8
VMEM_LIMIT = 56 * 1024 * 1024

TM = 512
TM_FFN = 1024
FF_CHUNK = 256
TQ_A = 256
GLA_TILE = 256
GLA_CHUNK = 32
SSD_STEP_CHUNKS = 4
GQA_CTX_SEQS = 4
MLA_CTX_SEQS = 4
GQA_LAT_BLOCKS = 2
CONV_TILE = 256
MOD_TN = 1536
GLA_SAFE_LOG_DECAY = 60.0

C_ACQ, C_ACKV, C_AKR = 0, 256, 384
C_BZ, C_BXBC, C_BDT = 512, 1024, 2048
C_CQ, C_CF, C_CI, C_CG = 2176, 2432, 2944, 3200
C_DQ, C_DK, C_DV = 3456, 3968, 4224
W_IN_P = 4480
W_XBC_P = 1024
HB = 128


def _dot(a, b, precision=None):
    return jnp.dot(a, b, preferred_element_type=F32, precision=precision)


def _dot_nt(a, b):
    return lax.dot_general(a, b, (((1,), (1,)), ((), ())), preferred_element_type=F32)


def _prefix_dot(tri, x):
    t = tri.astype(BF16)
    hi = x.astype(BF16)
    rest = x - hi.astype(F32)
    mid = rest.astype(BF16)
    lo = (rest - mid.astype(F32)).astype(BF16)
    return _dot(t, hi) + _dot(t, mid) + _dot(t, lo)


def _params(*sem):
    return pltpu.CompilerParams(dimension_semantics=sem, vmem_limit_bytes=VMEM_LIMIT)


def _resident(shape, index=None):
    index = (0,) * len(shape) if index is None else index
    return pl.BlockSpec(shape, lambda *_: index, pipeline_mode=pl.Buffered(1))


def _silu(x):
    return x * jax.nn.sigmoid(x)


def _layer_norm(y, g, b):
    mu = jnp.mean(y, axis=-1, keepdims=True)
    yc = y - mu
    var = jnp.mean(yc * yc, axis=-1, keepdims=True)
    return yc * lax.rsqrt(var + EPS) * g + b


def _index_map(width, pieces):
    idx = np.full((width,), -1, np.int32)
    for dst, src, w in pieces:
        idx[dst:dst + w] = np.arange(src, src + w)
    return idx


def _gather_pad(arr, idx, axis):
    parts = []
    i = 0
    n = idx.shape[0]
    while i < n:
        j = i
        if idx[i] < 0:
            while j < n and idx[j] < 0:
                j += 1
            shape = list(arr.shape)
            shape[axis] = j - i
            parts.append(jnp.zeros(shape, arr.dtype))
        else:
            while j + 1 < n and idx[j + 1] == idx[j] + 1:
                j += 1
            j += 1
            parts.append(lax.slice_in_dim(arr, int(idx[i]), int(idx[i]) + j - i, axis=axis))
        i = j
    return jnp.concatenate(parts, axis=axis)


def _heads(dst0, src0, n):
    return [(dst0 + HB * h, src0 + 64 * h, 64) for h in range(n)]


_IDX_W_IN = _index_map(W_IN_P, [
    (C_ACQ, 0, Q_RANK), (C_ACKV, 192, KV_RANK), (C_AKR, 320, ROPE_A),
    *_heads(C_BZ, 352, 4),
    *_heads(C_BXBC, 608, 4), *_heads(C_BXBC + 512, 864, 2), *_heads(C_BXBC + 768, 992, 2),
    (C_BDT, 1120, 2 * H_B),
    (C_CQ, 1128, 256), (C_CF, 1384, 512), (C_CI, 1896, 256), (C_CG, 2152, 256),
    *_heads(C_DQ, 2408, 4), *_heads(C_DK, 2664, 2), *_heads(C_DV, 2792, 2)])
_IDX_UQ_ROWS = _index_map(256, [(0, 0, Q_RANK)])
_IDX_UQ_COLS = _index_map(4 * HB, [(HB * h, 96 * h, 96) for h in range(H_A)])
_IDX_UKV_K = _index_map(4 * HB, [(HB * h, 128 * h, NOPE_A) for h in range(H_A)])
_IDX_UKV_V = _index_map(4 * V_A, [(V_A * h, 128 * h + NOPE_A, V_A) for h in range(H_A)])
_IDX_CONV = _index_map(W_XBC_P, [*_heads(0, 0, 4), *_heads(512, 256, 2), *_heads(768, 384, 2)])
_IDX_HEAD4 = _index_map(4 * HB, _heads(0, 0, 4))


def _rope_tables(half, lane0):
    t = np.arange(DEC_SEQ)
    pos = np.stack([t // GRID_W, t % GRID_W], 0).astype(np.float64)
    inv = ROPE_BASE ** (-np.arange(half, dtype=np.float64) / half)
    cos = np.ones((DEC_SEQ, LANE))
    sin = np.zeros((DEC_SEQ, LANE))
    for axis in range(2):
        ang = pos[axis][:, None] * inv[None, :]
        base = lane0 + axis * 2 * half
        cos[:, base:base + half] = np.cos(ang)
        cos[:, base + half:base + 2 * half] = np.cos(ang)
        sin[:, base:base + half] = -np.sin(ang)
        sin[:, base + half:base + 2 * half] = np.sin(ang)
    ident_c = np.ones((TM, LANE))
    ident_s = np.zeros((TM, LANE))
    return (jnp.asarray(np.concatenate([ident_c, cos], 0), F32),
            jnp.asarray(np.concatenate([ident_s, sin], 0), F32))


def _rope(x, cos, sin, first, half):
    partner = jnp.where(first, pltpu.roll(x, LANE - half, 1), pltpu.roll(x, half, 1))
    return x * cos + partner * sin


def _mod_kernel(c_ref, w_ref, b_ref, o_ref):
    c = c_ref[...]
    s = _silu(c).astype(BF16)
    o_ref[0] = _dot(s, w_ref[0].astype(BF16)) + b_ref[0]


def _modulation(cvec, w_mod, b_mod):
    return pl.pallas_call(
        _mod_kernel,
        grid=(DEPTH, N_MOD // MOD_TN),
        in_specs=[pl.BlockSpec((SUBLANE, D_MODEL), lambda l, j: (0, 0)),
                  pl.BlockSpec((1, D_MODEL, MOD_TN), lambda l, j: (l, 0, j)),
                  pl.BlockSpec((1, 1, MOD_TN), lambda l, j: (l, 0, j))],
        out_specs=pl.BlockSpec((1, SUBLANE, MOD_TN), lambda l, j: (l, 0, j)),
        out_shape=jax.ShapeDtypeStruct((DEPTH, SUBLANE, N_MOD), F32),
        compiler_params=_params("arbitrary", "arbitrary"),
        name="modulation",
    )(cvec, w_mod, b_mod.reshape(DEPTH, 1, N_MOD))


def _mod_spec(seq_len, tm=TM):
    return pl.BlockSpec((1, N_SUB * 3, D_MODEL), lambda i: (i * tm // seq_len, 0, 0))


def _ffn_kernel(x_ref, mod_ref, wg_ref, wu_ref, wd_ref, g_ref, b_ref, o_ref, *, sub):
    x = x_ref[...]
    shift = mod_ref[0, 3 * sub:3 * sub + 1, :]
    scale = mod_ref[0, 3 * sub + 1:3 * sub + 2, :]
    gate = mod_ref[0, 3 * sub + 2:3 * sub + 3, :]
    h = (x * (1.0 + scale) + shift).astype(BF16)
    acc = jnp.zeros((TM_FFN, D_MODEL), F32)
    for start in range(0, D_FF, FF_CHUNK):
        cols = slice(start, min(start + FF_CHUNK, D_FF))
        gt = _dot(h, wg_ref[:, cols])
        up = _dot(h, wu_ref[:, cols])
        acc = acc + _dot((_silu(gt) * up).astype(BF16), wd_ref[cols, :])
    y = ALPHA * x + 0.5 * gate * acc
    o_ref[...] = _layer_norm(y, g_ref[...], b_ref[...])


def _ffn(x, mod, seq_len, w_gu, w_down, layer_sub, ln_g, ln_b, sub):
    n = x.shape[0]
    l, s = layer_sub
    row = pl.BlockSpec((TM_FFN, D_MODEL), lambda i: (i, 0))
    return pl.pallas_call(
        functools.partial(_ffn_kernel, sub=sub),
        grid=(n // TM_FFN,),
        in_specs=[row, _mod_spec(seq_len, TM_FFN),
                  _resident((None, None, D_MODEL, D_FF), (l, s, 0, 0)),
                  _resident((None, None, D_MODEL, D_FF), (l, s, 0, 1)),
                  _resident((None, None, D_FF, D_MODEL), (l, s, 0, 0)),
                  _resident((1, D_MODEL)), _resident((1, D_MODEL))],
        out_specs=row,
        out_shape=jax.ShapeDtypeStruct((n, D_MODEL), F32),
        compiler_params=_params("arbitrary"),
        name="ffn",
    )(x, mod, w_gu, w_gu, w_down, ln_g, ln_b)


def _in_kernel(x_ref, mod_ref, w_ref, wuq_ref, wk_ref, wv_ref, gq_ref, gkv_ref, dtb_ref,
               cosq_ref, sinq_ref, cosd_ref, sind_ref, *out_refs, latent):
    out = dict(zip([name for name, _, _ in _in_outputs(latent)], out_refs))
    qa_ref, ka_ref, va_ref = out["qa"], out["ka"], out["va"]
    bz_ref, bxbc_ref, bdt_ref = out["bz"], out["bxbc"], out["bdt"]
    cq_ref, cf_ref, ci_ref, cg_ref = out["cq"], out["cf"], out["ci"], out["cg"]
    dq_ref, dk_ref, dv_ref = out["dq"], out["dk"], out["dv"]
    x = x_ref[...]
    h = (x * (1.0 + mod_ref[0, 4:5, :]) + mod_ref[0, 3:4, :]).astype(BF16)

    def proj(start, width):
        return _dot(h, w_ref[:, start:start + width])

    lane = lax.broadcasted_iota(jnp.int32, (TM, LANE), 1)
    first_a = (lane % 16) < 8
    first_d = (lane % 32) < 16

    def rope_a(blk):
        return _rope(blk, cosq_ref[...], sinq_ref[...], first_a, 8) if latent else blk

    def rope_d(blk):
        return _rope(blk, cosd_ref[...], sind_ref[...], first_d, 16) if latent else blk

    acq = proj(C_ACQ, 256)
    ms = jnp.sum(acq * acq, axis=-1, keepdims=True) * (1.0 / Q_RANK)
    qn = (acq * lax.rsqrt(ms + EPS) * gq_ref[...]).astype(BF16)
    q = _dot(qn, wuq_ref[...])
    scale_a = (NOPE_A + ROPE_A) ** -0.5
    for hh in range(H_A):
        blk = slice(HB * hh, HB * (hh + 1))
        qa_ref[:, blk] = (rope_a(q[:, blk]) * scale_a).astype(BF16)
    ackv = proj(C_ACKV, KV_RANK)
    ms = jnp.mean(ackv * ackv, axis=-1, keepdims=True)
    ckv = ackv * lax.rsqrt(ms + EPS) * gkv_ref[...]
    ckv_b = ckv.astype(BF16)
    kk = _dot(ckv_b, wk_ref[...])
    akr = proj(C_AKR, LANE)
    if not latent:
        out["ckv"][...] = ckv
        out["kr"][...] = akr
    krp = rope_a(pltpu.roll(akr, NOPE_A, 1))
    for hh in range(H_A):
        blk = slice(HB * hh, HB * (hh + 1))
        ka_ref[:, blk] = (kk[:, blk] + krp).astype(BF16)
    va_ref[...] = _dot(ckv_b, wv_ref[...]).astype(BF16)

    bz_ref[...] = proj(C_BZ, 4 * HB)
    bxbc_ref[...] = proj(C_BXBC, W_XBC_P)
    dtr = proj(C_BDT, LANE) + dtb_ref[...]
    bdt_ref[...] = jnp.maximum(dtr, 0.0) + jnp.log(1.0 + jnp.exp(-jnp.abs(dtr)))

    cq_ref[...] = proj(C_CQ, 256)
    cf_ref[...] = proj(C_CF, 512)
    ci_ref[...] = proj(C_CI, 256)
    cg_ref[...] = proj(C_CG, 256)

    dq = proj(C_DQ, 4 * HB)
    scale_d = HD_D ** -0.5
    for hh in range(H_D):
        blk = slice(HB * hh, HB * (hh + 1))
        dq_ref[:, blk] = (rope_d(dq[:, blk]) * scale_d).astype(BF16)
    dk = proj(C_DK, 2 * HB)
    for hh in range(KV_D):
        blk = slice(HB * hh, HB * (hh + 1))
        dk_ref[:, blk] = rope_d(dk[:, blk])
    dv = proj(C_DV, 2 * HB)
    dv_ref[...] = dv
    if not latent:
        low = lane < HD_D
        out["dkc"][...] = jnp.where(low, dk[:, 0:HB], pltpu.roll(dk[:, HB:2 * HB], HD_D, 1))
        out["dvc"][...] = jnp.where(low, dv[:, 0:HB], pltpu.roll(dv[:, HB:2 * HB], HD_D, 1))


def _in_outputs(latent):
    outs = [("qa", 512, BF16), ("ka", 512, BF16), ("va", 256, BF16),
            ("bz", 512, F32), ("bxbc", W_XBC_P, F32), ("bdt", 128, F32),
            ("cq", 256, F32), ("cf", 512, F32), ("ci", 256, F32), ("cg", 256, F32),
            ("dq", 512, BF16), ("dk", 256, F32), ("dv", 256, F32)]
    if not latent:
        outs += [("ckv", KV_RANK, F32), ("kr", LANE, F32), ("dkc", KV_D * HD_D, F32), ("dvc", KV_D * HD_D, F32)]
    return outs


def _in_proj(x, mod, group_len, latent, wp, tabs):
    n = x.shape[0]
    row = lambda w: pl.BlockSpec((TM, w), lambda i: (i, 0))
    tab = pl.BlockSpec((TM, LANE), (lambda i: (1 + i % (DEC_SEQ // TM), 0)) if latent else (lambda i: (0, 0)))
    outs = pl.pallas_call(
        functools.partial(_in_kernel, latent=latent),
        grid=(n // TM,),
        in_specs=[row(D_MODEL), _mod_spec(group_len), _resident((D_MODEL, W_IN_P)), _resident((256, 512)),
                  _resident((KV_RANK, 512)), _resident((KV_RANK, 256)), _resident((1, 256)),
                  _resident((1, KV_RANK)), _resident((1, LANE)), tab, tab, tab, tab],
        out_specs=[row(w) for _, w, _ in _in_outputs(latent)],
        out_shape=[jax.ShapeDtypeStruct((n, w), dt) for _, w, dt in _in_outputs(latent)],
        compiler_params=_params("arbitrary"),
        name="in_proj",
    )(x, mod, wp["w_in"], wp["w_uq"], wp["w_uk"], wp["w_uv"], wp["g_q"], wp["g_kv"], wp["dt_bias"],
      tabs[0], tabs[1], tabs[2], tabs[3])
    return dict(zip([k for k, _, _ in _in_outputs(latent)], outs))


def _mla_cache_kernel(ckv_ref, krp_ref, wk_ref, wv_ref, kc_ref, vc_ref):
    ckv_b = ckv_ref[0].astype(BF16)
    kk = _dot(ckv_b, wk_ref[...])
    krp = krp_ref[0]
    for hh in range(H_A):
        blk = slice(HB * hh, HB * (hh + 1))
        kc_ref[0, :, blk] = (kk[:, blk] + krp).astype(BF16)
    vc_ref[0] = _dot(ckv_b, wv_ref[...]).astype(BF16)


def _mla_cache(ckv, krope_placed, wp):
    nb = ckv.shape[0]
    return pl.pallas_call(
        _mla_cache_kernel,
        grid=(nb,),
        in_specs=[pl.BlockSpec((1, PAST_LEN, KV_RANK), lambda b: (b, 0, 0)),
                  pl.BlockSpec((1, PAST_LEN, LANE), lambda b: (b, 0, 0)),
                  _resident((KV_RANK, 512)), _resident((KV_RANK, 256))],
        out_specs=[pl.BlockSpec((1, PAST_LEN, 512), lambda b: (b, 0, 0)),
                   pl.BlockSpec((1, PAST_LEN, 256), lambda b: (b, 0, 0))],
        out_shape=[jax.ShapeDtypeStruct((nb, PAST_LEN, 512), BF16),
                   jax.ShapeDtypeStruct((nb, PAST_LEN, 256), BF16)],
        compiler_params=_params("arbitrary"),
        name="mla_cache",
    )(ckv, krope_placed, wp["w_uk"], wp["w_uv"])


def _mla_kernel(*refs, has_cache, seqs):
    if has_cache:
        q_ref, k_ref, v_ref, kc_ref, vc_ref, o_ref = refs
    else:
        q_ref, k_ref, v_ref, o_ref = refs
    tk = k_ref.shape[0] // seqs
    head_of_lane = lax.broadcasted_iota(jnp.int32, (1, H_A * V_A), 1) // V_A
    for b in range(seqs):
        qrows = slice(b * TQ_A, (b + 1) * TQ_A)
        krows = slice(b * tk, (b + 1) * tk)
        v = v_ref[krows, :]
        acc = jnp.zeros((TQ_A, H_A * V_A), F32)
        for hh in range(H_A):
            blk = slice(HB * hh, HB * (hh + 1))
            qh = q_ref[qrows, blk]
            s = _dot_nt(qh, k_ref[krows, blk])
            m = jnp.max(s, axis=-1, keepdims=True)
            if has_cache:
                sc = _dot_nt(qh, kc_ref[0, :, blk])
                m = jnp.maximum(m, jnp.max(sc, axis=-1, keepdims=True))
            e = jnp.exp(s - m)
            den = jnp.sum(e, axis=-1, keepdims=True)
            pv = _dot(e.astype(BF16), v)
            if has_cache:
                ec = jnp.exp(sc - m)
                den = den + jnp.sum(ec, axis=-1, keepdims=True)
                pv = pv + _dot(ec.astype(BF16), vc_ref[0])
            acc = jnp.where(head_of_lane == hh, pv / den, acc)
        o_ref[qrows, :] = acc


def _mla(q, k, v, n_seq, seq_len, cache=None):
    nq = seq_len // TQ_A
    seqs = 1 if cache is not None else MLA_CTX_SEQS
    in_specs = [pl.BlockSpec((seqs * TQ_A, 512), lambda b, i: (b * nq + i, 0)),
                pl.BlockSpec((seqs * seq_len, 512), lambda b, i: (b, 0)),
                pl.BlockSpec((seqs * seq_len, 256), lambda b, i: (b, 0))]
    args = [q, k, v]
    if cache is not None:
        in_specs += [pl.BlockSpec((1, PAST_LEN, 512), lambda b, i: (b, 0, 0)),
                     pl.BlockSpec((1, PAST_LEN, 256), lambda b, i: (b, 0, 0))]
        args += list(cache)
    else:
        assert nq == 1
    return pl.pallas_call(
        functools.partial(_mla_kernel, has_cache=cache is not None, seqs=seqs),
        grid=(n_seq // seqs, nq),
        in_specs=in_specs,
        out_specs=pl.BlockSpec((seqs * TQ_A, H_A * V_A), lambda b, i: (b * nq + i, 0)),
        out_shape=jax.ShapeDtypeStruct((n_seq * seq_len, H_A * V_A), F32),
        compiler_params=_params("arbitrary", "arbitrary"),
        name="mla_attention",
    )(*args)


def _sink_softmax_pv(parts, sink):
    m = sink
    for s, _ in parts:
        m = jnp.maximum(m, jnp.max(s, axis=-1, keepdims=True))
    den = jnp.exp(sink - m)
    pv = None
    for s, v in parts:
        e = jnp.exp(s - m)
        den = den + jnp.sum(e, axis=-1, keepdims=True)
        t = _dot(e.astype(BF16), v)
        pv = t if pv is None else pv + t
    return pv / den


def _gqa_pair(q_ref, rows, g, scores_and_values, sink_ref, o_ref):
    t = rows.stop - rows.start
    blocks = [slice(HB * (G_D * g + j), HB * (G_D * g + j + 1)) for j in range(G_D)]
    q2 = jnp.concatenate([q_ref[rows, blk] for blk in blocks], axis=0)
    head = lax.broadcasted_iota(jnp.int32, (G_D * t, 1), 0) // t
    sink = sink_ref[G_D * g:G_D * g + 1, 0:1]
    for j in range(1, G_D):
        sink = jnp.where(head == j, sink_ref[G_D * g + j:G_D * g + j + 1, 0:1], sink)
    o = _sink_softmax_pv(scores_and_values(q2), sink)
    for j, blk in enumerate(blocks):
        o_ref[rows, blk] = o[j * t:(j + 1) * t, :]


def _gqa_ctx_kernel(q_ref, k_ref, v_ref, sink_ref, o_ref):
    for b in range(GQA_CTX_SEQS):
        rows = slice(b * SEQ, (b + 1) * SEQ)
        for g in range(KV_D):
            kv = slice(HB * g, HB * (g + 1))
            k = k_ref[rows, kv].astype(BF16)
            v = v_ref[rows, kv].astype(BF16)
            _gqa_pair(q_ref, rows, g, lambda q2, k=k, v=v: [(_dot_nt(q2, k), v)], sink_ref, o_ref)


def _gqa_ctx(q, k, v, sink, n_seq):
    seq = lambda w: pl.BlockSpec((GQA_CTX_SEQS * SEQ, w), lambda b: (b, 0))
    return pl.pallas_call(
        _gqa_ctx_kernel,
        grid=(n_seq // GQA_CTX_SEQS,),
        in_specs=[seq(512), seq(256), seq(256), _resident((H_D, LANE))],
        out_specs=seq(512),
        out_shape=jax.ShapeDtypeStruct((n_seq * SEQ, 512), F32),
        compiler_params=_params("arbitrary"),
        name="gqa_context",
    )(q, k, v, sink)


def _gqa_lat_kernel(q_ref, k_ref, v_ref, kc_ref, vc_ref, sink_ref, o_ref):
    span = 3 * WINDOW
    row = lax.broadcasted_iota(jnp.int32, (G_D * WINDOW, span), 0) % WINDOW
    col = lax.broadcasted_iota(jnp.int32, (G_D * WINDOW, span), 1)
    for j in range(GQA_LAT_BLOCKS):
        n = pl.program_id(1) * GQA_LAT_BLOCKS + j
        rows = slice(j * WINDOW, (j + 1) * WINDOW)
        start = pl.multiple_of(jnp.clip((n - 1) * WINDOW, 0, DEC_SEQ - span), WINDOW)
        band = jnp.abs((start + col) - (n * WINDOW + row)) <= WINDOW
        for g in range(KV_D):
            kv = slice(HB * g, HB * (g + 1))
            kw = k_ref[pl.ds(start, span), kv].astype(BF16)
            vw = v_ref[pl.ds(start, span), kv].astype(BF16)
            kc = kc_ref[0, :, kv]
            vc = vc_ref[0, :, kv]

            def parts(q2, kw=kw, vw=vw, kc=kc, vc=vc, band=band):
                return [(jnp.where(band, _dot_nt(q2, kw), NEG), vw), (_dot_nt(q2, kc), vc)]

            _gqa_pair(q_ref, rows, g, parts, sink_ref, o_ref)


def _gqa_lat(q, k, v, kc, vc, sink, n_seq):
    nb = DEC_SEQ // (WINDOW * GQA_LAT_BLOCKS)
    qblk = pl.BlockSpec((WINDOW * GQA_LAT_BLOCKS, 512), lambda b, n: (b * nb + n, 0))
    return pl.pallas_call(
        _gqa_lat_kernel,
        grid=(n_seq, nb),
        in_specs=[qblk,
                  pl.BlockSpec((DEC_SEQ, 256), lambda b, n: (b, 0)),
                  pl.BlockSpec((DEC_SEQ, 256), lambda b, n: (b, 0)),
                  pl.BlockSpec((1, PAST_LEN, 256), lambda b, n: (b, 0, 0)),
                  pl.BlockSpec((1, PAST_LEN, 256), lambda b, n: (b, 0, 0)),
                  _resident((H_D, LANE))],
        out_specs=qblk,
        out_shape=jax.ShapeDtypeStruct((n_seq * DEC_SEQ, 512), F32),
        compiler_params=_params("arbitrary", "arbitrary"),
        name="gqa_latent",
    )(q, k, v, kc, vc, sink)


def _conv_kernel(cur_ref, prev_ref, next_ref, w_ref, b_ref, o_ref, pad_ref, *, tiles_per_seq):
    i = pl.program_id(0)
    has_prev = (i % tiles_per_seq) != 0
    has_next = (i % tiles_per_seq) != tiles_per_seq - 1
    pad_ref[0:SUBLANE, :] = jnp.where(has_prev, prev_ref[...], 0.0)
    pad_ref[SUBLANE:SUBLANE + CONV_TILE, :] = cur_ref[...]
    pad_ref[SUBLANE + CONV_TILE:, :] = jnp.where(has_next, next_ref[...], 0.0)
    y = jnp.zeros((CONV_TILE, W_XBC_P), F32) + b_ref[...]
    for k in range(D_CONV):
        off = SUBLANE - D_CONV // 2 + k
        y = y + w_ref[k:k + 1, :] * pad_ref[off:off + CONV_TILE, :]
    o_ref[...] = _silu(y)


def _conv(xbc, seq_len, w, b):
    n = xbc.shape[0]
    per = CONV_TILE // SUBLANE
    last = n // SUBLANE - 1
    return pl.pallas_call(
        functools.partial(_conv_kernel, tiles_per_seq=seq_len // CONV_TILE),
        grid=(n // CONV_TILE,),
        in_specs=[pl.BlockSpec((CONV_TILE, W_XBC_P), lambda i: (i, 0)),
                  pl.BlockSpec((SUBLANE, W_XBC_P), lambda i: (jnp.maximum(i * per - 1, 0), 0)),
                  pl.BlockSpec((SUBLANE, W_XBC_P), lambda i: (jnp.minimum((i + 1) * per, last), 0)),
                  _resident((SUBLANE, W_XBC_P)), _resident((1, W_XBC_P))],
        out_specs=pl.BlockSpec((CONV_TILE, W_XBC_P), lambda i: (i, 0)),
        out_shape=jax.ShapeDtypeStruct((n, W_XBC_P), F32),
        scratch_shapes=[pltpu.VMEM((CONV_TILE + 2 * SUBLANE, W_XBC_P), F32)],
        compiler_params=_params("arbitrary"),
        name="ssd_conv",
    )(xbc, xbc, xbc, w, b)


def _ssd_kernel(*refs, has_s0, chunks):
    if has_s0:
        xf_ref, xb_ref, dtf_ref, dtb_ref, alog_ref, dsk_ref, s0_ref, yf_ref, yb_ref, st_ref, s_scr = refs
    else:
        xf_ref, xb_ref, dtf_ref, dtb_ref, alog_ref, dsk_ref, yf_ref, yb_ref, st_ref, s_scr = refs
    c = pl.program_id(1)
    q = SSD_CHUNK

    @pl.when(c == 0)
    def _():
        s_scr[...] = s0_ref[0] if has_s0 else jnp.zeros(s_scr.shape, F32)

    row = lax.broadcasted_iota(jnp.int32, (q, q), 0)
    col = lax.broadcasted_iota(jnp.int32, (q, q), 1)
    a_coef = -jnp.exp(alog_ref[...])
    for d, (x_ref, dt_ref, y_ref) in enumerate(((xf_ref, dtf_ref, yf_ref), (xb_ref, dtb_ref, yb_ref))):
        tri = (row >= col) if d == 0 else (row <= col)
        for ci in (range(chunks) if d == 0 else range(chunks - 1, -1, -1)):
            rows = slice(ci * q, (ci + 1) * q)
            dt = dt_ref[rows, :]
            cum = _prefix_dot(tri, dt * a_coef)
            cum_t = cum.T
            total = cum[q - 1:q, :] if d == 0 else cum[0:1, :]
            for g in range(G_B):
                bg = x_ref[rows, 512 + HB * g:512 + HB * (g + 1)]
                cg = x_ref[rows, 768 + HB * g:768 + HB * (g + 1)]
                cb = _dot_nt(cg.astype(BF16), bg.astype(BF16))
                for hh in range(g * (H_B // G_B), (g + 1) * (H_B // G_B)):
                    k = d * H_B + hh
                    cum_c = cum[:, k:k + 1]
                    diff = cum_c - cum_t[k:k + 1, :]
                    seg = jnp.where(tri, jnp.exp(jnp.where(tri, diff, 0.0)), 0.0)
                    xh = x_ref[rows, HB * hh:HB * (hh + 1)]
                    xdt = xh * dt[:, k:k + 1]
                    xdt_b = xdt.astype(BF16)
                    s_in = s_scr[d, hh]
                    y = _dot((cb * seg).astype(BF16), xdt_b)
                    y = y + _dot_nt((cg * jnp.exp(cum_c)).astype(BF16), s_in.astype(BF16))
                    y_ref[rows, HB * hh:HB * (hh + 1)] = y + xh * dsk_ref[k:k + 1, :]
                    tot = total[:, k:k + 1]
                    bdec = bg * jnp.exp(tot - cum_c)
                    cs = _dot(xdt.T.astype(BF16), bdec.astype(BF16))
                    s_scr[d, hh] = jnp.exp(tot) * s_in + cs

    @pl.when(c == pl.num_programs(1) - 1)
    def _():
        for d in range(2):
            for hh in range(H_B):
                st_ref[0, d, hh] = s_scr[d, hh, 0:P_B, 0:N_B]


def _ssd(xbc, dt, n_seq, seq_len, a_log, dskip, s0=None):
    chunks = min(SSD_STEP_CHUNKS, seq_len // SSD_CHUNK)
    nc = seq_len // (SSD_CHUNK * chunks)
    fwd = lambda w: pl.BlockSpec((SSD_CHUNK * chunks, w), lambda b, c: (b * nc + c, 0))
    bwd = lambda w: pl.BlockSpec((SSD_CHUNK * chunks, w), lambda b, c: (b * nc + nc - 1 - c, 0))
    state = pl.BlockSpec((1, 2, H_B, HB, HB), lambda b, c: (b, 0, 0, 0, 0))
    in_specs = [fwd(W_XBC_P), bwd(W_XBC_P), fwd(LANE), bwd(LANE), _resident((1, LANE)), _resident((2 * H_B, LANE))]
    args = [xbc, xbc, dt, dt, a_log, dskip]
    if s0 is not None:
        in_specs.append(state)
        args.append(s0)
    n = n_seq * seq_len
    return pl.pallas_call(
        functools.partial(_ssd_kernel, has_s0=s0 is not None, chunks=chunks),
        grid=(n_seq, nc),
        in_specs=in_specs,
        out_specs=[fwd(4 * HB), bwd(4 * HB), pl.BlockSpec((1, 2, H_B, P_B, N_B), lambda b, c: (b, 0, 0, 0, 0))],
        out_shape=[jax.ShapeDtypeStruct((n, 4 * HB), F32), jax.ShapeDtypeStruct((n, 4 * HB), F32),
                   jax.ShapeDtypeStruct((n_seq, 2, H_B, P_B, N_B), F32)],
        scratch_shapes=[pltpu.VMEM((2, H_B, HB, HB), F32)],
        compiler_params=_params("arbitrary", "arbitrary"),
        name="ssd_scan",
    )(*args)


def _gla_kernel(*refs, has_s0):
    if has_s0:
        (qf_ref, qb_ref, ff_ref, fb_ref, vf_ref, vb_ref, lb_ref, s0_ref, of_ref, ob_ref, st_ref, s_scr,
         o_scr) = refs
    else:
        (qf_ref, qb_ref, ff_ref, fb_ref, vf_ref, vb_ref, lb_ref, of_ref, ob_ref, st_ref, s_scr, o_scr) = refs
    c = pl.program_id(1)
    t = GLA_TILE
    ch = GLA_CHUNK
    nch = t // ch
    w = H_C * K_C

    @pl.when(c == 0)
    def _():
        s_scr[...] = s0_ref[0] if has_s0 else jnp.zeros(s_scr.shape, F32)

    row = lax.broadcasted_iota(jnp.int32, (t, t), 0)
    col = lax.broadcasted_iota(jnp.int32, (t, t), 1)
    same_chunk = (row // ch) == (col // ch)
    head_of_lane = lax.broadcasted_iota(jnp.int32, (1, w), 1) // K_C
    row_head = lax.broadcasted_iota(jnp.int32, (H_C * ch, 1), 0) // ch
    low_half = lax.broadcasted_iota(jnp.int32, (1, LANE), 1) < V_C
    zero_b = jnp.zeros((), BF16)
    for d, (q_ref, f_ref, v_ref, o_ref) in enumerate(((qf_ref, ff_ref, vf_ref, of_ref),
                                                      (qb_ref, fb_ref, vb_ref, ob_ref))):
        qv = q_ref[...]
        fr = f_ref[...]
        vv = v_ref[...]
        lb = lb_ref[d:d + 1, :]
        f = lb + (1.0 - lb) * jax.nn.sigmoid(fr)
        log_f = jnp.log(jnp.maximum(f, F_MIN))
        key = (1.0 - lb) * jax.nn.sigmoid(-fr)
        tri = same_chunk & ((col <= row) if d == 0 else (col >= row))
        cum = _prefix_dot(tri, log_f)
        cum3 = cum.reshape(nch, ch, w)
        k3 = key.reshape(nch, ch, w)
        q_dec = (qv * jnp.exp(cum)).astype(BF16)
        q_heads = [jnp.where(head_of_lane == hh, q_dec, zero_b) for hh in range(H_C)]
        v_heads = []
        for hh in range(H_C):
            tile = vv[:, LANE * (hh // 2):LANE * (hh // 2 + 1)]
            v_heads.append(jnp.where(low_half, tile if hh % 2 == 0 else pltpu.roll(tile, V_C, 1), 0.0))

        k_inv = (key * jnp.exp(-cum)).astype(BF16)
        for hh in range(H_C):
            att = jnp.where(tri, _dot_nt(q_heads[hh], k_inv), 0.0).astype(BF16)
            o_scr[:, HB * hh:HB * (hh + 1)] = _dot(att, v_heads[hh].astype(BF16))

        @pl.when(jnp.min(cum) < -GLA_SAFE_LOG_DECAY)
        def _():
            wp = H_C * HB
            head_ones = ((lax.broadcasted_iota(jnp.int32, (w, wp), 0) // K_C)
                         == (lax.broadcasted_iota(jnp.int32, (w, wp), 1) // HB)).astype(F32)
            i_in_chunk = lax.broadcasted_iota(jnp.int32, (nch, ch, w), 1)
            q3 = qv.reshape(nch, ch, w)
            v3 = jnp.concatenate(v_heads, axis=1).reshape(nch, ch, wp)
            o3 = jnp.zeros((nch, ch, wp), F32)
            for j in range(ch):
                live = (i_in_chunk >= j) if d == 0 else (i_in_chunk <= j)
                e = jnp.exp(jnp.where(live, cum3 - cum3[:, j:j + 1, :], 0.0))
                term = jnp.where(live, q3 * e * k3[:, j:j + 1, :], 0.0)
                att = _dot(term.reshape(t, w), head_ones)
                o3 = o3 + att.reshape(nch, ch, wp) * v3[:, j:j + 1, :]
            o_scr[...] = o3.reshape(t, wp)

        edge = ch - 1 if d == 0 else 0
        last3 = jnp.broadcast_to(cum3[:, edge:edge + 1, :], (nch, ch, w))
        k_dec = (k3 * jnp.exp(last3 - cum3)).reshape(t, w).astype(BF16)
        order = range(nch) if d == 0 else range(nch - 1, -1, -1)
        for cc in order:
            rows = slice(cc * ch, (cc + 1) * ch)
            st = s_scr[d]
            q4 = jnp.concatenate([qh[rows, :] for qh in q_heads], axis=0)
            r = _dot_nt(q4, st.astype(BF16))
            for hh in range(H_C):
                blk = slice(HB * hh, HB * (hh + 1))
                o_scr[rows, blk] = o_scr[rows, blk] + r[hh * ch:(hh + 1) * ch, :]
            v4 = jnp.concatenate([vh[rows, :] for vh in v_heads], axis=0)
            k4 = jnp.where(row_head == head_of_lane, jnp.concatenate([k_dec[rows, :]] * H_C, axis=0), zero_b)
            decay = jnp.exp(cum[cc * ch + edge:cc * ch + edge + 1, :])
            s_scr[d] = decay * st + _dot(v4.T.astype(BF16), k4)

        for p in range(H_C // 2):
            o_ref[:, LANE * p:LANE * (p + 1)] = (o_scr[:, HB * 2 * p:HB * (2 * p + 1)]
                                                 + pltpu.roll(o_scr[:, HB * (2 * p + 1):HB * (2 * p + 2)], V_C, 1))

    @pl.when(c == pl.num_programs(1) - 1)
    def _():
        k_idx = lax.broadcasted_iota(jnp.int32, (K_C, w), 0)
        lane_idx = lax.broadcasted_iota(jnp.int32, (K_C, w), 1)
        for d in range(2):
            rows = s_scr[d, 0:V_C, :]
            for hh in range(H_C):
                sel = (lane_idx == hh * K_C + k_idx).astype(F32)
                st_ref[0, d, hh] = lax.dot_general(sel, rows, (((1,), (1,)), ((), ())),
                                                   preferred_element_type=F32, precision=lax.Precision.HIGHEST)


def _gla(cq, cf, ci, n_seq, seq_len, lb, s0=None):
    nt = seq_len // GLA_TILE
    w = H_C * K_C
    fwd = lambda j: pl.BlockSpec((GLA_TILE, w), lambda b, c: (b * nt + c, j))
    bwd = lambda j: pl.BlockSpec((GLA_TILE, w), lambda b, c: (b * nt + nt - 1 - c, j))
    state = pl.BlockSpec((1, 2, HB, w), lambda b, c: (b, 0, 0, 0))
    in_specs = [fwd(0), bwd(0), fwd(0), bwd(1), fwd(0), bwd(0), _resident((2, w))]
    args = [cq, cq, cf, cf, ci, ci, lb]
    if s0 is not None:
        in_specs.append(state)
        args.append(s0)
    n = n_seq * seq_len
    return pl.pallas_call(
        functools.partial(_gla_kernel, has_s0=s0 is not None),
        grid=(n_seq, nt),
        in_specs=in_specs,
        out_specs=[fwd(0), bwd(0), pl.BlockSpec((1, 2, H_C, K_C, V_C), lambda b, c: (b, 0, 0, 0, 0))],
        out_shape=[jax.ShapeDtypeStruct((n, w), F32), jax.ShapeDtypeStruct((n, w), F32),
                   jax.ShapeDtypeStruct((n_seq, 2, H_C, K_C, V_C), F32)],
        scratch_shapes=[pltpu.VMEM((2, HB, w), F32), pltpu.VMEM((GLA_TILE, H_C * HB), F32)],
        compiler_params=_params("arbitrary", "arbitrary"),
        name="hgrn_scan",
    )(*args)


def _out_kernel(x_ref, mod_ref, oa_ref, yf_ref, yb_ref, bz_ref, of_ref, ob_ref, cg_ref, od_ref,
                woa_ref, wob_ref, woc_ref, wod_ref, nb_ref, nc_ref, g_ref, b_ref, o_ref):
    x = x_ref[...]
    gate = mod_ref[0, 5:6, :]
    yb = (yf_ref[...] + yb_ref[...]) * _silu(bz_ref[...])
    parts = []
    for hh in range(H_B):
        blk = yb[:, HB * hh:HB * (hh + 1)]
        ms = jnp.sum(blk * blk, axis=-1, keepdims=True) * (1.0 / P_B)
        parts.append((blk * lax.rsqrt(ms + EPS), hh))
    u = _dot(oa_ref[...].astype(BF16), woa_ref[...])
    for blk, hh in parts:
        nb = nb_ref[:, HB * hh:HB * (hh + 1)]
        u = u + _dot((blk * nb).astype(BF16), wob_ref[HB * hh:HB * (hh + 1), :])
    oc = of_ref[...] + ob_ref[...]
    lane = lax.broadcasted_iota(jnp.int32, (1, LANE), 1)
    low = lane < V_C
    halves = []
    for t in range(H_C * V_C // LANE):
        blk = oc[:, LANE * t:LANE * (t + 1)]
        sq = blk * blk
        s_all = jnp.sum(sq, axis=-1, keepdims=True)
        s_low = jnp.sum(jnp.where(low, sq, 0.0), axis=-1, keepdims=True)
        ms = jnp.where(low, s_low, s_all - s_low) * (1.0 / V_C)
        halves.append(blk * lax.rsqrt(ms + EPS))
    ocn = jnp.concatenate(halves, axis=-1) * nc_ref[...] * _silu(cg_ref[...])
    u = u + _dot(ocn.astype(BF16), woc_ref[...])
    u = u + _dot(od_ref[...].astype(BF16), wod_ref[...])
    o_ref[...] = _layer_norm(ALPHA * x + gate * u, g_ref[...], b_ref[...])


def _out_proj(x, mod, seq_len, mix, wp, ln_g, ln_b):
    n = x.shape[0]
    row = lambda w: pl.BlockSpec((TM, w), lambda i: (i, 0))
    return pl.pallas_call(
        _out_kernel,
        grid=(n // TM,),
        in_specs=[row(D_MODEL), _mod_spec(seq_len), row(256), row(512), row(512), row(512), row(256), row(256),
                  row(256), row(512),
                  _resident((256, D_MODEL)), _resident((512, D_MODEL)), _resident((256, D_MODEL)),
                  _resident((512, D_MODEL)), _resident((1, 512)), _resident((1, 256)),
                  _resident((1, D_MODEL)), _resident((1, D_MODEL))],
        out_specs=row(D_MODEL),
        out_shape=jax.ShapeDtypeStruct((n, D_MODEL), F32),
        compiler_params=_params("arbitrary"),
        name="out_proj",
    )(x, mod, mix["oa"], mix["yf"], mix["yb"], mix["bz"], mix["of"], mix["ob"], mix["cg"], mix["od"],
      wp["w_oa"], wp["w_ob"], wp["w_oc"], wp["w_od"], wp["ssd_norm"], wp["hgrn_norm"], ln_g, ln_b)


def _prep_layer(l, w_in, w_out, mla_q_norm, mla_kv_norm, mla_w_uq, mla_w_ukv, ssd_conv_w, ssd_conv_b,
                ssd_a_log, ssd_dt_bias, ssd_d, ssd_norm, hgrn_lb, hgrn_norm, gqa_sink):
    wo = w_out[l]
    conv_w = _gather_pad(ssd_conv_w[l], _IDX_CONV, 1)
    return {
        "w_in": _gather_pad(w_in[l], _IDX_W_IN, 1).astype(BF16),
        "w_uq": _gather_pad(_gather_pad(mla_w_uq[l], _IDX_UQ_ROWS, 0), _IDX_UQ_COLS, 1).astype(BF16),
        "w_uk": _gather_pad(mla_w_ukv[l], _IDX_UKV_K, 1).astype(BF16),
        "w_uv": _gather_pad(mla_w_ukv[l], _IDX_UKV_V, 1).astype(BF16),
        "g_q": _gather_pad(mla_q_norm[l], _IDX_UQ_ROWS, 0).reshape(1, 256),
        "g_kv": mla_kv_norm[l].reshape(1, KV_RANK),
        "dt_bias": jnp.pad(ssd_dt_bias[l].reshape(1, 2 * H_B), ((0, 0), (0, LANE - 2 * H_B))),
        "conv_w": jnp.pad(conv_w, ((0, SUBLANE - D_CONV), (0, 0))),
        "conv_b": _gather_pad(ssd_conv_b[l], _IDX_CONV, 0).reshape(1, W_XBC_P),
        "a_log": jnp.pad(ssd_a_log[l].reshape(1, 2 * H_B), ((0, 0), (0, LANE - 2 * H_B))),
        "d_skip": jnp.broadcast_to(ssd_d[l].reshape(2 * H_B, 1), (2 * H_B, LANE)),
        "ssd_norm": _gather_pad(ssd_norm[l], _IDX_HEAD4, 0).reshape(1, 4 * HB),
        "hgrn_lb": hgrn_lb[l],
        "hgrn_norm": hgrn_norm[l].reshape(1, H_C * V_C),
        "sink": jnp.broadcast_to(gqa_sink[l].reshape(H_D, 1), (H_D, LANE)),
        "w_oa": wo[0:256].astype(BF16),
        "w_ob": _gather_pad(wo[256:512], _IDX_HEAD4, 0).astype(BF16),
        "w_oc": wo[512:768].astype(BF16),
        "w_od": _gather_pad(wo[768:1024], _IDX_HEAD4, 0).astype(BF16),
    }


def _pad_heads(t):
    return jnp.pad(t, [(0, 0)] * (t.ndim - 1) + [(0, HB - t.shape[-1])])


def _mixer(x, mod, group_len, wp, tabs, n_seq, seq_len, ctx):
    latent = ctx is not None
    p = _in_proj(x, mod, group_len, latent, wp, tabs)
    mix = {"bz": p["bz"], "cg": p["cg"]}
    cache = _mla_cache(ctx["ckv"], ctx["krope"], wp) if latent else None
    mix["oa"] = _mla(p["qa"], p["ka"], p["va"], n_seq, seq_len, cache)
    xbc = _conv(p["bxbc"], seq_len, wp["conv_w"], wp["conv_b"])
    mix["yf"], mix["yb"], st_b = _ssd(xbc, p["bdt"], n_seq, seq_len, wp["a_log"], wp["d_skip"],
                                      ctx["ssm"] if latent else None)
    mix["of"], mix["ob"], st_c = _gla(p["cq"], p["cf"], p["ci"], n_seq, seq_len, wp["hgrn_lb"],
                                      ctx["hgrn"] if latent else None)
    if latent:
        mix["od"] = _gqa_lat(p["dq"], p["dk"], p["dv"], ctx["dk"], ctx["dv"], wp["sink"], n_seq)
    else:
        mix["od"] = _gqa_ctx(p["dq"], p["dk"], p["dv"], wp["sink"], n_seq)
    state = None if latent else (p["ckv"], p["kr"], st_b, st_c, p["dkc"], p["dvc"])
    return mix, state


def _run_stream(x, mod, n_seq, seq_len, ctx, wp, ffn_w, lng, lnb, tabs):
    group_len = x.shape[0] if ctx is None else seq_len
    x = _ffn(x, mod, group_len, *ffn_w[0], lng[0], lnb[0], sub=0)
    mix, st = _mixer(x, mod, group_len, wp, tabs, n_seq, seq_len, ctx)
    x = _out_proj(x, mod, group_len, mix, wp, lng[1], lnb[1])
    x = _ffn(x, mod, group_len, *ffn_w[1], lng[2], lnb[2], sub=2)
    return x, st


def _layer_inputs(l, ctx_tensors, weights, hgrn_lb):
    (cache_a_ckv, cache_a_krope, state_b_ssm, state_c_hgrn, cache_d_k, cache_d_v) = ctx_tensors
    (ln_g, ln_b, ffn_w_gu, ffn_w_down, w_in, w_out, mla_q_norm, mla_kv_norm, mla_w_uq, mla_w_ukv, ssd_conv_w,
     ssd_conv_b, ssd_a_log, ssd_dt_bias, ssd_d, ssd_norm, hgrn_norm, gqa_sink) = weights
    wp = _prep_layer(l, w_in, w_out, mla_q_norm, mla_kv_norm, mla_w_uq, mla_w_ukv, ssd_conv_w, ssd_conv_b,
                     ssd_a_log, ssd_dt_bias, ssd_d, ssd_norm, hgrn_lb, hgrn_norm, gqa_sink)
    ffn_w = [(ffn_w_gu.astype(BF16), ffn_w_down.astype(BF16), (l, s)) for s in range(2)]
    lng = [ln_g[l, s].reshape(1, D_MODEL) for s in range(N_SUB)]
    lnb = [ln_b[l, s].reshape(1, D_MODEL) for s in range(N_SUB)]
    nb = cache_a_ckv.shape[0]
    ctx = {
        "ckv": cache_a_ckv[:, l],
        "krope": jnp.pad(cache_a_krope[:, l], ((0, 0), (0, 0), (NOPE_A, LANE - NOPE_A - ROPE_A))),
        "ssm": jnp.pad(state_b_ssm[:, l], ((0, 0),) * 3 + ((0, HB - P_B), (0, HB - N_B))),
        "hgrn": jnp.pad(jnp.transpose(state_c_hgrn[:, l], (0, 1, 4, 2, 3)).reshape(nb, 2, V_C, H_C * K_C),
                        ((0, 0), (0, 0), (0, HB - V_C), (0, 0))),
        "dk": _pad_heads(cache_d_k[:, l]).reshape(nb, PAST_LEN, KV_D * HB).astype(BF16),
        "dv": _pad_heads(cache_d_v[:, l]).reshape(nb, PAST_LEN, KV_D * HB).astype(BF16),
    }
    return wp, ffn_w, lng, lnb, ctx


def kernel(x_prompt, x_sample, cache_a_ckv, cache_a_krope, state_b_ssm, state_c_hgrn, cache_d_k, cache_d_v,
           c, c_ctx, w_mod, b_mod, ln_g, ln_b, ffn_w_gu, ffn_w_down, w_in, w_out, mla_q_norm, mla_kv_norm,
           mla_w_uq, mla_w_ukv, ssd_conv_w, ssd_conv_b, ssd_a_log, ssd_dt_bias, ssd_d, ssd_norm,
           hgrn_lb_logits, hgrn_norm, gqa_sink):
    lb_p = jax.nn.softmax(hgrn_lb_logits.astype(F32), axis=0)
    hgrn_lb = jnp.cumsum(lb_p, axis=0) - lb_p[:1]

    cvec = jnp.concatenate([c_ctx[None], c, jnp.zeros((SUBLANE - 1 - DEC_BATCH, D_MODEL), F32)], axis=0)
    mod_all = _modulation(cvec, w_mod, b_mod)
    tabs = _rope_tables(8, NOPE_A) + _rope_tables(16, 0)
    ctx_tensors = (cache_a_ckv, cache_a_krope, state_b_ssm, state_c_hgrn, cache_d_k, cache_d_v)
    weights = (ln_g, ln_b, ffn_w_gu, ffn_w_down, w_in, w_out, mla_q_norm, mla_kv_norm, mla_w_uq, mla_w_ukv,
               ssd_conv_w, ssd_conv_b, ssd_a_log, ssd_dt_bias, ssd_d, ssd_norm, hgrn_norm, gqa_sink)

    y_p = x_prompt.reshape(BATCH * SEQ, D_MODEL)
    y_s = x_sample.reshape(DEC_BATCH * DEC_SEQ, D_MODEL)
    states = []
    for l in range(DEPTH):
        wp, ffn_w, lng, lnb, ctx = _layer_inputs(l, ctx_tensors, weights, hgrn_lb)
        mod_ctx = mod_all[l, 0:1].reshape(1, N_SUB * 3, D_MODEL)
        mod_lat = mod_all[l, 1:1 + DEC_BATCH].reshape(DEC_BATCH, N_SUB * 3, D_MODEL)
        y_p, st = _run_stream(y_p, mod_ctx, BATCH, SEQ, None, wp, ffn_w, lng, lnb, tabs)
        y_s, _ = _run_stream(y_s, mod_lat, DEC_BATCH, DEC_SEQ, ctx, wp, ffn_w, lng, lnb, tabs)
        states.append(st)

    def stack(i, f):
        return jnp.stack([f(s[i]) for s in states], axis=1)

    new_a_ckv = stack(0, lambda t: t.reshape(BATCH, SEQ, KV_RANK))
    new_a_krope = stack(1, lambda t: t.reshape(BATCH, SEQ, LANE)[..., :ROPE_A])
    new_b_ssm = stack(2, lambda t: t)
    new_c_hgrn = stack(3, lambda t: t)
    new_d_k = stack(4, lambda t: t.reshape(BATCH, SEQ, KV_D, HD_D))
    new_d_v = stack(5, lambda t: t.reshape(BATCH, SEQ, KV_D, HD_D))
    return (y_p.reshape(BATCH, SEQ, D_MODEL), y_s.reshape(DEC_BATCH, DEC_SEQ, D_MODEL),
            new_a_ckv, new_a_krope, new_b_ssm, new_c_hgrn, new_d_k, new_d_v)
```

```python
import functools

import numpy as np
import jax
import jax.numpy as jnp
from jax import lax
from jax.experimental import pallas as pl
from jax.experimental.pallas import tpu as pltpu

F32 = jnp.float32
BF16 = jnp.bfloat16

D_MODEL = 1024
BATCH = 32
SEQ = 256
DEPTH = 2
DEC_BATCH = 2
DEC_SEQ = 4096
PAST_LEN = 512
GRID_W = 64
H_A, Q_RANK, KV_RANK, NOPE_A, ROPE_A, V_A = 4, 192, 128, 64, 32, 64
H_B, P_B, G_B, N_B, D_CONV, SSD_CHUNK = 4, 64, 2, 64, 5, 128
H_C, K_C, V_C, HGRN_CHUNK = 4, 64, 64, 16
H_D, KV_D, HD_D, WINDOW = 4, 2, 64, 128
G_D = H_D // KV_D
ROPE_BASE = 10000.0
D_FF = 2816
N_SUB = 3
ALPHA = (2 * DEPTH) ** 0.25
EPS = 1e-6
F_MIN = 1e-6
LOG2_E = 1.4426950408889634
NEG = -1e30
D_IN = 2920
N_MOD = N_SUB * 3 * D_MODEL

LANE = 128
SUBLANE = 8
VMEM_LIMIT = 56 * 1024 * 1024

TM = 512
TM_FFN = 1024
FF_CHUNK = 256
TQ_A = 256
GLA_TILE = 256
GLA_CHUNK = 32
SSD_STEP_CHUNKS = 4
GQA_CTX_SEQS = 4
MLA_CTX_SEQS = 4
GQA_LAT_BLOCKS = 2
CONV_TILE = 256
MOD_TN = 1536
GLA_SAFE_LOG_DECAY = 60.0

C_ACQ, C_ACKV, C_AKR = 0, 256, 384
C_BZ, C_BXBC, C_BDT = 512, 768, 1280
C_CQ, C_CF, C_CI, C_CG = 1408, 1664, 2176, 2432
C_DQ, C_DK, C_DV = 2688, 2944, 3200
W_IN_P = 3456
W_XBC = 512
HB = 128


def _dot(a, b, precision=None):
    return jnp.dot(a, b, preferred_element_type=F32, precision=precision)


def _dot_nt(a, b):
    return lax.dot_general(a, b, (((1,), (1,)), ((), ())), preferred_element_type=F32)


def _prefix_dot(tri, x):
    t = tri.astype(BF16)
    hi = x.astype(BF16)
    rest = x - hi.astype(F32)
    mid = rest.astype(BF16)
    lo = (rest - mid.astype(F32)).astype(BF16)
    return _dot(t, hi) + _dot(t, mid) + _dot(t, lo)


def _params(*sem):
    return pltpu.CompilerParams(dimension_semantics=sem, vmem_limit_bytes=VMEM_LIMIT)


def _resident(shape, index=None):
    index = (0,) * len(shape) if index is None else index
    return pl.BlockSpec(shape, lambda *_: index, pipeline_mode=pl.Buffered(1))


def _silu(x):
    return x * jax.nn.sigmoid(x)


def _layer_norm(y, g, b):
    mu = jnp.mean(y, axis=-1, keepdims=True)
    yc = y - mu
    var = jnp.mean(yc * yc, axis=-1, keepdims=True)
    return yc * lax.rsqrt(var + EPS) * g + b


def _index_map(width, pieces):
    idx = np.full((width,), -1, np.int32)
    for dst, src, w in pieces:
        idx[dst:dst + w] = np.arange(src, src + w)
    return idx


def _gather_pad(arr, idx, axis):
    parts = []
    i = 0
    n = idx.shape[0]
    while i < n:
        j = i
        if idx[i] < 0:
            while j < n and idx[j] < 0:
                j += 1
            shape = list(arr.shape)
            shape[axis] = j - i
            parts.append(jnp.zeros(shape, arr.dtype))
        else:
            while j + 1 < n and idx[j + 1] == idx[j] + 1:
                j += 1
            j += 1
            parts.append(lax.slice_in_dim(arr, int(idx[i]), int(idx[i]) + j - i, axis=axis))
        i = j
    return jnp.concatenate(parts, axis=axis)


_IDX_W_IN = _index_map(W_IN_P, [
    (C_ACQ, 0, Q_RANK), (C_ACKV, 192, KV_RANK), (C_AKR, 320, ROPE_A),
    (C_BZ, 352, 256), (C_BXBC, 608, W_XBC), (C_BDT, 1120, 2 * H_B),
    (C_CQ, 1128, 256), (C_CF, 1384, 512), (C_CI, 1896, 256), (C_CG, 2152, 256),
    (C_DQ, 2408, 256),
    *[(c0 + LANE * g + HD_D * half, s0 + HD_D * g, HD_D)
      for c0, s0 in ((C_DK, 2664), (C_DV, 2792)) for g in range(KV_D) for half in range(2)]])
_IDX_UQ_ROWS = _index_map(256, [(0, 0, Q_RANK)])
_IDX_UQ_COLS = _index_map(4 * HB, [(HB * h, 96 * h, 96) for h in range(H_A)])
_IDX_UKV_K = _index_map(4 * HB, [(HB * h, 128 * h, NOPE_A) for h in range(H_A)])
_IDX_UKV_V = _index_map(4 * V_A, [(V_A * h, 128 * h + NOPE_A, V_A) for h in range(H_A)])


def _rope_tables(half, lane0s):
    t = np.arange(DEC_SEQ)
    pos = np.stack([t // GRID_W, t % GRID_W], 0).astype(np.float64)
    inv = ROPE_BASE ** (-np.arange(half, dtype=np.float64) / half)
    cos = np.ones((DEC_SEQ, LANE))
    sin = np.zeros((DEC_SEQ, LANE))
    for lane0 in lane0s:
        for axis in range(2):
            ang = pos[axis][:, None] * inv[None, :]
            base = lane0 + axis * 2 * half
            cos[:, base:base + half] = np.cos(ang)
            cos[:, base + half:base + 2 * half] = np.cos(ang)
            sin[:, base:base + half] = -np.sin(ang)
            sin[:, base + half:base + 2 * half] = np.sin(ang)
    ident_c = np.ones((TM, LANE))
    ident_s = np.zeros((TM, LANE))
    return (jnp.asarray(np.concatenate([ident_c, cos], 0), F32),
            jnp.asarray(np.concatenate([ident_s, sin], 0), F32))


def _rope(x, cos, sin, first, half):
    partner = jnp.where(first, pltpu.roll(x, LANE - half, 1), pltpu.roll(x, half, 1))
    return x * cos + partner * sin


def _low_lanes():
    return lax.broadcasted_iota(jnp.int32, (1, LANE), 1) < LANE // 2


def _head_alone(tile, odd):
    return jnp.where(_low_lanes(), pltpu.roll(tile, LANE // 2, 1) if odd else tile, 0.0)


def _two_heads(h_even, h_odd):
    return h_even + pltpu.roll(h_odd, LANE // 2, 1)


def _rms_heads(x, width):
    low = _low_lanes()
    tiles = []
    for t in range(width // LANE):
        blk = x[:, LANE * t:LANE * (t + 1)]
        sq = blk * blk
        s_all = jnp.sum(sq, axis=-1, keepdims=True)
        s_low = jnp.sum(jnp.where(low, sq, 0.0), axis=-1, keepdims=True)
        ms = jnp.where(low, s_low, s_all - s_low) * (2.0 / LANE)
        tiles.append(blk * lax.rsqrt(ms + EPS))
    return jnp.concatenate(tiles, axis=-1)


def _mod_kernel(c_ref, w_ref, b_ref, o_ref):
    c = c_ref[...]
    s = _silu(c).astype(BF16)
    o_ref[0] = _dot(s, w_ref[0].astype(BF16)) + b_ref[0]


def _modulation(cvec, w_mod, b_mod):
    return pl.pallas_call(
        _mod_kernel,
        grid=(DEPTH, N_MOD // MOD_TN),
        in_specs=[pl.BlockSpec((SUBLANE, D_MODEL), lambda l, j: (0, 0)),
                  pl.BlockSpec((1, D_MODEL, MOD_TN), lambda l, j: (l, 0, j)),
                  pl.BlockSpec((1, 1, MOD_TN), lambda l, j: (l, 0, j))],
        out_specs=pl.BlockSpec((1, SUBLANE, MOD_TN), lambda l, j: (l, 0, j)),
        out_shape=jax.ShapeDtypeStruct((DEPTH, SUBLANE, N_MOD), F32),
        compiler_params=_params("arbitrary", "arbitrary"),
        name="modulation",
    )(cvec, w_mod, b_mod.reshape(DEPTH, 1, N_MOD))


def _mod_spec(seq_len, tm=TM):
    return pl.BlockSpec((1, N_SUB * 3, D_MODEL), lambda i: (i * tm // seq_len, 0, 0))


def _ffn_kernel(x_ref, mod_ref, wg_ref, wu_ref, wd_ref, g_ref, b_ref, o_ref, *, sub):
    x = x_ref[...]
    shift = mod_ref[0, 3 * sub:3 * sub + 1, :]
    scale = mod_ref[0, 3 * sub + 1:3 * sub + 2, :]
    gate = mod_ref[0, 3 * sub + 2:3 * sub + 3, :]
    h = (x * (1.0 + scale) + shift).astype(BF16)
    acc = jnp.zeros((TM_FFN, D_MODEL), F32)
    for start in range(0, D_FF, FF_CHUNK):
        cols = slice(start, min(start + FF_CHUNK, D_FF))
        gt = _dot(h, wg_ref[:, cols])
        up = _dot(h, wu_ref[:, cols])
        acc = acc + _dot((_silu(gt) * up).astype(BF16), wd_ref[cols, :])
    y = ALPHA * x + 0.5 * gate * acc
    o_ref[...] = _layer_norm(y, g_ref[...], b_ref[...])


def _ffn(x, mod, seq_len, w_gu, w_down, layer_sub, ln_g, ln_b, sub):
    n = x.shape[0]
    l, s = layer_sub
    row = pl.BlockSpec((TM_FFN, D_MODEL), lambda i: (i, 0))
    return pl.pallas_call(
        functools.partial(_ffn_kernel, sub=sub),
        grid=(n // TM_FFN,),
        in_specs=[row, _mod_spec(seq_len, TM_FFN),
                  _resident((None, None, D_MODEL, D_FF), (l, s, 0, 0)),
                  _resident((None, None, D_MODEL, D_FF), (l, s, 0, 1)),
                  _resident((None, None, D_FF, D_MODEL), (l, s, 0, 0)),
                  _resident((1, D_MODEL)), _resident((1, D_MODEL))],
        out_specs=row,
        out_shape=jax.ShapeDtypeStruct((n, D_MODEL), F32),
        compiler_params=_params("arbitrary"),
        name="ffn",
    )(x, mod, w_gu, w_gu, w_down, ln_g, ln_b)


def _in_kernel(x_ref, mod_ref, w_ref, wuq_ref, wk_ref, wv_ref, gq_ref, gkv_ref, dtb_ref,
               cosq_ref, sinq_ref, cosd_ref, sind_ref, *out_refs, latent):
    out = dict(zip([name for name, _, _ in _in_outputs(latent)], out_refs))
    qa_ref, ka_ref, va_ref = out["qa"], out["ka"], out["va"]
    bz_ref, bxbc_ref, bdt_ref = out["bz"], out["bxbc"], out["bdt"]
    cq_ref, cf_ref, ci_ref, cg_ref = out["cq"], out["cf"], out["ci"], out["cg"]
    dq_ref, dk_ref, dv_ref = out["dq"], out["dk"], out["dv"]
    x = x_ref[...]
    h = (x * (1.0 + mod_ref[0, 4:5, :]) + mod_ref[0, 3:4, :]).astype(BF16)

    def proj(start, width):
        return _dot(h, w_ref[:, start:start + width])

    lane = lax.broadcasted_iota(jnp.int32, (TM, LANE), 1)
    first_a = (lane % 16) < 8
    first_d = (lane % 32) < 16

    def rope_a(blk):
        return _rope(blk, cosq_ref[...], sinq_ref[...], first_a, 8) if latent else blk

    def rope_d(blk):
        return _rope(blk, cosd_ref[...], sind_ref[...], first_d, 16) if latent else blk

    acq = proj(C_ACQ, 256)
    ms = jnp.sum(acq * acq, axis=-1, keepdims=True) * (1.0 / Q_RANK)
    qn = (acq * lax.rsqrt(ms + EPS) * gq_ref[...]).astype(BF16)
    q = _dot(qn, wuq_ref[...])
    scale_a = (NOPE_A + ROPE_A) ** -0.5 * LOG2_E
    for hh in range(H_A):
        blk = slice(HB * hh, HB * (hh + 1))
        qa_ref[:, blk] = (rope_a(q[:, blk]) * scale_a).astype(BF16)
    ackv = proj(C_ACKV, KV_RANK)
    ms = jnp.mean(ackv * ackv, axis=-1, keepdims=True)
    ckv = ackv * lax.rsqrt(ms + EPS) * gkv_ref[...]
    ckv_b = ckv.astype(BF16)
    kk = _dot(ckv_b, wk_ref[...])
    akr = proj(C_AKR, LANE)
    if not latent:
        out["ckv"][...] = ckv
        out["kr"][...] = akr
    krp = rope_a(pltpu.roll(akr, NOPE_A, 1))
    for hh in range(H_A):
        blk = slice(HB * hh, HB * (hh + 1))
        ka_ref[:, blk] = (kk[:, blk] + krp).astype(BF16)
    va_ref[...] = _dot(ckv_b, wv_ref[...]).astype(BF16)

    bz_ref[...] = proj(C_BZ, H_B * P_B)
    bxbc_ref[...] = proj(C_BXBC, W_XBC)
    dtr = proj(C_BDT, LANE) + dtb_ref[...]
    bdt_ref[...] = jnp.maximum(dtr, 0.0) + jnp.log(1.0 + jnp.exp(-jnp.abs(dtr)))

    cq_ref[...] = proj(C_CQ, 256)
    cf_ref[...] = proj(C_CF, 512)
    ci_ref[...] = proj(C_CI, 256)
    cg_ref[...] = proj(C_CG, 256)

    dq = proj(C_DQ, H_D * HD_D)
    scale_d = HD_D ** -0.5
    for t in range(H_D * HD_D // LANE):
        blk = slice(LANE * t, LANE * (t + 1))
        dq_ref[:, blk] = (rope_d(dq[:, blk]) * scale_d).astype(BF16)
    dk = proj(C_DK, KV_D * LANE)
    dk = [rope_d(dk[:, LANE * g:LANE * (g + 1)]) for g in range(KV_D)]
    for g in range(KV_D):
        dk_ref[:, LANE * g:LANE * (g + 1)] = dk[g]
    dv = proj(C_DV, KV_D * LANE)
    dv_ref[...] = dv
    if not latent:
        out["dkc"][...] = jnp.where(_low_lanes(), dk[0], dk[1])
        out["dvc"][...] = jnp.where(_low_lanes(), dv[:, 0:LANE], dv[:, LANE:2 * LANE])


def _in_outputs(latent):
    outs = [("qa", 512, BF16), ("ka", 512, BF16), ("va", 256, BF16),
            ("bz", H_B * P_B, F32), ("bxbc", W_XBC, F32), ("bdt", LANE, F32),
            ("cq", 256, F32), ("cf", 512, F32), ("ci", 256, F32), ("cg", 256, F32),
            ("dq", H_D * HD_D, BF16), ("dk", KV_D * LANE, F32), ("dv", KV_D * LANE, F32)]
    if not latent:
        outs += [("ckv", KV_RANK, F32), ("kr", LANE, F32), ("dkc", KV_D * HD_D, F32), ("dvc", KV_D * HD_D, F32)]
    return outs


def _in_proj(x, mod, group_len, latent, wp, tabs):
    n = x.shape[0]
    row = lambda w: pl.BlockSpec((TM, w), lambda i: (i, 0))
    tab = pl.BlockSpec((TM, LANE), (lambda i: (1 + i % (DEC_SEQ // TM), 0)) if latent else (lambda i: (0, 0)))
    outs = pl.pallas_call(
        functools.partial(_in_kernel, latent=latent),
        grid=(n // TM,),
        in_specs=[row(D_MODEL), _mod_spec(group_len), _resident((D_MODEL, W_IN_P)), _resident((256, 512)),
                  _resident((KV_RANK, 512)), _resident((KV_RANK, 256)), _resident((1, 256)),
                  _resident((1, KV_RANK)), _resident((1, LANE)), tab, tab, tab, tab],
        out_specs=[row(w) for _, w, _ in _in_outputs(latent)],
        out_shape=[jax.ShapeDtypeStruct((n, w), dt) for _, w, dt in _in_outputs(latent)],
        compiler_params=_params("arbitrary"),
        name="in_proj",
    )(x, mod, wp["w_in"], wp["w_uq"], wp["w_uk"], wp["w_uv"], wp["g_q"], wp["g_kv"], wp["dt_bias"],
      tabs[0], tabs[1], tabs[2], tabs[3])
    return dict(zip([k for k, _, _ in _in_outputs(latent)], outs))


def _mla_cache_kernel(ckv_ref, krp_ref, wk_ref, wv_ref, kc_ref, vc_ref):
    ckv_b = ckv_ref[0].astype(BF16)
    kk = _dot(ckv_b, wk_ref[...])
    krp = krp_ref[0]
    for hh in range(H_A):
        blk = slice(HB * hh, HB * (hh + 1))
        kc_ref[0, :, blk] = (kk[:, blk] + krp).astype(BF16)
    vc_ref[0] = _dot(ckv_b, wv_ref[...]).astype(BF16)


def _mla_cache(ckv, krope_placed, wp):
    nb = ckv.shape[0]
    return pl.pallas_call(
        _mla_cache_kernel,
        grid=(nb,),
        in_specs=[pl.BlockSpec((1, PAST_LEN, KV_RANK), lambda b: (b, 0, 0)),
                  pl.BlockSpec((1, PAST_LEN, LANE), lambda b: (b, 0, 0)),
                  _resident((KV_RANK, 512)), _resident((KV_RANK, 256))],
        out_specs=[pl.BlockSpec((1, PAST_LEN, 512), lambda b: (b, 0, 0)),
                   pl.BlockSpec((1, PAST_LEN, 256), lambda b: (b, 0, 0))],
        out_shape=[jax.ShapeDtypeStruct((nb, PAST_LEN, 512), BF16),
                   jax.ShapeDtypeStruct((nb, PAST_LEN, 256), BF16)],
        compiler_params=_params("arbitrary"),
        name="mla_cache",
    )(ckv, krope_placed, wp["w_uk"], wp["w_uv"])


def _mla_kernel(*refs, has_cache, seqs):
    if has_cache:
        q_ref, k_ref, v_ref, kc_ref, vc_ref, o_ref = refs
    else:
        q_ref, k_ref, v_ref, o_ref = refs
    tk = k_ref.shape[0] // seqs
    head_of_lane = lax.broadcasted_iota(jnp.int32, (1, H_A * V_A), 1) // V_A
    for b in range(seqs):
        qrows = slice(b * TQ_A, (b + 1) * TQ_A)
        krows = slice(b * tk, (b + 1) * tk)
        v = v_ref[krows, :]
        blocks = [slice(HB * hh, HB * (hh + 1)) for hh in range(H_A)]
        scores = [_dot_nt(q_ref[qrows, blk], k_ref[krows, blk]) for blk in blocks]
        if has_cache:
            scores_c = [_dot_nt(q_ref[qrows, blk], kc_ref[0, :, blk]) for blk in blocks]
        acc = jnp.zeros((TQ_A, H_A * V_A), F32)
        for hh in range(H_A):
            s = scores[hh]
            m = jnp.max(s, axis=-1, keepdims=True)
            if has_cache:
                sc = scores_c[hh]
                m = jnp.maximum(m, jnp.max(sc, axis=-1, keepdims=True))
            e = jnp.exp2(s - m)
            den = jnp.sum(e, axis=-1, keepdims=True)
            pv = _dot(e.astype(BF16), v)
            if has_cache:
                ec = jnp.exp2(sc - m)
                den = den + jnp.sum(ec, axis=-1, keepdims=True)
                pv = pv + _dot(ec.astype(BF16), vc_ref[0])
            acc = jnp.where(head_of_lane == hh, pv / den, acc)
        o_ref[qrows, :] = acc


def _mla(q, k, v, n_seq, seq_len, cache=None):
    nq = seq_len // TQ_A
    seqs = 1 if cache is not None else MLA_CTX_SEQS
    in_specs = [pl.BlockSpec((seqs * TQ_A, 512), lambda b, i: (b * nq + i, 0)),
                pl.BlockSpec((seqs * seq_len, 512), lambda b, i: (b, 0)),
                pl.BlockSpec((seqs * seq_len, 256), lambda b, i: (b, 0))]
    args = [q, k, v]
    if cache is not None:
        in_specs += [pl.BlockSpec((1, PAST_LEN, 512), lambda b, i: (b, 0, 0)),
                     pl.BlockSpec((1, PAST_LEN, 256), lambda b, i: (b, 0, 0))]
        args += list(cache)
    else:
        assert nq == 1
    return pl.pallas_call(
        functools.partial(_mla_kernel, has_cache=cache is not None, seqs=seqs),
        grid=(n_seq // seqs, nq),
        in_specs=in_specs,
        out_specs=pl.BlockSpec((seqs * TQ_A, H_A * V_A), lambda b, i: (b * nq + i, 0)),
        out_shape=jax.ShapeDtypeStruct((n_seq * seq_len, H_A * V_A), F32),
        compiler_params=_params("arbitrary", "arbitrary"),
        name="mla_attention",
    )(*args)


def _sink_softmax_pv(parts, sink):
    m = sink
    for s, _ in parts:
        m = jnp.maximum(m, jnp.max(s, axis=-1, keepdims=True))
    den = jnp.exp(sink - m)
    pv = None
    for s, v in parts:
        e = jnp.exp(s - m)
        den = den + jnp.sum(e, axis=-1, keepdims=True)
        t = _dot(e.astype(BF16), v)
        pv = t if pv is None else pv + t
    return pv / den


def _gqa_pair(q_ref, rows, g, scores_and_values, sink_ref, o_ref):
    t = rows.stop - rows.start
    tile = slice(LANE * g, LANE * (g + 1))
    qt = q_ref[rows, tile]
    low = _low_lanes()
    zero_b = jnp.zeros((), BF16)
    q2 = jnp.concatenate([jnp.where(low, qt, zero_b), jnp.where(low, zero_b, qt)], axis=0)
    second = lax.broadcasted_iota(jnp.int32, (G_D * t, 1), 0) >= t
    sink = jnp.where(second, sink_ref[G_D * g + 1:G_D * g + 2, 0:1], sink_ref[G_D * g:G_D * g + 1, 0:1])
    o = _sink_softmax_pv(scores_and_values(q2), sink)
    o_ref[rows, tile] = jnp.where(low, o[0:t, :], o[t:2 * t, :])


def _gqa_ctx_kernel(q_ref, k_ref, v_ref, sink_ref, o_ref):
    for b in range(GQA_CTX_SEQS):
        rows = slice(b * SEQ, (b + 1) * SEQ)
        for g in range(KV_D):
            tile = slice(LANE * g, LANE * (g + 1))
            k = k_ref[rows, tile].astype(BF16)
            v = v_ref[rows, tile].astype(BF16)
            _gqa_pair(q_ref, rows, g, lambda q2, k=k, v=v: [(_dot_nt(q2, k), v)], sink_ref, o_ref)


def _gqa_ctx(q, k, v, sink, n_seq):
    seq = lambda w: pl.BlockSpec((GQA_CTX_SEQS * SEQ, w), lambda b: (b, 0))
    return pl.pallas_call(
        _gqa_ctx_kernel,
        grid=(n_seq // GQA_CTX_SEQS,),
        in_specs=[seq(H_D * HD_D), seq(KV_D * LANE), seq(KV_D * LANE), _resident((H_D, LANE))],
        out_specs=seq(H_D * HD_D),
        out_shape=jax.ShapeDtypeStruct((n_seq * SEQ, H_D * HD_D), F32),
        compiler_params=_params("arbitrary"),
        name="gqa_context",
    )(q, k, v, sink)


def _gqa_lat_kernel(q_ref, k_ref, v_ref, kc_ref, vc_ref, sink_ref, o_ref):
    span = 3 * WINDOW
    row = lax.broadcasted_iota(jnp.int32, (G_D * WINDOW, span), 0) % WINDOW
    col = lax.broadcasted_iota(jnp.int32, (G_D * WINDOW, span), 1)
    for j in range(GQA_LAT_BLOCKS):
        n = pl.program_id(1) * GQA_LAT_BLOCKS + j
        rows = slice(j * WINDOW, (j + 1) * WINDOW)
        start = pl.multiple_of(jnp.clip((n - 1) * WINDOW, 0, DEC_SEQ - span), WINDOW)
        band = jnp.abs((start + col) - (n * WINDOW + row)) <= WINDOW
        for g in range(KV_D):
            tile = slice(LANE * g, LANE * (g + 1))
            kw = k_ref[pl.ds(start, span), tile].astype(BF16)
            vw = v_ref[pl.ds(start, span), tile].astype(BF16)
            kc = kc_ref[0, :, tile]
            vc = vc_ref[0, :, tile]

            def parts(q2, kw=kw, vw=vw, kc=kc, vc=vc, band=band):
                return [(jnp.where(band, _dot_nt(q2, kw), NEG), vw), (_dot_nt(q2, kc), vc)]

            _gqa_pair(q_ref, rows, g, parts, sink_ref, o_ref)


def _gqa_lat(q, k, v, kc, vc, sink, n_seq):
    nb = DEC_SEQ // (WINDOW * GQA_LAT_BLOCKS)
    wq, wkv = H_D * HD_D, KV_D * LANE
    qblk = pl.BlockSpec((WINDOW * GQA_LAT_BLOCKS, wq), lambda b, n: (b * nb + n, 0))
    return pl.pallas_call(
        _gqa_lat_kernel,
        grid=(n_seq, nb),
        in_specs=[qblk,
                  pl.BlockSpec((DEC_SEQ, wkv), lambda b, n: (b, 0)),
                  pl.BlockSpec((DEC_SEQ, wkv), lambda b, n: (b, 0)),
                  pl.BlockSpec((1, PAST_LEN, wkv), lambda b, n: (b, 0, 0)),
                  pl.BlockSpec((1, PAST_LEN, wkv), lambda b, n: (b, 0, 0)),
                  _resident((H_D, LANE))],
        out_specs=qblk,
        out_shape=jax.ShapeDtypeStruct((n_seq * DEC_SEQ, wq), F32),
        compiler_params=_params("arbitrary", "arbitrary"),
        name="gqa_latent",
    )(q, k, v, kc, vc, sink)


def _conv_kernel(cur_ref, prev_ref, next_ref, w_ref, b_ref, o_ref, pad_ref, *, tiles_per_seq):
    i = pl.program_id(0)
    has_prev = (i % tiles_per_seq) != 0
    has_next = (i % tiles_per_seq) != tiles_per_seq - 1
    pad_ref[0:SUBLANE, :] = jnp.where(has_prev, prev_ref[...], 0.0)
    pad_ref[SUBLANE:SUBLANE + CONV_TILE, :] = cur_ref[...]
    pad_ref[SUBLANE + CONV_TILE:, :] = jnp.where(has_next, next_ref[...], 0.0)
    y = jnp.zeros((CONV_TILE, W_XBC), F32) + b_ref[...]
    for k in range(D_CONV):
        off = SUBLANE - D_CONV // 2 + k
        y = y + w_ref[k:k + 1, :] * pad_ref[off:off + CONV_TILE, :]
    o_ref[...] = _silu(y)


def _conv(xbc, seq_len, w, b):
    n = xbc.shape[0]
    per = CONV_TILE // SUBLANE
    last = n // SUBLANE - 1
    return pl.pallas_call(
        functools.partial(_conv_kernel, tiles_per_seq=seq_len // CONV_TILE),
        grid=(n // CONV_TILE,),
        in_specs=[pl.BlockSpec((CONV_TILE, W_XBC), lambda i: (i, 0)),
                  pl.BlockSpec((SUBLANE, W_XBC), lambda i: (jnp.maximum(i * per - 1, 0), 0)),
                  pl.BlockSpec((SUBLANE, W_XBC), lambda i: (jnp.minimum((i + 1) * per, last), 0)),
                  _resident((SUBLANE, W_XBC)), _resident((1, W_XBC))],
        out_specs=pl.BlockSpec((CONV_TILE, W_XBC), lambda i: (i, 0)),
        out_shape=jax.ShapeDtypeStruct((n, W_XBC), F32),
        scratch_shapes=[pltpu.VMEM((CONV_TILE + 2 * SUBLANE, W_XBC), F32)],
        compiler_params=_params("arbitrary"),
        name="ssd_conv",
    )(xbc, xbc, xbc, w, b)


def _ssd_kernel(*refs, has_s0, chunks):
    if has_s0:
        xf_ref, xb_ref, dtf_ref, dtb_ref, alog_ref, dsk_ref, s0_ref, yf_ref, yb_ref, st_ref, s_scr = refs
    else:
        xf_ref, xb_ref, dtf_ref, dtb_ref, alog_ref, dsk_ref, yf_ref, yb_ref, st_ref, s_scr = refs
    c = pl.program_id(1)
    q = SSD_CHUNK

    @pl.when(c == 0)
    def _():
        s_scr[...] = s0_ref[0] if has_s0 else jnp.zeros(s_scr.shape, F32)

    row = lax.broadcasted_iota(jnp.int32, (q, q), 0)
    col = lax.broadcasted_iota(jnp.int32, (q, q), 1)
    a_coef = -jnp.exp(alog_ref[...])
    for d, (x_ref, dt_ref, y_ref) in enumerate(((xf_ref, dtf_ref, yf_ref), (xb_ref, dtb_ref, yb_ref))):
        tri = (row >= col) if d == 0 else (row <= col)
        for ci in (range(chunks) if d == 0 else range(chunks - 1, -1, -1)):
            rows = slice(ci * q, (ci + 1) * q)
            dt = dt_ref[rows, :]
            cum = _prefix_dot(tri, dt * a_coef)
            cum_t = cum.T
            dt_t = dt.T
            total = cum[q - 1:q, :] if d == 0 else cum[0:1, :]
            ys = []
            for g in range(G_B):
                bg = _head_alone(x_ref[rows, 2 * LANE:3 * LANE], g)
                cg = _head_alone(x_ref[rows, 3 * LANE:4 * LANE], g)
                cb = _dot_nt(cg.astype(BF16), bg.astype(BF16))
                for hh in range(g * (H_B // G_B), (g + 1) * (H_B // G_B)):
                    k = d * H_B + hh
                    cum_b = jnp.broadcast_to(cum[:, k:k + 1], (q, LANE))
                    seg = jnp.where(tri, jnp.exp(jnp.where(tri, cum_b - cum_t[k:k + 1, :], 0.0)), 0.0)
                    xh = _head_alone(x_ref[rows, LANE * (hh // 2):LANE * (hh // 2 + 1)], hh % 2)
                    dt_row = dt_t[k:k + 1, :]
                    s_in = s_scr[d, hh]
                    y = _dot((cb * seg * dt_row).astype(BF16), xh.astype(BF16))
                    y = y + _dot_nt((cg * jnp.exp(cum_b)).astype(BF16), s_in.astype(BF16))
                    ys.append(y + xh * dsk_ref[k:k + 1, :])
                    tot = total[:, k:k + 1]
                    bdec = bg * jnp.exp(tot - cum_b)
                    cs = _dot((xh.T * dt_row).astype(BF16), bdec.astype(BF16))
                    s_scr[d, hh] = jnp.exp(tot) * s_in + cs
            for p in range(H_B // 2):
                y_ref[rows, LANE * p:LANE * (p + 1)] = _two_heads(ys[2 * p], ys[2 * p + 1])

    @pl.when(c == pl.num_programs(1) - 1)
    def _():
        for d in range(2):
            for hh in range(H_B):
                st_ref[0, d, hh] = s_scr[d, hh, 0:P_B, 0:N_B]


def _ssd(xbc, dt, n_seq, seq_len, a_log, dskip, s0=None):
    chunks = min(SSD_STEP_CHUNKS, seq_len // SSD_CHUNK)
    nc = seq_len // (SSD_CHUNK * chunks)
    fwd = lambda w: pl.BlockSpec((SSD_CHUNK * chunks, w), lambda b, c: (b * nc + c, 0))
    bwd = lambda w: pl.BlockSpec((SSD_CHUNK * chunks, w), lambda b, c: (b * nc + nc - 1 - c, 0))
    state = pl.BlockSpec((1, 2, H_B, HB, HB), lambda b, c: (b, 0, 0, 0, 0))
    in_specs = [fwd(W_XBC), bwd(W_XBC), fwd(LANE), bwd(LANE), _resident((1, LANE)), _resident((2 * H_B, LANE))]
    args = [xbc, xbc, dt, dt, a_log, dskip]
    if s0 is not None:
        in_specs.append(state)
        args.append(s0)
    n = n_seq * seq_len
    return pl.pallas_call(
        functools.partial(_ssd_kernel, has_s0=s0 is not None, chunks=chunks),
        grid=(n_seq, nc),
        in_specs=in_specs,
        out_specs=[fwd(H_B * P_B), bwd(H_B * P_B),
                   pl.BlockSpec((1, 2, H_B, P_B, N_B), lambda b, c: (b, 0, 0, 0, 0))],
        out_shape=[jax.ShapeDtypeStruct((n, H_B * P_B), F32), jax.ShapeDtypeStruct((n, H_B * P_B), F32),
                   jax.ShapeDtypeStruct((n_seq, 2, H_B, P_B, N_B), F32)],
        scratch_shapes=[pltpu.VMEM((2, H_B, HB, HB), F32)],
        compiler_params=_params("arbitrary", "arbitrary"),
        name="ssd_scan",
    )(*args)


def _gla_kernel(*refs, has_s0):
    if has_s0:
        (qf_ref, qb_ref, ff_ref, fb_ref, vf_ref, vb_ref, lb_ref, s0_ref, of_ref, ob_ref, st_ref, s_scr,
         o_scr) = refs
    else:
        (qf_ref, qb_ref, ff_ref, fb_ref, vf_ref, vb_ref, lb_ref, of_ref, ob_ref, st_ref, s_scr, o_scr) = refs
    c = pl.program_id(1)
    t = GLA_TILE
    ch = GLA_CHUNK
    nch = t // ch
    w = H_C * K_C

    @pl.when(c == 0)
    def _():
        s_scr[...] = s0_ref[0] if has_s0 else jnp.zeros(s_scr.shape, F32)

    row = lax.broadcasted_iota(jnp.int32, (t, t), 0)
    col = lax.broadcasted_iota(jnp.int32, (t, t), 1)
    same_chunk = (row // ch) == (col // ch)
    head_of_lane = lax.broadcasted_iota(jnp.int32, (1, w), 1) // K_C
    row_head = lax.broadcasted_iota(jnp.int32, (H_C * ch, 1), 0) // ch
    low_half = lax.broadcasted_iota(jnp.int32, (1, LANE), 1) < V_C
    zero_b = jnp.zeros((), BF16)
    for d, (q_ref, f_ref, v_ref, o_ref) in enumerate(((qf_ref, ff_ref, vf_ref, of_ref),
                                                      (qb_ref, fb_ref, vb_ref, ob_ref))):
        qv = q_ref[...]
        fr = f_ref[...]
        vv = v_ref[...]
        lb = lb_ref[d:d + 1, :]
        f = lb + (1.0 - lb) * jax.nn.sigmoid(fr)
        log_f = jnp.log(jnp.maximum(f, F_MIN))
        key = (1.0 - lb) * jax.nn.sigmoid(-fr)
        tri = same_chunk & ((col <= row) if d == 0 else (col >= row))
        cum = _prefix_dot(tri, log_f)
        cum3 = cum.reshape(nch, ch, w)
        k3 = key.reshape(nch, ch, w)
        q_dec = (qv * jnp.exp(cum)).astype(BF16)
        q_heads = [jnp.where(head_of_lane == hh, q_dec, zero_b) for hh in range(H_C)]
        v_heads = []
        for hh in range(H_C):
            tile = vv[:, LANE * (hh // 2):LANE * (hh // 2 + 1)]
            v_heads.append(jnp.where(low_half, tile if hh % 2 == 0 else pltpu.roll(tile, V_C, 1), 0.0))

        k_inv = (key * jnp.exp(-cum)).astype(BF16)
        for hh in range(H_C):
            att = jnp.where(tri, _dot_nt(q_heads[hh], k_inv), 0.0).astype(BF16)
            o_scr[:, HB * hh:HB * (hh + 1)] = _dot(att, v_heads[hh].astype(BF16))

        @pl.when(jnp.min(cum) < -GLA_SAFE_LOG_DECAY)
        def _():
            wp = H_C * HB
            head_ones = ((lax.broadcasted_iota(jnp.int32, (w, wp), 0) // K_C)
                         == (lax.broadcasted_iota(jnp.int32, (w, wp), 1) // HB)).astype(F32)
            i_in_chunk = lax.broadcasted_iota(jnp.int32, (nch, ch, w), 1)
            q3 = qv.reshape(nch, ch, w)
            v3 = jnp.concatenate(v_heads, axis=1).reshape(nch, ch, wp)
            o3 = jnp.zeros((nch, ch, wp), F32)
            for j in range(ch):
                live = (i_in_chunk >= j) if d == 0 else (i_in_chunk <= j)
                e = jnp.exp(jnp.where(live, cum3 - cum3[:, j:j + 1, :], 0.0))
                term = jnp.where(live, q3 * e * k3[:, j:j + 1, :], 0.0)
                att = _dot(term.reshape(t, w), head_ones)
                o3 = o3 + att.reshape(nch, ch, wp) * v3[:, j:j + 1, :]
            o_scr[...] = o3.reshape(t, wp)

        edge = ch - 1 if d == 0 else 0
        last3 = jnp.broadcast_to(cum3[:, edge:edge + 1, :], (nch, ch, w))
        k_dec = (k3 * jnp.exp(last3 - cum3)).reshape(t, w).astype(BF16)
        order = range(nch) if d == 0 else range(nch - 1, -1, -1)
        for cc in order:
            rows = slice(cc * ch, (cc + 1) * ch)
            st = s_scr[d]
            q4 = jnp.concatenate([qh[rows, :] for qh in q_heads], axis=0)
            r = _dot_nt(q4, st.astype(BF16))
            for hh in range(H_C):
                blk = slice(HB * hh, HB * (hh + 1))
                o_scr[rows, blk] = o_scr[rows, blk] + r[hh * ch:(hh + 1) * ch, :]
            v4 = jnp.concatenate([vh[rows, :] for vh in v_heads], axis=0)
            k4 = jnp.where(row_head == head_of_lane, jnp.concatenate([k_dec[rows, :]] * H_C, axis=0), zero_b)
            decay = jnp.exp(cum[cc * ch + edge:cc * ch + edge + 1, :])
            s_scr[d] = decay * st + _dot(v4.T.astype(BF16), k4)

        for p in range(H_C // 2):
            o_ref[:, LANE * p:LANE * (p + 1)] = (o_scr[:, HB * 2 * p:HB * (2 * p + 1)]
                                                 + pltpu.roll(o_scr[:, HB * (2 * p + 1):HB * (2 * p + 2)], V_C, 1))

    @pl.when(c == pl.num_programs(1) - 1)
    def _():
        k_idx = lax.broadcasted_iota(jnp.int32, (K_C, w), 0)
        lane_idx = lax.broadcasted_iota(jnp.int32, (K_C, w), 1)
        for d in range(2):
            rows = s_scr[d, 0:V_C, :]
            for hh in range(H_C):
                sel = (lane_idx == hh * K_C + k_idx).astype(F32)
                st_ref[0, d, hh] = lax.dot_general(sel, rows, (((1,), (1,)), ((), ())),
                                                   preferred_element_type=F32, precision=lax.Precision.HIGHEST)


def _gla(cq, cf, ci, n_seq, seq_len, lb, s0=None):
    nt = seq_len // GLA_TILE
    w = H_C * K_C
    fwd = lambda j: pl.BlockSpec((GLA_TILE, w), lambda b, c: (b * nt + c, j))
    bwd = lambda j: pl.BlockSpec((GLA_TILE, w), lambda b, c: (b * nt + nt - 1 - c, j))
    state = pl.BlockSpec((1, 2, HB, w), lambda b, c: (b, 0, 0, 0))
    in_specs = [fwd(0), bwd(0), fwd(0), bwd(1), fwd(0), bwd(0), _resident((2, w))]
    args = [cq, cq, cf, cf, ci, ci, lb]
    if s0 is not None:
        in_specs.append(state)
        args.append(s0)
    n = n_seq * seq_len
    return pl.pallas_call(
        functools.partial(_gla_kernel, has_s0=s0 is not None),
        grid=(n_seq, nt),
        in_specs=in_specs,
        out_specs=[fwd(0), bwd(0), pl.BlockSpec((1, 2, H_C, K_C, V_C), lambda b, c: (b, 0, 0, 0, 0))],
        out_shape=[jax.ShapeDtypeStruct((n, w), F32), jax.ShapeDtypeStruct((n, w), F32),
                   jax.ShapeDtypeStruct((n_seq, 2, H_C, K_C, V_C), F32)],
        scratch_shapes=[pltpu.VMEM((2, HB, w), F32), pltpu.VMEM((GLA_TILE, H_C * HB), F32)],
        compiler_params=_params("arbitrary", "arbitrary"),
        name="hgrn_scan",
    )(*args)


def _out_kernel(x_ref, mod_ref, oa_ref, yf_ref, yb_ref, bz_ref, of_ref, ob_ref, cg_ref, od_ref,
                wo_ref, nb_ref, nc_ref, g_ref, b_ref, o_ref):
    x = x_ref[...]
    gate = mod_ref[0, 5:6, :]
    yb = _rms_heads((yf_ref[...] + yb_ref[...]) * _silu(bz_ref[...]), H_B * P_B) * nb_ref[...]
    oc = _rms_heads(of_ref[...] + ob_ref[...], H_C * V_C) * nc_ref[...] * _silu(cg_ref[...])
    mixed = jnp.concatenate([oa_ref[...], yb, oc, od_ref[...]], axis=-1)
    u = _dot(mixed.astype(BF16), wo_ref[...])
    o_ref[...] = _layer_norm(ALPHA * x + gate * u, g_ref[...], b_ref[...])


def _out_proj(x, mod, seq_len, mix, wp, ln_g, ln_b):
    n = x.shape[0]
    row = lambda w: pl.BlockSpec((TM, w), lambda i: (i, 0))
    names = ("oa", "yf", "yb", "bz", "of", "ob", "cg", "od")
    return pl.pallas_call(
        _out_kernel,
        grid=(n // TM,),
        in_specs=[row(D_MODEL), _mod_spec(seq_len)] + [row(mix[k].shape[1]) for k in names]
                 + [_resident((D_MODEL, D_MODEL)), _resident((1, H_B * P_B)), _resident((1, H_C * V_C)),
                    _resident((1, D_MODEL)), _resident((1, D_MODEL))],
        out_specs=row(D_MODEL),
        out_shape=jax.ShapeDtypeStruct((n, D_MODEL), F32),
        compiler_params=_params("arbitrary"),
        name="out_proj",
    )(x, mod, *[mix[k] for k in names], wp["w_out"], wp["ssd_norm"], wp["hgrn_norm"], ln_g, ln_b)


def _prep_layer(l, w_in, w_out, mla_q_norm, mla_kv_norm, mla_w_uq, mla_w_ukv, ssd_conv_w, ssd_conv_b,
                ssd_a_log, ssd_dt_bias, ssd_d, ssd_norm, hgrn_lb, hgrn_norm, gqa_sink):
    return {
        "w_in": _gather_pad(w_in[l], _IDX_W_IN, 1).astype(BF16),
        "w_uq": _gather_pad(_gather_pad(mla_w_uq[l], _IDX_UQ_ROWS, 0), _IDX_UQ_COLS, 1).astype(BF16),
        "w_uk": _gather_pad(mla_w_ukv[l], _IDX_UKV_K, 1).astype(BF16),
        "w_uv": _gather_pad(mla_w_ukv[l], _IDX_UKV_V, 1).astype(BF16),
        "g_q": _gather_pad(mla_q_norm[l], _IDX_UQ_ROWS, 0).reshape(1, 256),
        "g_kv": mla_kv_norm[l].reshape(1, KV_RANK),
        "dt_bias": jnp.pad(ssd_dt_bias[l].reshape(1, 2 * H_B), ((0, 0), (0, LANE - 2 * H_B))),
        "conv_w": jnp.pad(ssd_conv_w[l], ((0, SUBLANE - D_CONV), (0, 0))),
        "conv_b": ssd_conv_b[l].reshape(1, W_XBC),
        "a_log": jnp.pad(ssd_a_log[l].reshape(1, 2 * H_B), ((0, 0), (0, LANE - 2 * H_B))),
        "d_skip": jnp.broadcast_to(ssd_d[l].reshape(2 * H_B, 1), (2 * H_B, LANE)),
        "ssd_norm": ssd_norm[l].reshape(1, H_B * P_B),
        "hgrn_lb": hgrn_lb[l],
        "hgrn_norm": hgrn_norm[l].reshape(1, H_C * V_C),
        "sink": jnp.broadcast_to(gqa_sink[l].reshape(H_D, 1), (H_D, LANE)),
        "w_out": w_out[l].astype(BF16),
    }


def _mixer(x, mod, group_len, wp, tabs, n_seq, seq_len, ctx):
    latent = ctx is not None
    p = _in_proj(x, mod, group_len, latent, wp, tabs)
    mix = {"bz": p["bz"], "cg": p["cg"]}
    cache = _mla_cache(ctx["ckv"], ctx["krope"], wp) if latent else None
    mix["oa"] = _mla(p["qa"], p["ka"], p["va"], n_seq, seq_len, cache)
    xbc = _conv(p["bxbc"], seq_len, wp["conv_w"], wp["conv_b"])
    mix["yf"], mix["yb"], st_b = _ssd(xbc, p["bdt"], n_seq, seq_len, wp["a_log"], wp["d_skip"],
                                      ctx["ssm"] if latent else None)
    mix["of"], mix["ob"], st_c = _gla(p["cq"], p["cf"], p["ci"], n_seq, seq_len, wp["hgrn_lb"],
                                      ctx["hgrn"] if latent else None)
    if latent:
        mix["od"] = _gqa_lat(p["dq"], p["dk"], p["dv"], ctx["dk"], ctx["dv"], wp["sink"], n_seq)
    else:
        mix["od"] = _gqa_ctx(p["dq"], p["dk"], p["dv"], wp["sink"], n_seq)
    state = None if latent else (p["ckv"], p["kr"], st_b, st_c, p["dkc"], p["dvc"])
    return mix, state


def _run_stream(x, mod, n_seq, seq_len, ctx, wp, ffn_w, lng, lnb, tabs):
    group_len = x.shape[0] if ctx is None else seq_len
    x = _ffn(x, mod, group_len, *ffn_w[0], lng[0], lnb[0], sub=0)
    mix, st = _mixer(x, mod, group_len, wp, tabs, n_seq, seq_len, ctx)
    x = _out_proj(x, mod, group_len, mix, wp, lng[1], lnb[1])
    x = _ffn(x, mod, group_len, *ffn_w[1], lng[2], lnb[2], sub=2)
    return x, st


def _layer_inputs(l, ctx_tensors, weights, hgrn_lb):
    (cache_a_ckv, cache_a_krope, state_b_ssm, state_c_hgrn, cache_d_k, cache_d_v) = ctx_tensors
    (ln_g, ln_b, ffn_w_gu, ffn_w_down, w_in, w_out, mla_q_norm, mla_kv_norm, mla_w_uq, mla_w_ukv, ssd_conv_w,
     ssd_conv_b, ssd_a_log, ssd_dt_bias, ssd_d, ssd_norm, hgrn_norm, gqa_sink) = weights
    wp = _prep_layer(l, w_in, w_out, mla_q_norm, mla_kv_norm, mla_w_uq, mla_w_ukv, ssd_conv_w, ssd_conv_b,
                     ssd_a_log, ssd_dt_bias, ssd_d, ssd_norm, hgrn_lb, hgrn_norm, gqa_sink)
    ffn_w = [(ffn_w_gu.astype(BF16), ffn_w_down.astype(BF16), (l, s)) for s in range(2)]
    lng = [ln_g[l, s].reshape(1, D_MODEL) for s in range(N_SUB)]
    lnb = [ln_b[l, s].reshape(1, D_MODEL) for s in range(N_SUB)]
    nb = cache_a_ckv.shape[0]
    ctx = {
        "ckv": cache_a_ckv[:, l],
        "krope": jnp.pad(cache_a_krope[:, l], ((0, 0), (0, 0), (NOPE_A, LANE - NOPE_A - ROPE_A))),
        "ssm": jnp.pad(state_b_ssm[:, l], ((0, 0),) * 3 + ((0, HB - P_B), (0, HB - N_B))),
        "hgrn": jnp.pad(jnp.transpose(state_c_hgrn[:, l], (0, 1, 4, 2, 3)).reshape(nb, 2, V_C, H_C * K_C),
                        ((0, 0), (0, 0), (0, HB - V_C), (0, 0))),
        "dk": jnp.concatenate([cache_d_k[:, l]] * 2, axis=-1).reshape(nb, PAST_LEN, KV_D * LANE).astype(BF16),
        "dv": jnp.concatenate([cache_d_v[:, l]] * 2, axis=-1).reshape(nb, PAST_LEN, KV_D * LANE).astype(BF16),
    }
    return wp, ffn_w, lng, lnb, ctx


def kernel(x_prompt, x_sample, cache_a_ckv, cache_a_krope, state_b_ssm, state_c_hgrn, cache_d_k, cache_d_v,
           c, c_ctx, w_mod, b_mod, ln_g, ln_b, ffn_w_gu, ffn_w_down, w_in, w_out, mla_q_norm, mla_kv_norm,
           mla_w_uq, mla_w_ukv, ssd_conv_w, ssd_conv_b, ssd_a_log, ssd_dt_bias, ssd_d, ssd_norm,
           hgrn_lb_logits, hgrn_norm, gqa_sink):
    lb_p = jax.nn.softmax(hgrn_lb_logits.astype(F32), axis=0)
    hgrn_lb = jnp.cumsum(lb_p, axis=0) - lb_p[:1]

    cvec = jnp.concatenate([c_ctx[None], c, jnp.zeros((SUBLANE - 1 - DEC_BATCH, D_MODEL), F32)], axis=0)
    mod_all = _modulation(cvec, w_mod, b_mod)
    tabs = _rope_tables(8, (NOPE_A,)) + _rope_tables(16, (0, HD_D))
    ctx_tensors = (cache_a_ckv, cache_a_krope, state_b_ssm, state_c_hgrn, cache_d_k, cache_d_v)
    weights = (ln_g, ln_b, ffn_w_gu, ffn_w_down, w_in, w_out, mla_q_norm, mla_kv_norm, mla_w_uq, mla_w_ukv,
               ssd_conv_w, ssd_conv_b, ssd_a_log, ssd_dt_bias, ssd_d, ssd_norm, hgrn_norm, gqa_sink)

    y_p = x_prompt.reshape(BATCH * SEQ, D_MODEL)
    y_s = x_sample.reshape(DEC_BATCH * DEC_SEQ, D_MODEL)
    states = []
    for l in range(DEPTH):
        wp, ffn_w, lng, lnb, ctx = _layer_inputs(l, ctx_tensors, weights, hgrn_lb)
        mod_ctx = mod_all[l, 0:1].reshape(1, N_SUB * 3, D_MODEL)
        mod_lat = mod_all[l, 1:1 + DEC_BATCH].reshape(DEC_BATCH, N_SUB * 3, D_MODEL)
        y_p, st = _run_stream(y_p, mod_ctx, BATCH, SEQ, None, wp, ffn_w, lng, lnb, tabs)
        y_s, _ = _run_stream(y_s, mod_lat, DEC_BATCH, DEC_SEQ, ctx, wp, ffn_w, lng, lnb, tabs)
        states.append(st)

    def stack(i, f):
        return jnp.stack([f(s[i]) for s in states], axis=1)

    new_a_ckv = stack(0, lambda t: t.reshape(BATCH, SEQ, KV_RANK))
    new_a_krope = stack(1, lambda t: t.reshape(BATCH, SEQ, LANE)[..., :ROPE_A])
    new_b_ssm = stack(2, lambda t: t)
    new_c_hgrn = stack(3, lambda t: t)
    new_d_k = stack(4, lambda t: t.reshape(BATCH, SEQ, KV_D, HD_D))
    new_d_v = stack(5, lambda t: t.reshape(BATCH, SEQ, KV_D, HD_D))
    return (y_p.reshape(BATCH, SEQ, D_MODEL), y_s.reshape(DEC_BATCH, DEC_SEQ, D_MODEL),
            new_a_ckv, new_a_krope, new_b_ssm, new_c_hgrn, new_d_k, new_d_v)
```

```python
import functools

import numpy as np
import jax
import jax.numpy as jnp
from jax import lax
from jax.experimental import pallas as pl
from jax.experimental.pallas import tpu as pltpu

F32 = jnp.float32
BF16 = jnp.bfloat16

D_MODEL = 1024
BATCH = 32
SEQ = 256
DEPTH = 2
DEC_BATCH = 2
DEC_SEQ = 4096
PAST_LEN = 512
GRID_W = 64
H_A, Q_RANK, KV_RANK, NOPE_A, ROPE_A, V_A = 4, 192, 128, 64, 32, 64
H_B, P_B, G_B, N_B, D_CONV, SSD_CHUNK = 4, 64, 2, 64, 5, 128
H_C, K_C, V_C, HGRN_CHUNK = 4, 64, 64, 16
H_D, KV_D, HD_D, WINDOW = 4, 2, 64, 128
G_D = H_D // KV_D
ROPE_BASE = 10000.0
D_FF = 2816
N_SUB = 3
ALPHA = (2 * DEPTH) ** 0.25
EPS = 1e-6
F_MIN = 1e-6
LOG2_E = 1.4426950408889634
NEG = -1e30
D_IN = 2920
N_MOD = N_SUB * 3 * D_MODEL

LANE = 128
SUBLANE = 8
VMEM_LIMIT = 56 * 1024 * 1024

TM = 512
TM_FFN = 1024
FF_CHUNK = 256
TQ_A = 256
GLA_TILE = 256
GLA_CHUNK = 32
SSD_STEP_CHUNKS = 8
GQA_CTX_SEQS = 8
MLA_CTX_SEQS = 8
GQA_LAT_BLOCKS = 4
CONV_TILE = 256
MOD_TN = 1536
GLA_SAFE_LOG_DECAY = 60.0

C_ACQ, C_ACKV, C_AKR = 0, 256, 384
C_BZ, C_BXBC, C_BDT = 512, 768, 1280
C_CQ, C_CF, C_CI, C_CG = 1408, 1664, 2176, 2432
C_DQ, C_DK, C_DV = 2688, 2944, 3200
W_IN_P = 3456
W_XBC = 512
HB = 128


def _dot(a, b, precision=None):
    return jnp.dot(a, b, preferred_element_type=F32, precision=precision)


def _dot_nt(a, b):
    return lax.dot_general(a, b, (((1,), (1,)), ((), ())), preferred_element_type=F32)


def _prefix_dot(tri, x):
    t = tri.astype(BF16)
    hi = x.astype(BF16)
    rest = x - hi.astype(F32)
    mid = rest.astype(BF16)
    lo = (rest - mid.astype(F32)).astype(BF16)
    return _dot(t, hi) + _dot(t, mid) + _dot(t, lo)


def _params(*sem):
    return pltpu.CompilerParams(dimension_semantics=sem, vmem_limit_bytes=VMEM_LIMIT)


def _resident(shape, index=None):
    index = (0,) * len(shape) if index is None else index
    return pl.BlockSpec(shape, lambda *_: index, pipeline_mode=pl.Buffered(1))


def _silu(x):
    return x * jax.nn.sigmoid(x)


def _layer_norm(y, g, b):
    mu = jnp.mean(y, axis=-1, keepdims=True)
    yc = y - mu
    var = jnp.mean(yc * yc, axis=-1, keepdims=True)
    return yc * lax.rsqrt(var + EPS) * g + b


def _index_map(width, pieces):
    idx = np.full((width,), -1, np.int32)
    for dst, src, w in pieces:
        idx[dst:dst + w] = np.arange(src, src + w)
    return idx


def _gather_pad(arr, idx, axis):
    parts = []
    i = 0
    n = idx.shape[0]
    while i < n:
        j = i
        if idx[i] < 0:
            while j < n and idx[j] < 0:
                j += 1
            shape = list(arr.shape)
            shape[axis] = j - i
            parts.append(jnp.zeros(shape, arr.dtype))
        else:
            while j + 1 < n and idx[j + 1] == idx[j] + 1:
                j += 1
            j += 1
            parts.append(lax.slice_in_dim(arr, int(idx[i]), int(idx[i]) + j - i, axis=axis))
        i = j
    return jnp.concatenate(parts, axis=axis)


_IDX_W_IN = _index_map(W_IN_P, [
    (C_ACQ, 0, Q_RANK), (C_ACKV, 192, KV_RANK), (C_AKR, 320, ROPE_A),
    (C_BZ, 352, 256), (C_BXBC, 608, W_XBC), (C_BDT, 1120, 2 * H_B),
    (C_CQ, 1128, 256), (C_CF, 1384, 512), (C_CI, 1896, 256), (C_CG, 2152, 256),
    (C_DQ, 2408, 256),
    *[(c0 + LANE * g + HD_D * half, s0 + HD_D * g, HD_D)
      for c0, s0 in ((C_DK, 2664), (C_DV, 2792)) for g in range(KV_D) for half in range(2)]])
_IDX_UQ_ROWS = _index_map(256, [(0, 0, Q_RANK)])
_IDX_UQ_COLS = _index_map(4 * HB, [(HB * h, 96 * h, 96) for h in range(H_A)])
_IDX_UKV_K = _index_map(4 * HB, [(HB * h, 128 * h, NOPE_A) for h in range(H_A)])
_IDX_UKV_V = _index_map(4 * V_A, [(V_A * h, 128 * h + NOPE_A, V_A) for h in range(H_A)])


def _rope_tables(half, lane0s):
    t = np.arange(DEC_SEQ)
    pos = np.stack([t // GRID_W, t % GRID_W], 0).astype(np.float64)
    inv = ROPE_BASE ** (-np.arange(half, dtype=np.float64) / half)
    cos = np.ones((DEC_SEQ, LANE))
    sin = np.zeros((DEC_SEQ, LANE))
    for lane0 in lane0s:
        for axis in range(2):
            ang = pos[axis][:, None] * inv[None, :]
            base = lane0 + axis * 2 * half
            cos[:, base:base + half] = np.cos(ang)
            cos[:, base + half:base + 2 * half] = np.cos(ang)
            sin[:, base:base + half] = -np.sin(ang)
            sin[:, base + half:base + 2 * half] = np.sin(ang)
    ident_c = np.ones((TM, LANE))
    ident_s = np.zeros((TM, LANE))
    return (jnp.asarray(np.concatenate([ident_c, cos], 0), F32),
            jnp.asarray(np.concatenate([ident_s, sin], 0), F32))


def _rope(x, cos, sin, first, half):
    partner = jnp.where(first, pltpu.roll(x, LANE - half, 1), pltpu.roll(x, half, 1))
    return x * cos + partner * sin


def _low_lanes():
    return lax.broadcasted_iota(jnp.int32, (1, LANE), 1) < LANE // 2


def _head_alone(tile, odd):
    return jnp.where(_low_lanes(), pltpu.roll(tile, LANE // 2, 1) if odd else tile, 0.0)


def _two_heads(h_even, h_odd):
    return h_even + pltpu.roll(h_odd, LANE // 2, 1)


def _rms_heads(x, width):
    low = _low_lanes()
    tiles = []
    for t in range(width // LANE):
        blk = x[:, LANE * t:LANE * (t + 1)]
        sq = blk * blk
        s_all = jnp.sum(sq, axis=-1, keepdims=True)
        s_low = jnp.sum(jnp.where(low, sq, 0.0), axis=-1, keepdims=True)
        ms = jnp.where(low, s_low, s_all - s_low) * (2.0 / LANE)
        tiles.append(blk * lax.rsqrt(ms + EPS))
    return jnp.concatenate(tiles, axis=-1)


def _mod_kernel(c_ref, w_ref, b_ref, o_ref):
    c = c_ref[...]
    s = _silu(c).astype(BF16)
    o_ref[0] = _dot(s, w_ref[0].astype(BF16)) + b_ref[0]


def _modulation(cvec, w_mod, b_mod):
    return pl.pallas_call(
        _mod_kernel,
        grid=(DEPTH, N_MOD // MOD_TN),
        in_specs=[pl.BlockSpec((SUBLANE, D_MODEL), lambda l, j: (0, 0)),
                  pl.BlockSpec((1, D_MODEL, MOD_TN), lambda l, j: (l, 0, j)),
                  pl.BlockSpec((1, 1, MOD_TN), lambda l, j: (l, 0, j))],
        out_specs=pl.BlockSpec((1, SUBLANE, MOD_TN), lambda l, j: (l, 0, j)),
        out_shape=jax.ShapeDtypeStruct((DEPTH, SUBLANE, N_MOD), F32),
        compiler_params=_params("arbitrary", "arbitrary"),
        name="modulation",
    )(cvec, w_mod, b_mod.reshape(DEPTH, 1, N_MOD))


def _mod_spec(seq_len, tm=TM):
    return pl.BlockSpec((1, N_SUB * 3, D_MODEL), lambda i: (i * tm // seq_len, 0, 0))


def _ffn_kernel(x_ref, mod_ref, wg_ref, wu_ref, wd_ref, g_ref, b_ref, o_ref, *, sub):
    x = x_ref[...]
    shift = mod_ref[0, 3 * sub:3 * sub + 1, :]
    scale = mod_ref[0, 3 * sub + 1:3 * sub + 2, :]
    gate = mod_ref[0, 3 * sub + 2:3 * sub + 3, :]
    h = (x * (1.0 + scale) + shift).astype(BF16)
    acc = jnp.zeros((TM_FFN, D_MODEL), F32)
    for start in range(0, D_FF, FF_CHUNK):
        cols = slice(start, min(start + FF_CHUNK, D_FF))
        gt = _dot(h, wg_ref[:, cols])
        up = _dot(h, wu_ref[:, cols])
        acc = acc + _dot((_silu(gt) * up).astype(BF16), wd_ref[cols, :])
    y = ALPHA * x + 0.5 * gate * acc
    o_ref[...] = _layer_norm(y, g_ref[...], b_ref[...])


def _ffn(x, mod, seq_len, w_gu, w_down, layer_sub, ln_g, ln_b, sub):
    n = x.shape[0]
    l, s = layer_sub
    row = pl.BlockSpec((TM_FFN, D_MODEL), lambda i: (i, 0))
    return pl.pallas_call(
        functools.partial(_ffn_kernel, sub=sub),
        grid=(n // TM_FFN,),
        in_specs=[row, _mod_spec(seq_len, TM_FFN),
                  _resident((None, None, D_MODEL, D_FF), (l, s, 0, 0)),
                  _resident((None, None, D_MODEL, D_FF), (l, s, 0, 1)),
                  _resident((None, None, D_FF, D_MODEL), (l, s, 0, 0)),
                  _resident((1, D_MODEL)), _resident((1, D_MODEL))],
        out_specs=row,
        out_shape=jax.ShapeDtypeStruct((n, D_MODEL), F32),
        compiler_params=_params("arbitrary"),
        name="ffn",
    )(x, mod, w_gu, w_gu, w_down, ln_g, ln_b)


def _in_kernel(x_ref, mod_ref, w_ref, wuq_ref, wk_ref, wv_ref, gq_ref, gkv_ref, dtb_ref,
               cosq_ref, sinq_ref, cosd_ref, sind_ref, *out_refs, latent):
    out = dict(zip([name for name, _, _ in _in_outputs(latent)], out_refs))
    qa_ref, ka_ref, va_ref = out["qa"], out["ka"], out["va"]
    bz_ref, bxbc_ref, bdt_ref = out["bz"], out["bxbc"], out["bdt"]
    cq_ref, cf_ref, ci_ref, cg_ref = out["cq"], out["cf"], out["ci"], out["cg"]
    dq_ref, dk_ref, dv_ref = out["dq"], out["dk"], out["dv"]
    x = x_ref[...]
    h = (x * (1.0 + mod_ref[0, 4:5, :]) + mod_ref[0, 3:4, :]).astype(BF16)

    def proj(start, width):
        return _dot(h, w_ref[:, start:start + width])

    lane = lax.broadcasted_iota(jnp.int32, (TM, LANE), 1)
    first_a = (lane % 16) < 8
    first_d = (lane % 32) < 16

    def rope_a(blk):
        return _rope(blk, cosq_ref[...], sinq_ref[...], first_a, 8) if latent else blk

    def rope_d(blk):
        return _rope(blk, cosd_ref[...], sind_ref[...], first_d, 16) if latent else blk

    pa = proj(C_ACQ, C_BZ - C_ACQ)
    acq = pa[:, 0:256]
    ms = jnp.sum(acq * acq, axis=-1, keepdims=True) * (1.0 / Q_RANK)
    qn = (acq * lax.rsqrt(ms + EPS) * gq_ref[...]).astype(BF16)
    q = _dot(qn, wuq_ref[...])
    scale_a = (NOPE_A + ROPE_A) ** -0.5 * LOG2_E
    for hh in range(H_A):
        blk = slice(HB * hh, HB * (hh + 1))
        qa_ref[:, blk] = (rope_a(q[:, blk]) * scale_a).astype(BF16)
    ackv = pa[:, C_ACKV:C_ACKV + KV_RANK]
    ms = jnp.mean(ackv * ackv, axis=-1, keepdims=True)
    ckv = ackv * lax.rsqrt(ms + EPS) * gkv_ref[...]
    ckv_b = ckv.astype(BF16)
    kk = _dot(ckv_b, wk_ref[...])
    akr = pa[:, C_AKR:C_AKR + LANE]
    if not latent:
        out["ckv"][...] = ckv
        out["kr"][...] = akr
    krp = rope_a(pltpu.roll(akr, NOPE_A, 1))
    for hh in range(H_A):
        blk = slice(HB * hh, HB * (hh + 1))
        ka_ref[:, blk] = (kk[:, blk] + krp).astype(BF16)
    va_ref[...] = _dot(ckv_b, wv_ref[...]).astype(BF16)

    pb = proj(C_BZ, C_CQ - C_BZ)
    bz_ref[...] = pb[:, 0:H_B * P_B]
    bxbc_ref[...] = pb[:, C_BXBC - C_BZ:C_BXBC - C_BZ + W_XBC]
    dtr = pb[:, C_BDT - C_BZ:C_BDT - C_BZ + LANE] + dtb_ref[...]
    bdt_ref[...] = jnp.maximum(dtr, 0.0) + jnp.log(1.0 + jnp.exp(-jnp.abs(dtr)))

    pc = proj(C_CQ, C_DQ - C_CQ)
    cq_ref[...] = pc[:, 0:256]
    cf_ref[...] = pc[:, C_CF - C_CQ:C_CF - C_CQ + 512]
    ci_ref[...] = pc[:, C_CI - C_CQ:C_CI - C_CQ + 256]
    cg_ref[...] = pc[:, C_CG - C_CQ:C_CG - C_CQ + 256]

    pd = proj(C_DQ, W_IN_P - C_DQ)
    dq = pd[:, 0:H_D * HD_D]
    scale_d = HD_D ** -0.5
    for t in range(H_D * HD_D // LANE):
        blk = slice(LANE * t, LANE * (t + 1))
        dq_ref[:, blk] = (rope_d(dq[:, blk]) * scale_d).astype(BF16)
    dk = pd[:, C_DK - C_DQ:C_DK - C_DQ + KV_D * LANE]
    dk = [rope_d(dk[:, LANE * g:LANE * (g + 1)]) for g in range(KV_D)]
    for g in range(KV_D):
        dk_ref[:, LANE * g:LANE * (g + 1)] = dk[g]
    dv = pd[:, C_DV - C_DQ:C_DV - C_DQ + KV_D * LANE]
    dv_ref[...] = dv
    if not latent:
        out["dkc"][...] = jnp.where(_low_lanes(), dk[0], dk[1])
        out["dvc"][...] = jnp.where(_low_lanes(), dv[:, 0:LANE], dv[:, LANE:2 * LANE])


def _in_outputs(latent):
    outs = [("qa", 512, BF16), ("ka", 512, BF16), ("va", 256, BF16),
            ("bz", H_B * P_B, F32), ("bxbc", W_XBC, F32), ("bdt", LANE, F32),
            ("cq", 256, F32), ("cf", 512, F32), ("ci", 256, F32), ("cg", 256, F32),
            ("dq", H_D * HD_D, BF16), ("dk", KV_D * LANE, F32), ("dv", KV_D * LANE, F32)]
    if not latent:
        outs += [("ckv", KV_RANK, F32), ("kr", LANE, F32), ("dkc", KV_D * HD_D, F32), ("dvc", KV_D * HD_D, F32)]
    return outs


def _in_proj(x, mod, group_len, latent, wp, tabs):
    n = x.shape[0]
    row = lambda w: pl.BlockSpec((TM, w), lambda i: (i, 0))
    tab = pl.BlockSpec((TM, LANE), (lambda i: (1 + i % (DEC_SEQ // TM), 0)) if latent else (lambda i: (0, 0)))
    outs = pl.pallas_call(
        functools.partial(_in_kernel, latent=latent),
        grid=(n // TM,),
        in_specs=[row(D_MODEL), _mod_spec(group_len), _resident((D_MODEL, W_IN_P)), _resident((256, 512)),
                  _resident((KV_RANK, 512)), _resident((KV_RANK, 256)), _resident((1, 256)),
                  _resident((1, KV_RANK)), _resident((1, LANE)), tab, tab, tab, tab],
        out_specs=[row(w) for _, w, _ in _in_outputs(latent)],
        out_shape=[jax.ShapeDtypeStruct((n, w), dt) for _, w, dt in _in_outputs(latent)],
        compiler_params=_params("arbitrary"),
        name="in_proj",
    )(x, mod, wp["w_in"], wp["w_uq"], wp["w_uk"], wp["w_uv"], wp["g_q"], wp["g_kv"], wp["dt_bias"],
      tabs[0], tabs[1], tabs[2], tabs[3])
    return dict(zip([k for k, _, _ in _in_outputs(latent)], outs))


def _mla_cache_kernel(ckv_ref, krp_ref, wk_ref, wv_ref, kc_ref, vc_ref):
    ckv_b = ckv_ref[0].astype(BF16)
    kk = _dot(ckv_b, wk_ref[...])
    krp = krp_ref[0]
    for hh in range(H_A):
        blk = slice(HB * hh, HB * (hh + 1))
        kc_ref[0, :, blk] = (kk[:, blk] + krp).astype(BF16)
    vc_ref[0] = _dot(ckv_b, wv_ref[...]).astype(BF16)


def _mla_cache(ckv, krope_placed, wp):
    nb = ckv.shape[0]
    return pl.pallas_call(
        _mla_cache_kernel,
        grid=(nb,),
        in_specs=[pl.BlockSpec((1, PAST_LEN, KV_RANK), lambda b: (b, 0, 0)),
                  pl.BlockSpec((1, PAST_LEN, LANE), lambda b: (b, 0, 0)),
                  _resident((KV_RANK, 512)), _resident((KV_RANK, 256))],
        out_specs=[pl.BlockSpec((1, PAST_LEN, 512), lambda b: (b, 0, 0)),
                   pl.BlockSpec((1, PAST_LEN, 256), lambda b: (b, 0, 0))],
        out_shape=[jax.ShapeDtypeStruct((nb, PAST_LEN, 512), BF16),
                   jax.ShapeDtypeStruct((nb, PAST_LEN, 256), BF16)],
        compiler_params=_params("arbitrary"),
        name="mla_cache",
    )(ckv, krope_placed, wp["w_uk"], wp["w_uv"])


def _mla_kernel(*refs, has_cache, seqs):
    if has_cache:
        q_ref, k_ref, v_ref, kc_ref, vc_ref, o_ref = refs
    else:
        q_ref, k_ref, v_ref, o_ref = refs
    tk = k_ref.shape[0] // seqs
    tq = q_ref.shape[0] // seqs
    head_of_lane = lax.broadcasted_iota(jnp.int32, (1, H_A * V_A), 1) // V_A
    for b in range(seqs):
        qrows = slice(b * tq, (b + 1) * tq)
        krows = slice(b * tk, (b + 1) * tk)
        v = v_ref[krows, :]
        blocks = [slice(HB * hh, HB * (hh + 1)) for hh in range(H_A)]
        scores = [_dot_nt(q_ref[qrows, blk], k_ref[krows, blk]) for blk in blocks]
        if has_cache:
            scores_c = [_dot_nt(q_ref[qrows, blk], kc_ref[0, :, blk]) for blk in blocks]
        acc = jnp.zeros((tq, H_A * V_A), F32)
        for hh in range(H_A):
            s = scores[hh]
            m = jnp.max(s, axis=-1, keepdims=True)
            if has_cache:
                sc = scores_c[hh]
                m = jnp.maximum(m, jnp.max(sc, axis=-1, keepdims=True))
            e = jnp.exp2(s - m)
            den = jnp.sum(e, axis=-1, keepdims=True)
            pv = _dot(e.astype(BF16), v)
            if has_cache:
                ec = jnp.exp2(sc - m)
                den = den + jnp.sum(ec, axis=-1, keepdims=True)
                pv = pv + _dot(ec.astype(BF16), vc_ref[0])
            acc = jnp.where(head_of_lane == hh, pv / den, acc)
        o_ref[qrows, :] = acc.astype(BF16)


def _mla(q, k, v, n_seq, seq_len, cache=None):
    tq = TQ_A if cache is not None else seq_len
    nq = seq_len // tq
    seqs = 1 if cache is not None else MLA_CTX_SEQS
    in_specs = [pl.BlockSpec((seqs * tq, 512), lambda b, i: (b * nq + i, 0)),
                pl.BlockSpec((seqs * seq_len, 512), lambda b, i: (b, 0)),
                pl.BlockSpec((seqs * seq_len, 256), lambda b, i: (b, 0))]
    args = [q, k, v]
    if cache is not None:
        in_specs += [pl.BlockSpec((1, PAST_LEN, 512), lambda b, i: (b, 0, 0)),
                     pl.BlockSpec((1, PAST_LEN, 256), lambda b, i: (b, 0, 0))]
        args += list(cache)
    return pl.pallas_call(
        functools.partial(_mla_kernel, has_cache=cache is not None, seqs=seqs),
        grid=(n_seq // seqs, nq),
        in_specs=in_specs,
        out_specs=pl.BlockSpec((seqs * tq, H_A * V_A), lambda b, i: (b * nq + i, 0)),
        out_shape=jax.ShapeDtypeStruct((n_seq * seq_len, H_A * V_A), BF16),
        compiler_params=_params("arbitrary", "arbitrary"),
        name="mla_attention",
    )(*args)


def _sink_softmax_pv(parts, sink):
    m = sink
    for s, _ in parts:
        m = jnp.maximum(m, jnp.max(s, axis=-1, keepdims=True))
    den = jnp.exp(sink - m)
    pv = None
    for s, v in parts:
        e = jnp.exp(s - m)
        den = den + jnp.sum(e, axis=-1, keepdims=True)
        t = _dot(e.astype(BF16), v)
        pv = t if pv is None else pv + t
    return pv / den


def _gqa_pair(q_ref, rows, g, scores_and_values, sink_ref, o_ref):
    t = rows.stop - rows.start
    tile = slice(LANE * g, LANE * (g + 1))
    qt = q_ref[rows, tile]
    low = _low_lanes()
    zero_b = jnp.zeros((), BF16)
    q2 = jnp.concatenate([jnp.where(low, qt, zero_b), jnp.where(low, zero_b, qt)], axis=0)
    second = lax.broadcasted_iota(jnp.int32, (G_D * t, 1), 0) >= t
    sink = jnp.where(second, sink_ref[G_D * g + 1:G_D * g + 2, 0:1], sink_ref[G_D * g:G_D * g + 1, 0:1])
    o = _sink_softmax_pv(scores_and_values(q2), sink)
    o_ref[rows, tile] = jnp.where(low, o[0:t, :], o[t:2 * t, :]).astype(BF16)


def _gqa_ctx_kernel(q_ref, k_ref, v_ref, sink_ref, o_ref):
    for b in range(GQA_CTX_SEQS):
        rows = slice(b * SEQ, (b + 1) * SEQ)
        for g in range(KV_D):
            tile = slice(LANE * g, LANE * (g + 1))
            k = k_ref[rows, tile].astype(BF16)
            v = v_ref[rows, tile].astype(BF16)
            _gqa_pair(q_ref, rows, g, lambda q2, k=k, v=v: [(_dot_nt(q2, k), v)], sink_ref, o_ref)


def _gqa_ctx(q, k, v, sink, n_seq):
    seq = lambda w: pl.BlockSpec((GQA_CTX_SEQS * SEQ, w), lambda b: (b, 0))
    return pl.pallas_call(
        _gqa_ctx_kernel,
        grid=(n_seq // GQA_CTX_SEQS,),
        in_specs=[seq(H_D * HD_D), seq(KV_D * LANE), seq(KV_D * LANE), _resident((H_D, LANE))],
        out_specs=seq(H_D * HD_D),
        out_shape=jax.ShapeDtypeStruct((n_seq * SEQ, H_D * HD_D), BF16),
        compiler_params=_params("arbitrary"),
        name="gqa_context",
    )(q, k, v, sink)


def _gqa_lat_kernel(q_ref, k_ref, v_ref, kc_ref, vc_ref, sink_ref, o_ref):
    span = 3 * WINDOW
    row = lax.broadcasted_iota(jnp.int32, (G_D * WINDOW, span), 0) % WINDOW
    col = lax.broadcasted_iota(jnp.int32, (G_D * WINDOW, span), 1)
    for j in range(GQA_LAT_BLOCKS):
        n = pl.program_id(1) * GQA_LAT_BLOCKS + j
        rows = slice(j * WINDOW, (j + 1) * WINDOW)
        start = pl.multiple_of(jnp.clip((n - 1) * WINDOW, 0, DEC_SEQ - span), WINDOW)
        band = jnp.abs((start + col) - (n * WINDOW + row)) <= WINDOW
        for g in range(KV_D):
            tile = slice(LANE * g, LANE * (g + 1))
            kw = k_ref[pl.ds(start, span), tile].astype(BF16)
            vw = v_ref[pl.ds(start, span), tile].astype(BF16)
            kc = kc_ref[0, :, tile]
            vc = vc_ref[0, :, tile]

            def parts(q2, kw=kw, vw=vw, kc=kc, vc=vc, band=band):
                return [(jnp.where(band, _dot_nt(q2, kw), NEG), vw), (_dot_nt(q2, kc), vc)]

            _gqa_pair(q_ref, rows, g, parts, sink_ref, o_ref)


def _gqa_lat(q, k, v, kc, vc, sink, n_seq):
    nb = DEC_SEQ // (WINDOW * GQA_LAT_BLOCKS)
    wq, wkv = H_D * HD_D, KV_D * LANE
    qblk = pl.BlockSpec((WINDOW * GQA_LAT_BLOCKS, wq), lambda b, n: (b * nb + n, 0))
    return pl.pallas_call(
        _gqa_lat_kernel,
        grid=(n_seq, nb),
        in_specs=[qblk,
                  pl.BlockSpec((DEC_SEQ, wkv), lambda b, n: (b, 0)),
                  pl.BlockSpec((DEC_SEQ, wkv), lambda b, n: (b, 0)),
                  pl.BlockSpec((1, PAST_LEN, wkv), lambda b, n: (b, 0, 0)),
                  pl.BlockSpec((1, PAST_LEN, wkv), lambda b, n: (b, 0, 0)),
                  _resident((H_D, LANE))],
        out_specs=qblk,
        out_shape=jax.ShapeDtypeStruct((n_seq * DEC_SEQ, wq), BF16),
        compiler_params=_params("arbitrary", "arbitrary"),
        name="gqa_latent",
    )(q, k, v, kc, vc, sink)


def _conv_kernel(cur_ref, prev_ref, next_ref, w_ref, b_ref, o_ref, pad_ref, *, tiles_per_seq):
    i = pl.program_id(0)
    has_prev = (i % tiles_per_seq) != 0
    has_next = (i % tiles_per_seq) != tiles_per_seq - 1
    pad_ref[0:SUBLANE, :] = jnp.where(has_prev, prev_ref[...], 0.0)
    pad_ref[SUBLANE:SUBLANE + CONV_TILE, :] = cur_ref[...]
    pad_ref[SUBLANE + CONV_TILE:, :] = jnp.where(has_next, next_ref[...], 0.0)
    y = jnp.zeros((CONV_TILE, W_XBC), F32) + b_ref[...]
    for k in range(D_CONV):
        off = SUBLANE - D_CONV // 2 + k
        y = y + w_ref[k:k + 1, :] * pad_ref[off:off + CONV_TILE, :]
    o_ref[...] = _silu(y)


def _conv(xbc, seq_len, w, b):
    n = xbc.shape[0]
    per = CONV_TILE // SUBLANE
    last = n // SUBLANE - 1
    return pl.pallas_call(
        functools.partial(_conv_kernel, tiles_per_seq=seq_len // CONV_TILE),
        grid=(n // CONV_TILE,),
        in_specs=[pl.BlockSpec((CONV_TILE, W_XBC), lambda i: (i, 0)),
                  pl.BlockSpec((SUBLANE, W_XBC), lambda i: (jnp.maximum(i * per - 1, 0), 0)),
                  pl.BlockSpec((SUBLANE, W_XBC), lambda i: (jnp.minimum((i + 1) * per, last), 0)),
                  _resident((SUBLANE, W_XBC)), _resident((1, W_XBC))],
        out_specs=pl.BlockSpec((CONV_TILE, W_XBC), lambda i: (i, 0)),
        out_shape=jax.ShapeDtypeStruct((n, W_XBC), F32),
        scratch_shapes=[pltpu.VMEM((CONV_TILE + 2 * SUBLANE, W_XBC), F32)],
        compiler_params=_params("arbitrary"),
        name="ssd_conv",
    )(xbc, xbc, xbc, w, b)


def _ssd_kernel(*refs, has_s0, chunks):
    if has_s0:
        xf_ref, xb_ref, dtf_ref, dtb_ref, alog_ref, dsk_ref, s0_ref, yf_ref, yb_ref, st_ref, s_scr = refs
    else:
        xf_ref, xb_ref, dtf_ref, dtb_ref, alog_ref, dsk_ref, yf_ref, yb_ref, st_ref, s_scr = refs
    c = pl.program_id(1)
    q = SSD_CHUNK

    @pl.when(c == 0)
    def _():
        s_scr[...] = s0_ref[0] if has_s0 else jnp.zeros(s_scr.shape, F32)

    row = lax.broadcasted_iota(jnp.int32, (q, q), 0)
    col = lax.broadcasted_iota(jnp.int32, (q, q), 1)
    a_coef = -jnp.exp(alog_ref[...])
    for d, (x_ref, dt_ref, y_ref) in enumerate(((xf_ref, dtf_ref, yf_ref), (xb_ref, dtb_ref, yb_ref))):
        tri = (row >= col) if d == 0 else (row <= col)
        for ci in (range(chunks) if d == 0 else range(chunks - 1, -1, -1)):
            rows = slice(ci * q, (ci + 1) * q)
            dt = dt_ref[rows, :]
            cum = _prefix_dot(tri, dt * a_coef)
            cum_t = cum.T
            dt_t = dt.T
            total = cum[q - 1:q, :] if d == 0 else cum[0:1, :]
            ys = []
            for g in range(G_B):
                bg = _head_alone(x_ref[rows, 2 * LANE:3 * LANE], g)
                cg = _head_alone(x_ref[rows, 3 * LANE:4 * LANE], g)
                cb = _dot_nt(cg.astype(BF16), bg.astype(BF16))
                for hh in range(g * (H_B // G_B), (g + 1) * (H_B // G_B)):
                    k = d * H_B + hh
                    cum_b = jnp.broadcast_to(cum[:, k:k + 1], (q, LANE))
                    seg = jnp.where(tri, jnp.exp(jnp.where(tri, cum_b - cum_t[k:k + 1, :], 0.0)), 0.0)
                    xh = _head_alone(x_ref[rows, LANE * (hh // 2):LANE * (hh // 2 + 1)], hh % 2)
                    dt_row = dt_t[k:k + 1, :]
                    s_in = s_scr[d, hh]
                    y = _dot((cb * seg * dt_row).astype(BF16), xh.astype(BF16))
                    y = y + _dot_nt((cg * jnp.exp(cum_b)).astype(BF16), s_in.astype(BF16))
                    ys.append(y + xh * dsk_ref[k:k + 1, :])
                    tot = total[:, k:k + 1]
                    bdec = bg * jnp.exp(tot - cum_b)
                    cs = _dot((xh.T * dt_row).astype(BF16), bdec.astype(BF16))
                    s_scr[d, hh] = jnp.exp(tot) * s_in + cs
            for p in range(H_B // 2):
                y_ref[rows, LANE * p:LANE * (p + 1)] = _two_heads(ys[2 * p], ys[2 * p + 1])

    @pl.when(c == pl.num_programs(1) - 1)
    def _():
        for d in range(2):
            for hh in range(H_B):
                st_ref[0, d, hh] = s_scr[d, hh, 0:P_B, 0:N_B]


def _ssd(xbc, dt, n_seq, seq_len, a_log, dskip, s0=None):
    chunks = min(SSD_STEP_CHUNKS, seq_len // SSD_CHUNK)
    nc = seq_len // (SSD_CHUNK * chunks)
    fwd = lambda w: pl.BlockSpec((SSD_CHUNK * chunks, w), lambda b, c: (b * nc + c, 0))
    bwd = lambda w: pl.BlockSpec((SSD_CHUNK * chunks, w), lambda b, c: (b * nc + nc - 1 - c, 0))
    state = pl.BlockSpec((1, 2, H_B, HB, HB), lambda b, c: (b, 0, 0, 0, 0))
    in_specs = [fwd(W_XBC), bwd(W_XBC), fwd(LANE), bwd(LANE), _resident((1, LANE)), _resident((2 * H_B, LANE))]
    args = [xbc, xbc, dt, dt, a_log, dskip]
    if s0 is not None:
        in_specs.append(state)
        args.append(s0)
    n = n_seq * seq_len
    return pl.pallas_call(
        functools.partial(_ssd_kernel, has_s0=s0 is not None, chunks=chunks),
        grid=(n_seq, nc),
        in_specs=in_specs,
        out_specs=[fwd(H_B * P_B), bwd(H_B * P_B),
                   pl.BlockSpec((1, 2, H_B, P_B, N_B), lambda b, c: (b, 0, 0, 0, 0))],
        out_shape=[jax.ShapeDtypeStruct((n, H_B * P_B), F32), jax.ShapeDtypeStruct((n, H_B * P_B), F32),
                   jax.ShapeDtypeStruct((n_seq, 2, H_B, P_B, N_B), F32)],
        scratch_shapes=[pltpu.VMEM((2, H_B, HB, HB), F32)],
        compiler_params=_params("arbitrary", "arbitrary"),
        name="ssd_scan",
    )(*args)


def _gla_kernel(*refs, has_s0):
    if has_s0:
        (qf_ref, qb_ref, ff_ref, fb_ref, vf_ref, vb_ref, lb_ref, s0_ref, of_ref, ob_ref, st_ref, s_scr,
         o_scr) = refs
    else:
        (qf_ref, qb_ref, ff_ref, fb_ref, vf_ref, vb_ref, lb_ref, of_ref, ob_ref, st_ref, s_scr, o_scr) = refs
    c = pl.program_id(1)
    t = GLA_TILE
    ch = GLA_CHUNK
    nch = t // ch
    w = H_C * K_C

    @pl.when(c == 0)
    def _():
        s_scr[...] = s0_ref[0] if has_s0 else jnp.zeros(s_scr.shape, F32)

    row = lax.broadcasted_iota(jnp.int32, (t, t), 0)
    col = lax.broadcasted_iota(jnp.int32, (t, t), 1)
    same_chunk = (row // ch) == (col // ch)
    head_of_lane = lax.broadcasted_iota(jnp.int32, (1, w), 1) // K_C
    row_head = lax.broadcasted_iota(jnp.int32, (H_C * ch, 1), 0) // ch
    low_half = lax.broadcasted_iota(jnp.int32, (1, LANE), 1) < V_C
    zero_b = jnp.zeros((), BF16)
    for d, (q_ref, f_ref, v_ref, o_ref) in enumerate(((qf_ref, ff_ref, vf_ref, of_ref),
                                                      (qb_ref, fb_ref, vb_ref, ob_ref))):
        qv = q_ref[...]
        fr = f_ref[...]
        vv = v_ref[...]
        lb = lb_ref[d:d + 1, :]
        f = lb + (1.0 - lb) * jax.nn.sigmoid(fr)
        log_f = jnp.log(jnp.maximum(f, F_MIN))
        key = (1.0 - lb) * jax.nn.sigmoid(-fr)
        tri = same_chunk & ((col <= row) if d == 0 else (col >= row))
        cum = _prefix_dot(tri, log_f)
        cum3 = cum.reshape(nch, ch, w)
        k3 = key.reshape(nch, ch, w)
        q_dec = (qv * jnp.exp(cum)).astype(BF16)
        q_heads = [jnp.where(head_of_lane == hh, q_dec, zero_b) for hh in range(H_C)]
        v_heads = []
        for hh in range(H_C):
            tile = vv[:, LANE * (hh // 2):LANE * (hh // 2 + 1)]
            v_heads.append(jnp.where(low_half, tile if hh % 2 == 0 else pltpu.roll(tile, V_C, 1), 0.0))

        k_inv = (key * jnp.exp(-cum)).astype(BF16)
        for hh in range(H_C):
            att = jnp.where(tri, _dot_nt(q_heads[hh], k_inv), 0.0).astype(BF16)
            o_scr[:, HB * hh:HB * (hh + 1)] = _dot(att, v_heads[hh].astype(BF16))

        @pl.when(jnp.min(cum) < -GLA_SAFE_LOG_DECAY)
        def _():
            wp = H_C * HB
            head_ones = ((lax.broadcasted_iota(jnp.int32, (w, wp), 0) // K_C)
                         == (lax.broadcasted_iota(jnp.int32, (w, wp), 1) // HB)).astype(F32)
            i_in_chunk = lax.broadcasted_iota(jnp.int32, (nch, ch, w), 1)
            q3 = qv.reshape(nch, ch, w)
            v3 = jnp.concatenate(v_heads, axis=1).reshape(nch, ch, wp)
            o3 = jnp.zeros((nch, ch, wp), F32)
            for j in range(ch):
                live = (i_in_chunk >= j) if d == 0 else (i_in_chunk <= j)
                e = jnp.exp(jnp.where(live, cum3 - cum3[:, j:j + 1, :], 0.0))
                term = jnp.where(live, q3 * e * k3[:, j:j + 1, :], 0.0)
                att = _dot(term.reshape(t, w), head_ones)
                o3 = o3 + att.reshape(nch, ch, wp) * v3[:, j:j + 1, :]
            o_scr[...] = o3.reshape(t, wp)

        edge = ch - 1 if d == 0 else 0
        last3 = jnp.broadcast_to(cum3[:, edge:edge + 1, :], (nch, ch, w))
        k_dec = (k3 * jnp.exp(last3 - cum3)).reshape(t, w).astype(BF16)
        order = range(nch) if d == 0 else range(nch - 1, -1, -1)
        for cc in order:
            rows = slice(cc * ch, (cc + 1) * ch)
            st = s_scr[d]
            q4 = jnp.concatenate([qh[rows, :] for qh in q_heads], axis=0)
            r = _dot_nt(q4, st.astype(BF16))
            for hh in range(H_C):
                blk = slice(HB * hh, HB * (hh + 1))
                o_scr[rows, blk] = o_scr[rows, blk] + r[hh * ch:(hh + 1) * ch, :]
            v4 = jnp.concatenate([vh[rows, :] for vh in v_heads], axis=0)
            k4 = jnp.where(row_head == head_of_lane, jnp.concatenate([k_dec[rows, :]] * H_C, axis=0), zero_b)
            decay = jnp.exp(cum[cc * ch + edge:cc * ch + edge + 1, :])
            s_scr[d] = decay * st + _dot(v4.T.astype(BF16), k4)

        for p in range(H_C // 2):
            o_ref[:, LANE * p:LANE * (p + 1)] = (o_scr[:, HB * 2 * p:HB * (2 * p + 1)]
                                                 + pltpu.roll(o_scr[:, HB * (2 * p + 1):HB * (2 * p + 2)], V_C, 1))

    @pl.when(c == pl.num_programs(1) - 1)
    def _():
        k_idx = lax.broadcasted_iota(jnp.int32, (K_C, w), 0)
        lane_idx = lax.broadcasted_iota(jnp.int32, (K_C, w), 1)
        for d in range(2):
            rows = s_scr[d, 0:V_C, :]
            for hh in range(H_C):
                sel = (lane_idx == hh * K_C + k_idx).astype(F32)
                st_ref[0, d, hh] = lax.dot_general(sel, rows, (((1,), (1,)), ((), ())),
                                                   preferred_element_type=F32, precision=lax.Precision.HIGHEST)


def _gla(cq, cf, ci, n_seq, seq_len, lb, s0=None):
    nt = seq_len // GLA_TILE
    w = H_C * K_C
    fwd = lambda j: pl.BlockSpec((GLA_TILE, w), lambda b, c: (b * nt + c, j))
    bwd = lambda j: pl.BlockSpec((GLA_TILE, w), lambda b, c: (b * nt + nt - 1 - c, j))
    state = pl.BlockSpec((1, 2, HB, w), lambda b, c: (b, 0, 0, 0))
    in_specs = [fwd(0), bwd(0), fwd(0), bwd(1), fwd(0), bwd(0), _resident((2, w))]
    args = [cq, cq, cf, cf, ci, ci, lb]
    if s0 is not None:
        in_specs.append(state)
        args.append(s0)
    n = n_seq * seq_len
    return pl.pallas_call(
        functools.partial(_gla_kernel, has_s0=s0 is not None),
        grid=(n_seq, nt),
        in_specs=in_specs,
        out_specs=[fwd(0), bwd(0), pl.BlockSpec((1, 2, H_C, K_C, V_C), lambda b, c: (b, 0, 0, 0, 0))],
        out_shape=[jax.ShapeDtypeStruct((n, w), F32), jax.ShapeDtypeStruct((n, w), F32),
                   jax.ShapeDtypeStruct((n_seq, 2, H_C, K_C, V_C), F32)],
        scratch_shapes=[pltpu.VMEM((2, HB, w), F32), pltpu.VMEM((GLA_TILE, H_C * HB), F32)],
        compiler_params=_params("arbitrary", "arbitrary"),
        name="hgrn_scan",
    )(*args)


def _out_kernel(x_ref, mod_ref, oa_ref, yf_ref, yb_ref, bz_ref, of_ref, ob_ref, cg_ref, od_ref,
                wo_ref, nb_ref, nc_ref, g_ref, b_ref, o_ref):
    x = x_ref[...]
    gate = mod_ref[0, 5:6, :]
    yb = _rms_heads((yf_ref[...] + yb_ref[...]) * _silu(bz_ref[...]), H_B * P_B) * nb_ref[...]
    oc = _rms_heads(of_ref[...] + ob_ref[...], H_C * V_C) * nc_ref[...] * _silu(cg_ref[...])
    mixed = jnp.concatenate([oa_ref[...], yb.astype(BF16), oc.astype(BF16), od_ref[...]], axis=-1)
    u = _dot(mixed, wo_ref[...])
    o_ref[...] = _layer_norm(ALPHA * x + gate * u, g_ref[...], b_ref[...])


def _out_proj(x, mod, seq_len, mix, wp, ln_g, ln_b):
    n = x.shape[0]
    row = lambda w: pl.BlockSpec((TM, w), lambda i: (i, 0))
    names = ("oa", "yf", "yb", "bz", "of", "ob", "cg", "od")
    return pl.pallas_call(
        _out_kernel,
        grid=(n // TM,),
        in_specs=[row(D_MODEL), _mod_spec(seq_len)] + [row(mix[k].shape[1]) for k in names]
                 + [_resident((D_MODEL, D_MODEL)), _resident((1, H_B * P_B)), _resident((1, H_C * V_C)),
                    _resident((1, D_MODEL)), _resident((1, D_MODEL))],
        out_specs=row(D_MODEL),
        out_shape=jax.ShapeDtypeStruct((n, D_MODEL), F32),
        compiler_params=_params("arbitrary"),
        name="out_proj",
    )(x, mod, *[mix[k] for k in names], wp["w_out"], wp["ssd_norm"], wp["hgrn_norm"], ln_g, ln_b)


def _prep_layer(l, w_in, w_out, mla_q_norm, mla_kv_norm, mla_w_uq, mla_w_ukv, ssd_conv_w, ssd_conv_b,
                ssd_a_log, ssd_dt_bias, ssd_d, ssd_norm, hgrn_lb, hgrn_norm, gqa_sink):
    return {
        "w_in": _gather_pad(w_in[l], _IDX_W_IN, 1).astype(BF16),
        "w_uq": _gather_pad(_gather_pad(mla_w_uq[l], _IDX_UQ_ROWS, 0), _IDX_UQ_COLS, 1).astype(BF16),
        "w_uk": _gather_pad(mla_w_ukv[l], _IDX_UKV_K, 1).astype(BF16),
        "w_uv": _gather_pad(mla_w_ukv[l], _IDX_UKV_V, 1).astype(BF16),
        "g_q": _gather_pad(mla_q_norm[l], _IDX_UQ_ROWS, 0).reshape(1, 256),
        "g_kv": mla_kv_norm[l].reshape(1, KV_RANK),
        "dt_bias": jnp.pad(ssd_dt_bias[l].reshape(1, 2 * H_B), ((0, 0), (0, LANE - 2 * H_B))),
        "conv_w": jnp.pad(ssd_conv_w[l], ((0, SUBLANE - D_CONV), (0, 0))),
        "conv_b": ssd_conv_b[l].reshape(1, W_XBC),
        "a_log": jnp.pad(ssd_a_log[l].reshape(1, 2 * H_B), ((0, 0), (0, LANE - 2 * H_B))),
        "d_skip": jnp.broadcast_to(ssd_d[l].reshape(2 * H_B, 1), (2 * H_B, LANE)),
        "ssd_norm": ssd_norm[l].reshape(1, H_B * P_B),
        "hgrn_lb": hgrn_lb[l],
        "hgrn_norm": hgrn_norm[l].reshape(1, H_C * V_C),
        "sink": jnp.broadcast_to(gqa_sink[l].reshape(H_D, 1), (H_D, LANE)),
        "w_out": w_out[l].astype(BF16),
    }


def _mixer(x, mod, group_len, wp, tabs, n_seq, seq_len, ctx):
    latent = ctx is not None
    p = _in_proj(x, mod, group_len, latent, wp, tabs)
    mix = {"bz": p["bz"], "cg": p["cg"]}
    cache = _mla_cache(ctx["ckv"], ctx["krope"], wp) if latent else None
    mix["oa"] = _mla(p["qa"], p["ka"], p["va"], n_seq, seq_len, cache)
    xbc = _conv(p["bxbc"], seq_len, wp["conv_w"], wp["conv_b"])
    mix["yf"], mix["yb"], st_b = _ssd(xbc, p["bdt"], n_seq, seq_len, wp["a_log"], wp["d_skip"],
                                      ctx["ssm"] if latent else None)
    mix["of"], mix["ob"], st_c = _gla(p["cq"], p["cf"], p["ci"], n_seq, seq_len, wp["hgrn_lb"],
                                      ctx["hgrn"] if latent else None)
    if latent:
        mix["od"] = _gqa_lat(p["dq"], p["dk"], p["dv"], ctx["dk"], ctx["dv"], wp["sink"], n_seq)
    else:
        mix["od"] = _gqa_ctx(p["dq"], p["dk"], p["dv"], wp["sink"], n_seq)
    state = None if latent else (p["ckv"], p["kr"], st_b, st_c, p["dkc"], p["dvc"])
    return mix, state


def _run_stream(x, mod, n_seq, seq_len, ctx, wp, ffn_w, lng, lnb, tabs):
    group_len = x.shape[0] if ctx is None else seq_len
    x = _ffn(x, mod, group_len, *ffn_w[0], lng[0], lnb[0], sub=0)
    mix, st = _mixer(x, mod, group_len, wp, tabs, n_seq, seq_len, ctx)
    x = _out_proj(x, mod, group_len, mix, wp, lng[1], lnb[1])
    x = _ffn(x, mod, group_len, *ffn_w[1], lng[2], lnb[2], sub=2)
    return x, st


def _layer_inputs(l, ctx_tensors, weights, hgrn_lb):
    (cache_a_ckv, cache_a_krope, state_b_ssm, state_c_hgrn, cache_d_k, cache_d_v) = ctx_tensors
    (ln_g, ln_b, ffn_w_gu, ffn_w_down, w_in, w_out, mla_q_norm, mla_kv_norm, mla_w_uq, mla_w_ukv, ssd_conv_w,
     ssd_conv_b, ssd_a_log, ssd_dt_bias, ssd_d, ssd_norm, hgrn_norm, gqa_sink) = weights
    wp = _prep_layer(l, w_in, w_out, mla_q_norm, mla_kv_norm, mla_w_uq, mla_w_ukv, ssd_conv_w, ssd_conv_b,
                     ssd_a_log, ssd_dt_bias, ssd_d, ssd_norm, hgrn_lb, hgrn_norm, gqa_sink)
    ffn_w = [(ffn_w_gu.astype(BF16), ffn_w_down.astype(BF16), (l, s)) for s in range(2)]
    lng = [ln_g[l, s].reshape(1, D_MODEL) for s in range(N_SUB)]
    lnb = [ln_b[l, s].reshape(1, D_MODEL) for s in range(N_SUB)]
    nb = cache_a_ckv.shape[0]
    ctx = {
        "ckv": cache_a_ckv[:, l],
        "krope": jnp.pad(cache_a_krope[:, l], ((0, 0), (0, 0), (NOPE_A, LANE - NOPE_A - ROPE_A))),
        "ssm": jnp.pad(state_b_ssm[:, l], ((0, 0),) * 3 + ((0, HB - P_B), (0, HB - N_B))),
        "hgrn": jnp.pad(jnp.transpose(state_c_hgrn[:, l], (0, 1, 4, 2, 3)).reshape(nb, 2, V_C, H_C * K_C),
                        ((0, 0), (0, 0), (0, HB - V_C), (0, 0))),
        "dk": jnp.concatenate([cache_d_k[:, l]] * 2, axis=-1).reshape(nb, PAST_LEN, KV_D * LANE).astype(BF16),
        "dv": jnp.concatenate([cache_d_v[:, l]] * 2, axis=-1).reshape(nb, PAST_LEN, KV_D * LANE).astype(BF16),
    }
    return wp, ffn_w, lng, lnb, ctx


def kernel(x_prompt, x_sample, cache_a_ckv, cache_a_krope, state_b_ssm, state_c_hgrn, cache_d_k, cache_d_v,
           c, c_ctx, w_mod, b_mod, ln_g, ln_b, ffn_w_gu, ffn_w_down, w_in, w_out, mla_q_norm, mla_kv_norm,
           mla_w_uq, mla_w_ukv, ssd_conv_w, ssd_conv_b, ssd_a_log, ssd_dt_bias, ssd_d, ssd_norm,
           hgrn_lb_logits, hgrn_norm, gqa_sink):
    lb_p = jax.nn.softmax(hgrn_lb_logits.astype(F32), axis=0)
    hgrn_lb = jnp.cumsum(lb_p, axis=0) - lb_p[:1]

    cvec = jnp.concatenate([c_ctx[None], c, jnp.zeros((SUBLANE - 1 - DEC_BATCH, D_MODEL), F32)], axis=0)
    mod_all = _modulation(cvec, w_mod, b_mod)
    tabs = _rope_tables(8, (NOPE_A,)) + _rope_tables(16, (0, HD_D))
    ctx_tensors = (cache_a_ckv, cache_a_krope, state_b_ssm, state_c_hgrn, cache_d_k, cache_d_v)
    weights = (ln_g, ln_b, ffn_w_gu, ffn_w_down, w_in, w_out, mla_q_norm, mla_kv_norm, mla_w_uq, mla_w_ukv,
               ssd_conv_w, ssd_conv_b, ssd_a_log, ssd_dt_bias, ssd_d, ssd_norm, hgrn_norm, gqa_sink)

    y_p = x_prompt.reshape(BATCH * SEQ, D_MODEL)
    y_s = x_sample.reshape(DEC_BATCH * DEC_SEQ, D_MODEL)
    states = []
    for l in range(DEPTH):
        wp, ffn_w, lng, lnb, ctx = _layer_inputs(l, ctx_tensors, weights, hgrn_lb)
        mod_ctx = mod_all[l, 0:1].reshape(1, N_SUB * 3, D_MODEL)
        mod_lat = mod_all[l, 1:1 + DEC_BATCH].reshape(DEC_BATCH, N_SUB * 3, D_MODEL)
        y_p, st = _run_stream(y_p, mod_ctx, BATCH, SEQ, None, wp, ffn_w, lng, lnb, tabs)
        y_s, _ = _run_stream(y_s, mod_lat, DEC_BATCH, DEC_SEQ, ctx, wp, ffn_w, lng, lnb, tabs)
        states.append(st)

    def stack(i, f):
        return jnp.stack([f(s[i]) for s in states], axis=1)

    new_a_ckv = stack(0, lambda t: t.reshape(BATCH, SEQ, KV_RANK))
    new_a_krope = stack(1, lambda t: t.reshape(BATCH, SEQ, LANE)[..., :ROPE_A])
    new_b_ssm = stack(2, lambda t: t)
    new_c_hgrn = stack(3, lambda t: t)
    new_d_k = stack(4, lambda t: t.reshape(BATCH, SEQ, KV_D, HD_D))
    new_d_v = stack(5, lambda t: t.reshape(BATCH, SEQ, KV_D, HD_D))
    return (y_p.reshape(BATCH, SEQ, D_MODEL), y_s.reshape(DEC_BATCH, DEC_SEQ, D_MODEL),
            new_a_ckv, new_a_krope, new_b_ssm, new_c_hgrn, new_d_k, new_d_v)
```

```python
import functools

import numpy as np
import jax
import jax.numpy as jnp
from jax import lax
from jax.experimental import pallas as pl
from jax.experimental.pallas import tpu as pltpu

F32 = jnp.float32
BF16 = jnp.bfloat16

D_MODEL = 1024
BATCH = 32
SEQ = 256
DEPTH = 2
DEC_BATCH = 2
DEC_SEQ = 4096
PAST_LEN = 512
GRID_W = 64
H_A, Q_RANK, KV_RANK, NOPE_A, ROPE_A, V_A = 4, 192, 128, 64, 32, 64
H_B, P_B, G_B, N_B, D_CONV, SSD_CHUNK = 4, 64, 2, 64, 5, 128
H_C, K_C, V_C, HGRN_CHUNK = 4, 64, 64, 16
H_D, KV_D, HD_D, WINDOW = 4, 2, 64, 128
G_D = H_D // KV_D
ROPE_BASE = 10000.0
D_FF = 2816
N_SUB = 3
ALPHA = (2 * DEPTH) ** 0.25
EPS = 1e-6
F_MIN = 1e-6
LOG2_E = 1.4426950408889634
NEG = -1e30
D_IN = 2920
N_MOD = N_SUB * 3 * D_MODEL

LANE = 128
SUBLANE = 8
VMEM_LIMIT = 56 * 1024 * 1024

TM = 512
TM_FFN = 1024
FF_CHUNK = 256
TQ_A = 256
GLA_TILE = 256
GLA_CHUNK = 32
SSD_STEP_CHUNKS = 8
GQA_CTX_SEQS = 8
MLA_CTX_SEQS = 8
GQA_LAT_BLOCKS = 4
CONV_TILE = 256
MOD_TN = 1536
GLA_SAFE_LOG_DECAY = 60.0

C_ACQ, C_ACKV, C_AKR = 0, 256, 384
C_BZ, C_BXBC, C_BDT = 512, 768, 1280
C_CQ, C_CF, C_CI, C_CG = 1408, 1664, 2176, 2432
C_DQ, C_DK, C_DV = 2688, 2944, 3200
W_IN_P = 3456
W_XBC = 512
HB = 128


def _dot(a, b, precision=None):
    return jnp.dot(a, b, preferred_element_type=F32, precision=precision)


def _dot_nt(a, b):
    return lax.dot_general(a, b, (((1,), (1,)), ((), ())), preferred_element_type=F32)


def _prefix_dot(tri, x):
    t = tri.astype(BF16)
    hi = x.astype(BF16)
    rest = x - hi.astype(F32)
    mid = rest.astype(BF16)
    lo = (rest - mid.astype(F32)).astype(BF16)
    return _dot(t, hi) + _dot(t, mid) + _dot(t, lo)


def _params(*sem):
    return pltpu.CompilerParams(dimension_semantics=sem, vmem_limit_bytes=VMEM_LIMIT)


def _resident(shape, index=None):
    index = (0,) * len(shape) if index is None else index
    return pl.BlockSpec(shape, lambda *_: index, pipeline_mode=pl.Buffered(1))


def _silu(x):
    return x * jax.nn.sigmoid(x)


def _layer_norm(y, g, b):
    mu = jnp.mean(y, axis=-1, keepdims=True)
    yc = y - mu
    var = jnp.mean(yc * yc, axis=-1, keepdims=True)
    return yc * lax.rsqrt(var + EPS) * g + b


def _index_map(width, pieces):
    idx = np.full((width,), -1, np.int32)
    for dst, src, w in pieces:
        idx[dst:dst + w] = np.arange(src, src + w)
    return idx


def _gather_pad(arr, idx, axis):
    parts = []
    i = 0
    n = idx.shape[0]
    while i < n:
        j = i
        if idx[i] < 0:
            while j < n and idx[j] < 0:
                j += 1
            shape = list(arr.shape)
            shape[axis] = j - i
            parts.append(jnp.zeros(shape, arr.dtype))
        else:
            while j + 1 < n and idx[j + 1] == idx[j] + 1:
                j += 1
            j += 1
            parts.append(lax.slice_in_dim(arr, int(idx[i]), int(idx[i]) + j - i, axis=axis))
        i = j
    return jnp.concatenate(parts, axis=axis)


_IDX_W_IN = _index_map(W_IN_P, [
    (C_ACQ, 0, Q_RANK), (C_ACKV, 192, KV_RANK), (C_AKR, 320, ROPE_A),
    (C_BZ, 352, 256), (C_BXBC, 608, W_XBC), (C_BDT, 1120, 2 * H_B),
    (C_CQ, 1128, 256), (C_CF, 1384, 512), (C_CI, 1896, 256), (C_CG, 2152, 256),
    (C_DQ, 2408, 256),
    *[(c0 + LANE * g + HD_D * half, s0 + HD_D * g, HD_D)
      for c0, s0 in ((C_DK, 2664), (C_DV, 2792)) for g in range(KV_D) for half in range(2)]])
_IDX_UQ_ROWS = _index_map(256, [(0, 0, Q_RANK)])
_IDX_UQ_COLS = _index_map(4 * HB, [(HB * h, 96 * h, 96) for h in range(H_A)])
_IDX_UKV_K = _index_map(4 * HB, [(HB * h, 128 * h, NOPE_A) for h in range(H_A)])
_IDX_UKV_V = _index_map(4 * V_A, [(V_A * h, 128 * h + NOPE_A, V_A) for h in range(H_A)])


def _rope_tables(half, lane0s):
    t = np.arange(DEC_SEQ)
    pos = np.stack([t // GRID_W, t % GRID_W], 0).astype(np.float64)
    inv = ROPE_BASE ** (-np.arange(half, dtype=np.float64) / half)
    cos = np.ones((DEC_SEQ, LANE))
    sin = np.zeros((DEC_SEQ, LANE))
    for lane0 in lane0s:
        for axis in range(2):
            ang = pos[axis][:, None] * inv[None, :]
            base = lane0 + axis * 2 * half
            cos[:, base:base + half] = np.cos(ang)
            cos[:, base + half:base + 2 * half] = np.cos(ang)
            sin[:, base:base + half] = -np.sin(ang)
            sin[:, base + half:base + 2 * half] = np.sin(ang)
    ident_c = np.ones((TM, LANE))
    ident_s = np.zeros((TM, LANE))
    return (jnp.asarray(np.concatenate([ident_c, cos], 0), F32),
            jnp.asarray(np.concatenate([ident_s, sin], 0), F32))


def _rope(x, cos, sin, first, half):
    partner = jnp.where(first, pltpu.roll(x, LANE - half, 1), pltpu.roll(x, half, 1))
    return x * cos + partner * sin


def _low_lanes():
    return lax.broadcasted_iota(jnp.int32, (1, LANE), 1) < LANE // 2


def _head_alone(tile, odd):
    return jnp.where(_low_lanes(), pltpu.roll(tile, LANE // 2, 1) if odd else tile, 0.0)


def _two_heads(h_even, h_odd):
    return h_even + pltpu.roll(h_odd, LANE // 2, 1)


def _rms_heads(x, width):
    low = _low_lanes()
    tiles = []
    for t in range(width // LANE):
        blk = x[:, LANE * t:LANE * (t + 1)]
        sq = blk * blk
        s_all = jnp.sum(sq, axis=-1, keepdims=True)
        s_low = jnp.sum(jnp.where(low, sq, 0.0), axis=-1, keepdims=True)
        ms = jnp.where(low, s_low, s_all - s_low) * (2.0 / LANE)
        tiles.append(blk * lax.rsqrt(ms + EPS))
    return jnp.concatenate(tiles, axis=-1)


def _mod_kernel(c_ref, w_ref, b_ref, o_ref):
    c = c_ref[...]
    s = _silu(c).astype(BF16)
    o_ref[0] = _dot(s, w_ref[0].astype(BF16)) + b_ref[0]


def _modulation(cvec, w_mod, b_mod):
    return pl.pallas_call(
        _mod_kernel,
        grid=(DEPTH, N_MOD // MOD_TN),
        in_specs=[pl.BlockSpec((SUBLANE, D_MODEL), lambda l, j: (0, 0)),
                  pl.BlockSpec((1, D_MODEL, MOD_TN), lambda l, j: (l, 0, j)),
                  pl.BlockSpec((1, 1, MOD_TN), lambda l, j: (l, 0, j))],
        out_specs=pl.BlockSpec((1, SUBLANE, MOD_TN), lambda l, j: (l, 0, j)),
        out_shape=jax.ShapeDtypeStruct((DEPTH, SUBLANE, N_MOD), F32),
        compiler_params=_params("arbitrary", "arbitrary"),
        name="modulation",
    )(cvec, w_mod, b_mod.reshape(DEPTH, 1, N_MOD))


def _mod_spec(seq_len, tm=TM):
    return pl.BlockSpec((1, N_SUB * 3, D_MODEL), lambda i: (i * tm // seq_len, 0, 0))


def _ffn_sublayer(x, mod_ref, wg_ref, wu_ref, wd_ref, g_ref, b_ref, sub):
    shift = mod_ref[0, 3 * sub:3 * sub + 1, :]
    scale = mod_ref[0, 3 * sub + 1:3 * sub + 2, :]
    gate = mod_ref[0, 3 * sub + 2:3 * sub + 3, :]
    h = (x * (1.0 + scale) + shift).astype(BF16)
    acc = jnp.zeros(x.shape, F32)
    for start in range(0, D_FF, FF_CHUNK):
        cols = slice(start, min(start + FF_CHUNK, D_FF))
        gt = _dot(h, wg_ref[:, cols])
        up = _dot(h, wu_ref[:, cols])
        acc = acc + _dot((_silu(gt) * up).astype(BF16), wd_ref[cols, :])
    y = ALPHA * x + 0.5 * gate * acc
    return _layer_norm(y, g_ref[...], b_ref[...])


def _ffn_kernel(x_ref, mod_ref, wg_ref, wu_ref, wd_ref, g_ref, b_ref, o_ref, *, sub):
    o_ref[...] = _ffn_sublayer(x_ref[...], mod_ref, wg_ref, wu_ref, wd_ref, g_ref, b_ref, sub)


def _ffn(x, mod, seq_len, w_gu, w_down, layer_sub, ln_g, ln_b, sub):
    n = x.shape[0]
    l, s = layer_sub
    row = pl.BlockSpec((TM_FFN, D_MODEL), lambda i: (i, 0))
    return pl.pallas_call(
        functools.partial(_ffn_kernel, sub=sub),
        grid=(n // TM_FFN,),
        in_specs=[row, _mod_spec(seq_len, TM_FFN),
                  _resident((None, None, D_MODEL, D_FF), (l, s, 0, 0)),
                  _resident((None, None, D_MODEL, D_FF), (l, s, 0, 1)),
                  _resident((None, None, D_FF, D_MODEL), (l, s, 0, 0)),
                  _resident((1, D_MODEL)), _resident((1, D_MODEL))],
        out_specs=row,
        out_shape=jax.ShapeDtypeStruct((n, D_MODEL), F32),
        compiler_params=_params("arbitrary"),
        name="ffn",
    )(x, mod, w_gu, w_gu, w_down, ln_g, ln_b)


def _in_kernel(x_ref, mod_ref, w_ref, wuq_ref, wk_ref, wv_ref, gq_ref, gkv_ref, dtb_ref,
               cosq_ref, sinq_ref, cosd_ref, sind_ref, *out_refs, latent):
    out = dict(zip([name for name, _, _ in _in_outputs(latent)], out_refs))
    qa_ref, ka_ref, va_ref = out["qa"], out["ka"], out["va"]
    bz_ref, bxbc_ref, bdt_ref = out["bz"], out["bxbc"], out["bdt"]
    cq_ref, cf_ref, ci_ref, cg_ref = out["cq"], out["cf"], out["ci"], out["cg"]
    dq_ref, dk_ref, dv_ref = out["dq"], out["dk"], out["dv"]
    x = x_ref[...]
    h = (x * (1.0 + mod_ref[0, 4:5, :]) + mod_ref[0, 3:4, :]).astype(BF16)

    def proj(start, width):
        return _dot(h, w_ref[:, start:start + width])

    lane = lax.broadcasted_iota(jnp.int32, (TM, LANE), 1)
    first_a = (lane % 16) < 8
    first_d = (lane % 32) < 16

    def rope_a(blk):
        return _rope(blk, cosq_ref[...], sinq_ref[...], first_a, 8) if latent else blk

    def rope_d(blk):
        return _rope(blk, cosd_ref[...], sind_ref[...], first_d, 16) if latent else blk

    pa = proj(C_ACQ, C_BZ - C_ACQ)
    acq = pa[:, 0:256]
    ms = jnp.sum(acq * acq, axis=-1, keepdims=True) * (1.0 / Q_RANK)
    qn = (acq * lax.rsqrt(ms + EPS) * gq_ref[...]).astype(BF16)
    q = _dot(qn, wuq_ref[...])
    scale_a = (NOPE_A + ROPE_A) ** -0.5 * LOG2_E
    for hh in range(H_A):
        blk = slice(HB * hh, HB * (hh + 1))
        qa_ref[:, blk] = (rope_a(q[:, blk]) * scale_a).astype(BF16)
    ackv = pa[:, C_ACKV:C_ACKV + KV_RANK]
    ms = jnp.mean(ackv * ackv, axis=-1, keepdims=True)
    ckv = ackv * lax.rsqrt(ms + EPS) * gkv_ref[...]
    ckv_b = ckv.astype(BF16)
    kk = _dot(ckv_b, wk_ref[...])
    akr = pa[:, C_AKR:C_AKR + LANE]
    if not latent:
        out["ckv"][...] = ckv
        out["kr"][...] = akr
    krp = rope_a(pltpu.roll(akr, NOPE_A, 1))
    for hh in range(H_A):
        blk = slice(HB * hh, HB * (hh + 1))
        ka_ref[:, blk] = (kk[:, blk] + krp).astype(BF16)
    va_ref[...] = _dot(ckv_b, wv_ref[...]).astype(BF16)

    pb = proj(C_BZ, C_CQ - C_BZ)
    bz_ref[...] = pb[:, 0:H_B * P_B]
    bxbc_ref[...] = pb[:, C_BXBC - C_BZ:C_BXBC - C_BZ + W_XBC]
    dtr = pb[:, C_BDT - C_BZ:C_BDT - C_BZ + LANE] + dtb_ref[...]
    bdt_ref[...] = jnp.maximum(dtr, 0.0) + jnp.log(1.0 + jnp.exp(-jnp.abs(dtr)))

    pc = proj(C_CQ, C_DQ - C_CQ)
    cq_ref[...] = pc[:, 0:256]
    cf_ref[...] = pc[:, C_CF - C_CQ:C_CF - C_CQ + 512]
    ci_ref[...] = pc[:, C_CI - C_CQ:C_CI - C_CQ + 256]
    cg_ref[...] = pc[:, C_CG - C_CQ:C_CG - C_CQ + 256]

    pd = proj(C_DQ, W_IN_P - C_DQ)
    dq = pd[:, 0:H_D * HD_D]
    scale_d = HD_D ** -0.5
    for t in range(H_D * HD_D // LANE):
        blk = slice(LANE * t, LANE * (t + 1))
        dq_ref[:, blk] = (rope_d(dq[:, blk]) * scale_d).astype(BF16)
    dk = pd[:, C_DK - C_DQ:C_DK - C_DQ + KV_D * LANE]
    dk = [rope_d(dk[:, LANE * g:LANE * (g + 1)]) for g in range(KV_D)]
    for g in range(KV_D):
        dk_ref[:, LANE * g:LANE * (g + 1)] = dk[g]
    dv = pd[:, C_DV - C_DQ:C_DV - C_DQ + KV_D * LANE]
    dv_ref[...] = dv
    if not latent:
        out["dkc"][...] = jnp.where(_low_lanes(), dk[0], dk[1])
        out["dvc"][...] = jnp.where(_low_lanes(), dv[:, 0:LANE], dv[:, LANE:2 * LANE])


def _in_outputs(latent):
    outs = [("qa", 512, BF16), ("ka", 512, BF16), ("va", 256, BF16),
            ("bz", H_B * P_B, F32), ("bxbc", W_XBC, F32), ("bdt", LANE, F32),
            ("cq", 256, F32), ("cf", 512, F32), ("ci", 256, F32), ("cg", 256, F32),
            ("dq", H_D * HD_D, BF16), ("dk", KV_D * LANE, F32), ("dv", KV_D * LANE, F32)]
    if not latent:
        outs += [("ckv", KV_RANK, F32), ("kr", LANE, F32), ("dkc", KV_D * HD_D, F32), ("dvc", KV_D * HD_D, F32)]
    return outs


def _in_proj(x, mod, group_len, latent, wp, tabs):
    n = x.shape[0]
    row = lambda w: pl.BlockSpec((TM, w), lambda i: (i, 0))
    tab = pl.BlockSpec((TM, LANE), (lambda i: (1 + i % (DEC_SEQ // TM), 0)) if latent else (lambda i: (0, 0)))
    outs = pl.pallas_call(
        functools.partial(_in_kernel, latent=latent),
        grid=(n // TM,),
        in_specs=[row(D_MODEL), _mod_spec(group_len), _resident((D_MODEL, W_IN_P)), _resident((256, 512)),
                  _resident((KV_RANK, 512)), _resident((KV_RANK, 256)), _resident((1, 256)),
                  _resident((1, KV_RANK)), _resident((1, LANE)), tab, tab, tab, tab],
        out_specs=[row(w) for _, w, _ in _in_outputs(latent)],
        out_shape=[jax.ShapeDtypeStruct((n, w), dt) for _, w, dt in _in_outputs(latent)],
        compiler_params=_params("arbitrary"),
        name="in_proj",
    )(x, mod, wp["w_in"], wp["w_uq"], wp["w_uk"], wp["w_uv"], wp["g_q"], wp["g_kv"], wp["dt_bias"],
      tabs[0], tabs[1], tabs[2], tabs[3])
    return dict(zip([k for k, _, _ in _in_outputs(latent)], outs))


def _mla_cache_kernel(ckv_ref, krp_ref, wk_ref, wv_ref, kc_ref, vc_ref):
    ckv_b = ckv_ref[0].astype(BF16)
    kk = _dot(ckv_b, wk_ref[...])
    krp = krp_ref[0]
    for hh in range(H_A):
        blk = slice(HB * hh, HB * (hh + 1))
        kc_ref[0, :, blk] = (kk[:, blk] + krp).astype(BF16)
    vc_ref[0] = _dot(ckv_b, wv_ref[...]).astype(BF16)


def _mla_cache(ckv, krope_placed, wp):
    nb = ckv.shape[0]
    return pl.pallas_call(
        _mla_cache_kernel,
        grid=(nb,),
        in_specs=[pl.BlockSpec((1, PAST_LEN, KV_RANK), lambda b: (b, 0, 0)),
                  pl.BlockSpec((1, PAST_LEN, LANE), lambda b: (b, 0, 0)),
                  _resident((KV_RANK, 512)), _resident((KV_RANK, 256))],
        out_specs=[pl.BlockSpec((1, PAST_LEN, 512), lambda b: (b, 0, 0)),
                   pl.BlockSpec((1, PAST_LEN, 256), lambda b: (b, 0, 0))],
        out_shape=[jax.ShapeDtypeStruct((nb, PAST_LEN, 512), BF16),
                   jax.ShapeDtypeStruct((nb, PAST_LEN, 256), BF16)],
        compiler_params=_params("arbitrary"),
        name="mla_cache",
    )(ckv, krope_placed, wp["w_uk"], wp["w_uv"])


def _mla_kernel(*refs, has_cache, seqs):
    if has_cache:
        q_ref, k_ref, v_ref, kc_ref, vc_ref, o_ref = refs
    else:
        q_ref, k_ref, v_ref, o_ref = refs
    tk = k_ref.shape[0] // seqs
    tq = q_ref.shape[0] // seqs
    head_of_lane = lax.broadcasted_iota(jnp.int32, (1, H_A * V_A), 1) // V_A
    for b in range(seqs):
        qrows = slice(b * tq, (b + 1) * tq)
        krows = slice(b * tk, (b + 1) * tk)
        v = v_ref[krows, :]
        blocks = [slice(HB * hh, HB * (hh + 1)) for hh in range(H_A)]
        scores = [_dot_nt(q_ref[qrows, blk], k_ref[krows, blk]) for blk in blocks]
        if has_cache:
            scores_c = [_dot_nt(q_ref[qrows, blk], kc_ref[0, :, blk]) for blk in blocks]
        acc = jnp.zeros((tq, H_A * V_A), F32)
        for hh in range(H_A):
            s = scores[hh]
            m = jnp.max(s, axis=-1, keepdims=True)
            if has_cache:
                sc = scores_c[hh]
                m = jnp.maximum(m, jnp.max(sc, axis=-1, keepdims=True))
            e = jnp.exp2(s - m)
            den = jnp.sum(e, axis=-1, keepdims=True)
            pv = _dot(e.astype(BF16), v)
            if has_cache:
                ec = jnp.exp2(sc - m)
                den = den + jnp.sum(ec, axis=-1, keepdims=True)
                pv = pv + _dot(ec.astype(BF16), vc_ref[0])
            acc = jnp.where(head_of_lane == hh, pv / den, acc)
        o_ref[qrows, :] = acc.astype(BF16)


def _mla(q, k, v, n_seq, seq_len, cache=None):
    tq = TQ_A if cache is not None else seq_len
    nq = seq_len // tq
    seqs = 1 if cache is not None else MLA_CTX_SEQS
    in_specs = [pl.BlockSpec((seqs * tq, 512), lambda b, i: (b * nq + i, 0)),
                pl.BlockSpec((seqs * seq_len, 512), lambda b, i: (b, 0)),
                pl.BlockSpec((seqs * seq_len, 256), lambda b, i: (b, 0))]
    args = [q, k, v]
    if cache is not None:
        in_specs += [pl.BlockSpec((1, PAST_LEN, 512), lambda b, i: (b, 0, 0)),
                     pl.BlockSpec((1, PAST_LEN, 256), lambda b, i: (b, 0, 0))]
        args += list(cache)
    return pl.pallas_call(
        functools.partial(_mla_kernel, has_cache=cache is not None, seqs=seqs),
        grid=(n_seq // seqs, nq),
        in_specs=in_specs,
        out_specs=pl.BlockSpec((seqs * tq, H_A * V_A), lambda b, i: (b * nq + i, 0)),
        out_shape=jax.ShapeDtypeStruct((n_seq * seq_len, H_A * V_A), BF16),
        compiler_params=_params("arbitrary", "arbitrary"),
        name="mla_attention",
    )(*args)


def _sink_softmax_pv(parts, sink):
    m = sink
    for s, _ in parts:
        m = jnp.maximum(m, jnp.max(s, axis=-1, keepdims=True))
    den = jnp.exp(sink - m)
    pv = None
    for s, v in parts:
        e = jnp.exp(s - m)
        den = den + jnp.sum(e, axis=-1, keepdims=True)
        t = _dot(e.astype(BF16), v)
        pv = t if pv is None else pv + t
    return pv / den


def _gqa_pair(q_ref, rows, g, scores_and_values, sink_ref, o_ref):
    t = rows.stop - rows.start
    tile = slice(LANE * g, LANE * (g + 1))
    qt = q_ref[rows, tile]
    low = _low_lanes()
    zero_b = jnp.zeros((), BF16)
    q2 = jnp.concatenate([jnp.where(low, qt, zero_b), jnp.where(low, zero_b, qt)], axis=0)
    second = lax.broadcasted_iota(jnp.int32, (G_D * t, 1), 0) >= t
    sink = jnp.where(second, sink_ref[G_D * g + 1:G_D * g + 2, 0:1], sink_ref[G_D * g:G_D * g + 1, 0:1])
    o = _sink_softmax_pv(scores_and_values(q2), sink)
    o_ref[rows, tile] = jnp.where(low, o[0:t, :], o[t:2 * t, :]).astype(BF16)


def _gqa_ctx_kernel(q_ref, k_ref, v_ref, sink_ref, o_ref):
    for b in range(GQA_CTX_SEQS):
        rows = slice(b * SEQ, (b + 1) * SEQ)
        for g in range(KV_D):
            tile = slice(LANE * g, LANE * (g + 1))
            k = k_ref[rows, tile].astype(BF16)
            v = v_ref[rows, tile].astype(BF16)
            _gqa_pair(q_ref, rows, g, lambda q2, k=k, v=v: [(_dot_nt(q2, k), v)], sink_ref, o_ref)


def _gqa_ctx(q, k, v, sink, n_seq):
    seq = lambda w: pl.BlockSpec((GQA_CTX_SEQS * SEQ, w), lambda b: (b, 0))
    return pl.pallas_call(
        _gqa_ctx_kernel,
        grid=(n_seq // GQA_CTX_SEQS,),
        in_specs=[seq(H_D * HD_D), seq(KV_D * LANE), seq(KV_D * LANE), _resident((H_D, LANE))],
        out_specs=seq(H_D * HD_D),
        out_shape=jax.ShapeDtypeStruct((n_seq * SEQ, H_D * HD_D), BF16),
        compiler_params=_params("arbitrary"),
        name="gqa_context",
    )(q, k, v, sink)


def _gqa_lat_kernel(q_ref, k_ref, v_ref, kc_ref, vc_ref, sink_ref, o_ref):
    span = 3 * WINDOW
    row = lax.broadcasted_iota(jnp.int32, (G_D * WINDOW, span), 0) % WINDOW
    col = lax.broadcasted_iota(jnp.int32, (G_D * WINDOW, span), 1)
    for j in range(GQA_LAT_BLOCKS):
        n = pl.program_id(1) * GQA_LAT_BLOCKS + j
        rows = slice(j * WINDOW, (j + 1) * WINDOW)
        start = pl.multiple_of(jnp.clip((n - 1) * WINDOW, 0, DEC_SEQ - span), WINDOW)
        band = jnp.abs((start + col) - (n * WINDOW + row)) <= WINDOW
        for g in range(KV_D):
            tile = slice(LANE * g, LANE * (g + 1))
            kw = k_ref[pl.ds(start, span), tile].astype(BF16)
            vw = v_ref[pl.ds(start, span), tile].astype(BF16)
            kc = kc_ref[0, :, tile]
            vc = vc_ref[0, :, tile]

            def parts(q2, kw=kw, vw=vw, kc=kc, vc=vc, band=band):
                return [(jnp.where(band, _dot_nt(q2, kw), NEG), vw), (_dot_nt(q2, kc), vc)]

            _gqa_pair(q_ref, rows, g, parts, sink_ref, o_ref)


def _gqa_lat(q, k, v, kc, vc, sink, n_seq):
    nb = DEC_SEQ // (WINDOW * GQA_LAT_BLOCKS)
    wq, wkv = H_D * HD_D, KV_D * LANE
    qblk = pl.BlockSpec((WINDOW * GQA_LAT_BLOCKS, wq), lambda b, n: (b * nb + n, 0))
    return pl.pallas_call(
        _gqa_lat_kernel,
        grid=(n_seq, nb),
        in_specs=[qblk,
                  pl.BlockSpec((DEC_SEQ, wkv), lambda b, n: (b, 0)),
                  pl.BlockSpec((DEC_SEQ, wkv), lambda b, n: (b, 0)),
                  pl.BlockSpec((1, PAST_LEN, wkv), lambda b, n: (b, 0, 0)),
                  pl.BlockSpec((1, PAST_LEN, wkv), lambda b, n: (b, 0, 0)),
                  _resident((H_D, LANE))],
        out_specs=qblk,
        out_shape=jax.ShapeDtypeStruct((n_seq * DEC_SEQ, wq), BF16),
        compiler_params=_params("arbitrary", "arbitrary"),
        name="gqa_latent",
    )(q, k, v, kc, vc, sink)


def _conv_kernel(cur_ref, prev_ref, next_ref, w_ref, b_ref, o_ref, pad_ref, *, tiles_per_seq):
    i = pl.program_id(0)
    has_prev = (i % tiles_per_seq) != 0
    has_next = (i % tiles_per_seq) != tiles_per_seq - 1
    pad_ref[0:SUBLANE, :] = jnp.where(has_prev, prev_ref[...], 0.0)
    pad_ref[SUBLANE:SUBLANE + CONV_TILE, :] = cur_ref[...]
    pad_ref[SUBLANE + CONV_TILE:, :] = jnp.where(has_next, next_ref[...], 0.0)
    y = jnp.zeros((CONV_TILE, W_XBC), F32) + b_ref[...]
    for k in range(D_CONV):
        off = SUBLANE - D_CONV // 2 + k
        y = y + w_ref[k:k + 1, :] * pad_ref[off:off + CONV_TILE, :]
    o_ref[...] = _silu(y)


def _conv(xbc, seq_len, w, b):
    n = xbc.shape[0]
    per = CONV_TILE // SUBLANE
    last = n // SUBLANE - 1
    return pl.pallas_call(
        functools.partial(_conv_kernel, tiles_per_seq=seq_len // CONV_TILE),
        grid=(n // CONV_TILE,),
        in_specs=[pl.BlockSpec((CONV_TILE, W_XBC), lambda i: (i, 0)),
                  pl.BlockSpec((SUBLANE, W_XBC), lambda i: (jnp.maximum(i * per - 1, 0), 0)),
                  pl.BlockSpec((SUBLANE, W_XBC), lambda i: (jnp.minimum((i + 1) * per, last), 0)),
                  _resident((SUBLANE, W_XBC)), _resident((1, W_XBC))],
        out_specs=pl.BlockSpec((CONV_TILE, W_XBC), lambda i: (i, 0)),
        out_shape=jax.ShapeDtypeStruct((n, W_XBC), F32),
        scratch_shapes=[pltpu.VMEM((CONV_TILE + 2 * SUBLANE, W_XBC), F32)],
        compiler_params=_params("arbitrary"),
        name="ssd_conv",
    )(xbc, xbc, xbc, w, b)


def _ssd_kernel(*refs, has_s0, chunks):
    if has_s0:
        xf_ref, xb_ref, dtf_ref, dtb_ref, alog_ref, dsk_ref, s0_ref, yf_ref, yb_ref, st_ref, s_scr = refs
    else:
        xf_ref, xb_ref, dtf_ref, dtb_ref, alog_ref, dsk_ref, yf_ref, yb_ref, st_ref, s_scr = refs
    c = pl.program_id(1)
    q = SSD_CHUNK

    @pl.when(c == 0)
    def _():
        s_scr[...] = s0_ref[0] if has_s0 else jnp.zeros(s_scr.shape, F32)

    row = lax.broadcasted_iota(jnp.int32, (q, q), 0)
    col = lax.broadcasted_iota(jnp.int32, (q, q), 1)
    a_coef = -jnp.exp(alog_ref[...])
    for d, (x_ref, dt_ref, y_ref) in enumerate(((xf_ref, dtf_ref, yf_ref), (xb_ref, dtb_ref, yb_ref))):
        tri = (row >= col) if d == 0 else (row <= col)
        for ci in (range(chunks) if d == 0 else range(chunks - 1, -1, -1)):
            rows = slice(ci * q, (ci + 1) * q)
            dt = dt_ref[rows, :]
            cum = _prefix_dot(tri, dt * a_coef)
            cum_t = cum.T
            dt_t = dt.T
            total = cum[q - 1:q, :] if d == 0 else cum[0:1, :]
            ys = []
            for g in range(G_B):
                bg = _head_alone(x_ref[rows, 2 * LANE:3 * LANE], g)
                cg = _head_alone(x_ref[rows, 3 * LANE:4 * LANE], g)
                cb = _dot_nt(cg.astype(BF16), bg.astype(BF16))
                for hh in range(g * (H_B // G_B), (g + 1) * (H_B // G_B)):
                    k = d * H_B + hh
                    cum_b = jnp.broadcast_to(cum[:, k:k + 1], (q, LANE))
                    seg = jnp.where(tri, jnp.exp(jnp.where(tri, cum_b - cum_t[k:k + 1, :], 0.0)), 0.0)
                    xh = _head_alone(x_ref[rows, LANE * (hh // 2):LANE * (hh // 2 + 1)], hh % 2)
                    dt_row = dt_t[k:k + 1, :]
                    s_in = s_scr[d, hh]
                    y = _dot((cb * seg * dt_row).astype(BF16), xh.astype(BF16))
                    y = y + _dot_nt((cg * jnp.exp(cum_b)).astype(BF16), s_in.astype(BF16))
                    ys.append(y + xh * dsk_ref[k:k + 1, :])
                    tot = total[:, k:k + 1]
                    bdec = bg * jnp.exp(tot - cum_b)
                    cs = _dot((xh.T * dt_row).astype(BF16), bdec.astype(BF16))
                    s_scr[d, hh] = jnp.exp(tot) * s_in + cs
            for p in range(H_B // 2):
                y_ref[rows, LANE * p:LANE * (p + 1)] = _two_heads(ys[2 * p], ys[2 * p + 1])

    @pl.when(c == pl.num_programs(1) - 1)
    def _():
        for d in range(2):
            for hh in range(H_B):
                st_ref[0, d, hh] = s_scr[d, hh, 0:P_B, 0:N_B]


def _ssd(xbc, dt, n_seq, seq_len, a_log, dskip, s0=None):
    chunks = min(SSD_STEP_CHUNKS, seq_len // SSD_CHUNK)
    nc = seq_len // (SSD_CHUNK * chunks)
    fwd = lambda w: pl.BlockSpec((SSD_CHUNK * chunks, w), lambda b, c: (b * nc + c, 0))
    bwd = lambda w: pl.BlockSpec((SSD_CHUNK * chunks, w), lambda b, c: (b * nc + nc - 1 - c, 0))
    state = pl.BlockSpec((1, 2, H_B, HB, HB), lambda b, c: (b, 0, 0, 0, 0))
    in_specs = [fwd(W_XBC), bwd(W_XBC), fwd(LANE), bwd(LANE), _resident((1, LANE)), _resident((2 * H_B, LANE))]
    args = [xbc, xbc, dt, dt, a_log, dskip]
    if s0 is not None:
        in_specs.append(state)
        args.append(s0)
    n = n_seq * seq_len
    return pl.pallas_call(
        functools.partial(_ssd_kernel, has_s0=s0 is not None, chunks=chunks),
        grid=(n_seq, nc),
        in_specs=in_specs,
        out_specs=[fwd(H_B * P_B), bwd(H_B * P_B),
                   pl.BlockSpec((1, 2, H_B, P_B, N_B), lambda b, c: (b, 0, 0, 0, 0))],
        out_shape=[jax.ShapeDtypeStruct((n, H_B * P_B), F32), jax.ShapeDtypeStruct((n, H_B * P_B), F32),
                   jax.ShapeDtypeStruct((n_seq, 2, H_B, P_B, N_B), F32)],
        scratch_shapes=[pltpu.VMEM((2, H_B, HB, HB), F32)],
        compiler_params=_params("arbitrary", "arbitrary"),
        name="ssd_scan",
    )(*args)


def _gla_kernel(*refs, has_s0):
    if has_s0:
        (qf_ref, qb_ref, ff_ref, fb_ref, vf_ref, vb_ref, lb_ref, s0_ref, of_ref, ob_ref, st_ref, s_scr,
         o_scr) = refs
    else:
        (qf_ref, qb_ref, ff_ref, fb_ref, vf_ref, vb_ref, lb_ref, of_ref, ob_ref, st_ref, s_scr, o_scr) = refs
    c = pl.program_id(1)
    t = GLA_TILE
    ch = GLA_CHUNK
    nch = t // ch
    w = H_C * K_C

    @pl.when(c == 0)
    def _():
        s_scr[...] = s0_ref[0] if has_s0 else jnp.zeros(s_scr.shape, F32)

    row = lax.broadcasted_iota(jnp.int32, (t, t), 0)
    col = lax.broadcasted_iota(jnp.int32, (t, t), 1)
    same_chunk = (row // ch) == (col // ch)
    head_of_lane = lax.broadcasted_iota(jnp.int32, (1, w), 1) // K_C
    row_head = lax.broadcasted_iota(jnp.int32, (H_C * ch, 1), 0) // ch
    low_half = lax.broadcasted_iota(jnp.int32, (1, LANE), 1) < V_C
    zero_b = jnp.zeros((), BF16)
    for d, (q_ref, f_ref, v_ref, o_ref) in enumerate(((qf_ref, ff_ref, vf_ref, of_ref),
                                                      (qb_ref, fb_ref, vb_ref, ob_ref))):
        qv = q_ref[...]
        fr = f_ref[...]
        vv = v_ref[...]
        lb = lb_ref[d:d + 1, :]
        f = lb + (1.0 - lb) * jax.nn.sigmoid(fr)
        log_f = jnp.log(jnp.maximum(f, F_MIN))
        key = (1.0 - lb) * jax.nn.sigmoid(-fr)
        tri = same_chunk & ((col <= row) if d == 0 else (col >= row))
        cum = _prefix_dot(tri, log_f)
        cum3 = cum.reshape(nch, ch, w)
        k3 = key.reshape(nch, ch, w)
        q_dec = (qv * jnp.exp(cum)).astype(BF16)
        q_heads = [jnp.where(head_of_lane == hh, q_dec, zero_b) for hh in range(H_C)]
        v_heads = []
        for hh in range(H_C):
            tile = vv[:, LANE * (hh // 2):LANE * (hh // 2 + 1)]
            v_heads.append(jnp.where(low_half, tile if hh % 2 == 0 else pltpu.roll(tile, V_C, 1), 0.0))

        k_inv = (key * jnp.exp(-cum)).astype(BF16)
        for hh in range(H_C):
            att = jnp.where(tri, _dot_nt(q_heads[hh], k_inv), 0.0).astype(BF16)
            o_scr[:, HB * hh:HB * (hh + 1)] = _dot(att, v_heads[hh].astype(BF16))

        @pl.when(jnp.min(cum) < -GLA_SAFE_LOG_DECAY)
        def _():
            wp = H_C * HB
            head_ones = ((lax.broadcasted_iota(jnp.int32, (w, wp), 0) // K_C)
                         == (lax.broadcasted_iota(jnp.int32, (w, wp), 1) // HB)).astype(F32)
            i_in_chunk = lax.broadcasted_iota(jnp.int32, (nch, ch, w), 1)
            q3 = qv.reshape(nch, ch, w)
            v3 = jnp.concatenate(v_heads, axis=1).reshape(nch, ch, wp)
            o3 = jnp.zeros((nch, ch, wp), F32)
            for j in range(ch):
                live = (i_in_chunk >= j) if d == 0 else (i_in_chunk <= j)
                e = jnp.exp(jnp.where(live, cum3 - cum3[:, j:j + 1, :], 0.0))
                term = jnp.where(live, q3 * e * k3[:, j:j + 1, :], 0.0)
                att = _dot(term.reshape(t, w), head_ones)
                o3 = o3 + att.reshape(nch, ch, wp) * v3[:, j:j + 1, :]
            o_scr[...] = o3.reshape(t, wp)

        edge = ch - 1 if d == 0 else 0
        last3 = jnp.broadcast_to(cum3[:, edge:edge + 1, :], (nch, ch, w))
        k_dec = (k3 * jnp.exp(last3 - cum3)).reshape(t, w).astype(BF16)
        order = range(nch) if d == 0 else range(nch - 1, -1, -1)
        for cc in order:
            rows = slice(cc * ch, (cc + 1) * ch)
            st = s_scr[d]
            q4 = jnp.concatenate([qh[rows, :] for qh in q_heads], axis=0)
            r = _dot_nt(q4, st.astype(BF16))
            for hh in range(H_C):
                blk = slice(HB * hh, HB * (hh + 1))
                o_scr[rows, blk] = o_scr[rows, blk] + r[hh * ch:(hh + 1) * ch, :]
            v4 = jnp.concatenate([vh[rows, :] for vh in v_heads], axis=0)
            k4 = jnp.where(row_head == head_of_lane, jnp.concatenate([k_dec[rows, :]] * H_C, axis=0), zero_b)
            decay = jnp.exp(cum[cc * ch + edge:cc * ch + edge + 1, :])
            s_scr[d] = decay * st + _dot(v4.T.astype(BF16), k4)

        for p in range(H_C // 2):
            o_ref[:, LANE * p:LANE * (p + 1)] = (o_scr[:, HB * 2 * p:HB * (2 * p + 1)]
                                                 + pltpu.roll(o_scr[:, HB * (2 * p + 1):HB * (2 * p + 2)], V_C, 1))

    @pl.when(c == pl.num_programs(1) - 1)
    def _():
        k_idx = lax.broadcasted_iota(jnp.int32, (K_C, w), 0)
        lane_idx = lax.broadcasted_iota(jnp.int32, (K_C, w), 1)
        for d in range(2):
            rows = s_scr[d, 0:V_C, :]
            for hh in range(H_C):
                sel = (lane_idx == hh * K_C + k_idx).astype(F32)
                st_ref[0, d, hh] = lax.dot_general(sel, rows, (((1,), (1,)), ((), ())),
                                                   preferred_element_type=F32, precision=lax.Precision.HIGHEST)


def _gla(cq, cf, ci, n_seq, seq_len, lb, s0=None):
    nt = seq_len // GLA_TILE
    w = H_C * K_C
    fwd = lambda j: pl.BlockSpec((GLA_TILE, w), lambda b, c: (b * nt + c, j))
    bwd = lambda j: pl.BlockSpec((GLA_TILE, w), lambda b, c: (b * nt + nt - 1 - c, j))
    state = pl.BlockSpec((1, 2, HB, w), lambda b, c: (b, 0, 0, 0))
    in_specs = [fwd(0), bwd(0), fwd(0), bwd(1), fwd(0), bwd(0), _resident((2, w))]
    args = [cq, cq, cf, cf, ci, ci, lb]
    if s0 is not None:
        in_specs.append(state)
        args.append(s0)
    n = n_seq * seq_len
    return pl.pallas_call(
        functools.partial(_gla_kernel, has_s0=s0 is not None),
        grid=(n_seq, nt),
        in_specs=in_specs,
        out_specs=[fwd(0), bwd(0), pl.BlockSpec((1, 2, H_C, K_C, V_C), lambda b, c: (b, 0, 0, 0, 0))],
        out_shape=[jax.ShapeDtypeStruct((n, w), F32), jax.ShapeDtypeStruct((n, w), F32),
                   jax.ShapeDtypeStruct((n_seq, 2, H_C, K_C, V_C), F32)],
        scratch_shapes=[pltpu.VMEM((2, HB, w), F32), pltpu.VMEM((GLA_TILE, H_C * HB), F32)],
        compiler_params=_params("arbitrary", "arbitrary"),
        name="hgrn_scan",
    )(*args)


def _out_ffn_kernel(x_ref, mod_ref, oa_ref, yf_ref, yb_ref, bz_ref, of_ref, ob_ref, cg_ref, od_ref,
                    wo_ref, nb_ref, nc_ref, g1_ref, b1_ref, wg_ref, wu_ref, wd_ref, g2_ref, b2_ref, o_ref):
    x = x_ref[...]
    gate = mod_ref[0, 5:6, :]
    yb = _rms_heads((yf_ref[...] + yb_ref[...]) * _silu(bz_ref[...]), H_B * P_B) * nb_ref[...]
    oc = _rms_heads(of_ref[...] + ob_ref[...], H_C * V_C) * nc_ref[...] * _silu(cg_ref[...])
    mixed = jnp.concatenate([oa_ref[...], yb.astype(BF16), oc.astype(BF16), od_ref[...]], axis=-1)
    u = _dot(mixed, wo_ref[...])
    x = _layer_norm(ALPHA * x + gate * u, g1_ref[...], b1_ref[...])
    o_ref[...] = _ffn_sublayer(x, mod_ref, wg_ref, wu_ref, wd_ref, g2_ref, b2_ref, 2)


def _out_ffn(x, mod, seq_len, mix, wp, ln1_g, ln1_b, w_gu, w_down, layer_sub, ln2_g, ln2_b):
    n = x.shape[0]
    l, s = layer_sub
    row = lambda w: pl.BlockSpec((TM, w), lambda i: (i, 0))
    names = ("oa", "yf", "yb", "bz", "of", "ob", "cg", "od")
    vec = _resident((1, D_MODEL))
    return pl.pallas_call(
        _out_ffn_kernel,
        grid=(n // TM,),
        in_specs=[row(D_MODEL), _mod_spec(seq_len)] + [row(mix[k].shape[1]) for k in names]
                 + [_resident((D_MODEL, D_MODEL)), _resident((1, H_B * P_B)), _resident((1, H_C * V_C)), vec, vec,
                    _resident((None, None, D_MODEL, D_FF), (l, s, 0, 0)),
                    _resident((None, None, D_MODEL, D_FF), (l, s, 0, 1)),
                    _resident((None, None, D_FF, D_MODEL), (l, s, 0, 0)), vec, vec],
        out_specs=row(D_MODEL),
        out_shape=jax.ShapeDtypeStruct((n, D_MODEL), F32),
        compiler_params=_params("arbitrary"),
        name="out_proj_ffn",
    )(x, mod, *[mix[k] for k in names], wp["w_out"], wp["ssd_norm"], wp["hgrn_norm"], ln1_g, ln1_b,
      w_gu, w_gu, w_down, ln2_g, ln2_b)


def _prep_layer(l, w_in, w_out, mla_q_norm, mla_kv_norm, mla_w_uq, mla_w_ukv, ssd_conv_w, ssd_conv_b,
                ssd_a_log, ssd_dt_bias, ssd_d, ssd_norm, hgrn_lb, hgrn_norm, gqa_sink):
    return {
        "w_in": _gather_pad(w_in[l], _IDX_W_IN, 1).astype(BF16),
        "w_uq": _gather_pad(_gather_pad(mla_w_uq[l], _IDX_UQ_ROWS, 0), _IDX_UQ_COLS, 1).astype(BF16),
        "w_uk": _gather_pad(mla_w_ukv[l], _IDX_UKV_K, 1).astype(BF16),
        "w_uv": _gather_pad(mla_w_ukv[l], _IDX_UKV_V, 1).astype(BF16),
        "g_q": _gather_pad(mla_q_norm[l], _IDX_UQ_ROWS, 0).reshape(1, 256),
        "g_kv": mla_kv_norm[l].reshape(1, KV_RANK),
        "dt_bias": jnp.pad(ssd_dt_bias[l].reshape(1, 2 * H_B), ((0, 0), (0, LANE - 2 * H_B))),
        "conv_w": jnp.pad(ssd_conv_w[l], ((0, SUBLANE - D_CONV), (0, 0))),
        "conv_b": ssd_conv_b[l].reshape(1, W_XBC),
        "a_log": jnp.pad(ssd_a_log[l].reshape(1, 2 * H_B), ((0, 0), (0, LANE - 2 * H_B))),
        "d_skip": jnp.broadcast_to(ssd_d[l].reshape(2 * H_B, 1), (2 * H_B, LANE)),
        "ssd_norm": ssd_norm[l].reshape(1, H_B * P_B),
        "hgrn_lb": hgrn_lb[l],
        "hgrn_norm": hgrn_norm[l].reshape(1, H_C * V_C),
        "sink": jnp.broadcast_to(gqa_sink[l].reshape(H_D, 1), (H_D, LANE)),
        "w_out": w_out[l].astype(BF16),
    }


def _mixer(x, mod, group_len, wp, tabs, n_seq, seq_len, ctx):
    latent = ctx is not None
    p = _in_proj(x, mod, group_len, latent, wp, tabs)
    mix = {"bz": p["bz"], "cg": p["cg"]}
    cache = _mla_cache(ctx["ckv"], ctx["krope"], wp) if latent else None
    mix["oa"] = _mla(p["qa"], p["ka"], p["va"], n_seq, seq_len, cache)
    xbc = _conv(p["bxbc"], seq_len, wp["conv_w"], wp["conv_b"])
    mix["yf"], mix["yb"], st_b = _ssd(xbc, p["bdt"], n_seq, seq_len, wp["a_log"], wp["d_skip"],
                                      ctx["ssm"] if latent else None)
    mix["of"], mix["ob"], st_c = _gla(p["cq"], p["cf"], p["ci"], n_seq, seq_len, wp["hgrn_lb"],
                                      ctx["hgrn"] if latent else None)
    if latent:
        mix["od"] = _gqa_lat(p["dq"], p["dk"], p["dv"], ctx["dk"], ctx["dv"], wp["sink"], n_seq)
    else:
        mix["od"] = _gqa_ctx(p["dq"], p["dk"], p["dv"], wp["sink"], n_seq)
    state = None if latent else (p["ckv"], p["kr"], st_b, st_c, p["dkc"], p["dvc"])
    return mix, state


def _run_stream(x, mod, n_seq, seq_len, ctx, wp, ffn_w, lng, lnb, tabs):
    group_len = x.shape[0] if ctx is None else seq_len
    x = _ffn(x, mod, group_len, *ffn_w[0], lng[0], lnb[0], sub=0)
    mix, st = _mixer(x, mod, group_len, wp, tabs, n_seq, seq_len, ctx)
    x = _out_ffn(x, mod, group_len, mix, wp, lng[1], lnb[1], *ffn_w[1], lng[2], lnb[2])
    return x, st


def _layer_inputs(l, ctx_tensors, weights, hgrn_lb):
    (cache_a_ckv, cache_a_krope, state_b_ssm, state_c_hgrn, cache_d_k, cache_d_v) = ctx_tensors
    (ln_g, ln_b, ffn_w_gu, ffn_w_down, w_in, w_out, mla_q_norm, mla_kv_norm, mla_w_uq, mla_w_ukv, ssd_conv_w,
     ssd_conv_b, ssd_a_log, ssd_dt_bias, ssd_d, ssd_norm, hgrn_norm, gqa_sink) = weights
    wp = _prep_layer(l, w_in, w_out, mla_q_norm, mla_kv_norm, mla_w_uq, mla_w_ukv, ssd_conv_w, ssd_conv_b,
                     ssd_a_log, ssd_dt_bias, ssd_d, ssd_norm, hgrn_lb, hgrn_norm, gqa_sink)
    ffn_w = [(ffn_w_gu.astype(BF16), ffn_w_down.astype(BF16), (l, s)) for s in range(2)]
    lng = [ln_g[l, s].reshape(1, D_MODEL) for s in range(N_SUB)]
    lnb = [ln_b[l, s].reshape(1, D_MODEL) for s in range(N_SUB)]
    nb = cache_a_ckv.shape[0]
    ctx = {
        "ckv": cache_a_ckv[:, l],
        "krope": jnp.pad(cache_a_krope[:, l], ((0, 0), (0, 0), (NOPE_A, LANE - NOPE_A - ROPE_A))),
        "ssm": jnp.pad(state_b_ssm[:, l], ((0, 0),) * 3 + ((0, HB - P_B), (0, HB - N_B))),
        "hgrn": jnp.pad(jnp.transpose(state_c_hgrn[:, l], (0, 1, 4, 2, 3)).reshape(nb, 2, V_C, H_C * K_C),
                        ((0, 0), (0, 0), (0, HB - V_C), (0, 0))),
        "dk": jnp.concatenate([cache_d_k[:, l]] * 2, axis=-1).reshape(nb, PAST_LEN, KV_D * LANE).astype(BF16),
        "dv": jnp.concatenate([cache_d_v[:, l]] * 2, axis=-1).reshape(nb, PAST_LEN, KV_D * LANE).astype(BF16),
    }
    return wp, ffn_w, lng, lnb, ctx


def kernel(x_prompt, x_sample, cache_a_ckv, cache_a_krope, state_b_ssm, state_c_hgrn, cache_d_k, cache_d_v,
           c, c_ctx, w_mod, b_mod, ln_g, ln_b, ffn_w_gu, ffn_w_down, w_in, w_out, mla_q_norm, mla_kv_norm,
           mla_w_uq, mla_w_ukv, ssd_conv_w, ssd_conv_b, ssd_a_log, ssd_dt_bias, ssd_d, ssd_norm,
           hgrn_lb_logits, hgrn_norm, gqa_sink):
    lb_p = jax.nn.softmax(hgrn_lb_logits.astype(F32), axis=0)
    hgrn_lb = jnp.cumsum(lb_p, axis=0) - lb_p[:1]

    cvec = jnp.concatenate([c_ctx[None], c, jnp.zeros((SUBLANE - 1 - DEC_BATCH, D_MODEL), F32)], axis=0)
    mod_all = _modulation(cvec, w_mod, b_mod)
    tabs = _rope_tables(8, (NOPE_A,)) + _rope_tables(16, (0, HD_D))
    ctx_tensors = (cache_a_ckv, cache_a_krope, state_b_ssm, state_c_hgrn, cache_d_k, cache_d_v)
    weights = (ln_g, ln_b, ffn_w_gu, ffn_w_down, w_in, w_out, mla_q_norm, mla_kv_norm, mla_w_uq, mla_w_ukv,
               ssd_conv_w, ssd_conv_b, ssd_a_log, ssd_dt_bias, ssd_d, ssd_norm, hgrn_norm, gqa_sink)

    y_p = x_prompt.reshape(BATCH * SEQ, D_MODEL)
    y_s = x_sample.reshape(DEC_BATCH * DEC_SEQ, D_MODEL)
    states = []
    for l in range(DEPTH):
        wp, ffn_w, lng, lnb, ctx = _layer_inputs(l, ctx_tensors, weights, hgrn_lb)
        mod_ctx = mod_all[l, 0:1].reshape(1, N_SUB * 3, D_MODEL)
        mod_lat = mod_all[l, 1:1 + DEC_BATCH].reshape(DEC_BATCH, N_SUB * 3, D_MODEL)
        y_p, st = _run_stream(y_p, mod_ctx, BATCH, SEQ, None, wp, ffn_w, lng, lnb, tabs)
        y_s, _ = _run_stream(y_s, mod_lat, DEC_BATCH, DEC_SEQ, ctx, wp, ffn_w, lng, lnb, tabs)
        states.append(st)

    def stack(i, f):
        return jnp.stack([f(s[i]) for s in states], axis=1)

    new_a_ckv = stack(0, lambda t: t.reshape(BATCH, SEQ, KV_RANK))
    new_a_krope = stack(1, lambda t: t.reshape(BATCH, SEQ, LANE)[..., :ROPE_A])
    new_b_ssm = stack(2, lambda t: t)
    new_c_hgrn = stack(3, lambda t: t)
    new_d_k = stack(4, lambda t: t.reshape(BATCH, SEQ, KV_D, HD_D))
    new_d_v = stack(5, lambda t: t.reshape(BATCH, SEQ, KV_D, HD_D))
    return (y_p.reshape(BATCH, SEQ, D_MODEL), y_s.reshape(DEC_BATCH, DEC_SEQ, D_MODEL),
            new_a_ckv, new_a_krope, new_b_ssm, new_c_hgrn, new_d_k, new_d_v)
```

```python
import functools

import numpy as np
import jax
import jax.numpy as jnp
from jax import lax
from jax.experimental import pallas as pl
from jax.experimental.pallas import tpu as pltpu

F32 = jnp.float32
BF16 = jnp.bfloat16

D_MODEL = 1024
BATCH = 32
SEQ = 256
DEPTH = 2
DEC_BATCH = 2
DEC_SEQ = 4096
PAST_LEN = 512
GRID_W = 64
H_A, Q_RANK, KV_RANK, NOPE_A, ROPE_A, V_A = 4, 192, 128, 64, 32, 64
H_B, P_B, G_B, N_B, D_CONV, SSD_CHUNK = 4, 64, 2, 64, 5, 128
H_C, K_C, V_C, HGRN_CHUNK = 4, 64, 64, 16
H_D, KV_D, HD_D, WINDOW = 4, 2, 64, 128
G_D = H_D // KV_D
ROPE_BASE = 10000.0
D_FF = 2816
N_SUB = 3
ALPHA = (2 * DEPTH) ** 0.25
EPS = 1e-6
F_MIN = 1e-6
LOG2_E = 1.4426950408889634
NEG = -1e30
D_IN = 2920
N_MOD = N_SUB * 3 * D_MODEL

LANE = 128
SUBLANE = 8
VMEM_LIMIT = 56 * 1024 * 1024

TM = 512
TM_FFN = 1024
FF_CHUNK = 256
TQ_A = 256
GLA_TILE = 256
GLA_CHUNK = 32
SSD_STEP_CHUNKS = 8
GQA_CTX_SEQS = 8
MLA_CTX_SEQS = 8
MLA_REDUCE_GROUPS = 8
GQA_LAT_BLOCKS = 4
CONV_TILE = 256
MOD_TN = 1536
GLA_SAFE_LOG_DECAY = 60.0

C_ACQ, C_ACKV, C_AKR = 0, 256, 384
C_BZ, C_BXBC, C_BDT = 512, 768, 1280
C_CQ, C_CF, C_CI, C_CG = 1408, 1664, 2176, 2432
C_DQ, C_DK, C_DV = 2688, 2944, 3200
W_IN_P = 3456
W_XBC = 512
HB = 128


def _dot(a, b, precision=None):
    return jnp.dot(a, b, preferred_element_type=F32, precision=precision)


def _dot_nt(a, b):
    return lax.dot_general(a, b, (((1,), (1,)), ((), ())), preferred_element_type=F32)


def _prefix_dot(tri, x):
    t = tri.astype(BF16)
    hi = x.astype(BF16)
    rest = x - hi.astype(F32)
    mid = rest.astype(BF16)
    lo = (rest - mid.astype(F32)).astype(BF16)
    return _dot(t, hi) + _dot(t, mid) + _dot(t, lo)


def _params(*sem):
    return pltpu.CompilerParams(dimension_semantics=sem, vmem_limit_bytes=VMEM_LIMIT)


def _resident(shape, index=None):
    index = (0,) * len(shape) if index is None else index
    return pl.BlockSpec(shape, lambda *_: index, pipeline_mode=pl.Buffered(1))


def _silu(x):
    return x * jax.nn.sigmoid(x)


def _layer_norm(y, g, b):
    mu = jnp.mean(y, axis=-1, keepdims=True)
    yc = y - mu
    var = jnp.mean(yc * yc, axis=-1, keepdims=True)
    return yc * lax.rsqrt(var + EPS) * g + b


def _index_map(width, pieces):
    idx = np.full((width,), -1, np.int32)
    for dst, src, w in pieces:
        idx[dst:dst + w] = np.arange(src, src + w)
    return idx


def _gather_pad(arr, idx, axis):
    parts = []
    i = 0
    n = idx.shape[0]
    while i < n:
        j = i
        if idx[i] < 0:
            while j < n and idx[j] < 0:
                j += 1
            shape = list(arr.shape)
            shape[axis] = j - i
            parts.append(jnp.zeros(shape, arr.dtype))
        else:
            while j + 1 < n and idx[j + 1] == idx[j] + 1:
                j += 1
            j += 1
            parts.append(lax.slice_in_dim(arr, int(idx[i]), int(idx[i]) + j - i, axis=axis))
        i = j
    return jnp.concatenate(parts, axis=axis)


_IDX_W_IN = _index_map(W_IN_P, [
    (C_ACQ, 0, Q_RANK), (C_ACKV, 192, KV_RANK), (C_AKR, 320, ROPE_A),
    (C_BZ, 352, 256), (C_BXBC, 608, W_XBC), (C_BDT, 1120, 2 * H_B),
    (C_CQ, 1128, 256), (C_CF, 1384, 512), (C_CI, 1896, 256), (C_CG, 2152, 256),
    (C_DQ, 2408, 256),
    *[(c0 + LANE * g + HD_D * half, s0 + HD_D * g, HD_D)
      for c0, s0 in ((C_DK, 2664), (C_DV, 2792)) for g in range(KV_D) for half in range(2)]])
_IDX_UQ_ROWS = _index_map(256, [(0, 0, Q_RANK)])
_IDX_UQ_COLS = _index_map(4 * HB, [(HB * h, 96 * h, 96) for h in range(H_A)])
_IDX_UKV_K = _index_map(4 * HB, [(HB * h, 128 * h, NOPE_A) for h in range(H_A)])
_IDX_UKV_V = _index_map(4 * V_A, [(V_A * h, 128 * h + NOPE_A, V_A) for h in range(H_A)])


def _rope_tables(half, lane0s):
    t = np.arange(DEC_SEQ)
    pos = np.stack([t // GRID_W, t % GRID_W], 0).astype(np.float64)
    inv = ROPE_BASE ** (-np.arange(half, dtype=np.float64) / half)
    cos = np.ones((DEC_SEQ, LANE))
    sin = np.zeros((DEC_SEQ, LANE))
    for lane0 in lane0s:
        for axis in range(2):
            ang = pos[axis][:, None] * inv[None, :]
            base = lane0 + axis * 2 * half
            cos[:, base:base + half] = np.cos(ang)
            cos[:, base + half:base + 2 * half] = np.cos(ang)
            sin[:, base:base + half] = -np.sin(ang)
            sin[:, base + half:base + 2 * half] = np.sin(ang)
    ident_c = np.ones((TM, LANE))
    ident_s = np.zeros((TM, LANE))
    return (jnp.asarray(np.concatenate([ident_c, cos], 0), F32),
            jnp.asarray(np.concatenate([ident_s, sin], 0), F32))


def _rope(x, cos, sin, first, half):
    partner = jnp.where(first, pltpu.roll(x, LANE - half, 1), pltpu.roll(x, half, 1))
    return x * cos + partner * sin


def _low_lanes():
    return lax.broadcasted_iota(jnp.int32, (1, LANE), 1) < LANE // 2


def _head_alone(tile, odd):
    return jnp.where(_low_lanes(), pltpu.roll(tile, LANE // 2, 1) if odd else tile, 0.0)


def _two_heads(h_even, h_odd):
    return h_even + pltpu.roll(h_odd, LANE // 2, 1)


def _rms_heads(x, width):
    low = _low_lanes()
    tiles = []
    for t in range(width // LANE):
        blk = x[:, LANE * t:LANE * (t + 1)]
        sq = blk * blk
        s_all = jnp.sum(sq, axis=-1, keepdims=True)
        s_low = jnp.sum(jnp.where(low, sq, 0.0), axis=-1, keepdims=True)
        ms = jnp.where(low, s_low, s_all - s_low) * (2.0 / LANE)
        tiles.append(blk * lax.rsqrt(ms + EPS))
    return jnp.concatenate(tiles, axis=-1)


def _mod_kernel(c_ref, w_ref, b_ref, o_ref):
    c = c_ref[...]
    s = _silu(c).astype(BF16)
    o_ref[0] = _dot(s, w_ref[0].astype(BF16)) + b_ref[0]


def _modulation(cvec, w_mod, b_mod):
    return pl.pallas_call(
        _mod_kernel,
        grid=(DEPTH, N_MOD // MOD_TN),
        in_specs=[pl.BlockSpec((SUBLANE, D_MODEL), lambda l, j: (0, 0)),
                  pl.BlockSpec((1, D_MODEL, MOD_TN), lambda l, j: (l, 0, j)),
                  pl.BlockSpec((1, 1, MOD_TN), lambda l, j: (l, 0, j))],
        out_specs=pl.BlockSpec((1, SUBLANE, MOD_TN), lambda l, j: (l, 0, j)),
        out_shape=jax.ShapeDtypeStruct((DEPTH, SUBLANE, N_MOD), F32),
        compiler_params=_params("arbitrary", "arbitrary"),
        name="modulation",
    )(cvec, w_mod, b_mod.reshape(DEPTH, 1, N_MOD))


def _mod_spec(seq_len, tm=TM):
    return pl.BlockSpec((1, N_SUB * 3, D_MODEL), lambda i: (i * tm // seq_len, 0, 0))


def _ffn_sublayer(x, mod_ref, wg_ref, wu_ref, wd_ref, g_ref, b_ref, sub):
    shift = mod_ref[0, 3 * sub:3 * sub + 1, :]
    scale = mod_ref[0, 3 * sub + 1:3 * sub + 2, :]
    gate = mod_ref[0, 3 * sub + 2:3 * sub + 3, :]
    h = (x * (1.0 + scale) + shift).astype(BF16)
    acc = jnp.zeros(x.shape, F32)
    for start in range(0, D_FF, FF_CHUNK):
        cols = slice(start, min(start + FF_CHUNK, D_FF))
        gt = _dot(h, wg_ref[:, cols])
        up = _dot(h, wu_ref[:, cols])
        acc = acc + _dot((_silu(gt) * up).astype(BF16), wd_ref[cols, :])
    y = ALPHA * x + 0.5 * gate * acc
    return _layer_norm(y, g_ref[...], b_ref[...])


def _ffn_kernel(x_ref, mod_ref, wg_ref, wu_ref, wd_ref, g_ref, b_ref, o_ref, *, sub):
    o_ref[...] = _ffn_sublayer(x_ref[...], mod_ref, wg_ref, wu_ref, wd_ref, g_ref, b_ref, sub)


def _ffn(x, mod, seq_len, w_gu, w_down, layer_sub, ln_g, ln_b, sub):
    n = x.shape[0]
    l, s = layer_sub
    row = pl.BlockSpec((TM_FFN, D_MODEL), lambda i: (i, 0))
    return pl.pallas_call(
        functools.partial(_ffn_kernel, sub=sub),
        grid=(n // TM_FFN,),
        in_specs=[row, _mod_spec(seq_len, TM_FFN),
                  _resident((None, None, D_MODEL, D_FF), (l, s, 0, 0)),
                  _resident((None, None, D_MODEL, D_FF), (l, s, 0, 1)),
                  _resident((None, None, D_FF, D_MODEL), (l, s, 0, 0)),
                  _resident((1, D_MODEL)), _resident((1, D_MODEL))],
        out_specs=row,
        out_shape=jax.ShapeDtypeStruct((n, D_MODEL), F32),
        compiler_params=_params("arbitrary"),
        name="ffn",
    )(x, mod, w_gu, w_gu, w_down, ln_g, ln_b)


def _in_kernel(x_ref, mod_ref, w_ref, wuq_ref, wk_ref, wv_ref, gq_ref, gkv_ref, dtb_ref,
               cosq_ref, sinq_ref, cosd_ref, sind_ref, *out_refs, latent):
    out = dict(zip([name for name, _, _ in _in_outputs(latent)], out_refs))
    qa_ref, ka_ref, va_ref = out["qa"], out["ka"], out["va"]
    bz_ref, bxbc_ref, bdt_ref = out["bz"], out["bxbc"], out["bdt"]
    cq_ref, cf_ref, ci_ref, cg_ref = out["cq"], out["cf"], out["ci"], out["cg"]
    dq_ref, dk_ref, dv_ref = out["dq"], out["dk"], out["dv"]
    x = x_ref[...]
    h = (x * (1.0 + mod_ref[0, 4:5, :]) + mod_ref[0, 3:4, :]).astype(BF16)

    def proj(start, width):
        return _dot(h, w_ref[:, start:start + width])

    lane = lax.broadcasted_iota(jnp.int32, (TM, LANE), 1)
    first_a = (lane % 16) < 8
    first_d = (lane % 32) < 16

    def rope_a(blk):
        return _rope(blk, cosq_ref[...], sinq_ref[...], first_a, 8) if latent else blk

    def rope_d(blk):
        return _rope(blk, cosd_ref[...], sind_ref[...], first_d, 16) if latent else blk

    pa = proj(C_ACQ, C_BZ - C_ACQ)
    acq = pa[:, 0:256]
    ms = jnp.sum(acq * acq, axis=-1, keepdims=True) * (1.0 / Q_RANK)
    qn = (acq * lax.rsqrt(ms + EPS) * gq_ref[...]).astype(BF16)
    q = _dot(qn, wuq_ref[...])
    scale_a = (NOPE_A + ROPE_A) ** -0.5 * LOG2_E
    for hh in range(H_A):
        blk = slice(HB * hh, HB * (hh + 1))
        qa_ref[:, blk] = (rope_a(q[:, blk]) * scale_a).astype(BF16)
    ackv = pa[:, C_ACKV:C_ACKV + KV_RANK]
    ms = jnp.mean(ackv * ackv, axis=-1, keepdims=True)
    ckv = ackv * lax.rsqrt(ms + EPS) * gkv_ref[...]
    ckv_b = ckv.astype(BF16)
    kk = _dot(ckv_b, wk_ref[...])
    akr = pa[:, C_AKR:C_AKR + LANE]
    if not latent:
        out["ckv"][...] = ckv
        out["kr"][...] = akr
    krp = rope_a(pltpu.roll(akr, NOPE_A, 1))
    for hh in range(H_A):
        blk = slice(HB * hh, HB * (hh + 1))
        ka_ref[:, blk] = (kk[:, blk] + krp).astype(BF16)
    va_ref[...] = _dot(ckv_b, wv_ref[...]).astype(BF16)

    pb = proj(C_BZ, C_CQ - C_BZ)
    bz_ref[...] = pb[:, 0:H_B * P_B]
    bxbc_ref[...] = pb[:, C_BXBC - C_BZ:C_BXBC - C_BZ + W_XBC]
    dtr = pb[:, C_BDT - C_BZ:C_BDT - C_BZ + LANE] + dtb_ref[...]
    bdt_ref[...] = jnp.maximum(dtr, 0.0) + jnp.log(1.0 + jnp.exp(-jnp.abs(dtr)))

    pc = proj(C_CQ, C_DQ - C_CQ)
    cq_ref[...] = pc[:, 0:256]
    cf_ref[...] = pc[:, C_CF - C_CQ:C_CF - C_CQ + 512]
    ci_ref[...] = pc[:, C_CI - C_CQ:C_CI - C_CQ + 256]
    cg_ref[...] = pc[:, C_CG - C_CQ:C_CG - C_CQ + 256]

    pd = proj(C_DQ, W_IN_P - C_DQ)
    dq = pd[:, 0:H_D * HD_D]
    scale_d = HD_D ** -0.5
    for t in range(H_D * HD_D // LANE):
        blk = slice(LANE * t, LANE * (t + 1))
        dq_ref[:, blk] = (rope_d(dq[:, blk]) * scale_d).astype(BF16)
    dk = pd[:, C_DK - C_DQ:C_DK - C_DQ + KV_D * LANE]
    dk = [rope_d(dk[:, LANE * g:LANE * (g + 1)]) for g in range(KV_D)]
    for g in range(KV_D):
        dk_ref[:, LANE * g:LANE * (g + 1)] = dk[g]
    dv = pd[:, C_DV - C_DQ:C_DV - C_DQ + KV_D * LANE]
    dv_ref[...] = dv
    if not latent:
        out["dkc"][...] = jnp.where(_low_lanes(), dk[0], dk[1])
        out["dvc"][...] = jnp.where(_low_lanes(), dv[:, 0:LANE], dv[:, LANE:2 * LANE])


def _in_outputs(latent):
    outs = [("qa", 512, BF16), ("ka", 512, BF16), ("va", 256, BF16),
            ("bz", H_B * P_B, F32), ("bxbc", W_XBC, F32), ("bdt", LANE, F32),
            ("cq", 256, F32), ("cf", 512, F32), ("ci", 256, F32), ("cg", 256, F32),
            ("dq", H_D * HD_D, BF16), ("dk", KV_D * LANE, F32), ("dv", KV_D * LANE, F32)]
    if not latent:
        outs += [("ckv", KV_RANK, F32), ("kr", LANE, F32), ("dkc", KV_D * HD_D, F32), ("dvc", KV_D * HD_D, F32)]
    return outs


def _in_proj(x, mod, group_len, latent, wp, tabs):
    n = x.shape[0]
    row = lambda w: pl.BlockSpec((TM, w), lambda i: (i, 0))
    tab = pl.BlockSpec((TM, LANE), (lambda i: (1 + i % (DEC_SEQ // TM), 0)) if latent else (lambda i: (0, 0)))
    outs = pl.pallas_call(
        functools.partial(_in_kernel, latent=latent),
        grid=(n // TM,),
        in_specs=[row(D_MODEL), _mod_spec(group_len), _resident((D_MODEL, W_IN_P)), _resident((256, 512)),
                  _resident((KV_RANK, 512)), _resident((KV_RANK, 256)), _resident((1, 256)),
                  _resident((1, KV_RANK)), _resident((1, LANE)), tab, tab, tab, tab],
        out_specs=[row(w) for _, w, _ in _in_outputs(latent)],
        out_shape=[jax.ShapeDtypeStruct((n, w), dt) for _, w, dt in _in_outputs(latent)],
        compiler_params=_params("arbitrary"),
        name="in_proj",
    )(x, mod, wp["w_in"], wp["w_uq"], wp["w_uk"], wp["w_uv"], wp["g_q"], wp["g_kv"], wp["dt_bias"],
      tabs[0], tabs[1], tabs[2], tabs[3])
    return dict(zip([k for k, _, _ in _in_outputs(latent)], outs))


def _mla_cache_kernel(ckv_ref, krp_ref, wk_ref, wv_ref, kc_ref, vc_ref):
    ckv_b = ckv_ref[0].astype(BF16)
    kk = _dot(ckv_b, wk_ref[...])
    krp = krp_ref[0]
    for hh in range(H_A):
        blk = slice(HB * hh, HB * (hh + 1))
        kc_ref[0, :, blk] = (kk[:, blk] + krp).astype(BF16)
    vc_ref[0] = _dot(ckv_b, wv_ref[...]).astype(BF16)


def _mla_cache(ckv, krope_placed, wp):
    nb = ckv.shape[0]
    return pl.pallas_call(
        _mla_cache_kernel,
        grid=(nb,),
        in_specs=[pl.BlockSpec((1, PAST_LEN, KV_RANK), lambda b: (b, 0, 0)),
                  pl.BlockSpec((1, PAST_LEN, LANE), lambda b: (b, 0, 0)),
                  _resident((KV_RANK, 512)), _resident((KV_RANK, 256))],
        out_specs=[pl.BlockSpec((1, PAST_LEN, 512), lambda b: (b, 0, 0)),
                   pl.BlockSpec((1, PAST_LEN, 256), lambda b: (b, 0, 0))],
        out_shape=[jax.ShapeDtypeStruct((nb, PAST_LEN, 512), BF16),
                   jax.ShapeDtypeStruct((nb, PAST_LEN, 256), BF16)],
        compiler_params=_params("arbitrary"),
        name="mla_cache",
    )(ckv, krope_placed, wp["w_uk"], wp["w_uv"])


def _mla_kernel(*refs, has_cache, seqs):
    if has_cache:
        q_ref, k_ref, v_ref, kc_ref, vc_ref, o_ref = refs
    else:
        q_ref, k_ref, v_ref, o_ref = refs
    tk = k_ref.shape[0] // seqs
    tq = q_ref.shape[0] // seqs
    head_of_lane = lax.broadcasted_iota(jnp.int32, (1, H_A * V_A), 1) // V_A
    for b in range(seqs):
        qrows = slice(b * tq, (b + 1) * tq)
        krows = slice(b * tk, (b + 1) * tk)
        v = v_ref[krows, :]
        blocks = [slice(HB * hh, HB * (hh + 1)) for hh in range(H_A)]
        scores = [_dot_nt(q_ref[qrows, blk], k_ref[krows, blk]) for blk in blocks]
        if has_cache:
            scores_c = [_dot_nt(q_ref[qrows, blk], kc_ref[0, :, blk]) for blk in blocks]
        acc = jnp.zeros((tq, H_A * V_A), F32)
        for hh in range(H_A):
            s = scores[hh]
            m = jnp.max(s, axis=-1, keepdims=True)
            if has_cache:
                sc = scores_c[hh]
                m = jnp.maximum(m, jnp.max(sc, axis=-1, keepdims=True))
            e = jnp.exp2(s - m)
            den = jnp.sum(e, axis=-1, keepdims=True)
            pv = _dot(e.astype(BF16), v)
            if has_cache:
                ec = jnp.exp2(sc - m)
                den = den + jnp.sum(ec, axis=-1, keepdims=True)
                pv = pv + _dot(ec.astype(BF16), vc_ref[0])
            acc = jnp.where(head_of_lane == hh, pv / den, acc)
        o_ref[qrows, :] = acc.astype(BF16)


def _col_reduce(op, x):
    rows, q = x.shape
    g = MLA_REDUCE_GROUPS if rows % (MLA_REDUCE_GROUPS * SUBLANE) == 0 else 1
    return op(op(x.reshape(g, rows // g, q), axis=1), axis=0, keepdims=True)


def _mla_lat_kernel(q_ref, k_ref, v_ref, kc_ref, vc_ref, o_ref, vt_scr, vct_scr):
    @pl.when(pl.program_id(1) == 0)
    def _():
        vt_scr[...] = v_ref[...].astype(F32).T.astype(BF16)
        vct_scr[...] = vc_ref[0].astype(F32).T.astype(BF16)

    blocks = [slice(HB * hh, HB * (hh + 1)) for hh in range(H_A)]
    scores = [_dot_nt(k_ref[:, blk], q_ref[:, blk]) for blk in blocks]
    scores_c = [_dot_nt(kc_ref[0, :, blk], q_ref[:, blk]) for blk in blocks]
    outs = []
    for hh in range(H_A):
        vrows = slice(V_A * hh, V_A * (hh + 1))
        s, sc = scores[hh], scores_c[hh]
        m = jnp.maximum(_col_reduce(jnp.max, s), _col_reduce(jnp.max, sc))
        e = jnp.exp2(s - m)
        ec = jnp.exp2(sc - m)
        den = _col_reduce(jnp.sum, e) + _col_reduce(jnp.sum, ec)
        o_t = _dot(vt_scr[vrows, :], e.astype(BF16)) + _dot(vct_scr[vrows, :], ec.astype(BF16))
        outs.append(o_t / den)
    o_ref[...] = jnp.concatenate(outs, axis=0).T.astype(BF16)


def _mla_lat(q, k, v, n_seq, seq_len, cache):
    nq = seq_len // TQ_A
    return pl.pallas_call(
        _mla_lat_kernel,
        grid=(n_seq, nq),
        in_specs=[pl.BlockSpec((TQ_A, 512), lambda b, i: (b * nq + i, 0)),
                  pl.BlockSpec((seq_len, 512), lambda b, i: (b, 0)),
                  pl.BlockSpec((seq_len, 256), lambda b, i: (b, 0)),
                  pl.BlockSpec((1, PAST_LEN, 512), lambda b, i: (b, 0, 0)),
                  pl.BlockSpec((1, PAST_LEN, 256), lambda b, i: (b, 0, 0))],
        out_specs=pl.BlockSpec((TQ_A, H_A * V_A), lambda b, i: (b * nq + i, 0)),
        out_shape=jax.ShapeDtypeStruct((n_seq * seq_len, H_A * V_A), BF16),
        scratch_shapes=[pltpu.VMEM((H_A * V_A, seq_len), BF16), pltpu.VMEM((H_A * V_A, PAST_LEN), BF16)],
        compiler_params=_params("arbitrary", "arbitrary"),
        name="mla_latent",
    )(q, k, v, *cache)


def _mla(q, k, v, n_seq, seq_len, cache=None):
    tq = TQ_A if cache is not None else seq_len
    nq = seq_len // tq
    seqs = 1 if cache is not None else MLA_CTX_SEQS
    in_specs = [pl.BlockSpec((seqs * tq, 512), lambda b, i: (b * nq + i, 0)),
                pl.BlockSpec((seqs * seq_len, 512), lambda b, i: (b, 0)),
                pl.BlockSpec((seqs * seq_len, 256), lambda b, i: (b, 0))]
    args = [q, k, v]
    if cache is not None:
        in_specs += [pl.BlockSpec((1, PAST_LEN, 512), lambda b, i: (b, 0, 0)),
                     pl.BlockSpec((1, PAST_LEN, 256), lambda b, i: (b, 0, 0))]
        args += list(cache)
    return pl.pallas_call(
        functools.partial(_mla_kernel, has_cache=cache is not None, seqs=seqs),
        grid=(n_seq // seqs, nq),
        in_specs=in_specs,
        out_specs=pl.BlockSpec((seqs * tq, H_A * V_A), lambda b, i: (b * nq + i, 0)),
        out_shape=jax.ShapeDtypeStruct((n_seq * seq_len, H_A * V_A), BF16),
        compiler_params=_params("arbitrary", "arbitrary"),
        name="mla_attention",
    )(*args)


def _sink_softmax_pv(parts, sink):
    m = sink
    for s, _ in parts:
        m = jnp.maximum(m, jnp.max(s, axis=-1, keepdims=True))
    den = jnp.exp(sink - m)
    pv = None
    for s, v in parts:
        e = jnp.exp(s - m)
        den = den + jnp.sum(e, axis=-1, keepdims=True)
        t = _dot(e.astype(BF16), v)
        pv = t if pv is None else pv + t
    return pv / den


def _gqa_pair(q_ref, rows, g, scores_and_values, sink_ref, o_ref):
    t = rows.stop - rows.start
    tile = slice(LANE * g, LANE * (g + 1))
    qt = q_ref[rows, tile]
    low = _low_lanes()
    zero_b = jnp.zeros((), BF16)
    q2 = jnp.concatenate([jnp.where(low, qt, zero_b), jnp.where(low, zero_b, qt)], axis=0)
    second = lax.broadcasted_iota(jnp.int32, (G_D * t, 1), 0) >= t
    sink = jnp.where(second, sink_ref[G_D * g + 1:G_D * g + 2, 0:1], sink_ref[G_D * g:G_D * g + 1, 0:1])
    o = _sink_softmax_pv(scores_and_values(q2), sink)
    o_ref[rows, tile] = jnp.where(low, o[0:t, :], o[t:2 * t, :]).astype(BF16)


def _gqa_ctx_kernel(q_ref, k_ref, v_ref, sink_ref, o_ref):
    for b in range(GQA_CTX_SEQS):
        rows = slice(b * SEQ, (b + 1) * SEQ)
        for g in range(KV_D):
            tile = slice(LANE * g, LANE * (g + 1))
            k = k_ref[rows, tile].astype(BF16)
            v = v_ref[rows, tile].astype(BF16)
            _gqa_pair(q_ref, rows, g, lambda q2, k=k, v=v: [(_dot_nt(q2, k), v)], sink_ref, o_ref)


def _gqa_ctx(q, k, v, sink, n_seq):
    seq = lambda w: pl.BlockSpec((GQA_CTX_SEQS * SEQ, w), lambda b: (b, 0))
    return pl.pallas_call(
        _gqa_ctx_kernel,
        grid=(n_seq // GQA_CTX_SEQS,),
        in_specs=[seq(H_D * HD_D), seq(KV_D * LANE), seq(KV_D * LANE), _resident((H_D, LANE))],
        out_specs=seq(H_D * HD_D),
        out_shape=jax.ShapeDtypeStruct((n_seq * SEQ, H_D * HD_D), BF16),
        compiler_params=_params("arbitrary"),
        name="gqa_context",
    )(q, k, v, sink)


def _gqa_lat_kernel(q_ref, k_ref, v_ref, kc_ref, vc_ref, sink_ref, o_ref):
    span = 3 * WINDOW
    row = lax.broadcasted_iota(jnp.int32, (G_D * WINDOW, span), 0) % WINDOW
    col = lax.broadcasted_iota(jnp.int32, (G_D * WINDOW, span), 1)
    for j in range(GQA_LAT_BLOCKS):
        n = pl.program_id(1) * GQA_LAT_BLOCKS + j
        rows = slice(j * WINDOW, (j + 1) * WINDOW)
        start = pl.multiple_of(jnp.clip((n - 1) * WINDOW, 0, DEC_SEQ - span), WINDOW)
        band = jnp.abs((start + col) - (n * WINDOW + row)) <= WINDOW
        for g in range(KV_D):
            tile = slice(LANE * g, LANE * (g + 1))
            kw = k_ref[pl.ds(start, span), tile].astype(BF16)
            vw = v_ref[pl.ds(start, span), tile].astype(BF16)
            kc = kc_ref[0, :, tile]
            vc = vc_ref[0, :, tile]

            def parts(q2, kw=kw, vw=vw, kc=kc, vc=vc, band=band):
                return [(jnp.where(band, _dot_nt(q2, kw), NEG), vw), (_dot_nt(q2, kc), vc)]

            _gqa_pair(q_ref, rows, g, parts, sink_ref, o_ref)


def _gqa_lat(q, k, v, kc, vc, sink, n_seq):
    nb = DEC_SEQ // (WINDOW * GQA_LAT_BLOCKS)
    wq, wkv = H_D * HD_D, KV_D * LANE
    qblk = pl.BlockSpec((WINDOW * GQA_LAT_BLOCKS, wq), lambda b, n: (b * nb + n, 0))
    return pl.pallas_call(
        _gqa_lat_kernel,
        grid=(n_seq, nb),
        in_specs=[qblk,
                  pl.BlockSpec((DEC_SEQ, wkv), lambda b, n: (b, 0)),
                  pl.BlockSpec((DEC_SEQ, wkv), lambda b, n: (b, 0)),
                  pl.BlockSpec((1, PAST_LEN, wkv), lambda b, n: (b, 0, 0)),
                  pl.BlockSpec((1, PAST_LEN, wkv), lambda b, n: (b, 0, 0)),
                  _resident((H_D, LANE))],
        out_specs=qblk,
        out_shape=jax.ShapeDtypeStruct((n_seq * DEC_SEQ, wq), BF16),
        compiler_params=_params("arbitrary", "arbitrary"),
        name="gqa_latent",
    )(q, k, v, kc, vc, sink)


def _conv_kernel(cur_ref, prev_ref, next_ref, w_ref, b_ref, o_ref, pad_ref, *, tiles_per_seq):
    i = pl.program_id(0)
    has_prev = (i % tiles_per_seq) != 0
    has_next = (i % tiles_per_seq) != tiles_per_seq - 1
    pad_ref[0:SUBLANE, :] = jnp.where(has_prev, prev_ref[...], 0.0)
    pad_ref[SUBLANE:SUBLANE + CONV_TILE, :] = cur_ref[...]
    pad_ref[SUBLANE + CONV_TILE:, :] = jnp.where(has_next, next_ref[...], 0.0)
    y = jnp.zeros((CONV_TILE, W_XBC), F32) + b_ref[...]
    for k in range(D_CONV):
        off = SUBLANE - D_CONV // 2 + k
        y = y + w_ref[k:k + 1, :] * pad_ref[off:off + CONV_TILE, :]
    o_ref[...] = _silu(y)


def _conv(xbc, seq_len, w, b):
    n = xbc.shape[0]
    per = CONV_TILE // SUBLANE
    last = n // SUBLANE - 1
    return pl.pallas_call(
        functools.partial(_conv_kernel, tiles_per_seq=seq_len // CONV_TILE),
        grid=(n // CONV_TILE,),
        in_specs=[pl.BlockSpec((CONV_TILE, W_XBC), lambda i: (i, 0)),
                  pl.BlockSpec((SUBLANE, W_XBC), lambda i: (jnp.maximum(i * per - 1, 0), 0)),
                  pl.BlockSpec((SUBLANE, W_XBC), lambda i: (jnp.minimum((i + 1) * per, last), 0)),
                  _resident((SUBLANE, W_XBC)), _resident((1, W_XBC))],
        out_specs=pl.BlockSpec((CONV_TILE, W_XBC), lambda i: (i, 0)),
        out_shape=jax.ShapeDtypeStruct((n, W_XBC), F32),
        scratch_shapes=[pltpu.VMEM((CONV_TILE + 2 * SUBLANE, W_XBC), F32)],
        compiler_params=_params("arbitrary"),
        name="ssd_conv",
    )(xbc, xbc, xbc, w, b)


def _ssd_kernel(*refs, has_s0, chunks):
    if has_s0:
        xf_ref, xb_ref, dtf_ref, dtb_ref, alog_ref, dsk_ref, s0_ref, yf_ref, yb_ref, st_ref, s_scr = refs
    else:
        xf_ref, xb_ref, dtf_ref, dtb_ref, alog_ref, dsk_ref, yf_ref, yb_ref, st_ref, s_scr = refs
    c = pl.program_id(1)
    q = SSD_CHUNK

    @pl.when(c == 0)
    def _():
        s_scr[...] = s0_ref[0] if has_s0 else jnp.zeros(s_scr.shape, F32)

    row = lax.broadcasted_iota(jnp.int32, (q, q), 0)
    col = lax.broadcasted_iota(jnp.int32, (q, q), 1)
    a_coef = -jnp.exp(alog_ref[...])
    for d, (x_ref, dt_ref, y_ref) in enumerate(((xf_ref, dtf_ref, yf_ref), (xb_ref, dtb_ref, yb_ref))):
        tri = (row >= col) if d == 0 else (row <= col)
        for ci in (range(chunks) if d == 0 else range(chunks - 1, -1, -1)):
            rows = slice(ci * q, (ci + 1) * q)
            dt = dt_ref[rows, :]
            cum = _prefix_dot(tri, dt * a_coef)
            cum_t = cum.T
            dt_t = dt.T
            total = cum[q - 1:q, :] if d == 0 else cum[0:1, :]
            ys = []
            for g in range(G_B):
                bg = _head_alone(x_ref[rows, 2 * LANE:3 * LANE], g)
                cg = _head_alone(x_ref[rows, 3 * LANE:4 * LANE], g)
                cb = _dot_nt(cg.astype(BF16), bg.astype(BF16))
                for hh in range(g * (H_B // G_B), (g + 1) * (H_B // G_B)):
                    k = d * H_B + hh
                    cum_b = jnp.broadcast_to(cum[:, k:k + 1], (q, LANE))
                    seg = jnp.where(tri, jnp.exp(jnp.where(tri, cum_b - cum_t[k:k + 1, :], 0.0)), 0.0)
                    xh = _head_alone(x_ref[rows, LANE * (hh // 2):LANE * (hh // 2 + 1)], hh % 2)
                    dt_row = dt_t[k:k + 1, :]
                    s_in = s_scr[d, hh]
                    y = _dot((cb * seg * dt_row).astype(BF16), xh.astype(BF16))
                    y = y + _dot_nt((cg * jnp.exp(cum_b)).astype(BF16), s_in.astype(BF16))
                    ys.append(y + xh * dsk_ref[k:k + 1, :])
                    tot = total[:, k:k + 1]
                    bdec = bg * jnp.exp(tot - cum_b)
                    cs = _dot((xh.T * dt_row).astype(BF16), bdec.astype(BF16))
                    s_scr[d, hh] = jnp.exp(tot) * s_in + cs
            for p in range(H_B // 2):
                y_ref[rows, LANE * p:LANE * (p + 1)] = _two_heads(ys[2 * p], ys[2 * p + 1])

    @pl.when(c == pl.num_programs(1) - 1)
    def _():
        for d in range(2):
            for hh in range(H_B):
                st_ref[0, d, hh] = s_scr[d, hh, 0:P_B, 0:N_B]


def _ssd(xbc, dt, n_seq, seq_len, a_log, dskip, s0=None):
    chunks = min(SSD_STEP_CHUNKS, seq_len // SSD_CHUNK)
    nc = seq_len // (SSD_CHUNK * chunks)
    fwd = lambda w: pl.BlockSpec((SSD_CHUNK * chunks, w), lambda b, c: (b * nc + c, 0))
    bwd = lambda w: pl.BlockSpec((SSD_CHUNK * chunks, w), lambda b, c: (b * nc + nc - 1 - c, 0))
    state = pl.BlockSpec((1, 2, H_B, HB, HB), lambda b, c: (b, 0, 0, 0, 0))
    in_specs = [fwd(W_XBC), bwd(W_XBC), fwd(LANE), bwd(LANE), _resident((1, LANE)), _resident((2 * H_B, LANE))]
    args = [xbc, xbc, dt, dt, a_log, dskip]
    if s0 is not None:
        in_specs.append(state)
        args.append(s0)
    n = n_seq * seq_len
    return pl.pallas_call(
        functools.partial(_ssd_kernel, has_s0=s0 is not None, chunks=chunks),
        grid=(n_seq, nc),
        in_specs=in_specs,
        out_specs=[fwd(H_B * P_B), bwd(H_B * P_B),
                   pl.BlockSpec((1, 2, H_B, P_B, N_B), lambda b, c: (b, 0, 0, 0, 0))],
        out_shape=[jax.ShapeDtypeStruct((n, H_B * P_B), F32), jax.ShapeDtypeStruct((n, H_B * P_B), F32),
                   jax.ShapeDtypeStruct((n_seq, 2, H_B, P_B, N_B), F32)],
        scratch_shapes=[pltpu.VMEM((2, H_B, HB, HB), F32)],
        compiler_params=_params("arbitrary", "arbitrary"),
        name="ssd_scan",
    )(*args)


def _gla_kernel(*refs, has_s0):
    if has_s0:
        (qf_ref, qb_ref, ff_ref, fb_ref, vf_ref, vb_ref, lb_ref, s0_ref, of_ref, ob_ref, st_ref, s_scr,
         o_scr) = refs
    else:
        (qf_ref, qb_ref, ff_ref, fb_ref, vf_ref, vb_ref, lb_ref, of_ref, ob_ref, st_ref, s_scr, o_scr) = refs
    c = pl.program_id(1)
    t = GLA_TILE
    ch = GLA_CHUNK
    nch = t // ch
    w = H_C * K_C

    @pl.when(c == 0)
    def _():
        s_scr[...] = s0_ref[0] if has_s0 else jnp.zeros(s_scr.shape, F32)

    row = lax.broadcasted_iota(jnp.int32, (t, t), 0)
    col = lax.broadcasted_iota(jnp.int32, (t, t), 1)
    same_chunk = (row // ch) == (col // ch)
    head_of_lane = lax.broadcasted_iota(jnp.int32, (1, w), 1) // K_C
    row_head = lax.broadcasted_iota(jnp.int32, (H_C * ch, 1), 0) // ch
    low_half = lax.broadcasted_iota(jnp.int32, (1, LANE), 1) < V_C
    zero_b = jnp.zeros((), BF16)
    for d, (q_ref, f_ref, v_ref, o_ref) in enumerate(((qf_ref, ff_ref, vf_ref, of_ref),
                                                      (qb_ref, fb_ref, vb_ref, ob_ref))):
        qv = q_ref[...]
        fr = f_ref[...]
        vv = v_ref[...]
        lb = lb_ref[d:d + 1, :]
        f = lb + (1.0 - lb) * jax.nn.sigmoid(fr)
        log_f = jnp.log(jnp.maximum(f, F_MIN))
        key = (1.0 - lb) * jax.nn.sigmoid(-fr)
        tri = same_chunk & ((col <= row) if d == 0 else (col >= row))
        cum = _prefix_dot(tri, log_f)
        cum3 = cum.reshape(nch, ch, w)
        k3 = key.reshape(nch, ch, w)
        q_dec = (qv * jnp.exp(cum)).astype(BF16)
        q_heads = [jnp.where(head_of_lane == hh, q_dec, zero_b) for hh in range(H_C)]
        v_heads = []
        for hh in range(H_C):
            tile = vv[:, LANE * (hh // 2):LANE * (hh // 2 + 1)]
            v_heads.append(jnp.where(low_half, tile if hh % 2 == 0 else pltpu.roll(tile, V_C, 1), 0.0))

        k_inv = (key * jnp.exp(-cum)).astype(BF16)
        for hh in range(H_C):
            att = jnp.where(tri, _dot_nt(q_heads[hh], k_inv), 0.0).astype(BF16)
            o_scr[:, HB * hh:HB * (hh + 1)] = _dot(att, v_heads[hh].astype(BF16))

        @pl.when(jnp.min(cum) < -GLA_SAFE_LOG_DECAY)
        def _():
            wp = H_C * HB
            head_ones = ((lax.broadcasted_iota(jnp.int32, (w, wp), 0) // K_C)
                         == (lax.broadcasted_iota(jnp.int32, (w, wp), 1) // HB)).astype(F32)
            i_in_chunk = lax.broadcasted_iota(jnp.int32, (nch, ch, w), 1)
            q3 = qv.reshape(nch, ch, w)
            v3 = jnp.concatenate(v_heads, axis=1).reshape(nch, ch, wp)
            o3 = jnp.zeros((nch, ch, wp), F32)
            for j in range(ch):
                live = (i_in_chunk >= j) if d == 0 else (i_in_chunk <= j)
                e = jnp.exp(jnp.where(live, cum3 - cum3[:, j:j + 1, :], 0.0))
                term = jnp.where(live, q3 * e * k3[:, j:j + 1, :], 0.0)
                att = _dot(term.reshape(t, w), head_ones)
                o3 = o3 + att.reshape(nch, ch, wp) * v3[:, j:j + 1, :]
            o_scr[...] = o3.reshape(t, wp)

        edge = ch - 1 if d == 0 else 0
        last3 = jnp.broadcast_to(cum3[:, edge:edge + 1, :], (nch, ch, w))
        k_dec = (k3 * jnp.exp(last3 - cum3)).reshape(t, w).astype(BF16)
        order = range(nch) if d == 0 else range(nch - 1, -1, -1)
        for cc in order:
            rows = slice(cc * ch, (cc + 1) * ch)
            st = s_scr[d]
            q4 = jnp.concatenate([qh[rows, :] for qh in q_heads], axis=0)
            r = _dot_nt(q4, st.astype(BF16))
            for hh in range(H_C):
                blk = slice(HB * hh, HB * (hh + 1))
                o_scr[rows, blk] = o_scr[rows, blk] + r[hh * ch:(hh + 1) * ch, :]
            v4 = jnp.concatenate([vh[rows, :] for vh in v_heads], axis=0)
            k4 = jnp.where(row_head == head_of_lane, jnp.concatenate([k_dec[rows, :]] * H_C, axis=0), zero_b)
            decay = jnp.exp(cum[cc * ch + edge:cc * ch + edge + 1, :])
            s_scr[d] = decay * st + _dot(v4.T.astype(BF16), k4)

        for p in range(H_C // 2):
            o_ref[:, LANE * p:LANE * (p + 1)] = (o_scr[:, HB * 2 * p:HB * (2 * p + 1)]
                                                 + pltpu.roll(o_scr[:, HB * (2 * p + 1):HB * (2 * p + 2)], V_C, 1))

    @pl.when(c == pl.num_programs(1) - 1)
    def _():
        k_idx = lax.broadcasted_iota(jnp.int32, (K_C, w), 0)
        lane_idx = lax.broadcasted_iota(jnp.int32, (K_C, w), 1)
        for d in range(2):
            rows = s_scr[d, 0:V_C, :]
            for hh in range(H_C):
                sel = (lane_idx == hh * K_C + k_idx).astype(F32)
                st_ref[0, d, hh] = lax.dot_general(sel, rows, (((1,), (1,)), ((), ())),
                                                   preferred_element_type=F32, precision=lax.Precision.HIGHEST)


def _gla(cq, cf, ci, n_seq, seq_len, lb, s0=None):
    nt = seq_len // GLA_TILE
    w = H_C * K_C
    fwd = lambda j: pl.BlockSpec((GLA_TILE, w), lambda b, c: (b * nt + c, j))
    bwd = lambda j: pl.BlockSpec((GLA_TILE, w), lambda b, c: (b * nt + nt - 1 - c, j))
    state = pl.BlockSpec((1, 2, HB, w), lambda b, c: (b, 0, 0, 0))
    in_specs = [fwd(0), bwd(0), fwd(0), bwd(1), fwd(0), bwd(0), _resident((2, w))]
    args = [cq, cq, cf, cf, ci, ci, lb]
    if s0 is not None:
        in_specs.append(state)
        args.append(s0)
    n = n_seq * seq_len
    return pl.pallas_call(
        functools.partial(_gla_kernel, has_s0=s0 is not None),
        grid=(n_seq, nt),
        in_specs=in_specs,
        out_specs=[fwd(0), bwd(0), pl.BlockSpec((1, 2, H_C, K_C, V_C), lambda b, c: (b, 0, 0, 0, 0))],
        out_shape=[jax.ShapeDtypeStruct((n, w), F32), jax.ShapeDtypeStruct((n, w), F32),
                   jax.ShapeDtypeStruct((n_seq, 2, H_C, K_C, V_C), F32)],
        scratch_shapes=[pltpu.VMEM((2, HB, w), F32), pltpu.VMEM((GLA_TILE, H_C * HB), F32)],
        compiler_params=_params("arbitrary", "arbitrary"),
        name="hgrn_scan",
    )(*args)


def _out_ffn_kernel(x_ref, mod_ref, oa_ref, yf_ref, yb_ref, bz_ref, of_ref, ob_ref, cg_ref, od_ref,
                    wo_ref, nb_ref, nc_ref, g1_ref, b1_ref, wg_ref, wu_ref, wd_ref, g2_ref, b2_ref, o_ref):
    x = x_ref[...]
    gate = mod_ref[0, 5:6, :]
    yb = _rms_heads((yf_ref[...] + yb_ref[...]) * _silu(bz_ref[...]), H_B * P_B) * nb_ref[...]
    oc = _rms_heads(of_ref[...] + ob_ref[...], H_C * V_C) * nc_ref[...] * _silu(cg_ref[...])
    mixed = jnp.concatenate([oa_ref[...], yb.astype(BF16), oc.astype(BF16), od_ref[...]], axis=-1)
    u = _dot(mixed, wo_ref[...])
    x = _layer_norm(ALPHA * x + gate * u, g1_ref[...], b1_ref[...])
    o_ref[...] = _ffn_sublayer(x, mod_ref, wg_ref, wu_ref, wd_ref, g2_ref, b2_ref, 2)


def _out_ffn(x, mod, seq_len, mix, wp, ln1_g, ln1_b, w_gu, w_down, layer_sub, ln2_g, ln2_b):
    n = x.shape[0]
    l, s = layer_sub
    row = lambda w: pl.BlockSpec((TM, w), lambda i: (i, 0))
    names = ("oa", "yf", "yb", "bz", "of", "ob", "cg", "od")
    vec = _resident((1, D_MODEL))
    return pl.pallas_call(
        _out_ffn_kernel,
        grid=(n // TM,),
        in_specs=[row(D_MODEL), _mod_spec(seq_len)] + [row(mix[k].shape[1]) for k in names]
                 + [_resident((D_MODEL, D_MODEL)), _resident((1, H_B * P_B)), _resident((1, H_C * V_C)), vec, vec,
                    _resident((None, None, D_MODEL, D_FF), (l, s, 0, 0)),
                    _resident((None, None, D_MODEL, D_FF), (l, s, 0, 1)),
                    _resident((None, None, D_FF, D_MODEL), (l, s, 0, 0)), vec, vec],
        out_specs=row(D_MODEL),
        out_shape=jax.ShapeDtypeStruct((n, D_MODEL), F32),
        compiler_params=_params("arbitrary"),
        name="out_proj_ffn",
    )(x, mod, *[mix[k] for k in names], wp["w_out"], wp["ssd_norm"], wp["hgrn_norm"], ln1_g, ln1_b,
      w_gu, w_gu, w_down, ln2_g, ln2_b)


def _prep_layer(l, w_in, w_out, mla_q_norm, mla_kv_norm, mla_w_uq, mla_w_ukv, ssd_conv_w, ssd_conv_b,
                ssd_a_log, ssd_dt_bias, ssd_d, ssd_norm, hgrn_lb, hgrn_norm, gqa_sink):
    return {
        "w_in": _gather_pad(w_in[l], _IDX_W_IN, 1).astype(BF16),
        "w_uq": _gather_pad(_gather_pad(mla_w_uq[l], _IDX_UQ_ROWS, 0), _IDX_UQ_COLS, 1).astype(BF16),
        "w_uk": _gather_pad(mla_w_ukv[l], _IDX_UKV_K, 1).astype(BF16),
        "w_uv": _gather_pad(mla_w_ukv[l], _IDX_UKV_V, 1).astype(BF16),
        "g_q": _gather_pad(mla_q_norm[l], _IDX_UQ_ROWS, 0).reshape(1, 256),
        "g_kv": mla_kv_norm[l].reshape(1, KV_RANK),
        "dt_bias": jnp.pad(ssd_dt_bias[l].reshape(1, 2 * H_B), ((0, 0), (0, LANE - 2 * H_B))),
        "conv_w": jnp.pad(ssd_conv_w[l], ((0, SUBLANE - D_CONV), (0, 0))),
        "conv_b": ssd_conv_b[l].reshape(1, W_XBC),
        "a_log": jnp.pad(ssd_a_log[l].reshape(1, 2 * H_B), ((0, 0), (0, LANE - 2 * H_B))),
        "d_skip": jnp.broadcast_to(ssd_d[l].reshape(2 * H_B, 1), (2 * H_B, LANE)),
        "ssd_norm": ssd_norm[l].reshape(1, H_B * P_B),
        "hgrn_lb": hgrn_lb[l],
        "hgrn_norm": hgrn_norm[l].reshape(1, H_C * V_C),
        "sink": jnp.broadcast_to(gqa_sink[l].reshape(H_D, 1), (H_D, LANE)),
        "w_out": w_out[l].astype(BF16),
    }


def _mixer(x, mod, group_len, wp, tabs, n_seq, seq_len, ctx):
    latent = ctx is not None
    p = _in_proj(x, mod, group_len, latent, wp, tabs)
    mix = {"bz": p["bz"], "cg": p["cg"]}
    cache = _mla_cache(ctx["ckv"], ctx["krope"], wp) if latent else None
    mix["oa"] = (_mla_lat if latent else _mla)(p["qa"], p["ka"], p["va"], n_seq, seq_len, cache)
    xbc = _conv(p["bxbc"], seq_len, wp["conv_w"], wp["conv_b"])
    mix["yf"], mix["yb"], st_b = _ssd(xbc, p["bdt"], n_seq, seq_len, wp["a_log"], wp["d_skip"],
                                      ctx["ssm"] if latent else None)
    mix["of"], mix["ob"], st_c = _gla(p["cq"], p["cf"], p["ci"], n_seq, seq_len, wp["hgrn_lb"],
                                      ctx["hgrn"] if latent else None)
    if latent:
        mix["od"] = _gqa_lat(p["dq"], p["dk"], p["dv"], ctx["dk"], ctx["dv"], wp["sink"], n_seq)
    else:
        mix["od"] = _gqa_ctx(p["dq"], p["dk"], p["dv"], wp["sink"], n_seq)
    state = None if latent else (p["ckv"], p["kr"], st_b, st_c, p["dkc"], p["dvc"])
    return mix, state


def _run_stream(x, mod, n_seq, seq_len, ctx, wp, ffn_w, lng, lnb, tabs):
    group_len = x.shape[0] if ctx is None else seq_len
    x = _ffn(x, mod, group_len, *ffn_w[0], lng[0], lnb[0], sub=0)
    mix, st = _mixer(x, mod, group_len, wp, tabs, n_seq, seq_len, ctx)
    x = _out_ffn(x, mod, group_len, mix, wp, lng[1], lnb[1], *ffn_w[1], lng[2], lnb[2])
    return x, st


def _layer_inputs(l, ctx_tensors, weights, hgrn_lb):
    (cache_a_ckv, cache_a_krope, state_b_ssm, state_c_hgrn, cache_d_k, cache_d_v) = ctx_tensors
    (ln_g, ln_b, ffn_w_gu, ffn_w_down, w_in, w_out, mla_q_norm, mla_kv_norm, mla_w_uq, mla_w_ukv, ssd_conv_w,
     ssd_conv_b, ssd_a_log, ssd_dt_bias, ssd_d, ssd_norm, hgrn_norm, gqa_sink) = weights
    wp = _prep_layer(l, w_in, w_out, mla_q_norm, mla_kv_norm, mla_w_uq, mla_w_ukv, ssd_conv_w, ssd_conv_b,
                     ssd_a_log, ssd_dt_bias, ssd_d, ssd_norm, hgrn_lb, hgrn_norm, gqa_sink)
    ffn_w = [(ffn_w_gu.astype(BF16), ffn_w_down.astype(BF16), (l, s)) for s in range(2)]
    lng = [ln_g[l, s].reshape(1, D_MODEL) for s in range(N_SUB)]
    lnb = [ln_b[l, s].reshape(1, D_MODEL) for s in range(N_SUB)]
    nb = cache_a_ckv.shape[0]
    ctx = {
        "ckv": cache_a_ckv[:, l],
        "krope": jnp.pad(cache_a_krope[:, l], ((0, 0), (0, 0), (NOPE_A, LANE - NOPE_A - ROPE_A))),
        "ssm": jnp.pad(state_b_ssm[:, l], ((0, 0),) * 3 + ((0, HB - P_B), (0, HB - N_B))),
        "hgrn": jnp.pad(jnp.transpose(state_c_hgrn[:, l], (0, 1, 4, 2, 3)).reshape(nb, 2, V_C, H_C * K_C),
                        ((0, 0), (0, 0), (0, HB - V_C), (0, 0))),
        "dk": jnp.concatenate([cache_d_k[:, l]] * 2, axis=-1).reshape(nb, PAST_LEN, KV_D * LANE).astype(BF16),
        "dv": jnp.concatenate([cache_d_v[:, l]] * 2, axis=-1).reshape(nb, PAST_LEN, KV_D * LANE).astype(BF16),
    }
    return wp, ffn_w, lng, lnb, ctx


def kernel(x_prompt, x_sample, cache_a_ckv, cache_a_krope, state_b_ssm, state_c_hgrn, cache_d_k, cache_d_v,
           c, c_ctx, w_mod, b_mod, ln_g, ln_b, ffn_w_gu, ffn_w_down, w_in, w_out, mla_q_norm, mla_kv_norm,
           mla_w_uq, mla_w_ukv, ssd_conv_w, ssd_conv_b, ssd_a_log, ssd_dt_bias, ssd_d, ssd_norm,
           hgrn_lb_logits, hgrn_norm, gqa_sink):
    lb_p = jax.nn.softmax(hgrn_lb_logits.astype(F32), axis=0)
    hgrn_lb = jnp.cumsum(lb_p, axis=0) - lb_p[:1]

    cvec = jnp.concatenate([c_ctx[None], c, jnp.zeros((SUBLANE - 1 - DEC_BATCH, D_MODEL), F32)], axis=0)
    mod_all = _modulation(cvec, w_mod, b_mod)
    tabs = _rope_tables(8, (NOPE_A,)) + _rope_tables(16, (0, HD_D))
    ctx_tensors = (cache_a_ckv, cache_a_krope, state_b_ssm, state_c_hgrn, cache_d_k, cache_d_v)
    weights = (ln_g, ln_b, ffn_w_gu, ffn_w_down, w_in, w_out, mla_q_norm, mla_kv_norm, mla_w_uq, mla_w_ukv,
               ssd_conv_w, ssd_conv_b, ssd_a_log, ssd_dt_bias, ssd_d, ssd_norm, hgrn_norm, gqa_sink)

    y_p = x_prompt.reshape(BATCH * SEQ, D_MODEL)
    y_s = x_sample.reshape(DEC_BATCH * DEC_SEQ, D_MODEL)
    states = []
    for l in range(DEPTH):
        wp, ffn_w, lng, lnb, ctx = _layer_inputs(l, ctx_tensors, weights, hgrn_lb)
        mod_ctx = mod_all[l, 0:1].reshape(1, N_SUB * 3, D_MODEL)
        mod_lat = mod_all[l, 1:1 + DEC_BATCH].reshape(DEC_BATCH, N_SUB * 3, D_MODEL)
        y_p, st = _run_stream(y_p, mod_ctx, BATCH, SEQ, None, wp, ffn_w, lng, lnb, tabs)
        y_s, _ = _run_stream(y_s, mod_lat, DEC_BATCH, DEC_SEQ, ctx, wp, ffn_w, lng, lnb, tabs)
        states.append(st)

    def stack(i, f):
        return jnp.stack([f(s[i]) for s in states], axis=1)

    new_a_ckv = stack(0, lambda t: t.reshape(BATCH, SEQ, KV_RANK))
    new_a_krope = stack(1, lambda t: t.reshape(BATCH, SEQ, LANE)[..., :ROPE_A])
    new_b_ssm = stack(2, lambda t: t)
    new_c_hgrn = stack(3, lambda t: t)
    new_d_k = stack(4, lambda t: t.reshape(BATCH, SEQ, KV_D, HD_D))
    new_d_v = stack(5, lambda t: t.reshape(BATCH, SEQ, KV_D, HD_D))
    return (y_p.reshape(BATCH, SEQ, D_MODEL), y_s.reshape(DEC_BATCH, DEC_SEQ, D_MODEL),
            new_a_ckv, new_a_krope, new_b_ssm, new_c_hgrn, new_d_k, new_d_v)
```

```python
import functools

import numpy as np
import jax
import jax.numpy as jnp
from jax import lax
from jax.experimental import pallas as pl
from jax.experimental.pallas import tpu as pltpu

F32 = jnp.float32
BF16 = jnp.bfloat16

D_MODEL = 1024
BATCH = 32
SEQ = 256
DEPTH = 2
DEC_BATCH = 2
DEC_SEQ = 4096
PAST_LEN = 512
GRID_W = 64
H_A, Q_RANK, KV_RANK, NOPE_A, ROPE_A, V_A = 4, 192, 128, 64, 32, 64
H_B, P_B, G_B, N_B, D_CONV, SSD_CHUNK = 4, 64, 2, 64, 5, 128
H_C, K_C, V_C, HGRN_CHUNK = 4, 64, 64, 16
H_D, KV_D, HD_D, WINDOW = 4, 2, 64, 128
G_D = H_D // KV_D
ROPE_BASE = 10000.0
D_FF = 2816
N_SUB = 3
ALPHA = (2 * DEPTH) ** 0.25
EPS = 1e-6
F_MIN = 1e-6
LOG2_E = 1.4426950408889634
NEG = -1e30
D_IN = 2920
N_MOD = N_SUB * 3 * D_MODEL

LANE = 128
SUBLANE = 8
VMEM_LIMIT = 56 * 1024 * 1024

TM = 512
TM_FFN = 1024
FF_CHUNK = 256
TQ_A = 256
GLA_TILE = 256
GLA_CHUNK = 32
SSD_STEP_CHUNKS = 8
GQA_CTX_SEQS = 8
MLA_CTX_SEQS = 8
MLA_REDUCE_GROUPS = 8
GQA_LAT_BLOCKS = 4
CONV_TILE = 256
MOD_TN = 1536
GLA_SAFE_LOG_DECAY = 60.0

C_ACQ, C_ACKV, C_AKR = 0, 256, 384
C_BZ, C_BXBC, C_BDT = 512, 768, 1280
C_CQ, C_CF, C_CI, C_CG = 1408, 1664, 2176, 2432
C_DQ, C_DK, C_DV = 2688, 2944, 3200
W_IN_P = 3456
W_XBC = 512
HB = 128


def _dot(a, b, precision=None):
    return jnp.dot(a, b, preferred_element_type=F32, precision=precision)


def _dot_nt(a, b):
    return lax.dot_general(a, b, (((1,), (1,)), ((), ())), preferred_element_type=F32)


def _prefix_dot(tri, x):
    t = tri.astype(BF16)
    hi = x.astype(BF16)
    rest = x - hi.astype(F32)
    mid = rest.astype(BF16)
    lo = (rest - mid.astype(F32)).astype(BF16)
    return _dot(t, hi) + _dot(t, mid) + _dot(t, lo)


def _params(*sem):
    return pltpu.CompilerParams(dimension_semantics=sem, vmem_limit_bytes=VMEM_LIMIT)


def _resident(shape, index=None):
    index = (0,) * len(shape) if index is None else index
    return pl.BlockSpec(shape, lambda *_: index, pipeline_mode=pl.Buffered(1))


def _silu(x):
    return x * jax.nn.sigmoid(x)


def _layer_norm(y, g, b):
    mu = jnp.mean(y, axis=-1, keepdims=True)
    yc = y - mu
    var = jnp.mean(yc * yc, axis=-1, keepdims=True)
    return yc * lax.rsqrt(var + EPS) * g + b


def _index_map(width, pieces):
    idx = np.full((width,), -1, np.int32)
    for dst, src, w in pieces:
        idx[dst:dst + w] = np.arange(src, src + w)
    return idx


def _gather_pad(arr, idx, axis):
    parts = []
    i = 0
    n = idx.shape[0]
    while i < n:
        j = i
        if idx[i] < 0:
            while j < n and idx[j] < 0:
                j += 1
            shape = list(arr.shape)
            shape[axis] = j - i
            parts.append(jnp.zeros(shape, arr.dtype))
        else:
            while j + 1 < n and idx[j + 1] == idx[j] + 1:
                j += 1
            j += 1
            parts.append(lax.slice_in_dim(arr, int(idx[i]), int(idx[i]) + j - i, axis=axis))
        i = j
    return jnp.concatenate(parts, axis=axis)


_IDX_W_IN = _index_map(W_IN_P, [
    (C_ACQ, 0, Q_RANK), (C_ACKV, 192, KV_RANK), (C_AKR, 320, ROPE_A),
    (C_BZ, 352, 256), (C_BXBC, 608, W_XBC), (C_BDT, 1120, 2 * H_B),
    (C_CQ, 1128, 256), (C_CF, 1384, 512), (C_CI, 1896, 256), (C_CG, 2152, 256),
    (C_DQ, 2408, 256),
    *[(c0 + LANE * g + HD_D * half, s0 + HD_D * g, HD_D)
      for c0, s0 in ((C_DK, 2664), (C_DV, 2792)) for g in range(KV_D) for half in range(2)]])
_IDX_UQ_ROWS = _index_map(256, [(0, 0, Q_RANK)])
_IDX_UQ_COLS = _index_map(4 * HB, [(HB * h, 96 * h, 96) for h in range(H_A)])
_IDX_UKV_K = _index_map(4 * HB, [(HB * h, 128 * h, NOPE_A) for h in range(H_A)])
_IDX_UKV_V = _index_map(4 * V_A, [(V_A * h, 128 * h + NOPE_A, V_A) for h in range(H_A)])


def _rope_tables(half, lane0s):
    t = np.arange(DEC_SEQ)
    pos = np.stack([t // GRID_W, t % GRID_W], 0).astype(np.float64)
    inv = ROPE_BASE ** (-np.arange(half, dtype=np.float64) / half)
    cos = np.ones((DEC_SEQ, LANE))
    sin = np.zeros((DEC_SEQ, LANE))
    for lane0 in lane0s:
        for axis in range(2):
            ang = pos[axis][:, None] * inv[None, :]
            base = lane0 + axis * 2 * half
            cos[:, base:base + half] = np.cos(ang)
            cos[:, base + half:base + 2 * half] = np.cos(ang)
            sin[:, base:base + half] = -np.sin(ang)
            sin[:, base + half:base + 2 * half] = np.sin(ang)
    ident_c = np.ones((TM, LANE))
    ident_s = np.zeros((TM, LANE))
    return (jnp.asarray(np.concatenate([ident_c, cos], 0), F32),
            jnp.asarray(np.concatenate([ident_s, sin], 0), F32))


def _rope(x, cos, sin, first, half):
    partner = jnp.where(first, pltpu.roll(x, LANE - half, 1), pltpu.roll(x, half, 1))
    return x * cos + partner * sin


def _low_lanes():
    return lax.broadcasted_iota(jnp.int32, (1, LANE), 1) < LANE // 2


def _head_alone(tile, odd):
    return jnp.where(_low_lanes(), pltpu.roll(tile, LANE // 2, 1) if odd else tile, 0.0)


def _two_heads(h_even, h_odd):
    return h_even + pltpu.roll(h_odd, LANE // 2, 1)


def _rms_heads(x, width):
    low = _low_lanes()
    tiles = []
    for t in range(width // LANE):
        blk = x[:, LANE * t:LANE * (t + 1)]
        sq = blk * blk
        s_all = jnp.sum(sq, axis=-1, keepdims=True)
        s_low = jnp.sum(jnp.where(low, sq, 0.0), axis=-1, keepdims=True)
        ms = jnp.where(low, s_low, s_all - s_low) * (2.0 / LANE)
        tiles.append(blk * lax.rsqrt(ms + EPS))
    return jnp.concatenate(tiles, axis=-1)


def _mod_kernel(c_ref, w_ref, b_ref, o_ref):
    c = c_ref[...]
    s = _silu(c).astype(BF16)
    o_ref[0] = _dot(s, w_ref[0].astype(BF16)) + b_ref[0]


def _modulation(cvec, w_mod, b_mod):
    return pl.pallas_call(
        _mod_kernel,
        grid=(DEPTH, N_MOD // MOD_TN),
        in_specs=[pl.BlockSpec((SUBLANE, D_MODEL), lambda l, j: (0, 0)),
                  pl.BlockSpec((1, D_MODEL, MOD_TN), lambda l, j: (l, 0, j)),
                  pl.BlockSpec((1, 1, MOD_TN), lambda l, j: (l, 0, j))],
        out_specs=pl.BlockSpec((1, SUBLANE, MOD_TN), lambda l, j: (l, 0, j)),
        out_shape=jax.ShapeDtypeStruct((DEPTH, SUBLANE, N_MOD), F32),
        compiler_params=_params("arbitrary", "arbitrary"),
        name="modulation",
    )(cvec, w_mod, b_mod.reshape(DEPTH, 1, N_MOD))


def _mod_spec(seq_len, tm=TM):
    return pl.BlockSpec((1, N_SUB * 3, D_MODEL), lambda i: (i * tm // seq_len, 0, 0))


def _ffn_sublayer(x, mod_ref, wg_ref, wu_ref, wd_ref, g_ref, b_ref, sub):
    shift = mod_ref[0, 3 * sub:3 * sub + 1, :]
    scale = mod_ref[0, 3 * sub + 1:3 * sub + 2, :]
    gate = mod_ref[0, 3 * sub + 2:3 * sub + 3, :]
    h = (x * (1.0 + scale) + shift).astype(BF16)
    acc = jnp.zeros(x.shape, F32)
    for start in range(0, D_FF, FF_CHUNK):
        cols = slice(start, min(start + FF_CHUNK, D_FF))
        gt = _dot(h, wg_ref[:, cols])
        up = _dot(h, wu_ref[:, cols])
        acc = acc + _dot((_silu(gt) * up).astype(BF16), wd_ref[cols, :])
    y = ALPHA * x + 0.5 * gate * acc
    return _layer_norm(y, g_ref[...], b_ref[...])


def _ffn_kernel(x_ref, mod_ref, wg_ref, wu_ref, wd_ref, g_ref, b_ref, o_ref, *, sub):
    o_ref[...] = _ffn_sublayer(x_ref[...], mod_ref, wg_ref, wu_ref, wd_ref, g_ref, b_ref, sub)


def _ffn(x, mod, seq_len, w_gu, w_down, layer_sub, ln_g, ln_b, sub):
    n = x.shape[0]
    l, s = layer_sub
    row = pl.BlockSpec((TM_FFN, D_MODEL), lambda i: (i, 0))
    return pl.pallas_call(
        functools.partial(_ffn_kernel, sub=sub),
        grid=(n // TM_FFN,),
        in_specs=[row, _mod_spec(seq_len, TM_FFN),
                  _resident((None, None, D_MODEL, D_FF), (l, s, 0, 0)),
                  _resident((None, None, D_MODEL, D_FF), (l, s, 0, 1)),
                  _resident((None, None, D_FF, D_MODEL), (l, s, 0, 0)),
                  _resident((1, D_MODEL)), _resident((1, D_MODEL))],
        out_specs=row,
        out_shape=jax.ShapeDtypeStruct((n, D_MODEL), F32),
        compiler_params=_params("arbitrary"),
        name="ffn",
    )(x, mod, w_gu, w_gu, w_down, ln_g, ln_b)


def _in_kernel(x_ref, mod_ref, w_ref, wuq_ref, wk_ref, wv_ref, gq_ref, gkv_ref, dtb_ref,
               cosq_ref, sinq_ref, cosd_ref, sind_ref, *out_refs, latent):
    out = dict(zip([name for name, _, _ in _in_outputs(latent)], out_refs))
    qa_ref, ka_ref, va_ref = out["qa"], out["ka"], out["va"]
    bz_ref, bxbc_ref, bdt_ref = out["bz"], out["bxbc"], out["bdt"]
    cq_ref, cf_ref, ci_ref, cg_ref = out["cq"], out["cf"], out["ci"], out["cg"]
    dq_ref, dk_ref, dv_ref = out["dq"], out["dk"], out["dv"]
    x = x_ref[...]
    h = (x * (1.0 + mod_ref[0, 4:5, :]) + mod_ref[0, 3:4, :]).astype(BF16)

    def proj(start, width):
        return _dot(h, w_ref[:, start:start + width])

    lane = lax.broadcasted_iota(jnp.int32, (TM, LANE), 1)
    first_a = (lane % 16) < 8
    first_d = (lane % 32) < 16

    def rope_a(blk):
        return _rope(blk, cosq_ref[...], sinq_ref[...], first_a, 8) if latent else blk

    def rope_d(blk):
        return _rope(blk, cosd_ref[...], sind_ref[...], first_d, 16) if latent else blk

    pa = proj(C_ACQ, C_BZ - C_ACQ)
    acq = pa[:, 0:256]
    ms = jnp.sum(acq * acq, axis=-1, keepdims=True) * (1.0 / Q_RANK)
    qn = (acq * lax.rsqrt(ms + EPS) * gq_ref[...]).astype(BF16)
    q = _dot(qn, wuq_ref[...])
    scale_a = (NOPE_A + ROPE_A) ** -0.5 * LOG2_E
    for hh in range(H_A):
        blk = slice(HB * hh, HB * (hh + 1))
        qa_ref[:, blk] = (rope_a(q[:, blk]) * scale_a).astype(BF16)
    ackv = pa[:, C_ACKV:C_ACKV + KV_RANK]
    ms = jnp.mean(ackv * ackv, axis=-1, keepdims=True)
    ckv = ackv * lax.rsqrt(ms + EPS) * gkv_ref[...]
    ckv_b = ckv.astype(BF16)
    kk = _dot(ckv_b, wk_ref[...])
    akr = pa[:, C_AKR:C_AKR + LANE]
    if not latent:
        out["ckv"][...] = ckv
        out["kr"][...] = akr
    krp = rope_a(pltpu.roll(akr, NOPE_A, 1))
    for hh in range(H_A):
        blk = slice(HB * hh, HB * (hh + 1))
        ka_ref[:, blk] = (kk[:, blk] + krp).astype(BF16)
    va_ref[...] = _dot(ckv_b, wv_ref[...]).astype(BF16)

    pb = proj(C_BZ, C_CQ - C_BZ)
    bz_ref[...] = pb[:, 0:H_B * P_B]
    bxbc_ref[...] = pb[:, C_BXBC - C_BZ:C_BXBC - C_BZ + W_XBC]
    dtr = pb[:, C_BDT - C_BZ:C_BDT - C_BZ + LANE] + dtb_ref[...]
    bdt_ref[...] = jnp.maximum(dtr, 0.0) + jnp.log(1.0 + jnp.exp(-jnp.abs(dtr)))

    pc = proj(C_CQ, C_DQ - C_CQ)
    cq_ref[...] = pc[:, 0:256]
    cf_ref[...] = pc[:, C_CF - C_CQ:C_CF - C_CQ + 512]
    ci_ref[...] = pc[:, C_CI - C_CQ:C_CI - C_CQ + 256]
    cg_ref[...] = pc[:, C_CG - C_CQ:C_CG - C_CQ + 256]

    pd = proj(C_DQ, W_IN_P - C_DQ)
    dq = pd[:, 0:H_D * HD_D]
    scale_d = HD_D ** -0.5
    for t in range(H_D * HD_D // LANE):
        blk = slice(LANE * t, LANE * (t + 1))
        dq_ref[:, blk] = (rope_d(dq[:, blk]) * scale_d).astype(BF16)
    dk = pd[:, C_DK - C_DQ:C_DK - C_DQ + KV_D * LANE]
    dk = [rope_d(dk[:, LANE * g:LANE * (g + 1)]) for g in range(KV_D)]
    for g in range(KV_D):
        dk_ref[:, LANE * g:LANE * (g + 1)] = dk[g]
    dv = pd[:, C_DV - C_DQ:C_DV - C_DQ + KV_D * LANE]
    dv_ref[...] = dv
    if not latent:
        out["dkc"][...] = jnp.where(_low_lanes(), dk[0], dk[1])
        out["dvc"][...] = jnp.where(_low_lanes(), dv[:, 0:LANE], dv[:, LANE:2 * LANE])


def _in_outputs(latent):
    outs = [("qa", 512, BF16), ("ka", 512, BF16), ("va", 256, BF16),
            ("bz", H_B * P_B, F32), ("bxbc", W_XBC, F32), ("bdt", LANE, F32),
            ("cq", 256, F32), ("cf", 512, F32), ("ci", 256, F32), ("cg", 256, F32),
            ("dq", H_D * HD_D, BF16), ("dk", KV_D * LANE, F32), ("dv", KV_D * LANE, F32)]
    if not latent:
        outs += [("ckv", KV_RANK, F32), ("kr", LANE, F32), ("dkc", KV_D * HD_D, F32), ("dvc", KV_D * HD_D, F32)]
    return outs


def _in_proj(x, mod, group_len, latent, wp, tabs):
    n = x.shape[0]
    row = lambda w: pl.BlockSpec((TM, w), lambda i: (i, 0))
    tab = pl.BlockSpec((TM, LANE), (lambda i: (1 + i % (DEC_SEQ // TM), 0)) if latent else (lambda i: (0, 0)))
    outs = pl.pallas_call(
        functools.partial(_in_kernel, latent=latent),
        grid=(n // TM,),
        in_specs=[row(D_MODEL), _mod_spec(group_len), _resident((D_MODEL, W_IN_P)), _resident((256, 512)),
                  _resident((KV_RANK, 512)), _resident((KV_RANK, 256)), _resident((1, 256)),
                  _resident((1, KV_RANK)), _resident((1, LANE)), tab, tab, tab, tab],
        out_specs=[row(w) for _, w, _ in _in_outputs(latent)],
        out_shape=[jax.ShapeDtypeStruct((n, w), dt) for _, w, dt in _in_outputs(latent)],
        compiler_params=_params("arbitrary"),
        name="in_proj",
    )(x, mod, wp["w_in"], wp["w_uq"], wp["w_uk"], wp["w_uv"], wp["g_q"], wp["g_kv"], wp["dt_bias"],
      tabs[0], tabs[1], tabs[2], tabs[3])
    return dict(zip([k for k, _, _ in _in_outputs(latent)], outs))


def _mla_cache_kernel(ckv_ref, krp_ref, wk_ref, wv_ref, kc_ref, vc_ref):
    ckv_b = ckv_ref[0].astype(BF16)
    kk = _dot(ckv_b, wk_ref[...])
    krp = krp_ref[0]
    for hh in range(H_A):
        blk = slice(HB * hh, HB * (hh + 1))
        kc_ref[0, :, blk] = (kk[:, blk] + krp).astype(BF16)
    vc_ref[0] = _dot(ckv_b, wv_ref[...]).astype(BF16)


def _mla_cache(ckv, krope_placed, wp):
    nb = ckv.shape[0]
    return pl.pallas_call(
        _mla_cache_kernel,
        grid=(nb,),
        in_specs=[pl.BlockSpec((1, PAST_LEN, KV_RANK), lambda b: (b, 0, 0)),
                  pl.BlockSpec((1, PAST_LEN, LANE), lambda b: (b, 0, 0)),
                  _resident((KV_RANK, 512)), _resident((KV_RANK, 256))],
        out_specs=[pl.BlockSpec((1, PAST_LEN, 512), lambda b: (b, 0, 0)),
                   pl.BlockSpec((1, PAST_LEN, 256), lambda b: (b, 0, 0))],
        out_shape=[jax.ShapeDtypeStruct((nb, PAST_LEN, 512), BF16),
                   jax.ShapeDtypeStruct((nb, PAST_LEN, 256), BF16)],
        compiler_params=_params("arbitrary"),
        name="mla_cache",
    )(ckv, krope_placed, wp["w_uk"], wp["w_uv"])


def _mla_kernel(*refs, has_cache, seqs):
    if has_cache:
        q_ref, k_ref, v_ref, kc_ref, vc_ref, o_ref = refs
    else:
        q_ref, k_ref, v_ref, o_ref = refs
    tk = k_ref.shape[0] // seqs
    tq = q_ref.shape[0] // seqs
    head_of_lane = lax.broadcasted_iota(jnp.int32, (1, H_A * V_A), 1) // V_A
    for b in range(seqs):
        qrows = slice(b * tq, (b + 1) * tq)
        krows = slice(b * tk, (b + 1) * tk)
        v = v_ref[krows, :]
        blocks = [slice(HB * hh, HB * (hh + 1)) for hh in range(H_A)]
        scores = [_dot_nt(q_ref[qrows, blk], k_ref[krows, blk]) for blk in blocks]
        if has_cache:
            scores_c = [_dot_nt(q_ref[qrows, blk], kc_ref[0, :, blk]) for blk in blocks]
        acc = jnp.zeros((tq, H_A * V_A), F32)
        for hh in range(H_A):
            s = scores[hh]
            m = jnp.max(s, axis=-1, keepdims=True)
            if has_cache:
                sc = scores_c[hh]
                m = jnp.maximum(m, jnp.max(sc, axis=-1, keepdims=True))
            e = jnp.exp2(s - m)
            den = jnp.sum(e, axis=-1, keepdims=True)
            pv = _dot(e.astype(BF16), v)
            if has_cache:
                ec = jnp.exp2(sc - m)
                den = den + jnp.sum(ec, axis=-1, keepdims=True)
                pv = pv + _dot(ec.astype(BF16), vc_ref[0])
            acc = jnp.where(head_of_lane == hh, pv / den, acc)
        o_ref[qrows, :] = acc.astype(BF16)


def _col_reduce(op, x):
    rows, q = x.shape
    g = MLA_REDUCE_GROUPS if rows % (MLA_REDUCE_GROUPS * SUBLANE) == 0 else 1
    return op(op(x.reshape(g, rows // g, q), axis=1), axis=0, keepdims=True)


def _mla_lat_kernel(q_ref, k_ref, v_ref, kc_ref, vc_ref, o_ref, vt_scr, vct_scr):
    @pl.when(pl.program_id(1) == 0)
    def _():
        vt_scr[...] = v_ref[...].astype(F32).T.astype(BF16)
        vct_scr[...] = vc_ref[0].astype(F32).T.astype(BF16)

    blocks = [slice(HB * hh, HB * (hh + 1)) for hh in range(H_A)]
    scores = [_dot_nt(k_ref[:, blk], q_ref[:, blk]) for blk in blocks]
    scores_c = [_dot_nt(kc_ref[0, :, blk], q_ref[:, blk]) for blk in blocks]
    outs = []
    for hh in range(H_A):
        vrows = slice(V_A * hh, V_A * (hh + 1))
        s, sc = scores[hh], scores_c[hh]
        m = jnp.maximum(_col_reduce(jnp.max, s), _col_reduce(jnp.max, sc))
        e = jnp.exp2(s - m)
        ec = jnp.exp2(sc - m)
        den = _col_reduce(jnp.sum, e) + _col_reduce(jnp.sum, ec)
        o_t = _dot(vt_scr[vrows, :], e.astype(BF16)) + _dot(vct_scr[vrows, :], ec.astype(BF16))
        outs.append(o_t / den)
    o_ref[...] = jnp.concatenate(outs, axis=0).T.astype(BF16)


def _mla_lat(q, k, v, n_seq, seq_len, cache):
    nq = seq_len // TQ_A
    return pl.pallas_call(
        _mla_lat_kernel,
        grid=(n_seq, nq),
        in_specs=[pl.BlockSpec((TQ_A, 512), lambda b, i: (b * nq + i, 0)),
                  pl.BlockSpec((seq_len, 512), lambda b, i: (b, 0)),
                  pl.BlockSpec((seq_len, 256), lambda b, i: (b, 0)),
                  pl.BlockSpec((1, PAST_LEN, 512), lambda b, i: (b, 0, 0)),
                  pl.BlockSpec((1, PAST_LEN, 256), lambda b, i: (b, 0, 0))],
        out_specs=pl.BlockSpec((TQ_A, H_A * V_A), lambda b, i: (b * nq + i, 0)),
        out_shape=jax.ShapeDtypeStruct((n_seq * seq_len, H_A * V_A), BF16),
        scratch_shapes=[pltpu.VMEM((H_A * V_A, seq_len), BF16), pltpu.VMEM((H_A * V_A, PAST_LEN), BF16)],
        compiler_params=_params("arbitrary", "arbitrary"),
        name="mla_latent",
    )(q, k, v, *cache)


def _mla(q, k, v, n_seq, seq_len, cache=None):
    tq = TQ_A if cache is not None else seq_len
    nq = seq_len // tq
    seqs = 1 if cache is not None else MLA_CTX_SEQS
    in_specs = [pl.BlockSpec((seqs * tq, 512), lambda b, i: (b * nq + i, 0)),
                pl.BlockSpec((seqs * seq_len, 512), lambda b, i: (b, 0)),
                pl.BlockSpec((seqs * seq_len, 256), lambda b, i: (b, 0))]
    args = [q, k, v]
    if cache is not None:
        in_specs += [pl.BlockSpec((1, PAST_LEN, 512), lambda b, i: (b, 0, 0)),
                     pl.BlockSpec((1, PAST_LEN, 256), lambda b, i: (b, 0, 0))]
        args += list(cache)
    return pl.pallas_call(
        functools.partial(_mla_kernel, has_cache=cache is not None, seqs=seqs),
        grid=(n_seq // seqs, nq),
        in_specs=in_specs,
        out_specs=pl.BlockSpec((seqs * tq, H_A * V_A), lambda b, i: (b * nq + i, 0)),
        out_shape=jax.ShapeDtypeStruct((n_seq * seq_len, H_A * V_A), BF16),
        compiler_params=_params("arbitrary", "arbitrary"),
        name="mla_attention",
    )(*args)


def _sink_softmax_pv(parts, sink):
    m = sink
    for s, _ in parts:
        m = jnp.maximum(m, jnp.max(s, axis=-1, keepdims=True))
    den = jnp.exp(sink - m)
    pv = None
    for s, v in parts:
        e = jnp.exp(s - m)
        den = den + jnp.sum(e, axis=-1, keepdims=True)
        t = _dot(e.astype(BF16), v)
        pv = t if pv is None else pv + t
    return pv / den


def _gqa_pair(q_ref, rows, g, scores_and_values, sink_ref, o_ref):
    t = rows.stop - rows.start
    tile = slice(LANE * g, LANE * (g + 1))
    qt = q_ref[rows, tile]
    low = _low_lanes()
    zero_b = jnp.zeros((), BF16)
    q2 = jnp.concatenate([jnp.where(low, qt, zero_b), jnp.where(low, zero_b, qt)], axis=0)
    second = lax.broadcasted_iota(jnp.int32, (G_D * t, 1), 0) >= t
    sink = jnp.where(second, sink_ref[G_D * g + 1:G_D * g + 2, 0:1], sink_ref[G_D * g:G_D * g + 1, 0:1])
    o = _sink_softmax_pv(scores_and_values(q2), sink)
    o_ref[rows, tile] = jnp.where(low, o[0:t, :], o[t:2 * t, :]).astype(BF16)


def _gqa_ctx_kernel(q_ref, k_ref, v_ref, sink_ref, o_ref):
    for b in range(GQA_CTX_SEQS):
        rows = slice(b * SEQ, (b + 1) * SEQ)
        for g in range(KV_D):
            tile = slice(LANE * g, LANE * (g + 1))
            k = k_ref[rows, tile].astype(BF16)
            v = v_ref[rows, tile].astype(BF16)
            _gqa_pair(q_ref, rows, g, lambda q2, k=k, v=v: [(_dot_nt(q2, k), v)], sink_ref, o_ref)


def _gqa_ctx(q, k, v, sink, n_seq):
    seq = lambda w: pl.BlockSpec((GQA_CTX_SEQS * SEQ, w), lambda b: (b, 0))
    return pl.pallas_call(
        _gqa_ctx_kernel,
        grid=(n_seq // GQA_CTX_SEQS,),
        in_specs=[seq(H_D * HD_D), seq(KV_D * LANE), seq(KV_D * LANE), _resident((H_D, LANE))],
        out_specs=seq(H_D * HD_D),
        out_shape=jax.ShapeDtypeStruct((n_seq * SEQ, H_D * HD_D), BF16),
        compiler_params=_params("arbitrary"),
        name="gqa_context",
    )(q, k, v, sink)


def _gqa_lat_kernel(q_ref, k_ref, v_ref, kc_ref, vc_ref, sink_ref, o_ref):
    span = 3 * WINDOW
    row = lax.broadcasted_iota(jnp.int32, (G_D * WINDOW, span), 0) % WINDOW
    col = lax.broadcasted_iota(jnp.int32, (G_D * WINDOW, span), 1)
    for j in range(GQA_LAT_BLOCKS):
        n = pl.program_id(1) * GQA_LAT_BLOCKS + j
        rows = slice(j * WINDOW, (j + 1) * WINDOW)
        start = pl.multiple_of(jnp.clip((n - 1) * WINDOW, 0, DEC_SEQ - span), WINDOW)
        band = jnp.abs((start + col) - (n * WINDOW + row)) <= WINDOW
        for g in range(KV_D):
            tile = slice(LANE * g, LANE * (g + 1))
            kw = k_ref[pl.ds(start, span), tile].astype(BF16)
            vw = v_ref[pl.ds(start, span), tile].astype(BF16)
            kc = kc_ref[0, :, tile]
            vc = vc_ref[0, :, tile]

            def parts(q2, kw=kw, vw=vw, kc=kc, vc=vc, band=band):
                return [(jnp.where(band, _dot_nt(q2, kw), NEG), vw), (_dot_nt(q2, kc), vc)]

            _gqa_pair(q_ref, rows, g, parts, sink_ref, o_ref)


def _gqa_lat(q, k, v, kc, vc, sink, n_seq):
    nb = DEC_SEQ // (WINDOW * GQA_LAT_BLOCKS)
    wq, wkv = H_D * HD_D, KV_D * LANE
    qblk = pl.BlockSpec((WINDOW * GQA_LAT_BLOCKS, wq), lambda b, n: (b * nb + n, 0))
    return pl.pallas_call(
        _gqa_lat_kernel,
        grid=(n_seq, nb),
        in_specs=[qblk,
                  pl.BlockSpec((DEC_SEQ, wkv), lambda b, n: (b, 0)),
                  pl.BlockSpec((DEC_SEQ, wkv), lambda b, n: (b, 0)),
                  pl.BlockSpec((1, PAST_LEN, wkv), lambda b, n: (b, 0, 0)),
                  pl.BlockSpec((1, PAST_LEN, wkv), lambda b, n: (b, 0, 0)),
                  _resident((H_D, LANE))],
        out_specs=qblk,
        out_shape=jax.ShapeDtypeStruct((n_seq * DEC_SEQ, wq), BF16),
        compiler_params=_params("arbitrary", "arbitrary"),
        name="gqa_latent",
    )(q, k, v, kc, vc, sink)


def _conv_kernel(cur_ref, prev_ref, next_ref, w_ref, b_ref, o_ref, pad_ref, *, tiles_per_seq):
    i = pl.program_id(0)
    has_prev = (i % tiles_per_seq) != 0
    has_next = (i % tiles_per_seq) != tiles_per_seq - 1
    pad_ref[0:SUBLANE, :] = jnp.where(has_prev, prev_ref[...], 0.0)
    pad_ref[SUBLANE:SUBLANE + CONV_TILE, :] = cur_ref[...]
    pad_ref[SUBLANE + CONV_TILE:, :] = jnp.where(has_next, next_ref[...], 0.0)
    y = jnp.zeros((CONV_TILE, W_XBC), F32) + b_ref[...]
    for k in range(D_CONV):
        off = SUBLANE - D_CONV // 2 + k
        y = y + w_ref[k:k + 1, :] * pad_ref[off:off + CONV_TILE, :]
    o_ref[...] = _silu(y)


def _conv(xbc, seq_len, w, b):
    n = xbc.shape[0]
    per = CONV_TILE // SUBLANE
    last = n // SUBLANE - 1
    return pl.pallas_call(
        functools.partial(_conv_kernel, tiles_per_seq=seq_len // CONV_TILE),
        grid=(n // CONV_TILE,),
        in_specs=[pl.BlockSpec((CONV_TILE, W_XBC), lambda i: (i, 0)),
                  pl.BlockSpec((SUBLANE, W_XBC), lambda i: (jnp.maximum(i * per - 1, 0), 0)),
                  pl.BlockSpec((SUBLANE, W_XBC), lambda i: (jnp.minimum((i + 1) * per, last), 0)),
                  _resident((SUBLANE, W_XBC)), _resident((1, W_XBC))],
        out_specs=pl.BlockSpec((CONV_TILE, W_XBC), lambda i: (i, 0)),
        out_shape=jax.ShapeDtypeStruct((n, W_XBC), F32),
        scratch_shapes=[pltpu.VMEM((CONV_TILE + 2 * SUBLANE, W_XBC), F32)],
        compiler_params=_params("arbitrary"),
        name="ssd_conv",
    )(xbc, xbc, xbc, w, b)


def _ssd_kernel(*refs, has_s0, chunks, seqs):
    if has_s0:
        xf_ref, xb_ref, dtf_ref, dtb_ref, alog_ref, dsk_ref, s0_ref, yf_ref, yb_ref, st_ref, s_scr = refs
    else:
        xf_ref, xb_ref, dtf_ref, dtb_ref, alog_ref, dsk_ref, yf_ref, yb_ref, st_ref, s_scr = refs
    c = pl.program_id(1)
    q = SSD_CHUNK

    @pl.when(c == 0)
    def _():
        s_scr[...] = s0_ref[...] if has_s0 else jnp.zeros(s_scr.shape, F32)

    row = lax.broadcasted_iota(jnp.int32, (q, q), 0)
    col = lax.broadcasted_iota(jnp.int32, (q, q), 1)
    a_coef = -jnp.exp(alog_ref[...])
    for sq, d in [(sq, d) for sq in range(seqs) for d in range(2)]:
        x_ref, dt_ref, y_ref = ((xf_ref, dtf_ref, yf_ref), (xb_ref, dtb_ref, yb_ref))[d]
        tri = (row >= col) if d == 0 else (row <= col)
        for ci in (range(chunks) if d == 0 else range(chunks - 1, -1, -1)):
            rows = slice((sq * chunks + ci) * q, (sq * chunks + ci + 1) * q)
            dt = dt_ref[rows, :]
            cum = _prefix_dot(tri, dt * a_coef)
            cum_t = cum.T
            dt_t = dt.T
            total = cum[q - 1:q, :] if d == 0 else cum[0:1, :]
            ys = []
            for g in range(G_B):
                bg = _head_alone(x_ref[rows, 2 * LANE:3 * LANE], g)
                cg = _head_alone(x_ref[rows, 3 * LANE:4 * LANE], g)
                cb = _dot_nt(cg.astype(BF16), bg.astype(BF16))
                for hh in range(g * (H_B // G_B), (g + 1) * (H_B // G_B)):
                    k = d * H_B + hh
                    cum_b = jnp.broadcast_to(cum[:, k:k + 1], (q, LANE))
                    seg = jnp.where(tri, jnp.exp(jnp.where(tri, cum_b - cum_t[k:k + 1, :], 0.0)), 0.0)
                    xh = _head_alone(x_ref[rows, LANE * (hh // 2):LANE * (hh // 2 + 1)], hh % 2)
                    dt_row = dt_t[k:k + 1, :]
                    s_in = s_scr[sq, d, hh]
                    y = _dot((cb * seg * dt_row).astype(BF16), xh.astype(BF16))
                    y = y + _dot_nt((cg * jnp.exp(cum_b)).astype(BF16), s_in.astype(BF16))
                    ys.append(y + xh * dsk_ref[k:k + 1, :])
                    tot = total[:, k:k + 1]
                    bdec = bg * jnp.exp(tot - cum_b)
                    cs = _dot((xh.T * dt_row).astype(BF16), bdec.astype(BF16))
                    s_scr[sq, d, hh] = jnp.exp(tot) * s_in + cs
            for p in range(H_B // 2):
                y_ref[rows, LANE * p:LANE * (p + 1)] = _two_heads(ys[2 * p], ys[2 * p + 1])

    @pl.when(c == pl.num_programs(1) - 1)
    def _():
        for sq in range(seqs):
            for d in range(2):
                for hh in range(H_B):
                    st_ref[sq, d, hh] = s_scr[sq, d, hh, 0:P_B, 0:N_B]


def _ssd(xbc, dt, n_seq, seq_len, a_log, dskip, s0=None):
    chunks = min(SSD_STEP_CHUNKS, seq_len // SSD_CHUNK)
    nc = seq_len // (SSD_CHUNK * chunks)
    seqs = SSD_STEP_CHUNKS // chunks if nc == 1 else 1
    fwd = lambda w: pl.BlockSpec((SSD_CHUNK * chunks * seqs, w), lambda b, c: (b * nc + c, 0))
    bwd = lambda w: pl.BlockSpec((SSD_CHUNK * chunks * seqs, w), lambda b, c: (b * nc + nc - 1 - c, 0))
    state = pl.BlockSpec((seqs, 2, H_B, HB, HB), lambda b, c: (b, 0, 0, 0, 0))
    in_specs = [fwd(W_XBC), bwd(W_XBC), fwd(LANE), bwd(LANE), _resident((1, LANE)), _resident((2 * H_B, LANE))]
    args = [xbc, xbc, dt, dt, a_log, dskip]
    if s0 is not None:
        in_specs.append(state)
        args.append(s0)
    n = n_seq * seq_len
    return pl.pallas_call(
        functools.partial(_ssd_kernel, has_s0=s0 is not None, chunks=chunks, seqs=seqs),
        grid=(n_seq // seqs, nc),
        in_specs=in_specs,
        out_specs=[fwd(H_B * P_B), bwd(H_B * P_B),
                   pl.BlockSpec((seqs, 2, H_B, P_B, N_B), lambda b, c: (b, 0, 0, 0, 0))],
        out_shape=[jax.ShapeDtypeStruct((n, H_B * P_B), F32), jax.ShapeDtypeStruct((n, H_B * P_B), F32),
                   jax.ShapeDtypeStruct((n_seq, 2, H_B, P_B, N_B), F32)],
        scratch_shapes=[pltpu.VMEM((seqs, 2, H_B, HB, HB), F32)],
        compiler_params=_params("arbitrary", "arbitrary"),
        name="ssd_scan",
    )(*args)


def _gla_kernel(*refs, has_s0):
    if has_s0:
        (qf_ref, qb_ref, ff_ref, fb_ref, vf_ref, vb_ref, lb_ref, s0_ref, of_ref, ob_ref, st_ref, s_scr,
         o_scr) = refs
    else:
        (qf_ref, qb_ref, ff_ref, fb_ref, vf_ref, vb_ref, lb_ref, of_ref, ob_ref, st_ref, s_scr, o_scr) = refs
    c = pl.program_id(1)
    t = GLA_TILE
    ch = GLA_CHUNK
    nch = t // ch
    w = H_C * K_C

    @pl.when(c == 0)
    def _():
        s_scr[...] = s0_ref[0] if has_s0 else jnp.zeros(s_scr.shape, F32)

    row = lax.broadcasted_iota(jnp.int32, (t, t), 0)
    col = lax.broadcasted_iota(jnp.int32, (t, t), 1)
    same_chunk = (row // ch) == (col // ch)
    head_of_lane = lax.broadcasted_iota(jnp.int32, (1, w), 1) // K_C
    row_head = lax.broadcasted_iota(jnp.int32, (H_C * ch, 1), 0) // ch
    low_half = lax.broadcasted_iota(jnp.int32, (1, LANE), 1) < V_C
    zero_b = jnp.zeros((), BF16)
    for d, (q_ref, f_ref, v_ref, o_ref) in enumerate(((qf_ref, ff_ref, vf_ref, of_ref),
                                                      (qb_ref, fb_ref, vb_ref, ob_ref))):
        qv = q_ref[...]
        fr = f_ref[...]
        vv = v_ref[...]
        lb = lb_ref[d:d + 1, :]
        f = lb + (1.0 - lb) * jax.nn.sigmoid(fr)
        log_f = jnp.log(jnp.maximum(f, F_MIN))
        key = (1.0 - lb) * jax.nn.sigmoid(-fr)
        tri = same_chunk & ((col <= row) if d == 0 else (col >= row))
        cum = _prefix_dot(tri, log_f)
        cum3 = cum.reshape(nch, ch, w)
        k3 = key.reshape(nch, ch, w)
        q_dec = (qv * jnp.exp(cum)).astype(BF16)
        q_heads = [jnp.where(head_of_lane == hh, q_dec, zero_b) for hh in range(H_C)]
        v_heads = []
        for hh in range(H_C):
            tile = vv[:, LANE * (hh // 2):LANE * (hh // 2 + 1)]
            v_heads.append(jnp.where(low_half, tile if hh % 2 == 0 else pltpu.roll(tile, V_C, 1), 0.0))

        k_inv = (key * jnp.exp(-cum)).astype(BF16)
        for hh in range(H_C):
            att = jnp.where(tri, _dot_nt(q_heads[hh], k_inv), 0.0).astype(BF16)
            o_scr[:, HB * hh:HB * (hh + 1)] = _dot(att, v_heads[hh].astype(BF16))

        @pl.when(jnp.min(cum) < -GLA_SAFE_LOG_DECAY)
        def _():
            wp = H_C * HB
            head_ones = ((lax.broadcasted_iota(jnp.int32, (w, wp), 0) // K_C)
                         == (lax.broadcasted_iota(jnp.int32, (w, wp), 1) // HB)).astype(F32)
            i_in_chunk = lax.broadcasted_iota(jnp.int32, (nch, ch, w), 1)
            q3 = qv.reshape(nch, ch, w)
            v3 = jnp.concatenate(v_heads, axis=1).reshape(nch, ch, wp)
            o3 = jnp.zeros((nch, ch, wp), F32)
            for j in range(ch):
                live = (i_in_chunk >= j) if d == 0 else (i_in_chunk <= j)
                e = jnp.exp(jnp.where(live, cum3 - cum3[:, j:j + 1, :], 0.0))
                term = jnp.where(live, q3 * e * k3[:, j:j + 1, :], 0.0)
                att = _dot(term.reshape(t, w), head_ones)
                o3 = o3 + att.reshape(nch, ch, wp) * v3[:, j:j + 1, :]
            o_scr[...] = o3.reshape(t, wp)

        edge = ch - 1 if d == 0 else 0
        last3 = jnp.broadcast_to(cum3[:, edge:edge + 1, :], (nch, ch, w))
        k_dec = (k3 * jnp.exp(last3 - cum3)).reshape(t, w).astype(BF16)
        order = range(nch) if d == 0 else range(nch - 1, -1, -1)
        for cc in order:
            rows = slice(cc * ch, (cc + 1) * ch)
            st = s_scr[d]
            q4 = jnp.concatenate([qh[rows, :] for qh in q_heads], axis=0)
            r = _dot_nt(q4, st.astype(BF16))
            for hh in range(H_C):
                blk = slice(HB * hh, HB * (hh + 1))
                o_scr[rows, blk] = o_scr[rows, blk] + r[hh * ch:(hh + 1) * ch, :]
            v4 = jnp.concatenate([vh[rows, :] for vh in v_heads], axis=0)
            k4 = jnp.where(row_head == head_of_lane, jnp.concatenate([k_dec[rows, :]] * H_C, axis=0), zero_b)
            decay = jnp.exp(cum[cc * ch + edge:cc * ch + edge + 1, :])
            s_scr[d] = decay * st + _dot(v4.T.astype(BF16), k4)

        for p in range(H_C // 2):
            o_ref[:, LANE * p:LANE * (p + 1)] = (o_scr[:, HB * 2 * p:HB * (2 * p + 1)]
                                                 + pltpu.roll(o_scr[:, HB * (2 * p + 1):HB * (2 * p + 2)], V_C, 1))

    @pl.when(c == pl.num_programs(1) - 1)
    def _():
        k_idx = lax.broadcasted_iota(jnp.int32, (K_C, w), 0)
        lane_idx = lax.broadcasted_iota(jnp.int32, (K_C, w), 1)
        for d in range(2):
            rows = s_scr[d, 0:V_C, :]
            for hh in range(H_C):
                sel = (lane_idx == hh * K_C + k_idx).astype(F32)
                st_ref[0, d, hh] = lax.dot_general(sel, rows, (((1,), (1,)), ((), ())),
                                                   preferred_element_type=F32, precision=lax.Precision.HIGHEST)


def _gla(cq, cf, ci, n_seq, seq_len, lb, s0=None):
    nt = seq_len // GLA_TILE
    w = H_C * K_C
    fwd = lambda j: pl.BlockSpec((GLA_TILE, w), lambda b, c: (b * nt + c, j))
    bwd = lambda j: pl.BlockSpec((GLA_TILE, w), lambda b, c: (b * nt + nt - 1 - c, j))
    state = pl.BlockSpec((1, 2, HB, w), lambda b, c: (b, 0, 0, 0))
    in_specs = [fwd(0), bwd(0), fwd(0), bwd(1), fwd(0), bwd(0), _resident((2, w))]
    args = [cq, cq, cf, cf, ci, ci, lb]
    if s0 is not None:
        in_specs.append(state)
        args.append(s0)
    n = n_seq * seq_len
    return pl.pallas_call(
        functools.partial(_gla_kernel, has_s0=s0 is not None),
        grid=(n_seq, nt),
        in_specs=in_specs,
        out_specs=[fwd(0), bwd(0), pl.BlockSpec((1, 2, H_C, K_C, V_C), lambda b, c: (b, 0, 0, 0, 0))],
        out_shape=[jax.ShapeDtypeStruct((n, w), F32), jax.ShapeDtypeStruct((n, w), F32),
                   jax.ShapeDtypeStruct((n_seq, 2, H_C, K_C, V_C), F32)],
        scratch_shapes=[pltpu.VMEM((2, HB, w), F32), pltpu.VMEM((GLA_TILE, H_C * HB), F32)],
        compiler_params=_params("arbitrary", "arbitrary"),
        name="hgrn_scan",
    )(*args)


def _out_ffn_kernel(x_ref, mod_ref, oa_ref, yf_ref, yb_ref, bz_ref, of_ref, ob_ref, cg_ref, od_ref,
                    wo_ref, nb_ref, nc_ref, g1_ref, b1_ref, wg_ref, wu_ref, wd_ref, g2_ref, b2_ref, o_ref):
    x = x_ref[...]
    gate = mod_ref[0, 5:6, :]
    yb = _rms_heads((yf_ref[...] + yb_ref[...]) * _silu(bz_ref[...]), H_B * P_B) * nb_ref[...]
    oc = _rms_heads(of_ref[...] + ob_ref[...], H_C * V_C) * nc_ref[...] * _silu(cg_ref[...])
    mixed = jnp.concatenate([oa_ref[...], yb.astype(BF16), oc.astype(BF16), od_ref[...]], axis=-1)
    u = _dot(mixed, wo_ref[...])
    x = _layer_norm(ALPHA * x + gate * u, g1_ref[...], b1_ref[...])
    o_ref[...] = _ffn_sublayer(x, mod_ref, wg_ref, wu_ref, wd_ref, g2_ref, b2_ref, 2)


def _out_ffn(x, mod, seq_len, mix, wp, ln1_g, ln1_b, w_gu, w_down, layer_sub, ln2_g, ln2_b):
    n = x.shape[0]
    l, s = layer_sub
    row = lambda w: pl.BlockSpec((TM, w), lambda i: (i, 0))
    names = ("oa", "yf", "yb", "bz", "of", "ob", "cg", "od")
    vec = _resident((1, D_MODEL))
    return pl.pallas_call(
        _out_ffn_kernel,
        grid=(n // TM,),
        in_specs=[row(D_MODEL), _mod_spec(seq_len)] + [row(mix[k].shape[1]) for k in names]
                 + [_resident((D_MODEL, D_MODEL)), _resident((1, H_B * P_B)), _resident((1, H_C * V_C)), vec, vec,
                    _resident((None, None, D_MODEL, D_FF), (l, s, 0, 0)),
                    _resident((None, None, D_MODEL, D_FF), (l, s, 0, 1)),
                    _resident((None, None, D_FF, D_MODEL), (l, s, 0, 0)), vec, vec],
        out_specs=row(D_MODEL),
        out_shape=jax.ShapeDtypeStruct((n, D_MODEL), F32),
        compiler_params=_params("arbitrary"),
        name="out_proj_ffn",
    )(x, mod, *[mix[k] for k in names], wp["w_out"], wp["ssd_norm"], wp["hgrn_norm"], ln1_g, ln1_b,
      w_gu, w_gu, w_down, ln2_g, ln2_b)


def _prep_layer(l, w_in, w_out, mla_q_norm, mla_kv_norm, mla_w_uq, mla_w_ukv, ssd_conv_w, ssd_conv_b,
                ssd_a_log, ssd_dt_bias, ssd_d, ssd_norm, hgrn_lb, hgrn_norm, gqa_sink):
    return {
        "w_in": _gather_pad(w_in[l], _IDX_W_IN, 1).astype(BF16),
        "w_uq": _gather_pad(_gather_pad(mla_w_uq[l], _IDX_UQ_ROWS, 0), _IDX_UQ_COLS, 1).astype(BF16),
        "w_uk": _gather_pad(mla_w_ukv[l], _IDX_UKV_K, 1).astype(BF16),
        "w_uv": _gather_pad(mla_w_ukv[l], _IDX_UKV_V, 1).astype(BF16),
        "g_q": _gather_pad(mla_q_norm[l], _IDX_UQ_ROWS, 0).reshape(1, 256),
        "g_kv": mla_kv_norm[l].reshape(1, KV_RANK),
        "dt_bias": jnp.pad(ssd_dt_bias[l].reshape(1, 2 * H_B), ((0, 0), (0, LANE - 2 * H_B))),
        "conv_w": jnp.pad(ssd_conv_w[l], ((0, SUBLANE - D_CONV), (0, 0))),
        "conv_b": ssd_conv_b[l].reshape(1, W_XBC),
        "a_log": jnp.pad(ssd_a_log[l].reshape(1, 2 * H_B), ((0, 0), (0, LANE - 2 * H_B))),
        "d_skip": jnp.broadcast_to(ssd_d[l].reshape(2 * H_B, 1), (2 * H_B, LANE)),
        "ssd_norm": ssd_norm[l].reshape(1, H_B * P_B),
        "hgrn_lb": hgrn_lb[l],
        "hgrn_norm": hgrn_norm[l].reshape(1, H_C * V_C),
        "sink": jnp.broadcast_to(gqa_sink[l].reshape(H_D, 1), (H_D, LANE)),
        "w_out": w_out[l].astype(BF16),
    }


def _mixer(x, mod, group_len, wp, tabs, n_seq, seq_len, ctx):
    latent = ctx is not None
    p = _in_proj(x, mod, group_len, latent, wp, tabs)
    mix = {"bz": p["bz"], "cg": p["cg"]}
    cache = _mla_cache(ctx["ckv"], ctx["krope"], wp) if latent else None
    mix["oa"] = (_mla_lat if latent else _mla)(p["qa"], p["ka"], p["va"], n_seq, seq_len, cache)
    xbc = _conv(p["bxbc"], seq_len, wp["conv_w"], wp["conv_b"])
    mix["yf"], mix["yb"], st_b = _ssd(xbc, p["bdt"], n_seq, seq_len, wp["a_log"], wp["d_skip"],
                                      ctx["ssm"] if latent else None)
    mix["of"], mix["ob"], st_c = _gla(p["cq"], p["cf"], p["ci"], n_seq, seq_len, wp["hgrn_lb"],
                                      ctx["hgrn"] if latent else None)
    if latent:
        mix["od"] = _gqa_lat(p["dq"], p["dk"], p["dv"], ctx["dk"], ctx["dv"], wp["sink"], n_seq)
    else:
        mix["od"] = _gqa_ctx(p["dq"], p["dk"], p["dv"], wp["sink"], n_seq)
    state = None if latent else (p["ckv"], p["kr"], st_b, st_c, p["dkc"], p["dvc"])
    return mix, state


def _run_stream(x, mod, n_seq, seq_len, ctx, wp, ffn_w, lng, lnb, tabs):
    group_len = x.shape[0] if ctx is None else seq_len
    x = _ffn(x, mod, group_len, *ffn_w[0], lng[0], lnb[0], sub=0)
    mix, st = _mixer(x, mod, group_len, wp, tabs, n_seq, seq_len, ctx)
    x = _out_ffn(x, mod, group_len, mix, wp, lng[1], lnb[1], *ffn_w[1], lng[2], lnb[2])
    return x, st


def _layer_inputs(l, ctx_tensors, weights, hgrn_lb):
    (cache_a_ckv, cache_a_krope, state_b_ssm, state_c_hgrn, cache_d_k, cache_d_v) = ctx_tensors
    (ln_g, ln_b, ffn_w_gu, ffn_w_down, w_in, w_out, mla_q_norm, mla_kv_norm, mla_w_uq, mla_w_ukv, ssd_conv_w,
     ssd_conv_b, ssd_a_log, ssd_dt_bias, ssd_d, ssd_norm, hgrn_norm, gqa_sink) = weights
    wp = _prep_layer(l, w_in, w_out, mla_q_norm, mla_kv_norm, mla_w_uq, mla_w_ukv, ssd_conv_w, ssd_conv_b,
                     ssd_a_log, ssd_dt_bias, ssd_d, ssd_norm, hgrn_lb, hgrn_norm, gqa_sink)
    ffn_w = [(ffn_w_gu.astype(BF16), ffn_w_down.astype(BF16), (l, s)) for s in range(2)]
    lng = [ln_g[l, s].reshape(1, D_MODEL) for s in range(N_SUB)]
    lnb = [ln_b[l, s].reshape(1, D_MODEL) for s in range(N_SUB)]
    nb = cache_a_ckv.shape[0]
    ctx = {
        "ckv": cache_a_ckv[:, l],
        "krope": jnp.pad(cache_a_krope[:, l], ((0, 0), (0, 0), (NOPE_A, LANE - NOPE_A - ROPE_A))),
        "ssm": jnp.pad(state_b_ssm[:, l], ((0, 0),) * 3 + ((0, HB - P_B), (0, HB - N_B))),
        "hgrn": jnp.pad(jnp.transpose(state_c_hgrn[:, l], (0, 1, 4, 2, 3)).reshape(nb, 2, V_C, H_C * K_C),
                        ((0, 0), (0, 0), (0, HB - V_C), (0, 0))),
        "dk": jnp.concatenate([cache_d_k[:, l]] * 2, axis=-1).reshape(nb, PAST_LEN, KV_D * LANE).astype(BF16),
        "dv": jnp.concatenate([cache_d_v[:, l]] * 2, axis=-1).reshape(nb, PAST_LEN, KV_D * LANE).astype(BF16),
    }
    return wp, ffn_w, lng, lnb, ctx


def kernel(x_prompt, x_sample, cache_a_ckv, cache_a_krope, state_b_ssm, state_c_hgrn, cache_d_k, cache_d_v,
           c, c_ctx, w_mod, b_mod, ln_g, ln_b, ffn_w_gu, ffn_w_down, w_in, w_out, mla_q_norm, mla_kv_norm,
           mla_w_uq, mla_w_ukv, ssd_conv_w, ssd_conv_b, ssd_a_log, ssd_dt_bias, ssd_d, ssd_norm,
           hgrn_lb_logits, hgrn_norm, gqa_sink):
    lb_p = jax.nn.softmax(hgrn_lb_logits.astype(F32), axis=0)
    hgrn_lb = jnp.cumsum(lb_p, axis=0) - lb_p[:1]

    cvec = jnp.concatenate([c_ctx[None], c, jnp.zeros((SUBLANE - 1 - DEC_BATCH, D_MODEL), F32)], axis=0)
    mod_all = _modulation(cvec, w_mod, b_mod)
    tabs = _rope_tables(8, (NOPE_A,)) + _rope_tables(16, (0, HD_D))
    ctx_tensors = (cache_a_ckv, cache_a_krope, state_b_ssm, state_c_hgrn, cache_d_k, cache_d_v)
    weights = (ln_g, ln_b, ffn_w_gu, ffn_w_down, w_in, w_out, mla_q_norm, mla_kv_norm, mla_w_uq, mla_w_ukv,
               ssd_conv_w, ssd_conv_b, ssd_a_log, ssd_dt_bias, ssd_d, ssd_norm, hgrn_norm, gqa_sink)

    y_p = x_prompt.reshape(BATCH * SEQ, D_MODEL)
    y_s = x_sample.reshape(DEC_BATCH * DEC_SEQ, D_MODEL)
    states = []
    for l in range(DEPTH):
        wp, ffn_w, lng, lnb, ctx = _layer_inputs(l, ctx_tensors, weights, hgrn_lb)
        mod_ctx = mod_all[l, 0:1].reshape(1, N_SUB * 3, D_MODEL)
        mod_lat = mod_all[l, 1:1 + DEC_BATCH].reshape(DEC_BATCH, N_SUB * 3, D_MODEL)
        y_p, st = _run_stream(y_p, mod_ctx, BATCH, SEQ, None, wp, ffn_w, lng, lnb, tabs)
        y_s, _ = _run_stream(y_s, mod_lat, DEC_BATCH, DEC_SEQ, ctx, wp, ffn_w, lng, lnb, tabs)
        states.append(st)

    def stack(i, f):
        return jnp.stack([f(s[i]) for s in states], axis=1)

    new_a_ckv = stack(0, lambda t: t.reshape(BATCH, SEQ, KV_RANK))
    new_a_krope = stack(1, lambda t: t.reshape(BATCH, SEQ, LANE)[..., :ROPE_A])
    new_b_ssm = stack(2, lambda t: t)
    new_c_hgrn = stack(3, lambda t: t)
    new_d_k = stack(4, lambda t: t.reshape(BATCH, SEQ, KV_D, HD_D))
    new_d_v = stack(5, lambda t: t.reshape(BATCH, SEQ, KV_D, HD_D))
    return (y_p.reshape(BATCH, SEQ, D_MODEL), y_s.reshape(DEC_BATCH, DEC_SEQ, D_MODEL),
            new_a_ckv, new_a_krope, new_b_ssm, new_c_hgrn, new_d_k, new_d_v)
```

```python
import functools

import numpy as np
import jax
import jax.numpy as jnp
from jax import lax
from jax.experimental import pallas as pl
from jax.experimental.pallas import tpu as pltpu

F32 = jnp.float32
BF16 = jnp.bfloat16

D_MODEL = 1024
BATCH = 32
SEQ = 256
DEPTH = 2
DEC_BATCH = 2
DEC_SEQ = 4096
PAST_LEN = 512
GRID_W = 64
H_A, Q_RANK, KV_RANK, NOPE_A, ROPE_A, V_A = 4, 192, 128, 64, 32, 64
H_B, P_B, G_B, N_B, D_CONV, SSD_CHUNK = 4, 64, 2, 64, 5, 128
H_C, K_C, V_C = 4, 64, 64
H_D, KV_D, HD_D, WINDOW = 4, 2, 64, 128
G_D = H_D // KV_D
ROPE_BASE = 10000.0
D_FF = 2816
N_SUB = 3
ALPHA = (2 * DEPTH) ** 0.25
EPS = 1e-6
F_MIN = 1e-6
LOG2_E = 1.4426950408889634
NEG = -1e30
N_MOD = N_SUB * 3 * D_MODEL

LANE = 128
SUBLANE = 8
VMEM_LIMIT = 56 * 1024 * 1024

TM = 512
TM_FFN = 1024
FF_CHUNK = 256
TQ_A = 256
GLA_TILE = 256
GLA_CHUNK = 32
SSD_STEP_CHUNKS = 8
GQA_CTX_SEQS = 8
MLA_CTX_SEQS = 8
MLA_REDUCE_GROUPS = 8
GQA_LAT_BLOCKS = 4
CONV_TILE = 256
MOD_TN = 1536
GLA_SAFE_LOG_DECAY = 60.0

C_ACQ, C_ACKV, C_AKR = 0, 256, 384
C_BZ, C_BXBC, C_BDT = 512, 768, 1280
C_CQ, C_CF, C_CI, C_CG = 1408, 1664, 2176, 2432
C_DQ, C_DK, C_DV = 2688, 2944, 3200
W_IN_P = 3456
W_XBC = 512
HB = 128


def _dot(a, b, precision=None):
    return jnp.dot(a, b, preferred_element_type=F32, precision=precision)


def _dot_nt(a, b):
    return lax.dot_general(a, b, (((1,), (1,)), ((), ())), preferred_element_type=F32)


def _prefix_dot(tri, x):
    t = tri.astype(BF16)
    hi = x.astype(BF16)
    rest = x - hi.astype(F32)
    mid = rest.astype(BF16)
    lo = (rest - mid.astype(F32)).astype(BF16)
    return _dot(t, hi) + _dot(t, mid) + _dot(t, lo)


def _params(*sem):
    return pltpu.CompilerParams(dimension_semantics=sem, vmem_limit_bytes=VMEM_LIMIT)


def _resident(shape, index=None):
    index = (0,) * len(shape) if index is None else index
    return pl.BlockSpec(shape, lambda *_: index, pipeline_mode=pl.Buffered(1))


def _silu(x):
    return x * jax.nn.sigmoid(x)


def _layer_norm(y, g, b):
    mu = jnp.mean(y, axis=-1, keepdims=True)
    yc = y - mu
    var = jnp.mean(yc * yc, axis=-1, keepdims=True)
    return yc * lax.rsqrt(var + EPS) * g + b


def _index_map(width, pieces):
    idx = np.full((width,), -1, np.int32)
    for dst, src, w in pieces:
        idx[dst:dst + w] = np.arange(src, src + w)
    return idx


def _gather_pad(arr, idx, axis):
    parts = []
    i = 0
    n = idx.shape[0]
    while i < n:
        j = i
        if idx[i] < 0:
            while j < n and idx[j] < 0:
                j += 1
            shape = list(arr.shape)
            shape[axis] = j - i
            parts.append(jnp.zeros(shape, arr.dtype))
        else:
            while j + 1 < n and idx[j + 1] == idx[j] + 1:
                j += 1
            j += 1
            parts.append(lax.slice_in_dim(arr, int(idx[i]), int(idx[i]) + j - i, axis=axis))
        i = j
    return jnp.concatenate(parts, axis=axis)


_IDX_W_IN = _index_map(W_IN_P, [
    (C_ACQ, 0, Q_RANK), (C_ACKV, 192, KV_RANK), (C_AKR, 320, ROPE_A),
    (C_BZ, 352, 256), (C_BXBC, 608, W_XBC), (C_BDT, 1120, 2 * H_B),
    (C_CQ, 1128, 256), (C_CF, 1384, 512), (C_CI, 1896, 256), (C_CG, 2152, 256),
    (C_DQ, 2408, 256),
    *[(c0 + LANE * g + HD_D * half, s0 + HD_D * g, HD_D)
      for c0, s0 in ((C_DK, 2664), (C_DV, 2792)) for g in range(KV_D) for half in range(2)]])
_IDX_UQ_ROWS = _index_map(256, [(0, 0, Q_RANK)])
_IDX_UQ_COLS = _index_map(4 * HB, [(HB * h, 96 * h, 96) for h in range(H_A)])
_IDX_UKV_K = _index_map(4 * HB, [(HB * h, 128 * h, NOPE_A) for h in range(H_A)])
_IDX_UKV_V = _index_map(4 * V_A, [(V_A * h, 128 * h + NOPE_A, V_A) for h in range(H_A)])


def _rope_tables(half, lane0s):
    t = np.arange(DEC_SEQ)
    pos = np.stack([t // GRID_W, t % GRID_W], 0).astype(np.float64)
    inv = ROPE_BASE ** (-np.arange(half, dtype=np.float64) / half)
    cos = np.ones((DEC_SEQ, LANE))
    sin = np.zeros((DEC_SEQ, LANE))
    for lane0 in lane0s:
        for axis in range(2):
            ang = pos[axis][:, None] * inv[None, :]
            base = lane0 + axis * 2 * half
            cos[:, base:base + half] = np.cos(ang)
            cos[:, base + half:base + 2 * half] = np.cos(ang)
            sin[:, base:base + half] = -np.sin(ang)
            sin[:, base + half:base + 2 * half] = np.sin(ang)
    ident_c = np.ones((TM, LANE))
    ident_s = np.zeros((TM, LANE))
    return (jnp.asarray(np.concatenate([ident_c, cos], 0), F32),
            jnp.asarray(np.concatenate([ident_s, sin], 0), F32))


def _rope(x, cos, sin, first, half):
    partner = jnp.where(first, pltpu.roll(x, LANE - half, 1), pltpu.roll(x, half, 1))
    return x * cos + partner * sin


def _low_lanes():
    return lax.broadcasted_iota(jnp.int32, (1, LANE), 1) < LANE // 2


def _head_alone(tile, odd):
    return jnp.where(_low_lanes(), pltpu.roll(tile, LANE // 2, 1) if odd else tile, 0.0)


def _two_heads(h_even, h_odd):
    return h_even + pltpu.roll(h_odd, LANE // 2, 1)


def _rms_heads(x, width):
    low = _low_lanes()
    tiles = []
    for t in range(width // LANE):
        blk = x[:, LANE * t:LANE * (t + 1)]
        sq = blk * blk
        s_all = jnp.sum(sq, axis=-1, keepdims=True)
        s_low = jnp.sum(jnp.where(low, sq, 0.0), axis=-1, keepdims=True)
        ms = jnp.where(low, s_low, s_all - s_low) * (2.0 / LANE)
        tiles.append(blk * lax.rsqrt(ms + EPS))
    return jnp.concatenate(tiles, axis=-1)


def _mod_kernel(c_ref, w_ref, b_ref, o_ref):
    c = c_ref[...]
    s = _silu(c).astype(BF16)
    o_ref[0] = _dot(s, w_ref[0].astype(BF16)) + b_ref[0]


def _modulation(cvec, w_mod, b_mod):
    return pl.pallas_call(
        _mod_kernel,
        grid=(DEPTH, N_MOD // MOD_TN),
        in_specs=[pl.BlockSpec((SUBLANE, D_MODEL), lambda l, j: (0, 0)),
                  pl.BlockSpec((1, D_MODEL, MOD_TN), lambda l, j: (l, 0, j)),
                  pl.BlockSpec((1, 1, MOD_TN), lambda l, j: (l, 0, j))],
        out_specs=pl.BlockSpec((1, SUBLANE, MOD_TN), lambda l, j: (l, 0, j)),
        out_shape=jax.ShapeDtypeStruct((DEPTH, SUBLANE, N_MOD), F32),
        compiler_params=_params("arbitrary", "arbitrary"),
        name="modulation",
    )(cvec, w_mod, b_mod.reshape(DEPTH, 1, N_MOD))


def _mod_spec(seq_len, tm=TM):
    return pl.BlockSpec((1, N_SUB * 3, D_MODEL), lambda i: (i * tm // seq_len, 0, 0))


def _ffn_sublayer(x, mod_ref, wg_ref, wu_ref, wd_ref, g_ref, b_ref, sub):
    shift = mod_ref[0, 3 * sub:3 * sub + 1, :]
    scale = mod_ref[0, 3 * sub + 1:3 * sub + 2, :]
    gate = mod_ref[0, 3 * sub + 2:3 * sub + 3, :]
    h = (x * (1.0 + scale) + shift).astype(BF16)
    acc = jnp.zeros(x.shape, F32)
    for start in range(0, D_FF, FF_CHUNK):
        cols = slice(start, min(start + FF_CHUNK, D_FF))
        gt = _dot(h, wg_ref[:, cols])
        up = _dot(h, wu_ref[:, cols])
        acc = acc + _dot((_silu(gt) * up).astype(BF16), wd_ref[cols, :])
    y = ALPHA * x + 0.5 * gate * acc
    return _layer_norm(y, g_ref[...], b_ref[...])


def _ffn_kernel(x_ref, mod_ref, wg_ref, wu_ref, wd_ref, g_ref, b_ref, o_ref, *, sub):
    o_ref[...] = _ffn_sublayer(x_ref[...], mod_ref, wg_ref, wu_ref, wd_ref, g_ref, b_ref, sub)


def _ffn(x, mod, seq_len, w_gu, w_down, layer_sub, ln_g, ln_b, sub):
    n = x.shape[0]
    l, s = layer_sub
    row = pl.BlockSpec((TM_FFN, D_MODEL), lambda i: (i, 0))
    return pl.pallas_call(
        functools.partial(_ffn_kernel, sub=sub),
        grid=(n // TM_FFN,),
        in_specs=[row, _mod_spec(seq_len, TM_FFN),
                  _resident((None, None, D_MODEL, D_FF), (l, s, 0, 0)),
                  _resident((None, None, D_MODEL, D_FF), (l, s, 0, 1)),
                  _resident((None, None, D_FF, D_MODEL), (l, s, 0, 0)),
                  _resident((1, D_MODEL)), _resident((1, D_MODEL))],
        out_specs=row,
        out_shape=jax.ShapeDtypeStruct((n, D_MODEL), F32),
        compiler_params=_params("arbitrary"),
        name="ffn",
    )(x, mod, w_gu, w_gu, w_down, ln_g, ln_b)


def _in_kernel(x_ref, mod_ref, w_ref, wuq_ref, wk_ref, wv_ref, gq_ref, gkv_ref, dtb_ref,
               cosq_ref, sinq_ref, cosd_ref, sind_ref, *out_refs, latent):
    out = dict(zip([name for name, _, _ in _in_outputs(latent)], out_refs))
    qa_ref, ka_ref, va_ref = out["qa"], out["ka"], out["va"]
    bz_ref, bxbc_ref, bdt_ref = out["bz"], out["bxbc"], out["bdt"]
    cq_ref, cf_ref, ci_ref, cg_ref = out["cq"], out["cf"], out["ci"], out["cg"]
    dq_ref, dk_ref, dv_ref = out["dq"], out["dk"], out["dv"]
    x = x_ref[...]
    h = (x * (1.0 + mod_ref[0, 4:5, :]) + mod_ref[0, 3:4, :]).astype(BF16)

    def proj(start, width):
        return _dot(h, w_ref[:, start:start + width])

    lane = lax.broadcasted_iota(jnp.int32, (TM, LANE), 1)
    first_a = (lane % 16) < 8
    first_d = (lane % 32) < 16

    def rope_a(blk):
        return _rope(blk, cosq_ref[...], sinq_ref[...], first_a, 8) if latent else blk

    def rope_d(blk):
        return _rope(blk, cosd_ref[...], sind_ref[...], first_d, 16) if latent else blk

    pa = proj(C_ACQ, C_BZ - C_ACQ)
    acq = pa[:, 0:256]
    ms = jnp.sum(acq * acq, axis=-1, keepdims=True) * (1.0 / Q_RANK)
    qn = (acq * lax.rsqrt(ms + EPS) * gq_ref[...]).astype(BF16)
    q = _dot(qn, wuq_ref[...])
    scale_a = (NOPE_A + ROPE_A) ** -0.5 * LOG2_E
    for hh in range(H_A):
        blk = slice(HB * hh, HB * (hh + 1))
        qa_ref[:, blk] = (rope_a(q[:, blk]) * scale_a).astype(BF16)
    ackv = pa[:, C_ACKV:C_ACKV + KV_RANK]
    ms = jnp.mean(ackv * ackv, axis=-1, keepdims=True)
    ckv = ackv * lax.rsqrt(ms + EPS) * gkv_ref[...]
    ckv_b = ckv.astype(BF16)
    kk = _dot(ckv_b, wk_ref[...])
    akr = pa[:, C_AKR:C_AKR + LANE]
    if not latent:
        out["ckv"][...] = ckv
        out["kr"][...] = akr
    krp = rope_a(pltpu.roll(akr, NOPE_A, 1))
    for hh in range(H_A):
        blk = slice(HB * hh, HB * (hh + 1))
        ka_ref[:, blk] = (kk[:, blk] + krp).astype(BF16)
    va_ref[...] = _dot(ckv_b, wv_ref[...]).astype(BF16)

    pb = proj(C_BZ, C_CQ - C_BZ)
    bz_ref[...] = pb[:, 0:H_B * P_B]
    bxbc_ref[...] = pb[:, C_BXBC - C_BZ:C_BXBC - C_BZ + W_XBC]
    dtr = pb[:, C_BDT - C_BZ:C_BDT - C_BZ + LANE] + dtb_ref[...]
    bdt_ref[...] = jnp.maximum(dtr, 0.0) + jnp.log(1.0 + jnp.exp(-jnp.abs(dtr)))

    pc = proj(C_CQ, C_DQ - C_CQ)
    cq_ref[...] = pc[:, 0:256]
    cf_ref[...] = pc[:, C_CF - C_CQ:C_CF - C_CQ + 512]
    ci_ref[...] = pc[:, C_CI - C_CQ:C_CI - C_CQ + 256]
    cg_ref[...] = pc[:, C_CG - C_CQ:C_CG - C_CQ + 256]

    pd = proj(C_DQ, W_IN_P - C_DQ)
    dq = pd[:, 0:H_D * HD_D]
    scale_d = HD_D ** -0.5
    for t in range(H_D * HD_D // LANE):
        blk = slice(LANE * t, LANE * (t + 1))
        dq_ref[:, blk] = (rope_d(dq[:, blk]) * scale_d).astype(BF16)
    dk = pd[:, C_DK - C_DQ:C_DK - C_DQ + KV_D * LANE]
    dk = [rope_d(dk[:, LANE * g:LANE * (g + 1)]) for g in range(KV_D)]
    for g in range(KV_D):
        dk_ref[:, LANE * g:LANE * (g + 1)] = dk[g]
    dv = pd[:, C_DV - C_DQ:C_DV - C_DQ + KV_D * LANE]
    dv_ref[...] = dv
    if not latent:
        out["dkc"][...] = jnp.where(_low_lanes(), dk[0], dk[1])
        out["dvc"][...] = jnp.where(_low_lanes(), dv[:, 0:LANE], dv[:, LANE:2 * LANE])


def _in_outputs(latent):
    outs = [("qa", 512, BF16), ("ka", 512, BF16), ("va", 256, BF16),
            ("bz", H_B * P_B, F32), ("bxbc", W_XBC, F32), ("bdt", LANE, F32),
            ("cq", 256, F32), ("cf", 512, F32), ("ci", 256, F32), ("cg", 256, F32),
            ("dq", H_D * HD_D, BF16), ("dk", KV_D * LANE, F32), ("dv", KV_D * LANE, F32)]
    if not latent:
        outs += [("ckv", KV_RANK, F32), ("kr", LANE, F32), ("dkc", KV_D * HD_D, F32), ("dvc", KV_D * HD_D, F32)]
    return outs


def _in_proj(x, mod, group_len, latent, wp, tabs):
    n = x.shape[0]
    row = lambda w: pl.BlockSpec((TM, w), lambda i: (i, 0))
    tab = pl.BlockSpec((TM, LANE), (lambda i: (1 + i % (DEC_SEQ // TM), 0)) if latent else (lambda i: (0, 0)))
    outs = pl.pallas_call(
        functools.partial(_in_kernel, latent=latent),
        grid=(n // TM,),
        in_specs=[row(D_MODEL), _mod_spec(group_len), _resident((D_MODEL, W_IN_P)), _resident((256, 512)),
                  _resident((KV_RANK, 512)), _resident((KV_RANK, 256)), _resident((1, 256)),
                  _resident((1, KV_RANK)), _resident((1, LANE)), tab, tab, tab, tab],
        out_specs=[row(w) for _, w, _ in _in_outputs(latent)],
        out_shape=[jax.ShapeDtypeStruct((n, w), dt) for _, w, dt in _in_outputs(latent)],
        compiler_params=_params("arbitrary"),
        name="in_proj",
    )(x, mod, wp["w_in"], wp["w_uq"], wp["w_uk"], wp["w_uv"], wp["g_q"], wp["g_kv"], wp["dt_bias"],
      tabs[0], tabs[1], tabs[2], tabs[3])
    return dict(zip([k for k, _, _ in _in_outputs(latent)], outs))


def _mla_cache_kernel(ckv_ref, krp_ref, wk_ref, wv_ref, kc_ref, vc_ref):
    ckv_b = ckv_ref[0].astype(BF16)
    kk = _dot(ckv_b, wk_ref[...])
    krp = krp_ref[0]
    for hh in range(H_A):
        blk = slice(HB * hh, HB * (hh + 1))
        kc_ref[0, :, blk] = (kk[:, blk] + krp).astype(BF16)
    vc_ref[0] = _dot(ckv_b, wv_ref[...]).astype(BF16)


def _mla_cache(ckv, krope_placed, wp):
    nb = ckv.shape[0]
    return pl.pallas_call(
        _mla_cache_kernel,
        grid=(nb,),
        in_specs=[pl.BlockSpec((1, PAST_LEN, KV_RANK), lambda b: (b, 0, 0)),
                  pl.BlockSpec((1, PAST_LEN, LANE), lambda b: (b, 0, 0)),
                  _resident((KV_RANK, 512)), _resident((KV_RANK, 256))],
        out_specs=[pl.BlockSpec((1, PAST_LEN, 512), lambda b: (b, 0, 0)),
                   pl.BlockSpec((1, PAST_LEN, 256), lambda b: (b, 0, 0))],
        out_shape=[jax.ShapeDtypeStruct((nb, PAST_LEN, 512), BF16),
                   jax.ShapeDtypeStruct((nb, PAST_LEN, 256), BF16)],
        compiler_params=_params("arbitrary"),
        name="mla_cache",
    )(ckv, krope_placed, wp["w_uk"], wp["w_uv"])


def _mla_kernel(*refs, has_cache, seqs):
    if has_cache:
        q_ref, k_ref, v_ref, kc_ref, vc_ref, o_ref = refs
    else:
        q_ref, k_ref, v_ref, o_ref = refs
    tk = k_ref.shape[0] // seqs
    tq = q_ref.shape[0] // seqs
    head_of_lane = lax.broadcasted_iota(jnp.int32, (1, H_A * V_A), 1) // V_A
    for b in range(seqs):
        qrows = slice(b * tq, (b + 1) * tq)
        krows = slice(b * tk, (b + 1) * tk)
        v = v_ref[krows, :]
        blocks = [slice(HB * hh, HB * (hh + 1)) for hh in range(H_A)]
        scores = [_dot_nt(q_ref[qrows, blk], k_ref[krows, blk]) for blk in blocks]
        if has_cache:
            scores_c = [_dot_nt(q_ref[qrows, blk], kc_ref[0, :, blk]) for blk in blocks]
        acc = jnp.zeros((tq, H_A * V_A), F32)
        for hh in range(H_A):
            s = scores[hh]
            m = jnp.max(s, axis=-1, keepdims=True)
            if has_cache:
                sc = scores_c[hh]
                m = jnp.maximum(m, jnp.max(sc, axis=-1, keepdims=True))
            e = jnp.exp2(s - m)
            den = jnp.sum(e, axis=-1, keepdims=True)
            pv = _dot(e.astype(BF16), v)
            if has_cache:
                ec = jnp.exp2(sc - m)
                den = den + jnp.sum(ec, axis=-1, keepdims=True)
                pv = pv + _dot(ec.astype(BF16), vc_ref[0])
            acc = jnp.where(head_of_lane == hh, pv / den, acc)
        o_ref[qrows, :] = acc.astype(BF16)


def _col_reduce(op, x):
    rows, q = x.shape
    g = MLA_REDUCE_GROUPS if rows % (MLA_REDUCE_GROUPS * SUBLANE) == 0 else 1
    return op(op(x.reshape(g, rows // g, q), axis=1), axis=0, keepdims=True)


def _mla_lat_kernel(q_ref, k_ref, v_ref, kc_ref, vc_ref, o_ref, vt_scr, vct_scr):
    @pl.when(pl.program_id(1) == 0)
    def _():
        vt_scr[...] = v_ref[...].astype(F32).T.astype(BF16)
        vct_scr[...] = vc_ref[0].astype(F32).T.astype(BF16)

    blocks = [slice(HB * hh, HB * (hh + 1)) for hh in range(H_A)]
    scores = [_dot_nt(k_ref[:, blk], q_ref[:, blk]) for blk in blocks]
    scores_c = [_dot_nt(kc_ref[0, :, blk], q_ref[:, blk]) for blk in blocks]
    outs = []
    for hh in range(H_A):
        vrows = slice(V_A * hh, V_A * (hh + 1))
        s, sc = scores[hh], scores_c[hh]
        m = jnp.maximum(_col_reduce(jnp.max, s), _col_reduce(jnp.max, sc))
        e = jnp.exp2(s - m)
        ec = jnp.exp2(sc - m)
        den = _col_reduce(jnp.sum, e) + _col_reduce(jnp.sum, ec)
        o_t = _dot(vt_scr[vrows, :], e.astype(BF16)) + _dot(vct_scr[vrows, :], ec.astype(BF16))
        outs.append(o_t / den)
    o_ref[...] = jnp.concatenate(outs, axis=0).T.astype(BF16)


def _mla_lat(q, k, v, n_seq, seq_len, cache):
    nq = seq_len // TQ_A
    return pl.pallas_call(
        _mla_lat_kernel,
        grid=(n_seq, nq),
        in_specs=[pl.BlockSpec((TQ_A, 512), lambda b, i: (b * nq + i, 0)),
                  pl.BlockSpec((seq_len, 512), lambda b, i: (b, 0)),
                  pl.BlockSpec((seq_len, 256), lambda b, i: (b, 0)),
                  pl.BlockSpec((1, PAST_LEN, 512), lambda b, i: (b, 0, 0)),
                  pl.BlockSpec((1, PAST_LEN, 256), lambda b, i: (b, 0, 0))],
        out_specs=pl.BlockSpec((TQ_A, H_A * V_A), lambda b, i: (b * nq + i, 0)),
        out_shape=jax.ShapeDtypeStruct((n_seq * seq_len, H_A * V_A), BF16),
        scratch_shapes=[pltpu.VMEM((H_A * V_A, seq_len), BF16), pltpu.VMEM((H_A * V_A, PAST_LEN), BF16)],
        compiler_params=_params("arbitrary", "arbitrary"),
        name="mla_latent",
    )(q, k, v, *cache)


def _mla(q, k, v, n_seq, seq_len, cache=None):
    tq = TQ_A if cache is not None else seq_len
    nq = seq_len // tq
    seqs = 1 if cache is not None else MLA_CTX_SEQS
    in_specs = [pl.BlockSpec((seqs * tq, 512), lambda b, i: (b * nq + i, 0)),
                pl.BlockSpec((seqs * seq_len, 512), lambda b, i: (b, 0)),
                pl.BlockSpec((seqs * seq_len, 256), lambda b, i: (b, 0))]
    args = [q, k, v]
    if cache is not None:
        in_specs += [pl.BlockSpec((1, PAST_LEN, 512), lambda b, i: (b, 0, 0)),
                     pl.BlockSpec((1, PAST_LEN, 256), lambda b, i: (b, 0, 0))]
        args += list(cache)
    return pl.pallas_call(
        functools.partial(_mla_kernel, has_cache=cache is not None, seqs=seqs),
        grid=(n_seq // seqs, nq),
        in_specs=in_specs,
        out_specs=pl.BlockSpec((seqs * tq, H_A * V_A), lambda b, i: (b * nq + i, 0)),
        out_shape=jax.ShapeDtypeStruct((n_seq * seq_len, H_A * V_A), BF16),
        compiler_params=_params("arbitrary", "arbitrary"),
        name="mla_attention",
    )(*args)


def _sink_softmax_pv(parts, sink):
    m = sink
    for s, _ in parts:
        m = jnp.maximum(m, jnp.max(s, axis=-1, keepdims=True))
    den = jnp.exp(sink - m)
    pv = None
    for s, v in parts:
        e = jnp.exp(s - m)
        den = den + jnp.sum(e, axis=-1, keepdims=True)
        t = _dot(e.astype(BF16), v)
        pv = t if pv is None else pv + t
    return pv / den


def _gqa_pair(q_ref, rows, g, scores_and_values, sink_ref, o_ref):
    t = rows.stop - rows.start
    tile = slice(LANE * g, LANE * (g + 1))
    qt = q_ref[rows, tile]
    low = _low_lanes()
    zero_b = jnp.zeros((), BF16)
    q2 = jnp.concatenate([jnp.where(low, qt, zero_b), jnp.where(low, zero_b, qt)], axis=0)
    second = lax.broadcasted_iota(jnp.int32, (G_D * t, 1), 0) >= t
    sink = jnp.where(second, sink_ref[G_D * g + 1:G_D * g + 2, 0:1], sink_ref[G_D * g:G_D * g + 1, 0:1])
    o = _sink_softmax_pv(scores_and_values(q2), sink)
    o_ref[rows, tile] = jnp.where(low, o[0:t, :], o[t:2 * t, :]).astype(BF16)


def _gqa_ctx_kernel(q_ref, k_ref, v_ref, sink_ref, o_ref):
    for b in range(GQA_CTX_SEQS):
        rows = slice(b * SEQ, (b + 1) * SEQ)
        for g in range(KV_D):
            tile = slice(LANE * g, LANE * (g + 1))
            k = k_ref[rows, tile].astype(BF16)
            v = v_ref[rows, tile].astype(BF16)
            _gqa_pair(q_ref, rows, g, lambda q2, k=k, v=v: [(_dot_nt(q2, k), v)], sink_ref, o_ref)


def _gqa_ctx(q, k, v, sink, n_seq):
    seq = lambda w: pl.BlockSpec((GQA_CTX_SEQS * SEQ, w), lambda b: (b, 0))
    return pl.pallas_call(
        _gqa_ctx_kernel,
        grid=(n_seq // GQA_CTX_SEQS,),
        in_specs=[seq(H_D * HD_D), seq(KV_D * LANE), seq(KV_D * LANE), _resident((H_D, LANE))],
        out_specs=seq(H_D * HD_D),
        out_shape=jax.ShapeDtypeStruct((n_seq * SEQ, H_D * HD_D), BF16),
        compiler_params=_params("arbitrary"),
        name="gqa_context",
    )(q, k, v, sink)


def _gqa_lat_kernel(q_ref, k_ref, v_ref, kc_ref, vc_ref, sink_ref, o_ref):
    span = 3 * WINDOW
    row = lax.broadcasted_iota(jnp.int32, (G_D * WINDOW, span), 0) % WINDOW
    col = lax.broadcasted_iota(jnp.int32, (G_D * WINDOW, span), 1)
    for j in range(GQA_LAT_BLOCKS):
        n = pl.program_id(1) * GQA_LAT_BLOCKS + j
        rows = slice(j * WINDOW, (j + 1) * WINDOW)
        start = pl.multiple_of(jnp.clip((n - 1) * WINDOW, 0, DEC_SEQ - span), WINDOW)
        band = jnp.abs((start + col) - (n * WINDOW + row)) <= WINDOW
        for g in range(KV_D):
            tile = slice(LANE * g, LANE * (g + 1))
            kw = k_ref[pl.ds(start, span), tile].astype(BF16)
            vw = v_ref[pl.ds(start, span), tile].astype(BF16)
            kc = kc_ref[0, :, tile]
            vc = vc_ref[0, :, tile]

            def parts(q2, kw=kw, vw=vw, kc=kc, vc=vc, band=band):
                return [(jnp.where(band, _dot_nt(q2, kw), NEG), vw), (_dot_nt(q2, kc), vc)]

            _gqa_pair(q_ref, rows, g, parts, sink_ref, o_ref)


def _gqa_lat(q, k, v, kc, vc, sink, n_seq):
    nb = DEC_SEQ // (WINDOW * GQA_LAT_BLOCKS)
    wq, wkv = H_D * HD_D, KV_D * LANE
    qblk = pl.BlockSpec((WINDOW * GQA_LAT_BLOCKS, wq), lambda b, n: (b * nb + n, 0))
    return pl.pallas_call(
        _gqa_lat_kernel,
        grid=(n_seq, nb),
        in_specs=[qblk,
                  pl.BlockSpec((DEC_SEQ, wkv), lambda b, n: (b, 0)),
                  pl.BlockSpec((DEC_SEQ, wkv), lambda b, n: (b, 0)),
                  pl.BlockSpec((1, PAST_LEN, wkv), lambda b, n: (b, 0, 0)),
                  pl.BlockSpec((1, PAST_LEN, wkv), lambda b, n: (b, 0, 0)),
                  _resident((H_D, LANE))],
        out_specs=qblk,
        out_shape=jax.ShapeDtypeStruct((n_seq * DEC_SEQ, wq), BF16),
        compiler_params=_params("arbitrary", "arbitrary"),
        name="gqa_latent",
    )(q, k, v, kc, vc, sink)


def _conv_kernel(cur_ref, prev_ref, next_ref, w_ref, b_ref, o_ref, pad_ref, *, tiles_per_seq):
    i = pl.program_id(0)
    has_prev = (i % tiles_per_seq) != 0
    has_next = (i % tiles_per_seq) != tiles_per_seq - 1
    pad_ref[0:SUBLANE, :] = jnp.where(has_prev, prev_ref[...], 0.0)
    pad_ref[SUBLANE:SUBLANE + CONV_TILE, :] = cur_ref[...]
    pad_ref[SUBLANE + CONV_TILE:, :] = jnp.where(has_next, next_ref[...], 0.0)
    y = jnp.zeros((CONV_TILE, W_XBC), F32) + b_ref[...]
    for k in range(D_CONV):
        off = SUBLANE - D_CONV // 2 + k
        y = y + w_ref[k:k + 1, :] * pad_ref[off:off + CONV_TILE, :]
    o_ref[...] = _silu(y)


def _conv(xbc, seq_len, w, b):
    n = xbc.shape[0]
    per = CONV_TILE // SUBLANE
    last = n // SUBLANE - 1
    return pl.pallas_call(
        functools.partial(_conv_kernel, tiles_per_seq=seq_len // CONV_TILE),
        grid=(n // CONV_TILE,),
        in_specs=[pl.BlockSpec((CONV_TILE, W_XBC), lambda i: (i, 0)),
                  pl.BlockSpec((SUBLANE, W_XBC), lambda i: (jnp.maximum(i * per - 1, 0), 0)),
                  pl.BlockSpec((SUBLANE, W_XBC), lambda i: (jnp.minimum((i + 1) * per, last), 0)),
                  _resident((SUBLANE, W_XBC)), _resident((1, W_XBC))],
        out_specs=pl.BlockSpec((CONV_TILE, W_XBC), lambda i: (i, 0)),
        out_shape=jax.ShapeDtypeStruct((n, W_XBC), F32),
        scratch_shapes=[pltpu.VMEM((CONV_TILE + 2 * SUBLANE, W_XBC), F32)],
        compiler_params=_params("arbitrary"),
        name="ssd_conv",
    )(xbc, xbc, xbc, w, b)


def _ssd_kernel(*refs, has_s0, chunks, seqs):
    if has_s0:
        xf_ref, xb_ref, dtf_ref, dtb_ref, alog_ref, dsk_ref, s0_ref, yf_ref, yb_ref, st_ref, s_scr = refs
    else:
        xf_ref, xb_ref, dtf_ref, dtb_ref, alog_ref, dsk_ref, yf_ref, yb_ref, st_ref, s_scr = refs
    c = pl.program_id(1)
    q = SSD_CHUNK

    @pl.when(c == 0)
    def _():
        s_scr[...] = s0_ref[...] if has_s0 else jnp.zeros(s_scr.shape, F32)

    row = lax.broadcasted_iota(jnp.int32, (q, q), 0)
    col = lax.broadcasted_iota(jnp.int32, (q, q), 1)
    a_coef = -jnp.exp(alog_ref[...])
    low = _low_lanes()
    first_head_rows = lax.broadcasted_iota(jnp.int32, (HB, 1), 0) < P_B
    for sq, d in [(sq, d) for sq in range(seqs) for d in range(2)]:
        x_ref, dt_ref, y_ref = ((xf_ref, dtf_ref, yf_ref), (xb_ref, dtb_ref, yb_ref))[d]
        tri = (row >= col) if d == 0 else (row <= col)
        for ci in (range(chunks) if d == 0 else range(chunks - 1, -1, -1)):
            rows = slice((sq * chunks + ci) * q, (sq * chunks + ci + 1) * q)
            dt = dt_ref[rows, :]
            cum = _prefix_dot(tri, dt * a_coef)
            cum_t = cum.T
            dt_t = dt.T
            total = cum[q - 1:q, :] if d == 0 else cum[0:1, :]
            for g in range(G_B):
                half = low if g == 0 else jnp.logical_not(low)
                bg = jnp.where(half, x_ref[rows, 2 * LANE:3 * LANE], 0.0)
                cg = jnp.where(half, x_ref[rows, 3 * LANE:4 * LANE], 0.0)
                cb = _dot_nt(cg.astype(BF16), bg.astype(BF16))
                xp = x_ref[rows, LANE * g:LANE * (g + 1)]
                xp_b = xp.astype(BF16)
                xp_t = xp.T
                s_in = s_scr[sq, d, g]
                s_in_b = s_in.astype(BF16)
                ys, states = [], []
                for j in range(H_B // G_B):
                    k = d * H_B + g * (H_B // G_B) + j
                    cum_b = jnp.broadcast_to(cum[:, k:k + 1], (q, LANE))
                    seg = jnp.where(tri, jnp.exp(jnp.where(tri, cum_b - cum_t[k:k + 1, :], 0.0)), 0.0)
                    dt_row = dt_t[k:k + 1, :]
                    y = _dot((cb * seg * dt_row).astype(BF16), xp_b)
                    ys.append(y + _dot_nt((cg * jnp.exp(cum_b)).astype(BF16), s_in_b))
                    tot = total[:, k:k + 1]
                    bdec = bg * jnp.exp(tot - cum_b)
                    cs = _dot((xp_t * dt_row).astype(BF16), bdec.astype(BF16))
                    states.append(jnp.exp(tot) * s_in + cs)
                k0 = d * H_B + g * (H_B // G_B)
                d_skip = jnp.where(low, dsk_ref[k0:k0 + 1, :], dsk_ref[k0 + 1:k0 + 2, :])
                y_ref[rows, LANE * g:LANE * (g + 1)] = jnp.where(low, ys[0], ys[1]) + xp * d_skip
                s_scr[sq, d, g] = jnp.where(first_head_rows, states[0], states[1])

    @pl.when(c == pl.num_programs(1) - 1)
    def _():
        for sq in range(seqs):
            for d in range(2):
                for hh in range(H_B):
                    g, j = hh // (H_B // G_B), hh % (H_B // G_B)
                    st_ref[sq, d, hh] = s_scr[sq, d, g, P_B * j:P_B * (j + 1), N_B * g:N_B * (g + 1)]


def _ssd(xbc, dt, n_seq, seq_len, a_log, dskip, s0=None):
    chunks = min(SSD_STEP_CHUNKS, seq_len // SSD_CHUNK)
    nc = seq_len // (SSD_CHUNK * chunks)
    seqs = SSD_STEP_CHUNKS // chunks if nc == 1 else 1
    fwd = lambda w: pl.BlockSpec((SSD_CHUNK * chunks * seqs, w), lambda b, c: (b * nc + c, 0))
    bwd = lambda w: pl.BlockSpec((SSD_CHUNK * chunks * seqs, w), lambda b, c: (b * nc + nc - 1 - c, 0))
    state = pl.BlockSpec((seqs, 2, G_B, HB, HB), lambda b, c: (b, 0, 0, 0, 0))
    in_specs = [fwd(W_XBC), bwd(W_XBC), fwd(LANE), bwd(LANE), _resident((1, LANE)), _resident((2 * H_B, LANE))]
    args = [xbc, xbc, dt, dt, a_log, dskip]
    if s0 is not None:
        in_specs.append(state)
        args.append(s0)
    n = n_seq * seq_len
    return pl.pallas_call(
        functools.partial(_ssd_kernel, has_s0=s0 is not None, chunks=chunks, seqs=seqs),
        grid=(n_seq // seqs, nc),
        in_specs=in_specs,
        out_specs=[fwd(H_B * P_B), bwd(H_B * P_B),
                   pl.BlockSpec((seqs, 2, H_B, P_B, N_B), lambda b, c: (b, 0, 0, 0, 0))],
        out_shape=[jax.ShapeDtypeStruct((n, H_B * P_B), F32), jax.ShapeDtypeStruct((n, H_B * P_B), F32),
                   jax.ShapeDtypeStruct((n_seq, 2, H_B, P_B, N_B), F32)],
        scratch_shapes=[pltpu.VMEM((seqs, 2, G_B, HB, HB), F32)],
        compiler_params=_params("arbitrary", "arbitrary"),
        name="ssd_scan",
    )(*args)


def _gla_kernel(*refs, has_s0):
    if has_s0:
        (qf_ref, qb_ref, ff_ref, fb_ref, vf_ref, vb_ref, lb_ref, s0_ref, of_ref, ob_ref, st_ref, s_scr,
         o_scr) = refs
    else:
        (qf_ref, qb_ref, ff_ref, fb_ref, vf_ref, vb_ref, lb_ref, of_ref, ob_ref, st_ref, s_scr, o_scr) = refs
    c = pl.program_id(1)
    t = GLA_TILE
    ch = GLA_CHUNK
    nch = t // ch
    w = H_C * K_C

    @pl.when(c == 0)
    def _():
        s_scr[...] = s0_ref[0] if has_s0 else jnp.zeros(s_scr.shape, F32)

    row = lax.broadcasted_iota(jnp.int32, (t, t), 0)
    col = lax.broadcasted_iota(jnp.int32, (t, t), 1)
    same_chunk = (row // ch) == (col // ch)
    head_of_lane = lax.broadcasted_iota(jnp.int32, (1, w), 1) // K_C
    row_head = lax.broadcasted_iota(jnp.int32, (H_C * ch, 1), 0) // ch
    zero_b = jnp.zeros((), BF16)
    for d, (q_ref, f_ref, v_ref, o_ref) in enumerate(((qf_ref, ff_ref, vf_ref, of_ref),
                                                      (qb_ref, fb_ref, vb_ref, ob_ref))):
        qv = q_ref[...]
        fr = f_ref[...]
        vv = v_ref[...]
        lb = lb_ref[d:d + 1, :]
        f = lb + (1.0 - lb) * jax.nn.sigmoid(fr)
        log_f = jnp.log(jnp.maximum(f, F_MIN))
        key = (1.0 - lb) * jax.nn.sigmoid(-fr)
        tri = same_chunk & ((col <= row) if d == 0 else (col >= row))
        cum = _prefix_dot(tri, log_f)
        cum3 = cum.reshape(nch, ch, w)
        k3 = key.reshape(nch, ch, w)
        q_dec = (qv * jnp.exp(cum)).astype(BF16)
        q_heads = [jnp.where(head_of_lane == hh, q_dec, zero_b) for hh in range(H_C)]
        v_heads = [_head_alone(vv[:, LANE * (hh // 2):LANE * (hh // 2 + 1)], hh % 2) for hh in range(H_C)]

        k_inv = (key * jnp.exp(-cum)).astype(BF16)
        for hh in range(H_C):
            att = jnp.where(tri, _dot_nt(q_heads[hh], k_inv), 0.0).astype(BF16)
            o_scr[:, HB * hh:HB * (hh + 1)] = _dot(att, v_heads[hh].astype(BF16))

        @pl.when(jnp.min(cum) < -GLA_SAFE_LOG_DECAY)
        def _():
            wp = H_C * HB
            head_ones = ((lax.broadcasted_iota(jnp.int32, (w, wp), 0) // K_C)
                         == (lax.broadcasted_iota(jnp.int32, (w, wp), 1) // HB)).astype(F32)
            i_in_chunk = lax.broadcasted_iota(jnp.int32, (nch, ch, w), 1)
            q3 = qv.reshape(nch, ch, w)
            v3 = jnp.concatenate(v_heads, axis=1).reshape(nch, ch, wp)
            o3 = jnp.zeros((nch, ch, wp), F32)
            for j in range(ch):
                live = (i_in_chunk >= j) if d == 0 else (i_in_chunk <= j)
                e = jnp.exp(jnp.where(live, cum3 - cum3[:, j:j + 1, :], 0.0))
                term = jnp.where(live, q3 * e * k3[:, j:j + 1, :], 0.0)
                att = _dot(term.reshape(t, w), head_ones)
                o3 = o3 + att.reshape(nch, ch, wp) * v3[:, j:j + 1, :]
            o_scr[...] = o3.reshape(t, wp)

        edge = ch - 1 if d == 0 else 0
        last3 = jnp.broadcast_to(cum3[:, edge:edge + 1, :], (nch, ch, w))
        k_dec = (k3 * jnp.exp(last3 - cum3)).reshape(t, w).astype(BF16)
        order = range(nch) if d == 0 else range(nch - 1, -1, -1)
        for cc in order:
            rows = slice(cc * ch, (cc + 1) * ch)
            st = s_scr[d]
            q4 = jnp.concatenate([qh[rows, :] for qh in q_heads], axis=0)
            r = _dot_nt(q4, st.astype(BF16))
            for hh in range(H_C):
                blk = slice(HB * hh, HB * (hh + 1))
                o_scr[rows, blk] = o_scr[rows, blk] + r[hh * ch:(hh + 1) * ch, :]
            v4 = jnp.concatenate([vh[rows, :] for vh in v_heads], axis=0)
            k4 = jnp.where(row_head == head_of_lane, jnp.concatenate([k_dec[rows, :]] * H_C, axis=0), zero_b)
            decay = jnp.exp(cum[cc * ch + edge:cc * ch + edge + 1, :])
            s_scr[d] = decay * st + _dot(v4.T.astype(BF16), k4)

        for p in range(H_C // 2):
            o_ref[:, LANE * p:LANE * (p + 1)] = _two_heads(o_scr[:, HB * 2 * p:HB * (2 * p + 1)],
                                                           o_scr[:, HB * (2 * p + 1):HB * (2 * p + 2)])

    @pl.when(c == pl.num_programs(1) - 1)
    def _():
        k_idx = lax.broadcasted_iota(jnp.int32, (K_C, w), 0)
        lane_idx = lax.broadcasted_iota(jnp.int32, (K_C, w), 1)
        for d in range(2):
            rows = s_scr[d, 0:V_C, :]
            for hh in range(H_C):
                sel = (lane_idx == hh * K_C + k_idx).astype(F32)
                st_ref[0, d, hh] = lax.dot_general(sel, rows, (((1,), (1,)), ((), ())),
                                                   preferred_element_type=F32, precision=lax.Precision.HIGHEST)


def _gla(cq, cf, ci, n_seq, seq_len, lb, s0=None):
    nt = seq_len // GLA_TILE
    w = H_C * K_C
    fwd = lambda j: pl.BlockSpec((GLA_TILE, w), lambda b, c: (b * nt + c, j))
    bwd = lambda j: pl.BlockSpec((GLA_TILE, w), lambda b, c: (b * nt + nt - 1 - c, j))
    state = pl.BlockSpec((1, 2, HB, w), lambda b, c: (b, 0, 0, 0))
    in_specs = [fwd(0), bwd(0), fwd(0), bwd(1), fwd(0), bwd(0), _resident((2, w))]
    args = [cq, cq, cf, cf, ci, ci, lb]
    if s0 is not None:
        in_specs.append(state)
        args.append(s0)
    n = n_seq * seq_len
    return pl.pallas_call(
        functools.partial(_gla_kernel, has_s0=s0 is not None),
        grid=(n_seq, nt),
        in_specs=in_specs,
        out_specs=[fwd(0), bwd(0), pl.BlockSpec((1, 2, H_C, K_C, V_C), lambda b, c: (b, 0, 0, 0, 0))],
        out_shape=[jax.ShapeDtypeStruct((n, w), F32), jax.ShapeDtypeStruct((n, w), F32),
                   jax.ShapeDtypeStruct((n_seq, 2, H_C, K_C, V_C), F32)],
        scratch_shapes=[pltpu.VMEM((2, HB, w), F32), pltpu.VMEM((GLA_TILE, H_C * HB), F32)],
        compiler_params=_params("arbitrary", "arbitrary"),
        name="hgrn_scan",
    )(*args)


def _out_ffn_kernel(x_ref, mod_ref, oa_ref, yf_ref, yb_ref, bz_ref, of_ref, ob_ref, cg_ref, od_ref,
                    wo_ref, nb_ref, nc_ref, g1_ref, b1_ref, wg_ref, wu_ref, wd_ref, g2_ref, b2_ref, o_ref):
    x = x_ref[...]
    gate = mod_ref[0, 5:6, :]
    yb = _rms_heads((yf_ref[...] + yb_ref[...]) * _silu(bz_ref[...]), H_B * P_B) * nb_ref[...]
    oc = _rms_heads(of_ref[...] + ob_ref[...], H_C * V_C) * nc_ref[...] * _silu(cg_ref[...])
    mixed = jnp.concatenate([oa_ref[...], yb.astype(BF16), oc.astype(BF16), od_ref[...]], axis=-1)
    u = _dot(mixed, wo_ref[...])
    x = _layer_norm(ALPHA * x + gate * u, g1_ref[...], b1_ref[...])
    o_ref[...] = _ffn_sublayer(x, mod_ref, wg_ref, wu_ref, wd_ref, g2_ref, b2_ref, 2)


def _out_ffn(x, mod, seq_len, mix, wp, ln1_g, ln1_b, w_gu, w_down, layer_sub, ln2_g, ln2_b):
    n = x.shape[0]
    l, s = layer_sub
    row = lambda w: pl.BlockSpec((TM, w), lambda i: (i, 0))
    names = ("oa", "yf", "yb", "bz", "of", "ob", "cg", "od")
    vec = _resident((1, D_MODEL))
    return pl.pallas_call(
        _out_ffn_kernel,
        grid=(n // TM,),
        in_specs=[row(D_MODEL), _mod_spec(seq_len)] + [row(mix[k].shape[1]) for k in names]
                 + [_resident((D_MODEL, D_MODEL)), _resident((1, H_B * P_B)), _resident((1, H_C * V_C)), vec, vec,
                    _resident((None, None, D_MODEL, D_FF), (l, s, 0, 0)),
                    _resident((None, None, D_MODEL, D_FF), (l, s, 0, 1)),
                    _resident((None, None, D_FF, D_MODEL), (l, s, 0, 0)), vec, vec],
        out_specs=row(D_MODEL),
        out_shape=jax.ShapeDtypeStruct((n, D_MODEL), F32),
        compiler_params=_params("arbitrary"),
        name="out_proj_ffn",
    )(x, mod, *[mix[k] for k in names], wp["w_out"], wp["ssd_norm"], wp["hgrn_norm"], ln1_g, ln1_b,
      w_gu, w_gu, w_down, ln2_g, ln2_b)


def _prep_layer(l, w_in, w_out, mla_q_norm, mla_kv_norm, mla_w_uq, mla_w_ukv, ssd_conv_w, ssd_conv_b,
                ssd_a_log, ssd_dt_bias, ssd_d, ssd_norm, hgrn_lb, hgrn_norm, gqa_sink):
    return {
        "w_in": _gather_pad(w_in[l], _IDX_W_IN, 1).astype(BF16),
        "w_uq": _gather_pad(_gather_pad(mla_w_uq[l], _IDX_UQ_ROWS, 0), _IDX_UQ_COLS, 1).astype(BF16),
        "w_uk": _gather_pad(mla_w_ukv[l], _IDX_UKV_K, 1).astype(BF16),
        "w_uv": _gather_pad(mla_w_ukv[l], _IDX_UKV_V, 1).astype(BF16),
        "g_q": _gather_pad(mla_q_norm[l], _IDX_UQ_ROWS, 0).reshape(1, 256),
        "g_kv": mla_kv_norm[l].reshape(1, KV_RANK),
        "dt_bias": jnp.pad(ssd_dt_bias[l].reshape(1, 2 * H_B), ((0, 0), (0, LANE - 2 * H_B))),
        "conv_w": jnp.pad(ssd_conv_w[l], ((0, SUBLANE - D_CONV), (0, 0))),
        "conv_b": ssd_conv_b[l].reshape(1, W_XBC),
        "a_log": jnp.pad(ssd_a_log[l].reshape(1, 2 * H_B), ((0, 0), (0, LANE - 2 * H_B))),
        "d_skip": jnp.broadcast_to(ssd_d[l].reshape(2 * H_B, 1), (2 * H_B, LANE)),
        "ssd_norm": ssd_norm[l].reshape(1, H_B * P_B),
        "hgrn_lb": hgrn_lb[l],
        "hgrn_norm": hgrn_norm[l].reshape(1, H_C * V_C),
        "sink": jnp.broadcast_to(gqa_sink[l].reshape(H_D, 1), (H_D, LANE)),
        "w_out": w_out[l].astype(BF16),
    }


def _mixer(x, mod, group_len, wp, tabs, n_seq, seq_len, ctx):
    latent = ctx is not None
    p = _in_proj(x, mod, group_len, latent, wp, tabs)
    mix = {"bz": p["bz"], "cg": p["cg"]}
    cache = _mla_cache(ctx["ckv"], ctx["krope"], wp) if latent else None
    mix["oa"] = (_mla_lat if latent else _mla)(p["qa"], p["ka"], p["va"], n_seq, seq_len, cache)
    xbc = _conv(p["bxbc"], seq_len, wp["conv_w"], wp["conv_b"])
    mix["yf"], mix["yb"], st_b = _ssd(xbc, p["bdt"], n_seq, seq_len, wp["a_log"], wp["d_skip"],
                                      ctx["ssm"] if latent else None)
    mix["of"], mix["ob"], st_c = _gla(p["cq"], p["cf"], p["ci"], n_seq, seq_len, wp["hgrn_lb"],
                                      ctx["hgrn"] if latent else None)
    if latent:
        mix["od"] = _gqa_lat(p["dq"], p["dk"], p["dv"], ctx["dk"], ctx["dv"], wp["sink"], n_seq)
    else:
        mix["od"] = _gqa_ctx(p["dq"], p["dk"], p["dv"], wp["sink"], n_seq)
    state = None if latent else (p["ckv"], p["kr"], st_b, st_c, p["dkc"], p["dvc"])
    return mix, state


def _run_stream(x, mod, n_seq, seq_len, ctx, wp, ffn_w, lng, lnb, tabs):
    group_len = x.shape[0] if ctx is None else seq_len
    x = _ffn(x, mod, group_len, *ffn_w[0], lng[0], lnb[0], sub=0)
    mix, st = _mixer(x, mod, group_len, wp, tabs, n_seq, seq_len, ctx)
    x = _out_ffn(x, mod, group_len, mix, wp, lng[1], lnb[1], *ffn_w[1], lng[2], lnb[2])
    return x, st


def _layer_inputs(l, ctx_tensors, weights, hgrn_lb):
    (cache_a_ckv, cache_a_krope, state_b_ssm, state_c_hgrn, cache_d_k, cache_d_v) = ctx_tensors
    (ln_g, ln_b, ffn_w_gu, ffn_w_down, w_in, w_out, mla_q_norm, mla_kv_norm, mla_w_uq, mla_w_ukv, ssd_conv_w,
     ssd_conv_b, ssd_a_log, ssd_dt_bias, ssd_d, ssd_norm, hgrn_norm, gqa_sink) = weights
    wp = _prep_layer(l, w_in, w_out, mla_q_norm, mla_kv_norm, mla_w_uq, mla_w_ukv, ssd_conv_w, ssd_conv_b,
                     ssd_a_log, ssd_dt_bias, ssd_d, ssd_norm, hgrn_lb, hgrn_norm, gqa_sink)
    ffn_w = [(ffn_w_gu.astype(BF16), ffn_w_down.astype(BF16), (l, s)) for s in range(2)]
    lng = [ln_g[l, s].reshape(1, D_MODEL) for s in range(N_SUB)]
    lnb = [ln_b[l, s].reshape(1, D_MODEL) for s in range(N_SUB)]
    nb = cache_a_ckv.shape[0]
    ctx = {
        "ckv": cache_a_ckv[:, l],
        "krope": jnp.pad(cache_a_krope[:, l], ((0, 0), (0, 0), (NOPE_A, LANE - NOPE_A - ROPE_A))),
        "ssm": jnp.stack([jnp.pad(state_b_ssm[:, l, :, 2 * g:2 * g + 2].reshape(nb, 2, 2 * P_B, N_B),
                                  ((0, 0), (0, 0), (0, 0), (N_B * g, N_B * (G_B - 1 - g)))) for g in range(G_B)],
                         axis=2),
        "hgrn": jnp.pad(jnp.transpose(state_c_hgrn[:, l], (0, 1, 4, 2, 3)).reshape(nb, 2, V_C, H_C * K_C),
                        ((0, 0), (0, 0), (0, HB - V_C), (0, 0))),
        "dk": jnp.concatenate([cache_d_k[:, l]] * 2, axis=-1).reshape(nb, PAST_LEN, KV_D * LANE).astype(BF16),
        "dv": jnp.concatenate([cache_d_v[:, l]] * 2, axis=-1).reshape(nb, PAST_LEN, KV_D * LANE).astype(BF16),
    }
    return wp, ffn_w, lng, lnb, ctx


def kernel(x_prompt, x_sample, cache_a_ckv, cache_a_krope, state_b_ssm, state_c_hgrn, cache_d_k, cache_d_v,
           c, c_ctx, w_mod, b_mod, ln_g, ln_b, ffn_w_gu, ffn_w_down, w_in, w_out, mla_q_norm, mla_kv_norm,
           mla_w_uq, mla_w_ukv, ssd_conv_w, ssd_conv_b, ssd_a_log, ssd_dt_bias, ssd_d, ssd_norm,
           hgrn_lb_logits, hgrn_norm, gqa_sink):
    lb_p = jax.nn.softmax(hgrn_lb_logits.astype(F32), axis=0)
    hgrn_lb = jnp.cumsum(lb_p, axis=0) - lb_p[:1]

    cvec = jnp.concatenate([c_ctx[None], c, jnp.zeros((SUBLANE - 1 - DEC_BATCH, D_MODEL), F32)], axis=0)
    mod_all = _modulation(cvec, w_mod, b_mod)
    tabs = _rope_tables(8, (NOPE_A,)) + _rope_tables(16, (0, HD_D))
    ctx_tensors = (cache_a_ckv, cache_a_krope, state_b_ssm, state_c_hgrn, cache_d_k, cache_d_v)
    weights = (ln_g, ln_b, ffn_w_gu, ffn_w_down, w_in, w_out, mla_q_norm, mla_kv_norm, mla_w_uq, mla_w_ukv,
               ssd_conv_w, ssd_conv_b, ssd_a_log, ssd_dt_bias, ssd_d, ssd_norm, hgrn_norm, gqa_sink)

    y_p = x_prompt.reshape(BATCH * SEQ, D_MODEL)
    y_s = x_sample.reshape(DEC_BATCH * DEC_SEQ, D_MODEL)
    states = []
    for l in range(DEPTH):
        wp, ffn_w, lng, lnb, ctx = _layer_inputs(l, ctx_tensors, weights, hgrn_lb)
        mod_ctx = mod_all[l, 0:1].reshape(1, N_SUB * 3, D_MODEL)
        mod_lat = mod_all[l, 1:1 + DEC_BATCH].reshape(DEC_BATCH, N_SUB * 3, D_MODEL)
        y_p, st = _run_stream(y_p, mod_ctx, BATCH, SEQ, None, wp, ffn_w, lng, lnb, tabs)
        y_s, _ = _run_stream(y_s, mod_lat, DEC_BATCH, DEC_SEQ, ctx, wp, ffn_w, lng, lnb, tabs)
        states.append(st)

    def stack(i, f):
        return jnp.stack([f(s[i]) for s in states], axis=1)

    new_a_ckv = stack(0, lambda t: t.reshape(BATCH, SEQ, KV_RANK))
    new_a_krope = stack(1, lambda t: t.reshape(BATCH, SEQ, LANE)[..., :ROPE_A])
    new_b_ssm = stack(2, lambda t: t)
    new_c_hgrn = stack(3, lambda t: t)
    new_d_k = stack(4, lambda t: t.reshape(BATCH, SEQ, KV_D, HD_D))
    new_d_v = stack(5, lambda t: t.reshape(BATCH, SEQ, KV_D, HD_D))
    return (y_p.reshape(BATCH, SEQ, D_MODEL), y_s.reshape(DEC_BATCH, DEC_SEQ, D_MODEL),
            new_a_ckv, new_a_krope, new_b_ssm, new_c_hgrn, new_d_k, new_d_v)
```

```python
import functools

import numpy as np
import jax
import jax.numpy as jnp
from jax import lax
from jax.experimental import pallas as pl
from jax.experimental.pallas import tpu as pltpu

F32 = jnp.float32
BF16 = jnp.bfloat16

D_MODEL = 1024
BATCH = 32
SEQ = 256
DEPTH = 2
DEC_BATCH = 2
DEC_SEQ = 4096
PAST_LEN = 512
GRID_W = 64
H_A, Q_RANK, KV_RANK, NOPE_A, ROPE_A, V_A = 4, 192, 128, 64, 32, 64
H_B, P_B, G_B, N_B, D_CONV, SSD_CHUNK = 4, 64, 2, 64, 5, 128
H_C, K_C, V_C = 4, 64, 64
H_D, KV_D, HD_D, WINDOW = 4, 2, 64, 128
G_D = H_D // KV_D
ROPE_BASE = 10000.0
D_FF = 2816
N_SUB = 3
ALPHA = (2 * DEPTH) ** 0.25
EPS = 1e-6
F_MIN = 1e-6
LOG2_E = 1.4426950408889634
NEG = -1e30
N_MOD = N_SUB * 3 * D_MODEL

LANE = 128
SUBLANE = 8
VMEM_LIMIT = 56 * 1024 * 1024

TM = 512
TM_FFN = 512
FF_CHUNK = 256
TQ_A = 256
GLA_TILE = 256
GLA_CHUNK = 32
SSD_STEP_CHUNKS = 8
GQA_CTX_SEQS = 8
MLA_CTX_SEQS = 8
MLA_REDUCE_GROUPS = 8
GQA_LAT_BLOCKS = 4
CONV_TILE = 256
MOD_TN = 1536
GLA_SAFE_LOG_DECAY = 60.0

C_ACQ, C_ACKV, C_AKR = 0, 256, 384
C_BZ, C_BXBC, C_BDT = 512, 768, 1280
C_CQ, C_CF, C_CI, C_CG = 1408, 1664, 2176, 2432
C_DQ, C_DK, C_DV = 2688, 2944, 3200
W_IN_P = 3456
W_XBC = 512
HB = 128


def _dot(a, b, precision=None):
    return jnp.dot(a, b, preferred_element_type=F32, precision=precision)


def _dot_nt(a, b):
    return lax.dot_general(a, b, (((1,), (1,)), ((), ())), preferred_element_type=F32)


def _prefix_dot(tri, x):
    t = tri.astype(BF16)
    hi = x.astype(BF16)
    rest = x - hi.astype(F32)
    mid = rest.astype(BF16)
    lo = (rest - mid.astype(F32)).astype(BF16)
    return _dot(t, hi) + _dot(t, mid) + _dot(t, lo)


def _params(*sem):
    return pltpu.CompilerParams(dimension_semantics=sem, vmem_limit_bytes=VMEM_LIMIT)


def _resident(shape, index=None):
    index = (0,) * len(shape) if index is None else index
    return pl.BlockSpec(shape, lambda *_: index, pipeline_mode=pl.Buffered(1))


def _silu(x):
    return x * jax.nn.sigmoid(x)


def _layer_norm(y, g, b):
    mu = jnp.mean(y, axis=-1, keepdims=True)
    yc = y - mu
    var = jnp.mean(yc * yc, axis=-1, keepdims=True)
    return yc * lax.rsqrt(var + EPS) * g + b


def _index_map(width, pieces):
    idx = np.full((width,), -1, np.int32)
    for dst, src, w in pieces:
        idx[dst:dst + w] = np.arange(src, src + w)
    return idx


def _gather_pad(arr, idx, axis):
    parts = []
    i = 0
    n = idx.shape[0]
    while i < n:
        j = i
        if idx[i] < 0:
            while j < n and idx[j] < 0:
                j += 1
            shape = list(arr.shape)
            shape[axis] = j - i
            parts.append(jnp.zeros(shape, arr.dtype))
        else:
            while j + 1 < n and idx[j + 1] == idx[j] + 1:
                j += 1
            j += 1
            parts.append(lax.slice_in_dim(arr, int(idx[i]), int(idx[i]) + j - i, axis=axis))
        i = j
    return jnp.concatenate(parts, axis=axis)


_IDX_W_IN = _index_map(W_IN_P, [
    (C_ACQ, 0, Q_RANK), (C_ACKV, 192, KV_RANK), (C_AKR, 320, ROPE_A),
    (C_BZ, 352, 256), (C_BXBC, 608, W_XBC), (C_BDT, 1120, 2 * H_B),
    (C_CQ, 1128, 256), (C_CF, 1384, 512), (C_CI, 1896, 256), (C_CG, 2152, 256),
    (C_DQ, 2408, 256),
    *[(c0 + LANE * g + HD_D * half, s0 + HD_D * g, HD_D)
      for c0, s0 in ((C_DK, 2664), (C_DV, 2792)) for g in range(KV_D) for half in range(2)]])
_IDX_UQ_ROWS = _index_map(256, [(0, 0, Q_RANK)])
_IDX_UQ_COLS = _index_map(4 * HB, [(HB * h, 96 * h, 96) for h in range(H_A)])
_IDX_UKV_K = _index_map(4 * HB, [(HB * h, 128 * h, NOPE_A) for h in range(H_A)])
_IDX_UKV_V = _index_map(4 * V_A, [(V_A * h, 128 * h + NOPE_A, V_A) for h in range(H_A)])


def _rope_tables(half, lane0s):
    t = np.arange(DEC_SEQ)
    pos = np.stack([t // GRID_W, t % GRID_W], 0).astype(np.float64)
    inv = ROPE_BASE ** (-np.arange(half, dtype=np.float64) / half)
    cos = np.ones((DEC_SEQ, LANE))
    sin = np.zeros((DEC_SEQ, LANE))
    for lane0 in lane0s:
        for axis in range(2):
            ang = pos[axis][:, None] * inv[None, :]
            base = lane0 + axis * 2 * half
            cos[:, base:base + half] = np.cos(ang)
            cos[:, base + half:base + 2 * half] = np.cos(ang)
            sin[:, base:base + half] = -np.sin(ang)
            sin[:, base + half:base + 2 * half] = np.sin(ang)
    ident_c = np.ones((TM, LANE))
    ident_s = np.zeros((TM, LANE))
    return (jnp.asarray(np.concatenate([ident_c, cos], 0), F32),
            jnp.asarray(np.concatenate([ident_s, sin], 0), F32))


def _rope(x, cos, sin, first, half):
    partner = jnp.where(first, pltpu.roll(x, LANE - half, 1), pltpu.roll(x, half, 1))
    return x * cos + partner * sin


def _low_lanes():
    return lax.broadcasted_iota(jnp.int32, (1, LANE), 1) < LANE // 2


def _head_alone(tile, odd):
    return jnp.where(_low_lanes(), pltpu.roll(tile, LANE // 2, 1) if odd else tile, 0.0)


def _two_heads(h_even, h_odd):
    return h_even + pltpu.roll(h_odd, LANE // 2, 1)


def _rms_heads(x, width):
    low = _low_lanes()
    tiles = []
    for t in range(width // LANE):
        blk = x[:, LANE * t:LANE * (t + 1)]
        sq = blk * blk
        s_all = jnp.sum(sq, axis=-1, keepdims=True)
        s_low = jnp.sum(jnp.where(low, sq, 0.0), axis=-1, keepdims=True)
        ms = jnp.where(low, s_low, s_all - s_low) * (2.0 / LANE)
        tiles.append(blk * lax.rsqrt(ms + EPS))
    return jnp.concatenate(tiles, axis=-1)


def _mod_kernel(c_ref, w_ref, b_ref, o_ref):
    c = c_ref[...]
    s = _silu(c).astype(BF16)
    o_ref[0] = _dot(s, w_ref[0].astype(BF16)) + b_ref[0]


def _modulation(cvec, w_mod, b_mod):
    return pl.pallas_call(
        _mod_kernel,
        grid=(DEPTH, N_MOD // MOD_TN),
        in_specs=[pl.BlockSpec((SUBLANE, D_MODEL), lambda l, j: (0, 0)),
                  pl.BlockSpec((1, D_MODEL, MOD_TN), lambda l, j: (l, 0, j)),
                  pl.BlockSpec((1, 1, MOD_TN), lambda l, j: (l, 0, j))],
        out_specs=pl.BlockSpec((1, SUBLANE, MOD_TN), lambda l, j: (l, 0, j)),
        out_shape=jax.ShapeDtypeStruct((DEPTH, SUBLANE, N_MOD), F32),
        compiler_params=_params("arbitrary", "arbitrary"),
        name="modulation",
    )(cvec, w_mod, b_mod.reshape(DEPTH, 1, N_MOD))


def _mod_spec(seq_len, tm=TM):
    return pl.BlockSpec((1, N_SUB * 3, D_MODEL), lambda i: (i * tm // seq_len, 0, 0))


def _ffn_sublayer(x, mod_ref, wg_ref, wu_ref, wd_ref, g_ref, b_ref, sub):
    shift = mod_ref[0, 3 * sub:3 * sub + 1, :]
    scale = mod_ref[0, 3 * sub + 1:3 * sub + 2, :]
    gate = mod_ref[0, 3 * sub + 2:3 * sub + 3, :]
    h = (x * (1.0 + scale) + shift).astype(BF16)
    acc = jnp.zeros(x.shape, F32)
    for start in range(0, D_FF, FF_CHUNK):
        cols = slice(start, min(start + FF_CHUNK, D_FF))
        gt = _dot(h, wg_ref[:, cols])
        up = _dot(h, wu_ref[:, cols])
        acc = acc + _dot((_silu(gt) * up).astype(BF16), wd_ref[cols, :])
    y = ALPHA * x + 0.5 * gate * acc
    return _layer_norm(y, g_ref[...], b_ref[...])


def _ffn_kernel(xa_ref, xb_ref, moda_ref, modb_ref, wg_ref, wu_ref, wd_ref, g_ref, b_ref, oa_ref, ob_ref, *,
                sub, steps_a):
    i = pl.program_id(0)

    @pl.when(i < steps_a)
    def _():
        oa_ref[...] = _ffn_sublayer(xa_ref[...], moda_ref, wg_ref, wu_ref, wd_ref, g_ref, b_ref, sub)

    @pl.when(i >= steps_a)
    def _():
        ob_ref[...] = _ffn_sublayer(xb_ref[...], modb_ref, wg_ref, wu_ref, wd_ref, g_ref, b_ref, sub)


def _ffn(xa, xb, mod_a, mod_b, group_a, group_b, w_gu, w_down, layer_sub, ln_g, ln_b, sub):
    l, s = layer_sub
    steps_a, steps_b = xa.shape[0] // TM_FFN, xb.shape[0] // TM_FFN
    ia = lambda i: jnp.minimum(i, steps_a - 1)
    ib = lambda i: jnp.maximum(i - steps_a, 0)
    row_a = pl.BlockSpec((TM_FFN, D_MODEL), lambda i: (ia(i), 0))
    row_b = pl.BlockSpec((TM_FFN, D_MODEL), lambda i: (ib(i), 0))
    mod_spec = lambda blk, group: pl.BlockSpec((1, N_SUB * 3, D_MODEL), lambda i: (blk(i) * TM_FFN // group, 0, 0))
    return pl.pallas_call(
        functools.partial(_ffn_kernel, sub=sub, steps_a=steps_a),
        grid=(steps_a + steps_b,),
        in_specs=[row_a, row_b, mod_spec(ia, group_a), mod_spec(ib, group_b),
                  _resident((None, None, D_MODEL, D_FF), (l, s, 0, 0)),
                  _resident((None, None, D_MODEL, D_FF), (l, s, 0, 1)),
                  _resident((None, None, D_FF, D_MODEL), (l, s, 0, 0)),
                  _resident((1, D_MODEL)), _resident((1, D_MODEL))],
        out_specs=[row_a, row_b],
        out_shape=[jax.ShapeDtypeStruct(xa.shape, F32), jax.ShapeDtypeStruct(xb.shape, F32)],
        compiler_params=_params("arbitrary"),
        name="ffn",
    )(xa, xb, mod_a, mod_b, w_gu, w_gu, w_down, ln_g, ln_b)


def _in_kernel(x_ref, mod_ref, w_ref, wuq_ref, wk_ref, wv_ref, gq_ref, gkv_ref, dtb_ref,
               cosq_ref, sinq_ref, cosd_ref, sind_ref, *out_refs, latent):
    out = dict(zip([name for name, _, _ in _in_outputs(latent)], out_refs))
    qa_ref, ka_ref, va_ref = out["qa"], out["ka"], out["va"]
    bz_ref, bxbc_ref, bdt_ref = out["bz"], out["bxbc"], out["bdt"]
    cq_ref, cf_ref, ci_ref, cg_ref = out["cq"], out["cf"], out["ci"], out["cg"]
    dq_ref, dk_ref, dv_ref = out["dq"], out["dk"], out["dv"]
    x = x_ref[...]
    h = (x * (1.0 + mod_ref[0, 4:5, :]) + mod_ref[0, 3:4, :]).astype(BF16)

    def proj(start, width):
        return _dot(h, w_ref[:, start:start + width])

    lane = lax.broadcasted_iota(jnp.int32, (TM, LANE), 1)
    first_a = (lane % 16) < 8
    first_d = (lane % 32) < 16

    def rope_a(blk):
        return _rope(blk, cosq_ref[...], sinq_ref[...], first_a, 8) if latent else blk

    def rope_d(blk):
        return _rope(blk, cosd_ref[...], sind_ref[...], first_d, 16) if latent else blk

    pa = proj(C_ACQ, C_BZ - C_ACQ)
    acq = pa[:, 0:256]
    ms = jnp.sum(acq * acq, axis=-1, keepdims=True) * (1.0 / Q_RANK)
    qn = (acq * lax.rsqrt(ms + EPS) * gq_ref[...]).astype(BF16)
    q = _dot(qn, wuq_ref[...])
    scale_a = (NOPE_A + ROPE_A) ** -0.5 * LOG2_E
    for hh in range(H_A):
        blk = slice(HB * hh, HB * (hh + 1))
        qa_ref[:, blk] = (rope_a(q[:, blk]) * scale_a).astype(BF16)
    ackv = pa[:, C_ACKV:C_ACKV + KV_RANK]
    ms = jnp.mean(ackv * ackv, axis=-1, keepdims=True)
    ckv = ackv * lax.rsqrt(ms + EPS) * gkv_ref[...]
    ckv_b = ckv.astype(BF16)
    kk = _dot(ckv_b, wk_ref[...])
    akr = pa[:, C_AKR:C_AKR + LANE]
    if not latent:
        out["ckv"][...] = ckv
        out["kr"][...] = akr
    krp = rope_a(pltpu.roll(akr, NOPE_A, 1))
    for hh in range(H_A):
        blk = slice(HB * hh, HB * (hh + 1))
        ka_ref[:, blk] = (kk[:, blk] + krp).astype(BF16)
    va_ref[...] = _dot(ckv_b, wv_ref[...]).astype(BF16)

    pb = proj(C_BZ, C_CQ - C_BZ)
    bz_ref[...] = pb[:, 0:H_B * P_B]
    bxbc_ref[...] = pb[:, C_BXBC - C_BZ:C_BXBC - C_BZ + W_XBC]
    dtr = pb[:, C_BDT - C_BZ:C_BDT - C_BZ + LANE] + dtb_ref[...]
    bdt_ref[...] = jnp.maximum(dtr, 0.0) + jnp.log(1.0 + jnp.exp(-jnp.abs(dtr)))

    pc = proj(C_CQ, C_DQ - C_CQ)
    cq_ref[...] = pc[:, 0:256]
    cf_ref[...] = pc[:, C_CF - C_CQ:C_CF - C_CQ + 512]
    ci_ref[...] = pc[:, C_CI - C_CQ:C_CI - C_CQ + 256]
    cg_ref[...] = pc[:, C_CG - C_CQ:C_CG - C_CQ + 256]

    pd = proj(C_DQ, W_IN_P - C_DQ)
    dq = pd[:, 0:H_D * HD_D]
    scale_d = HD_D ** -0.5
    for t in range(H_D * HD_D // LANE):
        blk = slice(LANE * t, LANE * (t + 1))
        dq_ref[:, blk] = (rope_d(dq[:, blk]) * scale_d).astype(BF16)
    dk = pd[:, C_DK - C_DQ:C_DK - C_DQ + KV_D * LANE]
    dk = [rope_d(dk[:, LANE * g:LANE * (g + 1)]) for g in range(KV_D)]
    for g in range(KV_D):
        dk_ref[:, LANE * g:LANE * (g + 1)] = dk[g]
    dv = pd[:, C_DV - C_DQ:C_DV - C_DQ + KV_D * LANE]
    dv_ref[...] = dv
    if not latent:
        out["dkc"][...] = jnp.where(_low_lanes(), dk[0], dk[1])
        out["dvc"][...] = jnp.where(_low_lanes(), dv[:, 0:LANE], dv[:, LANE:2 * LANE])


def _in_outputs(latent):
    outs = [("qa", 512, BF16), ("ka", 512, BF16), ("va", 256, BF16),
            ("bz", H_B * P_B, F32), ("bxbc", W_XBC, F32), ("bdt", LANE, F32),
            ("cq", 256, F32), ("cf", 512, F32), ("ci", 256, F32), ("cg", 256, F32),
            ("dq", H_D * HD_D, BF16), ("dk", KV_D * LANE, F32), ("dv", KV_D * LANE, F32)]
    if not latent:
        outs += [("ckv", KV_RANK, F32), ("kr", LANE, F32), ("dkc", KV_D * HD_D, F32), ("dvc", KV_D * HD_D, F32)]
    return outs


def _in_proj(x, mod, group_len, latent, wp, tabs):
    n = x.shape[0]
    row = lambda w: pl.BlockSpec((TM, w), lambda i: (i, 0))
    tab = pl.BlockSpec((TM, LANE), (lambda i: (1 + i % (DEC_SEQ // TM), 0)) if latent else (lambda i: (0, 0)))
    outs = pl.pallas_call(
        functools.partial(_in_kernel, latent=latent),
        grid=(n // TM,),
        in_specs=[row(D_MODEL), _mod_spec(group_len), _resident((D_MODEL, W_IN_P)), _resident((256, 512)),
                  _resident((KV_RANK, 512)), _resident((KV_RANK, 256)), _resident((1, 256)),
                  _resident((1, KV_RANK)), _resident((1, LANE)), tab, tab, tab, tab],
        out_specs=[row(w) for _, w, _ in _in_outputs(latent)],
        out_shape=[jax.ShapeDtypeStruct((n, w), dt) for _, w, dt in _in_outputs(latent)],
        compiler_params=_params("arbitrary"),
        name="in_proj",
    )(x, mod, wp["w_in"], wp["w_uq"], wp["w_uk"], wp["w_uv"], wp["g_q"], wp["g_kv"], wp["dt_bias"],
      tabs[0], tabs[1], tabs[2], tabs[3])
    return dict(zip([k for k, _, _ in _in_outputs(latent)], outs))


def _mla_cache_kernel(ckv_ref, krp_ref, wk_ref, wv_ref, kc_ref, vc_ref):
    ckv_b = ckv_ref[0].astype(BF16)
    kk = _dot(ckv_b, wk_ref[...])
    krp = krp_ref[0]
    for hh in range(H_A):
        blk = slice(HB * hh, HB * (hh + 1))
        kc_ref[0, :, blk] = (kk[:, blk] + krp).astype(BF16)
    vc_ref[0] = _dot(ckv_b, wv_ref[...]).astype(BF16)


def _mla_cache(ckv, krope_placed, wp):
    nb = ckv.shape[0]
    return pl.pallas_call(
        _mla_cache_kernel,
        grid=(nb,),
        in_specs=[pl.BlockSpec((1, PAST_LEN, KV_RANK), lambda b: (b, 0, 0)),
                  pl.BlockSpec((1, PAST_LEN, LANE), lambda b: (b, 0, 0)),
                  _resident((KV_RANK, 512)), _resident((KV_RANK, 256))],
        out_specs=[pl.BlockSpec((1, PAST_LEN, 512), lambda b: (b, 0, 0)),
                   pl.BlockSpec((1, PAST_LEN, 256), lambda b: (b, 0, 0))],
        out_shape=[jax.ShapeDtypeStruct((nb, PAST_LEN, 512), BF16),
                   jax.ShapeDtypeStruct((nb, PAST_LEN, 256), BF16)],
        compiler_params=_params("arbitrary"),
        name="mla_cache",
    )(ckv, krope_placed, wp["w_uk"], wp["w_uv"])


def _mla_kernel(*refs, has_cache, seqs):
    if has_cache:
        q_ref, k_ref, v_ref, kc_ref, vc_ref, o_ref = refs
    else:
        q_ref, k_ref, v_ref, o_ref = refs
    tk = k_ref.shape[0] // seqs
    tq = q_ref.shape[0] // seqs
    head_of_lane = lax.broadcasted_iota(jnp.int32, (1, H_A * V_A), 1) // V_A
    for b in range(seqs):
        qrows = slice(b * tq, (b + 1) * tq)
        krows = slice(b * tk, (b + 1) * tk)
        v = v_ref[krows, :]
        blocks = [slice(HB * hh, HB * (hh + 1)) for hh in range(H_A)]
        scores = [_dot_nt(q_ref[qrows, blk], k_ref[krows, blk]) for blk in blocks]
        if has_cache:
            scores_c = [_dot_nt(q_ref[qrows, blk], kc_ref[0, :, blk]) for blk in blocks]
        acc = jnp.zeros((tq, H_A * V_A), F32)
        for hh in range(H_A):
            s = scores[hh]
            m = jnp.max(s, axis=-1, keepdims=True)
            if has_cache:
                sc = scores_c[hh]
                m = jnp.maximum(m, jnp.max(sc, axis=-1, keepdims=True))
            e = jnp.exp2(s - m)
            den = jnp.sum(e, axis=-1, keepdims=True)
            pv = _dot(e.astype(BF16), v)
            if has_cache:
                ec = jnp.exp2(sc - m)
                den = den + jnp.sum(ec, axis=-1, keepdims=True)
                pv = pv + _dot(ec.astype(BF16), vc_ref[0])
            acc = jnp.where(head_of_lane == hh, pv / den, acc)
        o_ref[qrows, :] = acc.astype(BF16)


def _col_reduce(op, x):
    rows, q = x.shape
    g = MLA_REDUCE_GROUPS if rows % (MLA_REDUCE_GROUPS * SUBLANE) == 0 else 1
    return op(op(x.reshape(g, rows // g, q), axis=1), axis=0, keepdims=True)


def _mla_lat_kernel(q_ref, k_ref, v_ref, kc_ref, vc_ref, o_ref, vt_scr, vct_scr):
    @pl.when(pl.program_id(1) == 0)
    def _():
        vt_scr[...] = v_ref[...].astype(F32).T.astype(BF16)
        vct_scr[...] = vc_ref[0].astype(F32).T.astype(BF16)

    blocks = [slice(HB * hh, HB * (hh + 1)) for hh in range(H_A)]
    scores = [_dot_nt(k_ref[:, blk], q_ref[:, blk]) for blk in blocks]
    scores_c = [_dot_nt(kc_ref[0, :, blk], q_ref[:, blk]) for blk in blocks]
    outs = []
    for hh in range(H_A):
        vrows = slice(V_A * hh, V_A * (hh + 1))
        s, sc = scores[hh], scores_c[hh]
        m = jnp.maximum(_col_reduce(jnp.max, s), _col_reduce(jnp.max, sc))
        e = jnp.exp2(s - m)
        ec = jnp.exp2(sc - m)
        den = _col_reduce(jnp.sum, e) + _col_reduce(jnp.sum, ec)
        o_t = _dot(vt_scr[vrows, :], e.astype(BF16)) + _dot(vct_scr[vrows, :], ec.astype(BF16))
        outs.append(o_t / den)
    o_ref[...] = jnp.concatenate(outs, axis=0).T.astype(BF16)


def _mla_lat(q, k, v, n_seq, seq_len, cache):
    nq = seq_len // TQ_A
    return pl.pallas_call(
        _mla_lat_kernel,
        grid=(n_seq, nq),
        in_specs=[pl.BlockSpec((TQ_A, 512), lambda b, i: (b * nq + i, 0)),
                  pl.BlockSpec((seq_len, 512), lambda b, i: (b, 0)),
                  pl.BlockSpec((seq_len, 256), lambda b, i: (b, 0)),
                  pl.BlockSpec((1, PAST_LEN, 512), lambda b, i: (b, 0, 0)),
                  pl.BlockSpec((1, PAST_LEN, 256), lambda b, i: (b, 0, 0))],
        out_specs=pl.BlockSpec((TQ_A, H_A * V_A), lambda b, i: (b * nq + i, 0)),
        out_shape=jax.ShapeDtypeStruct((n_seq * seq_len, H_A * V_A), BF16),
        scratch_shapes=[pltpu.VMEM((H_A * V_A, seq_len), BF16), pltpu.VMEM((H_A * V_A, PAST_LEN), BF16)],
        compiler_params=_params("arbitrary", "arbitrary"),
        name="mla_latent",
    )(q, k, v, *cache)


def _mla(q, k, v, n_seq, seq_len, cache=None):
    tq = TQ_A if cache is not None else seq_len
    nq = seq_len // tq
    seqs = 1 if cache is not None else MLA_CTX_SEQS
    in_specs = [pl.BlockSpec((seqs * tq, 512), lambda b, i: (b * nq + i, 0)),
                pl.BlockSpec((seqs * seq_len, 512), lambda b, i: (b, 0)),
                pl.BlockSpec((seqs * seq_len, 256), lambda b, i: (b, 0))]
    args = [q, k, v]
    if cache is not None:
        in_specs += [pl.BlockSpec((1, PAST_LEN, 512), lambda b, i: (b, 0, 0)),
                     pl.BlockSpec((1, PAST_LEN, 256), lambda b, i: (b, 0, 0))]
        args += list(cache)
    return pl.pallas_call(
        functools.partial(_mla_kernel, has_cache=cache is not None, seqs=seqs),
        grid=(n_seq // seqs, nq),
        in_specs=in_specs,
        out_specs=pl.BlockSpec((seqs * tq, H_A * V_A), lambda b, i: (b * nq + i, 0)),
        out_shape=jax.ShapeDtypeStruct((n_seq * seq_len, H_A * V_A), BF16),
        compiler_params=_params("arbitrary", "arbitrary"),
        name="mla_attention",
    )(*args)


def _sink_softmax_pv(parts, sink):
    m = sink
    for s, _ in parts:
        m = jnp.maximum(m, jnp.max(s, axis=-1, keepdims=True))
    den = jnp.exp(sink - m)
    pv = None
    for s, v in parts:
        e = jnp.exp(s - m)
        den = den + jnp.sum(e, axis=-1, keepdims=True)
        t = _dot(e.astype(BF16), v)
        pv = t if pv is None else pv + t
    return pv / den


def _gqa_pair(q_ref, rows, g, scores_and_values, sink_ref, o_ref):
    t = rows.stop - rows.start
    tile = slice(LANE * g, LANE * (g + 1))
    qt = q_ref[rows, tile]
    low = _low_lanes()
    zero_b = jnp.zeros((), BF16)
    q2 = jnp.concatenate([jnp.where(low, qt, zero_b), jnp.where(low, zero_b, qt)], axis=0)
    second = lax.broadcasted_iota(jnp.int32, (G_D * t, 1), 0) >= t
    sink = jnp.where(second, sink_ref[G_D * g + 1:G_D * g + 2, 0:1], sink_ref[G_D * g:G_D * g + 1, 0:1])
    o = _sink_softmax_pv(scores_and_values(q2), sink)
    o_ref[rows, tile] = jnp.where(low, o[0:t, :], o[t:2 * t, :]).astype(BF16)


def _gqa_ctx_kernel(q_ref, k_ref, v_ref, sink_ref, o_ref):
    for b in range(GQA_CTX_SEQS):
        rows = slice(b * SEQ, (b + 1) * SEQ)
        for g in range(KV_D):
            tile = slice(LANE * g, LANE * (g + 1))
            k = k_ref[rows, tile].astype(BF16)
            v = v_ref[rows, tile].astype(BF16)
            _gqa_pair(q_ref, rows, g, lambda q2, k=k, v=v: [(_dot_nt(q2, k), v)], sink_ref, o_ref)


def _gqa_ctx(q, k, v, sink, n_seq):
    seq = lambda w: pl.BlockSpec((GQA_CTX_SEQS * SEQ, w), lambda b: (b, 0))
    return pl.pallas_call(
        _gqa_ctx_kernel,
        grid=(n_seq // GQA_CTX_SEQS,),
        in_specs=[seq(H_D * HD_D), seq(KV_D * LANE), seq(KV_D * LANE), _resident((H_D, LANE))],
        out_specs=seq(H_D * HD_D),
        out_shape=jax.ShapeDtypeStruct((n_seq * SEQ, H_D * HD_D), BF16),
        compiler_params=_params("arbitrary"),
        name="gqa_context",
    )(q, k, v, sink)


def _gqa_lat_kernel(q_ref, k_ref, v_ref, kc_ref, vc_ref, sink_ref, o_ref):
    span = 3 * WINDOW
    row = lax.broadcasted_iota(jnp.int32, (G_D * WINDOW, span), 0) % WINDOW
    col = lax.broadcasted_iota(jnp.int32, (G_D * WINDOW, span), 1)
    for j in range(GQA_LAT_BLOCKS):
        n = pl.program_id(1) * GQA_LAT_BLOCKS + j
        rows = slice(j * WINDOW, (j + 1) * WINDOW)
        start = pl.multiple_of(jnp.clip((n - 1) * WINDOW, 0, DEC_SEQ - span), WINDOW)
        band = jnp.abs((start + col) - (n * WINDOW + row)) <= WINDOW
        for g in range(KV_D):
            tile = slice(LANE * g, LANE * (g + 1))
            kw = k_ref[pl.ds(start, span), tile].astype(BF16)
            vw = v_ref[pl.ds(start, span), tile].astype(BF16)
            kc = kc_ref[0, :, tile]
            vc = vc_ref[0, :, tile]

            def parts(q2, kw=kw, vw=vw, kc=kc, vc=vc, band=band):
                return [(jnp.where(band, _dot_nt(q2, kw), NEG), vw), (_dot_nt(q2, kc), vc)]

            _gqa_pair(q_ref, rows, g, parts, sink_ref, o_ref)


def _gqa_lat(q, k, v, kc, vc, sink, n_seq):
    nb = DEC_SEQ // (WINDOW * GQA_LAT_BLOCKS)
    wq, wkv = H_D * HD_D, KV_D * LANE
    qblk = pl.BlockSpec((WINDOW * GQA_LAT_BLOCKS, wq), lambda b, n: (b * nb + n, 0))
    return pl.pallas_call(
        _gqa_lat_kernel,
        grid=(n_seq, nb),
        in_specs=[qblk,
                  pl.BlockSpec((DEC_SEQ, wkv), lambda b, n: (b, 0)),
                  pl.BlockSpec((DEC_SEQ, wkv), lambda b, n: (b, 0)),
                  pl.BlockSpec((1, PAST_LEN, wkv), lambda b, n: (b, 0, 0)),
                  pl.BlockSpec((1, PAST_LEN, wkv), lambda b, n: (b, 0, 0)),
                  _resident((H_D, LANE))],
        out_specs=qblk,
        out_shape=jax.ShapeDtypeStruct((n_seq * DEC_SEQ, wq), BF16),
        compiler_params=_params("arbitrary", "arbitrary"),
        name="gqa_latent",
    )(q, k, v, kc, vc, sink)


def _conv_kernel(cur_ref, prev_ref, next_ref, w_ref, b_ref, o_ref, pad_ref, *, tiles_per_seq):
    i = pl.program_id(0)
    has_prev = (i % tiles_per_seq) != 0
    has_next = (i % tiles_per_seq) != tiles_per_seq - 1
    pad_ref[0:SUBLANE, :] = jnp.where(has_prev, prev_ref[...], 0.0)
    pad_ref[SUBLANE:SUBLANE + CONV_TILE, :] = cur_ref[...]
    pad_ref[SUBLANE + CONV_TILE:, :] = jnp.where(has_next, next_ref[...], 0.0)
    y = jnp.zeros((CONV_TILE, W_XBC), F32) + b_ref[...]
    for k in range(D_CONV):
        off = SUBLANE - D_CONV // 2 + k
        y = y + w_ref[k:k + 1, :] * pad_ref[off:off + CONV_TILE, :]
    o_ref[...] = _silu(y)


def _conv(xbc, seq_len, w, b):
    n = xbc.shape[0]
    per = CONV_TILE // SUBLANE
    last = n // SUBLANE - 1
    return pl.pallas_call(
        functools.partial(_conv_kernel, tiles_per_seq=seq_len // CONV_TILE),
        grid=(n // CONV_TILE,),
        in_specs=[pl.BlockSpec((CONV_TILE, W_XBC), lambda i: (i, 0)),
                  pl.BlockSpec((SUBLANE, W_XBC), lambda i: (jnp.maximum(i * per - 1, 0), 0)),
                  pl.BlockSpec((SUBLANE, W_XBC), lambda i: (jnp.minimum((i + 1) * per, last), 0)),
                  _resident((SUBLANE, W_XBC)), _resident((1, W_XBC))],
        out_specs=pl.BlockSpec((CONV_TILE, W_XBC), lambda i: (i, 0)),
        out_shape=jax.ShapeDtypeStruct((n, W_XBC), F32),
        scratch_shapes=[pltpu.VMEM((CONV_TILE + 2 * SUBLANE, W_XBC), F32)],
        compiler_params=_params("arbitrary"),
        name="ssd_conv",
    )(xbc, xbc, xbc, w, b)


def _ssd_kernel(*refs, has_s0, chunks, seqs):
    if has_s0:
        xf_ref, xb_ref, dtf_ref, dtb_ref, alog_ref, dsk_ref, s0_ref, yf_ref, yb_ref, st_ref, s_scr = refs
    else:
        xf_ref, xb_ref, dtf_ref, dtb_ref, alog_ref, dsk_ref, yf_ref, yb_ref, st_ref, s_scr = refs
    c = pl.program_id(1)
    q = SSD_CHUNK

    @pl.when(c == 0)
    def _():
        s_scr[...] = s0_ref[...] if has_s0 else jnp.zeros(s_scr.shape, F32)

    row = lax.broadcasted_iota(jnp.int32, (q, q), 0)
    col = lax.broadcasted_iota(jnp.int32, (q, q), 1)
    a_coef = -jnp.exp(alog_ref[...])
    low = _low_lanes()
    first_head_rows = lax.broadcasted_iota(jnp.int32, (HB, 1), 0) < P_B
    for sq, d in [(sq, d) for sq in range(seqs) for d in range(2)]:
        x_ref, dt_ref, y_ref = ((xf_ref, dtf_ref, yf_ref), (xb_ref, dtb_ref, yb_ref))[d]
        tri = (row >= col) if d == 0 else (row <= col)
        for ci in (range(chunks) if d == 0 else range(chunks - 1, -1, -1)):
            rows = slice((sq * chunks + ci) * q, (sq * chunks + ci + 1) * q)
            dt = dt_ref[rows, :]
            cum = _prefix_dot(tri, dt * a_coef)
            cum_t = cum.T
            dt_t = dt.T
            total = cum[q - 1:q, :] if d == 0 else cum[0:1, :]
            for g in range(G_B):
                half = low if g == 0 else jnp.logical_not(low)
                bg = jnp.where(half, x_ref[rows, 2 * LANE:3 * LANE], 0.0)
                cg = jnp.where(half, x_ref[rows, 3 * LANE:4 * LANE], 0.0)
                cb = _dot_nt(cg.astype(BF16), bg.astype(BF16))
                xp = x_ref[rows, LANE * g:LANE * (g + 1)]
                xp_b = xp.astype(BF16)
                xp_t = xp.T
                s_in = s_scr[sq, d, g]
                s_in_b = s_in.astype(BF16)
                ys, states = [], []
                for j in range(H_B // G_B):
                    k = d * H_B + g * (H_B // G_B) + j
                    cum_b = jnp.broadcast_to(cum[:, k:k + 1], (q, LANE))
                    seg = jnp.where(tri, jnp.exp(jnp.where(tri, cum_b - cum_t[k:k + 1, :], 0.0)), 0.0)
                    dt_row = dt_t[k:k + 1, :]
                    y = _dot((cb * seg * dt_row).astype(BF16), xp_b)
                    ys.append(y + _dot_nt((cg * jnp.exp(cum_b)).astype(BF16), s_in_b))
                    tot = total[:, k:k + 1]
                    bdec = bg * jnp.exp(tot - cum_b)
                    cs = _dot((xp_t * dt_row).astype(BF16), bdec.astype(BF16))
                    states.append(jnp.exp(tot) * s_in + cs)
                k0 = d * H_B + g * (H_B // G_B)
                d_skip = jnp.where(low, dsk_ref[k0:k0 + 1, :], dsk_ref[k0 + 1:k0 + 2, :])
                y_ref[rows, LANE * g:LANE * (g + 1)] = jnp.where(low, ys[0], ys[1]) + xp * d_skip
                s_scr[sq, d, g] = jnp.where(first_head_rows, states[0], states[1])

    @pl.when(c == pl.num_programs(1) - 1)
    def _():
        for sq in range(seqs):
            for d in range(2):
                for hh in range(H_B):
                    g, j = hh // (H_B // G_B), hh % (H_B // G_B)
                    st_ref[sq, d, hh] = s_scr[sq, d, g, P_B * j:P_B * (j + 1), N_B * g:N_B * (g + 1)]


def _ssd(xbc, dt, n_seq, seq_len, a_log, dskip, s0=None):
    chunks = min(SSD_STEP_CHUNKS, seq_len // SSD_CHUNK)
    nc = seq_len // (SSD_CHUNK * chunks)
    seqs = SSD_STEP_CHUNKS // chunks if nc == 1 else 1
    fwd = lambda w: pl.BlockSpec((SSD_CHUNK * chunks * seqs, w), lambda b, c: (b * nc + c, 0))
    bwd = lambda w: pl.BlockSpec((SSD_CHUNK * chunks * seqs, w), lambda b, c: (b * nc + nc - 1 - c, 0))
    state = pl.BlockSpec((seqs, 2, G_B, HB, HB), lambda b, c: (b, 0, 0, 0, 0))
    in_specs = [fwd(W_XBC), bwd(W_XBC), fwd(LANE), bwd(LANE), _resident((1, LANE)), _resident((2 * H_B, LANE))]
    args = [xbc, xbc, dt, dt, a_log, dskip]
    if s0 is not None:
        in_specs.append(state)
        args.append(s0)
    n = n_seq * seq_len
    return pl.pallas_call(
        functools.partial(_ssd_kernel, has_s0=s0 is not None, chunks=chunks, seqs=seqs),
        grid=(n_seq // seqs, nc),
        in_specs=in_specs,
        out_specs=[fwd(H_B * P_B), bwd(H_B * P_B),
                   pl.BlockSpec((seqs, 2, H_B, P_B, N_B), lambda b, c: (b, 0, 0, 0, 0))],
        out_shape=[jax.ShapeDtypeStruct((n, H_B * P_B), F32), jax.ShapeDtypeStruct((n, H_B * P_B), F32),
                   jax.ShapeDtypeStruct((n_seq, 2, H_B, P_B, N_B), F32)],
        scratch_shapes=[pltpu.VMEM((seqs, 2, G_B, HB, HB), F32)],
        compiler_params=_params("arbitrary", "arbitrary"),
        name="ssd_scan",
    )(*args)


def _gla_kernel(*refs, has_s0):
    if has_s0:
        (qf_ref, qb_ref, ff_ref, fb_ref, vf_ref, vb_ref, lb_ref, s0_ref, of_ref, ob_ref, st_ref, s_scr,
         o_scr) = refs
    else:
        (qf_ref, qb_ref, ff_ref, fb_ref, vf_ref, vb_ref, lb_ref, of_ref, ob_ref, st_ref, s_scr, o_scr) = refs
    c = pl.program_id(1)
    t = GLA_TILE
    ch = GLA_CHUNK
    nch = t // ch
    w = H_C * K_C

    @pl.when(c == 0)
    def _():
        s_scr[...] = s0_ref[0] if has_s0 else jnp.zeros(s_scr.shape, F32)

    row = lax.broadcasted_iota(jnp.int32, (t, t), 0)
    col = lax.broadcasted_iota(jnp.int32, (t, t), 1)
    same_chunk = (row // ch) == (col // ch)
    head_of_lane = lax.broadcasted_iota(jnp.int32, (1, w), 1) // K_C
    row_head = lax.broadcasted_iota(jnp.int32, (H_C * ch, 1), 0) // ch
    zero_b = jnp.zeros((), BF16)
    for d, (q_ref, f_ref, v_ref, o_ref) in enumerate(((qf_ref, ff_ref, vf_ref, of_ref),
                                                      (qb_ref, fb_ref, vb_ref, ob_ref))):
        qv = q_ref[...]
        fr = f_ref[...]
        vv = v_ref[...]
        lb = lb_ref[d:d + 1, :]
        f = lb + (1.0 - lb) * jax.nn.sigmoid(fr)
        log_f = jnp.log(jnp.maximum(f, F_MIN))
        key = (1.0 - lb) * jax.nn.sigmoid(-fr)
        tri = same_chunk & ((col <= row) if d == 0 else (col >= row))
        cum = _prefix_dot(tri, log_f)
        cum3 = cum.reshape(nch, ch, w)
        k3 = key.reshape(nch, ch, w)
        q_dec = (qv * jnp.exp(cum)).astype(BF16)
        q_heads = [jnp.where(head_of_lane == hh, q_dec, zero_b) for hh in range(H_C)]
        v_heads = [_head_alone(vv[:, LANE * (hh // 2):LANE * (hh // 2 + 1)], hh % 2) for hh in range(H_C)]

        k_inv = (key * jnp.exp(-cum)).astype(BF16)
        for hh in range(H_C):
            att = jnp.where(tri, _dot_nt(q_heads[hh], k_inv), 0.0).astype(BF16)
            o_scr[:, HB * hh:HB * (hh + 1)] = _dot(att, v_heads[hh].astype(BF16))

        @pl.when(jnp.min(cum) < -GLA_SAFE_LOG_DECAY)
        def _():
            wp = H_C * HB
            head_ones = ((lax.broadcasted_iota(jnp.int32, (w, wp), 0) // K_C)
                         == (lax.broadcasted_iota(jnp.int32, (w, wp), 1) // HB)).astype(F32)
            i_in_chunk = lax.broadcasted_iota(jnp.int32, (nch, ch, w), 1)
            q3 = qv.reshape(nch, ch, w)
            v3 = jnp.concatenate(v_heads, axis=1).reshape(nch, ch, wp)
            o3 = jnp.zeros((nch, ch, wp), F32)
            for j in range(ch):
                live = (i_in_chunk >= j) if d == 0 else (i_in_chunk <= j)
                e = jnp.exp(jnp.where(live, cum3 - cum3[:, j:j + 1, :], 0.0))
                term = jnp.where(live, q3 * e * k3[:, j:j + 1, :], 0.0)
                att = _dot(term.reshape(t, w), head_ones)
                o3 = o3 + att.reshape(nch, ch, wp) * v3[:, j:j + 1, :]
            o_scr[...] = o3.reshape(t, wp)

        edge = ch - 1 if d == 0 else 0
        last3 = jnp.broadcast_to(cum3[:, edge:edge + 1, :], (nch, ch, w))
        k_dec = (k3 * jnp.exp(last3 - cum3)).reshape(t, w).astype(BF16)
        order = range(nch) if d == 0 else range(nch - 1, -1, -1)
        for cc in order:
            rows = slice(cc * ch, (cc + 1) * ch)
            st = s_scr[d]
            q4 = jnp.concatenate([qh[rows, :] for qh in q_heads], axis=0)
            r = _dot_nt(q4, st.astype(BF16))
            for hh in range(H_C):
                blk = slice(HB * hh, HB * (hh + 1))
                o_scr[rows, blk] = o_scr[rows, blk] + r[hh * ch:(hh + 1) * ch, :]
            v4 = jnp.concatenate([vh[rows, :] for vh in v_heads], axis=0)
            k4 = jnp.where(row_head == head_of_lane, jnp.concatenate([k_dec[rows, :]] * H_C, axis=0), zero_b)
            decay = jnp.exp(cum[cc * ch + edge:cc * ch + edge + 1, :])
            s_scr[d] = decay * st + _dot(v4.T.astype(BF16), k4)

        for p in range(H_C // 2):
            o_ref[:, LANE * p:LANE * (p + 1)] = _two_heads(o_scr[:, HB * 2 * p:HB * (2 * p + 1)],
                                                           o_scr[:, HB * (2 * p + 1):HB * (2 * p + 2)])

    @pl.when(c == pl.num_programs(1) - 1)
    def _():
        k_idx = lax.broadcasted_iota(jnp.int32, (K_C, w), 0)
        lane_idx = lax.broadcasted_iota(jnp.int32, (K_C, w), 1)
        for d in range(2):
            rows = s_scr[d, 0:V_C, :]
            for hh in range(H_C):
                sel = (lane_idx == hh * K_C + k_idx).astype(F32)
                st_ref[0, d, hh] = lax.dot_general(sel, rows, (((1,), (1,)), ((), ())),
                                                   preferred_element_type=F32, precision=lax.Precision.HIGHEST)


def _gla(cq, cf, ci, n_seq, seq_len, lb, s0=None):
    nt = seq_len // GLA_TILE
    w = H_C * K_C
    fwd = lambda j: pl.BlockSpec((GLA_TILE, w), lambda b, c: (b * nt + c, j))
    bwd = lambda j: pl.BlockSpec((GLA_TILE, w), lambda b, c: (b * nt + nt - 1 - c, j))
    state = pl.BlockSpec((1, 2, HB, w), lambda b, c: (b, 0, 0, 0))
    in_specs = [fwd(0), bwd(0), fwd(0), bwd(1), fwd(0), bwd(0), _resident((2, w))]
    args = [cq, cq, cf, cf, ci, ci, lb]
    if s0 is not None:
        in_specs.append(state)
        args.append(s0)
    n = n_seq * seq_len
    return pl.pallas_call(
        functools.partial(_gla_kernel, has_s0=s0 is not None),
        grid=(n_seq, nt),
        in_specs=in_specs,
        out_specs=[fwd(0), bwd(0), pl.BlockSpec((1, 2, H_C, K_C, V_C), lambda b, c: (b, 0, 0, 0, 0))],
        out_shape=[jax.ShapeDtypeStruct((n, w), F32), jax.ShapeDtypeStruct((n, w), F32),
                   jax.ShapeDtypeStruct((n_seq, 2, H_C, K_C, V_C), F32)],
        scratch_shapes=[pltpu.VMEM((2, HB, w), F32), pltpu.VMEM((GLA_TILE, H_C * HB), F32)],
        compiler_params=_params("arbitrary", "arbitrary"),
        name="hgrn_scan",
    )(*args)


def _out_ffn_kernel(x_ref, mod_ref, oa_ref, yf_ref, yb_ref, bz_ref, of_ref, ob_ref, cg_ref, od_ref,
                    wo_ref, nb_ref, nc_ref, g1_ref, b1_ref, wg_ref, wu_ref, wd_ref, g2_ref, b2_ref, o_ref):
    x = x_ref[...]
    gate = mod_ref[0, 5:6, :]
    yb = _rms_heads((yf_ref[...] + yb_ref[...]) * _silu(bz_ref[...]), H_B * P_B) * nb_ref[...]
    oc = _rms_heads(of_ref[...] + ob_ref[...], H_C * V_C) * nc_ref[...] * _silu(cg_ref[...])
    mixed = jnp.concatenate([oa_ref[...], yb.astype(BF16), oc.astype(BF16), od_ref[...]], axis=-1)
    u = _dot(mixed, wo_ref[...])
    x = _layer_norm(ALPHA * x + gate * u, g1_ref[...], b1_ref[...])
    o_ref[...] = _ffn_sublayer(x, mod_ref, wg_ref, wu_ref, wd_ref, g2_ref, b2_ref, 2)


def _out_ffn(x, mod, seq_len, mix, wp, ln1_g, ln1_b, w_gu, w_down, layer_sub, ln2_g, ln2_b):
    n = x.shape[0]
    l, s = layer_sub
    row = lambda w: pl.BlockSpec((TM, w), lambda i: (i, 0))
    names = ("oa", "yf", "yb", "bz", "of", "ob", "cg", "od")
    vec = _resident((1, D_MODEL))
    return pl.pallas_call(
        _out_ffn_kernel,
        grid=(n // TM,),
        in_specs=[row(D_MODEL), _mod_spec(seq_len)] + [row(mix[k].shape[1]) for k in names]
                 + [_resident((D_MODEL, D_MODEL)), _resident((1, H_B * P_B)), _resident((1, H_C * V_C)), vec, vec,
                    _resident((None, None, D_MODEL, D_FF), (l, s, 0, 0)),
                    _resident((None, None, D_MODEL, D_FF), (l, s, 0, 1)),
                    _resident((None, None, D_FF, D_MODEL), (l, s, 0, 0)), vec, vec],
        out_specs=row(D_MODEL),
        out_shape=jax.ShapeDtypeStruct((n, D_MODEL), F32),
        compiler_params=_params("arbitrary"),
        name="out_proj_ffn",
    )(x, mod, *[mix[k] for k in names], wp["w_out"], wp["ssd_norm"], wp["hgrn_norm"], ln1_g, ln1_b,
      w_gu, w_gu, w_down, ln2_g, ln2_b)


def _prep_layer(l, w_in, w_out, mla_q_norm, mla_kv_norm, mla_w_uq, mla_w_ukv, ssd_conv_w, ssd_conv_b,
                ssd_a_log, ssd_dt_bias, ssd_d, ssd_norm, hgrn_lb, hgrn_norm, gqa_sink):
    return {
        "w_in": _gather_pad(w_in[l], _IDX_W_IN, 1).astype(BF16),
        "w_uq": _gather_pad(_gather_pad(mla_w_uq[l], _IDX_UQ_ROWS, 0), _IDX_UQ_COLS, 1).astype(BF16),
        "w_uk": _gather_pad(mla_w_ukv[l], _IDX_UKV_K, 1).astype(BF16),
        "w_uv": _gather_pad(mla_w_ukv[l], _IDX_UKV_V, 1).astype(BF16),
        "g_q": _gather_pad(mla_q_norm[l], _IDX_UQ_ROWS, 0).reshape(1, 256),
        "g_kv": mla_kv_norm[l].reshape(1, KV_RANK),
        "dt_bias": jnp.pad(ssd_dt_bias[l].reshape(1, 2 * H_B), ((0, 0), (0, LANE - 2 * H_B))),
        "conv_w": jnp.pad(ssd_conv_w[l], ((0, SUBLANE - D_CONV), (0, 0))),
        "conv_b": ssd_conv_b[l].reshape(1, W_XBC),
        "a_log": jnp.pad(ssd_a_log[l].reshape(1, 2 * H_B), ((0, 0), (0, LANE - 2 * H_B))),
        "d_skip": jnp.broadcast_to(ssd_d[l].reshape(2 * H_B, 1), (2 * H_B, LANE)),
        "ssd_norm": ssd_norm[l].reshape(1, H_B * P_B),
        "hgrn_lb": hgrn_lb[l],
        "hgrn_norm": hgrn_norm[l].reshape(1, H_C * V_C),
        "sink": jnp.broadcast_to(gqa_sink[l].reshape(H_D, 1), (H_D, LANE)),
        "w_out": w_out[l].astype(BF16),
    }


def _mixer(x, mod, group_len, wp, tabs, n_seq, seq_len, ctx):
    latent = ctx is not None
    p = _in_proj(x, mod, group_len, latent, wp, tabs)
    mix = {"bz": p["bz"], "cg": p["cg"]}
    cache = _mla_cache(ctx["ckv"], ctx["krope"], wp) if latent else None
    mix["oa"] = (_mla_lat if latent else _mla)(p["qa"], p["ka"], p["va"], n_seq, seq_len, cache)
    xbc = _conv(p["bxbc"], seq_len, wp["conv_w"], wp["conv_b"])
    mix["yf"], mix["yb"], st_b = _ssd(xbc, p["bdt"], n_seq, seq_len, wp["a_log"], wp["d_skip"],
                                      ctx["ssm"] if latent else None)
    mix["of"], mix["ob"], st_c = _gla(p["cq"], p["cf"], p["ci"], n_seq, seq_len, wp["hgrn_lb"],
                                      ctx["hgrn"] if latent else None)
    if latent:
        mix["od"] = _gqa_lat(p["dq"], p["dk"], p["dv"], ctx["dk"], ctx["dv"], wp["sink"], n_seq)
    else:
        mix["od"] = _gqa_ctx(p["dq"], p["dk"], p["dv"], wp["sink"], n_seq)
    state = None if latent else (p["ckv"], p["kr"], st_b, st_c, p["dkc"], p["dvc"])
    return mix, state


def _run_layer(y_p, y_s, mod_ctx, mod_lat, n_ctx, n_lat, ctx, wp, ffn_w, lng, lnb, tabs):
    y_p, y_s = _ffn(y_p, y_s, mod_ctx, mod_lat, y_p.shape[0], DEC_SEQ, *ffn_w[0], lng[0], lnb[0], sub=0)
    outs = []
    for x, mod, group_len, n_seq, seq_len, cx in ((y_p, mod_ctx, y_p.shape[0], n_ctx, SEQ, None),
                                                  (y_s, mod_lat, DEC_SEQ, n_lat, DEC_SEQ, ctx)):
        mix, st = _mixer(x, mod, group_len, wp, tabs, n_seq, seq_len, cx)
        outs.append((_out_ffn(x, mod, group_len, mix, wp, lng[1], lnb[1], *ffn_w[1], lng[2], lnb[2]), st))
    return outs[0][0], outs[1][0], outs[0][1]


def _layer_inputs(l, ctx_tensors, weights, hgrn_lb):
    (cache_a_ckv, cache_a_krope, state_b_ssm, state_c_hgrn, cache_d_k, cache_d_v) = ctx_tensors
    (ln_g, ln_b, ffn_w_gu, ffn_w_down, w_in, w_out, mla_q_norm, mla_kv_norm, mla_w_uq, mla_w_ukv, ssd_conv_w,
     ssd_conv_b, ssd_a_log, ssd_dt_bias, ssd_d, ssd_norm, hgrn_norm, gqa_sink) = weights
    wp = _prep_layer(l, w_in, w_out, mla_q_norm, mla_kv_norm, mla_w_uq, mla_w_ukv, ssd_conv_w, ssd_conv_b,
                     ssd_a_log, ssd_dt_bias, ssd_d, ssd_norm, hgrn_lb, hgrn_norm, gqa_sink)
    ffn_w = [(ffn_w_gu.astype(BF16), ffn_w_down.astype(BF16), (l, s)) for s in range(2)]
    lng = [ln_g[l, s].reshape(1, D_MODEL) for s in range(N_SUB)]
    lnb = [ln_b[l, s].reshape(1, D_MODEL) for s in range(N_SUB)]
    nb = cache_a_ckv.shape[0]
    ctx = {
        "ckv": cache_a_ckv[:, l],
        "krope": jnp.pad(cache_a_krope[:, l], ((0, 0), (0, 0), (NOPE_A, LANE - NOPE_A - ROPE_A))),
        "ssm": jnp.stack([jnp.pad(state_b_ssm[:, l, :, 2 * g:2 * g + 2].reshape(nb, 2, 2 * P_B, N_B),
                                  ((0, 0), (0, 0), (0, 0), (N_B * g, N_B * (G_B - 1 - g)))) for g in range(G_B)],
                         axis=2),
        "hgrn": jnp.pad(jnp.transpose(state_c_hgrn[:, l], (0, 1, 4, 2, 3)).reshape(nb, 2, V_C, H_C * K_C),
                        ((0, 0), (0, 0), (0, HB - V_C), (0, 0))),
        "dk": jnp.concatenate([cache_d_k[:, l]] * 2, axis=-1).reshape(nb, PAST_LEN, KV_D * LANE).astype(BF16),
        "dv": jnp.concatenate([cache_d_v[:, l]] * 2, axis=-1).reshape(nb, PAST_LEN, KV_D * LANE).astype(BF16),
    }
    return wp, ffn_w, lng, lnb, ctx


def kernel(x_prompt, x_sample, cache_a_ckv, cache_a_krope, state_b_ssm, state_c_hgrn, cache_d_k, cache_d_v,
           c, c_ctx, w_mod, b_mod, ln_g, ln_b, ffn_w_gu, ffn_w_down, w_in, w_out, mla_q_norm, mla_kv_norm,
           mla_w_uq, mla_w_ukv, ssd_conv_w, ssd_conv_b, ssd_a_log, ssd_dt_bias, ssd_d, ssd_norm,
           hgrn_lb_logits, hgrn_norm, gqa_sink):
    lb_p = jax.nn.softmax(hgrn_lb_logits.astype(F32), axis=0)
    hgrn_lb = jnp.cumsum(lb_p, axis=0) - lb_p[:1]

    cvec = jnp.concatenate([c_ctx[None], c, jnp.zeros((SUBLANE - 1 - DEC_BATCH, D_MODEL), F32)], axis=0)
    mod_all = _modulation(cvec, w_mod, b_mod)
    tabs = _rope_tables(8, (NOPE_A,)) + _rope_tables(16, (0, HD_D))
    ctx_tensors = (cache_a_ckv, cache_a_krope, state_b_ssm, state_c_hgrn, cache_d_k, cache_d_v)
    weights = (ln_g, ln_b, ffn_w_gu, ffn_w_down, w_in, w_out, mla_q_norm, mla_kv_norm, mla_w_uq, mla_w_ukv,
               ssd_conv_w, ssd_conv_b, ssd_a_log, ssd_dt_bias, ssd_d, ssd_norm, hgrn_norm, gqa_sink)

    y_p = x_prompt.reshape(BATCH * SEQ, D_MODEL)
    y_s = x_sample.reshape(DEC_BATCH * DEC_SEQ, D_MODEL)
    states = []
    for l in range(DEPTH):
        wp, ffn_w, lng, lnb, ctx = _layer_inputs(l, ctx_tensors, weights, hgrn_lb)
        mod_ctx = mod_all[l, 0:1].reshape(1, N_SUB * 3, D_MODEL)
        mod_lat = mod_all[l, 1:1 + DEC_BATCH].reshape(DEC_BATCH, N_SUB * 3, D_MODEL)
        y_p, y_s, st = _run_layer(y_p, y_s, mod_ctx, mod_lat, BATCH, DEC_BATCH, ctx, wp, ffn_w, lng, lnb, tabs)
        states.append(st)

    def stack(i, f):
        return jnp.stack([f(s[i]) for s in states], axis=1)

    new_a_ckv = stack(0, lambda t: t.reshape(BATCH, SEQ, KV_RANK))
    new_a_krope = stack(1, lambda t: t.reshape(BATCH, SEQ, LANE)[..., :ROPE_A])
    new_b_ssm = stack(2, lambda t: t)
    new_c_hgrn = stack(3, lambda t: t)
    new_d_k = stack(4, lambda t: t.reshape(BATCH, SEQ, KV_D, HD_D))
    new_d_v = stack(5, lambda t: t.reshape(BATCH, SEQ, KV_D, HD_D))
    return (y_p.reshape(BATCH, SEQ, D_MODEL), y_s.reshape(DEC_BATCH, DEC_SEQ, D_MODEL),
            new_a_ckv, new_a_krope, new_b_ssm, new_c_hgrn, new_d_k, new_d_v)
```

```python
import functools

import numpy as np
import jax
import jax.numpy as jnp
from jax import lax
from jax.experimental import pallas as pl
from jax.experimental.pallas import tpu as pltpu

F32 = jnp.float32
BF16 = jnp.bfloat16

D_MODEL = 1024
BATCH = 32
SEQ = 256
DEPTH = 2
DEC_BATCH = 2
DEC_SEQ = 4096
PAST_LEN = 512
GRID_W = 64
H_A, Q_RANK, KV_RANK, NOPE_A, ROPE_A, V_A = 4, 192, 128, 64, 32, 64
H_B, P_B, G_B, N_B, D_CONV, SSD_CHUNK = 4, 64, 2, 64, 5, 128
H_C, K_C, V_C = 4, 64, 64
H_D, KV_D, HD_D, WINDOW = 4, 2, 64, 128
G_D = H_D // KV_D
ROPE_BASE = 10000.0
D_FF = 2816
N_SUB = 3
ALPHA = (2 * DEPTH) ** 0.25
EPS = 1e-6
F_MIN = 1e-6
LOG2_E = 1.4426950408889634
NEG = -1e30
N_MOD = N_SUB * 3 * D_MODEL

LANE = 128
SUBLANE = 8
VMEM_LIMIT = 56 * 1024 * 1024

TM = 512
TM_FFN = 1024
FF_CHUNK = 256
TQ_A = 256
GLA_TILE = 256
GLA_CHUNK = 32
SSD_STEP_CHUNKS = 8
GQA_CTX_SEQS = 8
MLA_CTX_SEQS = 8
GQA_LAT_BLOCKS = 4
CONV_TILE = 256
MOD_TN = 1536
GLA_SAFE_LOG_DECAY = 60.0

C_ACQ, C_ACKV, C_AKR = 0, 256, 384
C_BZ, C_BXBC, C_BDT = 512, 768, 1280
C_CQ, C_CF, C_CI, C_CG = 1408, 1664, 2176, 2432
C_DQ, C_DK, C_DV = 2688, 2944, 3200
W_IN_P = 3456
W_XBC = 512
HB = 128


def _dot(a, b, precision=None):
    return jnp.dot(a, b, preferred_element_type=F32, precision=precision)


def _dot_nt(a, b):
    return lax.dot_general(a, b, (((1,), (1,)), ((), ())), preferred_element_type=F32)


def _prefix_dot(tri, x):
    t = tri.astype(BF16)
    hi = x.astype(BF16)
    rest = x - hi.astype(F32)
    mid = rest.astype(BF16)
    lo = (rest - mid.astype(F32)).astype(BF16)
    return _dot(t, hi) + _dot(t, mid) + _dot(t, lo)


def _params(*sem):
    return pltpu.CompilerParams(dimension_semantics=sem, vmem_limit_bytes=VMEM_LIMIT)


def _resident(shape, index=None):
    index = (0,) * len(shape) if index is None else index
    return pl.BlockSpec(shape, lambda *_: index, pipeline_mode=pl.Buffered(1))


def _silu(x):
    return x * jax.nn.sigmoid(x)


def _layer_norm(y, g, b):
    mu = jnp.mean(y, axis=-1, keepdims=True)
    yc = y - mu
    var = jnp.mean(yc * yc, axis=-1, keepdims=True)
    return yc * lax.rsqrt(var + EPS) * g + b


def _index_map(width, pieces):
    idx = np.full((width,), -1, np.int32)
    for dst, src, w in pieces:
        idx[dst:dst + w] = np.arange(src, src + w)
    return idx


def _gather_pad(arr, idx, axis):
    parts = []
    i = 0
    n = idx.shape[0]
    while i < n:
        j = i
        if idx[i] < 0:
            while j < n and idx[j] < 0:
                j += 1
            shape = list(arr.shape)
            shape[axis] = j - i
            parts.append(jnp.zeros(shape, arr.dtype))
        else:
            while j + 1 < n and idx[j + 1] == idx[j] + 1:
                j += 1
            j += 1
            parts.append(lax.slice_in_dim(arr, int(idx[i]), int(idx[i]) + j - i, axis=axis))
        i = j
    return jnp.concatenate(parts, axis=axis)


_IDX_W_IN = _index_map(W_IN_P, [
    (C_ACQ, 0, Q_RANK), (C_ACKV, 192, KV_RANK), (C_AKR, 320, ROPE_A),
    (C_BZ, 352, 256), (C_BXBC, 608, W_XBC), (C_BDT, 1120, 2 * H_B),
    (C_CQ, 1128, 256), (C_CF, 1384, 512), (C_CI, 1896, 256), (C_CG, 2152, 256),
    (C_DQ, 2408, 256),
    *[(c0 + LANE * g + HD_D * half, s0 + HD_D * g, HD_D)
      for c0, s0 in ((C_DK, 2664), (C_DV, 2792)) for g in range(KV_D) for half in range(2)]])
_IDX_UQ_ROWS = _index_map(256, [(0, 0, Q_RANK)])
_IDX_UQ_COLS = _index_map(4 * HB, [(HB * h, 96 * h, 96) for h in range(H_A)])
_IDX_UKV_K = _index_map(4 * HB, [(HB * h, 128 * h, NOPE_A) for h in range(H_A)])
_IDX_UKV_V = _index_map(4 * V_A, [(V_A * h, 128 * h + NOPE_A, V_A) for h in range(H_A)])


def _rope_tables(half, lane0s):
    t = np.arange(DEC_SEQ)
    pos = np.stack([t // GRID_W, t % GRID_W], 0).astype(np.float64)
    inv = ROPE_BASE ** (-np.arange(half, dtype=np.float64) / half)
    cos = np.ones((DEC_SEQ, LANE))
    sin = np.zeros((DEC_SEQ, LANE))
    for lane0 in lane0s:
        for axis in range(2):
            ang = pos[axis][:, None] * inv[None, :]
            base = lane0 + axis * 2 * half
            cos[:, base:base + half] = np.cos(ang)
            cos[:, base + half:base + 2 * half] = np.cos(ang)
            sin[:, base:base + half] = -np.sin(ang)
            sin[:, base + half:base + 2 * half] = np.sin(ang)
    ident_c = np.ones((TM, LANE))
    ident_s = np.zeros((TM, LANE))
    return (jnp.asarray(np.concatenate([ident_c, cos], 0), F32),
            jnp.asarray(np.concatenate([ident_s, sin], 0), F32))


def _rope(x, cos, sin, first, half):
    partner = jnp.where(first, pltpu.roll(x, LANE - half, 1), pltpu.roll(x, half, 1))
    return x * cos + partner * sin


def _low_lanes():
    return lax.broadcasted_iota(jnp.int32, (1, LANE), 1) < LANE // 2


def _head_alone(tile, odd):
    return jnp.where(_low_lanes(), pltpu.roll(tile, LANE // 2, 1) if odd else tile, 0.0)


def _two_heads(h_even, h_odd):
    return h_even + pltpu.roll(h_odd, LANE // 2, 1)


def _rms_heads(x, width):
    low = _low_lanes()
    tiles = []
    for t in range(width // LANE):
        blk = x[:, LANE * t:LANE * (t + 1)]
        sq = blk * blk
        s_all = jnp.sum(sq, axis=-1, keepdims=True)
        s_low = jnp.sum(jnp.where(low, sq, 0.0), axis=-1, keepdims=True)
        ms = jnp.where(low, s_low, s_all - s_low) * (2.0 / LANE)
        tiles.append(blk * lax.rsqrt(ms + EPS))
    return jnp.concatenate(tiles, axis=-1)


def _mod_kernel(c_ref, w_ref, b_ref, o_ref):
    c = c_ref[...]
    s = _silu(c).astype(BF16)
    o_ref[0] = _dot(s, w_ref[0].astype(BF16)) + b_ref[0]


def _modulation(cvec, w_mod, b_mod):
    return pl.pallas_call(
        _mod_kernel,
        grid=(DEPTH, N_MOD // MOD_TN),
        in_specs=[pl.BlockSpec((SUBLANE, D_MODEL), lambda l, j: (0, 0)),
                  pl.BlockSpec((1, D_MODEL, MOD_TN), lambda l, j: (l, 0, j)),
                  pl.BlockSpec((1, 1, MOD_TN), lambda l, j: (l, 0, j))],
        out_specs=pl.BlockSpec((1, SUBLANE, MOD_TN), lambda l, j: (l, 0, j)),
        out_shape=jax.ShapeDtypeStruct((DEPTH, SUBLANE, N_MOD), F32),
        compiler_params=_params("arbitrary", "arbitrary"),
        name="modulation",
    )(cvec, w_mod, b_mod.reshape(DEPTH, 1, N_MOD))


def _mod_spec(seq_len, tm=TM):
    return pl.BlockSpec((1, N_SUB * 3, D_MODEL), lambda i: (i * tm // seq_len, 0, 0))


def _ffn_sublayer(x, mod_ref, wg_ref, wu_ref, wd_ref, g_ref, b_ref, sub):
    shift = mod_ref[0, 3 * sub:3 * sub + 1, :]
    scale = mod_ref[0, 3 * sub + 1:3 * sub + 2, :]
    gate = mod_ref[0, 3 * sub + 2:3 * sub + 3, :]
    h = (x * (1.0 + scale) + shift).astype(BF16)
    acc = jnp.zeros(x.shape, F32)
    for start in range(0, D_FF, FF_CHUNK):
        cols = slice(start, min(start + FF_CHUNK, D_FF))
        gt = _dot(h, wg_ref[:, cols])
        up = _dot(h, wu_ref[:, cols])
        acc = acc + _dot((_silu(gt) * up).astype(BF16), wd_ref[cols, :])
    y = ALPHA * x + 0.5 * gate * acc
    return _layer_norm(y, g_ref[...], b_ref[...])


def _ffn_kernel(x_ref, mod_ref, wg_ref, wu_ref, wd_ref, g_ref, b_ref, o_ref, *, sub):
    o_ref[...] = _ffn_sublayer(x_ref[...], mod_ref, wg_ref, wu_ref, wd_ref, g_ref, b_ref, sub)


def _ffn(x, mod, seq_len, w_gu, w_down, layer_sub, ln_g, ln_b, sub):
    n = x.shape[0]
    l, s = layer_sub
    row = pl.BlockSpec((TM_FFN, D_MODEL), lambda i: (i, 0))
    return pl.pallas_call(
        functools.partial(_ffn_kernel, sub=sub),
        grid=(n // TM_FFN,),
        in_specs=[row, _mod_spec(seq_len, TM_FFN),
                  _resident((None, None, D_MODEL, D_FF), (l, s, 0, 0)),
                  _resident((None, None, D_MODEL, D_FF), (l, s, 0, 1)),
                  _resident((None, None, D_FF, D_MODEL), (l, s, 0, 0)),
                  _resident((1, D_MODEL)), _resident((1, D_MODEL))],
        out_specs=row,
        out_shape=jax.ShapeDtypeStruct((n, D_MODEL), F32),
        compiler_params=_params("arbitrary"),
        name="ffn",
    )(x, mod, w_gu, w_gu, w_down, ln_g, ln_b)


def _in_kernel(x_ref, mod_ref, w_ref, wuq_ref, wk_ref, wv_ref, gq_ref, gkv_ref, dtb_ref,
               cosq_ref, sinq_ref, cosd_ref, sind_ref, *out_refs, latent):
    out = dict(zip([name for name, _, _ in _in_outputs(latent)], out_refs))
    qa_ref, ka_ref, va_ref = out["qa"], out["ka"], out["va"]
    bz_ref, bxbc_ref, bdt_ref = out["bz"], out["bxbc"], out["bdt"]
    cq_ref, cf_ref, ci_ref, cg_ref = out["cq"], out["cf"], out["ci"], out["cg"]
    dq_ref, dk_ref, dv_ref = out["dq"], out["dk"], out["dv"]
    x = x_ref[...]
    h = (x * (1.0 + mod_ref[0, 4:5, :]) + mod_ref[0, 3:4, :]).astype(BF16)

    def proj(start, width):
        return _dot(h, w_ref[:, start:start + width])

    lane = lax.broadcasted_iota(jnp.int32, (TM, LANE), 1)
    first_a = (lane % 16) < 8
    first_d = (lane % 32) < 16

    def rope_a(blk):
        return _rope(blk, cosq_ref[...], sinq_ref[...], first_a, 8) if latent else blk

    def rope_d(blk):
        return _rope(blk, cosd_ref[...], sind_ref[...], first_d, 16) if latent else blk

    pa = proj(C_ACQ, C_BZ - C_ACQ)
    acq = pa[:, 0:256]
    ms = jnp.sum(acq * acq, axis=-1, keepdims=True) * (1.0 / Q_RANK)
    qn = (acq * lax.rsqrt(ms + EPS) * gq_ref[...]).astype(BF16)
    q = _dot(qn, wuq_ref[...])
    scale_a = (NOPE_A + ROPE_A) ** -0.5 * LOG2_E
    for hh in range(H_A):
        blk = slice(HB * hh, HB * (hh + 1))
        qa_ref[:, blk] = (rope_a(q[:, blk]) * scale_a).astype(BF16)
    ackv = pa[:, C_ACKV:C_ACKV + KV_RANK]
    ms = jnp.mean(ackv * ackv, axis=-1, keepdims=True)
    ckv = ackv * lax.rsqrt(ms + EPS) * gkv_ref[...]
    ckv_b = ckv.astype(BF16)
    kk = _dot(ckv_b, wk_ref[...])
    akr = pa[:, C_AKR:C_AKR + LANE]
    if not latent:
        out["ckv"][...] = ckv
        out["kr"][...] = akr
    krp = rope_a(pltpu.roll(akr, NOPE_A, 1))
    for hh in range(H_A):
        blk = slice(HB * hh, HB * (hh + 1))
        ka_ref[:, blk] = (kk[:, blk] + krp).astype(BF16)
    va_ref[...] = _dot(ckv_b, wv_ref[...]).astype(BF16)

    pb = proj(C_BZ, C_CQ - C_BZ)
    bz_ref[...] = pb[:, 0:H_B * P_B]
    bxbc_ref[...] = pb[:, C_BXBC - C_BZ:C_BXBC - C_BZ + W_XBC]
    dtr = pb[:, C_BDT - C_BZ:C_BDT - C_BZ + LANE] + dtb_ref[...]
    bdt_ref[...] = jnp.maximum(dtr, 0.0) + jnp.log(1.0 + jnp.exp(-jnp.abs(dtr)))

    pc = proj(C_CQ, C_DQ - C_CQ)
    cq_ref[...] = pc[:, 0:256]
    cf_ref[...] = pc[:, C_CF - C_CQ:C_CF - C_CQ + 512]
    ci_ref[...] = pc[:, C_CI - C_CQ:C_CI - C_CQ + 256]
    cg_ref[...] = pc[:, C_CG - C_CQ:C_CG - C_CQ + 256]

    pd = proj(C_DQ, W_IN_P - C_DQ)
    dq = pd[:, 0:H_D * HD_D]
    scale_d = HD_D ** -0.5
    for t in range(H_D * HD_D // LANE):
        blk = slice(LANE * t, LANE * (t + 1))
        dq_ref[:, blk] = (rope_d(dq[:, blk]) * scale_d).astype(BF16)
    dk = pd[:, C_DK - C_DQ:C_DK - C_DQ + KV_D * LANE]
    dk = [rope_d(dk[:, LANE * g:LANE * (g + 1)]) for g in range(KV_D)]
    for g in range(KV_D):
        dk_ref[:, LANE * g:LANE * (g + 1)] = dk[g]
    dv = pd[:, C_DV - C_DQ:C_DV - C_DQ + KV_D * LANE]
    dv_ref[...] = dv
    if not latent:
        out["dkc"][...] = jnp.where(_low_lanes(), dk[0], dk[1])
        out["dvc"][...] = jnp.where(_low_lanes(), dv[:, 0:LANE], dv[:, LANE:2 * LANE])


def _in_outputs(latent):
    outs = [("qa", 512, BF16), ("ka", 512, BF16), ("va", 256, BF16),
            ("bz", H_B * P_B, F32), ("bxbc", W_XBC, F32), ("bdt", LANE, F32),
            ("cq", 256, F32), ("cf", 512, F32), ("ci", 256, F32), ("cg", 256, F32),
            ("dq", H_D * HD_D, BF16), ("dk", KV_D * LANE, F32), ("dv", KV_D * LANE, F32)]
    if not latent:
        outs += [("ckv", KV_RANK, F32), ("kr", LANE, F32), ("dkc", KV_D * HD_D, F32), ("dvc", KV_D * HD_D, F32)]
    return outs


def _in_proj(x, mod, group_len, latent, wp, tabs):
    n = x.shape[0]
    row = lambda w: pl.BlockSpec((TM, w), lambda i: (i, 0))
    tab = pl.BlockSpec((TM, LANE), (lambda i: (1 + i % (DEC_SEQ // TM), 0)) if latent else (lambda i: (0, 0)))
    outs = pl.pallas_call(
        functools.partial(_in_kernel, latent=latent),
        grid=(n // TM,),
        in_specs=[row(D_MODEL), _mod_spec(group_len), _resident((D_MODEL, W_IN_P)), _resident((256, 512)),
                  _resident((KV_RANK, 512)), _resident((KV_RANK, 256)), _resident((1, 256)),
                  _resident((1, KV_RANK)), _resident((1, LANE)), tab, tab, tab, tab],
        out_specs=[row(w) for _, w, _ in _in_outputs(latent)],
        out_shape=[jax.ShapeDtypeStruct((n, w), dt) for _, w, dt in _in_outputs(latent)],
        compiler_params=_params("arbitrary"),
        name="in_proj",
    )(x, mod, wp["w_in"], wp["w_uq"], wp["w_uk"], wp["w_uv"], wp["g_q"], wp["g_kv"], wp["dt_bias"],
      tabs[0], tabs[1], tabs[2], tabs[3])
    return dict(zip([k for k, _, _ in _in_outputs(latent)], outs))


def _mla_cache_kernel(ckv_ref, krp_ref, wk_ref, wv_ref, kc_ref, vc_ref):
    ckv_b = ckv_ref[0].astype(BF16)
    kk = _dot(ckv_b, wk_ref[...])
    krp = krp_ref[0]
    for hh in range(H_A):
        blk = slice(HB * hh, HB * (hh + 1))
        kc_ref[0, :, blk] = (kk[:, blk] + krp).astype(BF16)
    vc_ref[0] = _dot(ckv_b, wv_ref[...]).astype(BF16)


def _mla_cache(ckv, krope_placed, wp):
    nb = ckv.shape[0]
    return pl.pallas_call(
        _mla_cache_kernel,
        grid=(nb,),
        in_specs=[pl.BlockSpec((1, PAST_LEN, KV_RANK), lambda b: (b, 0, 0)),
                  pl.BlockSpec((1, PAST_LEN, LANE), lambda b: (b, 0, 0)),
                  _resident((KV_RANK, 512)), _resident((KV_RANK, 256))],
        out_specs=[pl.BlockSpec((1, PAST_LEN, 512), lambda b: (b, 0, 0)),
                   pl.BlockSpec((1, PAST_LEN, 256), lambda b: (b, 0, 0))],
        out_shape=[jax.ShapeDtypeStruct((nb, PAST_LEN, 512), BF16),
                   jax.ShapeDtypeStruct((nb, PAST_LEN, 256), BF16)],
        compiler_params=_params("arbitrary"),
        name="mla_cache",
    )(ckv, krope_placed, wp["w_uk"], wp["w_uv"])


def _mla_kernel(*refs, has_cache, seqs):
    if has_cache:
        q_ref, k_ref, v_ref, kc_ref, vc_ref, o_ref = refs
    else:
        q_ref, k_ref, v_ref, o_ref = refs
    tk = k_ref.shape[0] // seqs
    tq = q_ref.shape[0] // seqs
    head_of_lane = lax.broadcasted_iota(jnp.int32, (1, H_A * V_A), 1) // V_A
    for b in range(seqs):
        qrows = slice(b * tq, (b + 1) * tq)
        krows = slice(b * tk, (b + 1) * tk)
        v = v_ref[krows, :]
        blocks = [slice(HB * hh, HB * (hh + 1)) for hh in range(H_A)]
        scores = [_dot_nt(q_ref[qrows, blk], k_ref[krows, blk]) for blk in blocks]
        if has_cache:
            scores_c = [_dot_nt(q_ref[qrows, blk], kc_ref[0, :, blk]) for blk in blocks]
        acc = jnp.zeros((tq, H_A * V_A), F32)
        for hh in range(H_A):
            s = scores[hh]
            m = jnp.max(s, axis=-1, keepdims=True)
            if has_cache:
                sc = scores_c[hh]
                m = jnp.maximum(m, jnp.max(sc, axis=-1, keepdims=True))
            e = jnp.exp2(s - m)
            den = jnp.sum(e, axis=-1, keepdims=True)
            pv = _dot(e.astype(BF16), v)
            if has_cache:
                ec = jnp.exp2(sc - m)
                den = den + jnp.sum(ec, axis=-1, keepdims=True)
                pv = pv + _dot(ec.astype(BF16), vc_ref[0])
            acc = jnp.where(head_of_lane == hh, pv / den, acc)
        o_ref[qrows, :] = acc.astype(BF16)


def _mla_lat_kernel(q_ref, k_ref, v_ref, kc_ref, vc_ref, o_ref, vt_scr, vct_scr):
    @pl.when(pl.program_id(1) == 0)
    def _():
        vt_scr[...] = v_ref[...].astype(F32).T.astype(BF16)
        vct_scr[...] = vc_ref[0].astype(F32).T.astype(BF16)

    blocks = [slice(HB * hh, HB * (hh + 1)) for hh in range(H_A)]
    scores = [_dot_nt(k_ref[:, blk], q_ref[:, blk]) for blk in blocks]
    scores_c = [_dot_nt(kc_ref[0, :, blk], q_ref[:, blk]) for blk in blocks]
    outs = []
    for hh in range(H_A):
        vrows = slice(V_A * hh, V_A * (hh + 1))
        s, sc = scores[hh], scores_c[hh]
        m = jnp.maximum(jnp.max(s, axis=0, keepdims=True), jnp.max(sc, axis=0, keepdims=True))
        e = jnp.exp2(s - m)
        ec = jnp.exp2(sc - m)
        den = jnp.sum(e, axis=0, keepdims=True) + jnp.sum(ec, axis=0, keepdims=True)
        o_t = _dot(vt_scr[vrows, :], e.astype(BF16)) + _dot(vct_scr[vrows, :], ec.astype(BF16))
        outs.append(o_t / den)
    o_ref[...] = jnp.concatenate(outs, axis=0).T.astype(BF16)


def _mla_lat(q, k, v, n_seq, seq_len, cache):
    nq = seq_len // TQ_A
    return pl.pallas_call(
        _mla_lat_kernel,
        grid=(n_seq, nq),
        in_specs=[pl.BlockSpec((TQ_A, 512), lambda b, i: (b * nq + i, 0)),
                  pl.BlockSpec((seq_len, 512), lambda b, i: (b, 0)),
                  pl.BlockSpec((seq_len, 256), lambda b, i: (b, 0)),
                  pl.BlockSpec((1, PAST_LEN, 512), lambda b, i: (b, 0, 0)),
                  pl.BlockSpec((1, PAST_LEN, 256), lambda b, i: (b, 0, 0))],
        out_specs=pl.BlockSpec((TQ_A, H_A * V_A), lambda b, i: (b * nq + i, 0)),
        out_shape=jax.ShapeDtypeStruct((n_seq * seq_len, H_A * V_A), BF16),
        scratch_shapes=[pltpu.VMEM((H_A * V_A, seq_len), BF16), pltpu.VMEM((H_A * V_A, PAST_LEN), BF16)],
        compiler_params=_params("arbitrary", "arbitrary"),
        name="mla_latent",
    )(q, k, v, *cache)


def _mla(q, k, v, n_seq, seq_len, cache=None):
    tq = TQ_A if cache is not None else seq_len
    nq = seq_len // tq
    seqs = 1 if cache is not None else MLA_CTX_SEQS
    in_specs = [pl.BlockSpec((seqs * tq, 512), lambda b, i: (b * nq + i, 0)),
                pl.BlockSpec((seqs * seq_len, 512), lambda b, i: (b, 0)),
                pl.BlockSpec((seqs * seq_len, 256), lambda b, i: (b, 0))]
    args = [q, k, v]
    if cache is not None:
        in_specs += [pl.BlockSpec((1, PAST_LEN, 512), lambda b, i: (b, 0, 0)),
                     pl.BlockSpec((1, PAST_LEN, 256), lambda b, i: (b, 0, 0))]
        args += list(cache)
    return pl.pallas_call(
        functools.partial(_mla_kernel, has_cache=cache is not None, seqs=seqs),
        grid=(n_seq // seqs, nq),
        in_specs=in_specs,
        out_specs=pl.BlockSpec((seqs * tq, H_A * V_A), lambda b, i: (b * nq + i, 0)),
        out_shape=jax.ShapeDtypeStruct((n_seq * seq_len, H_A * V_A), BF16),
        compiler_params=_params("arbitrary", "arbitrary"),
        name="mla_attention",
    )(*args)


def _sink_softmax_pv(parts, sink):
    m = sink
    for s, _ in parts:
        m = jnp.maximum(m, jnp.max(s, axis=-1, keepdims=True))
    den = jnp.exp(sink - m)
    pv = None
    for s, v in parts:
        e = jnp.exp(s - m)
        den = den + jnp.sum(e, axis=-1, keepdims=True)
        t = _dot(e.astype(BF16), v)
        pv = t if pv is None else pv + t
    return pv / den


def _gqa_pair(q_ref, rows, g, scores_and_values, sink_ref, o_ref):
    t = rows.stop - rows.start
    tile = slice(LANE * g, LANE * (g + 1))
    qt = q_ref[rows, tile]
    low = _low_lanes()
    zero_b = jnp.zeros((), BF16)
    q2 = jnp.concatenate([jnp.where(low, qt, zero_b), jnp.where(low, zero_b, qt)], axis=0)
    second = lax.broadcasted_iota(jnp.int32, (G_D * t, 1), 0) >= t
    sink = jnp.where(second, sink_ref[G_D * g + 1:G_D * g + 2, 0:1], sink_ref[G_D * g:G_D * g + 1, 0:1])
    o = _sink_softmax_pv(scores_and_values(q2), sink)
    o_ref[rows, tile] = jnp.where(low, o[0:t, :], o[t:2 * t, :]).astype(BF16)


def _gqa_ctx_kernel(q_ref, k_ref, v_ref, sink_ref, o_ref):
    for b in range(GQA_CTX_SEQS):
        rows = slice(b * SEQ, (b + 1) * SEQ)
        for g in range(KV_D):
            tile = slice(LANE * g, LANE * (g + 1))
            k = k_ref[rows, tile].astype(BF16)
            v = v_ref[rows, tile].astype(BF16)
            _gqa_pair(q_ref, rows, g, lambda q2, k=k, v=v: [(_dot_nt(q2, k), v)], sink_ref, o_ref)


def _gqa_ctx(q, k, v, sink, n_seq):
    seq = lambda w: pl.BlockSpec((GQA_CTX_SEQS * SEQ, w), lambda b: (b, 0))
    return pl.pallas_call(
        _gqa_ctx_kernel,
        grid=(n_seq // GQA_CTX_SEQS,),
        in_specs=[seq(H_D * HD_D), seq(KV_D * LANE), seq(KV_D * LANE), _resident((H_D, LANE))],
        out_specs=seq(H_D * HD_D),
        out_shape=jax.ShapeDtypeStruct((n_seq * SEQ, H_D * HD_D), BF16),
        compiler_params=_params("arbitrary"),
        name="gqa_context",
    )(q, k, v, sink)


def _gqa_lat_kernel(q_ref, k_ref, v_ref, kc_ref, vc_ref, sink_ref, o_ref):
    span = 3 * WINDOW
    row = lax.broadcasted_iota(jnp.int32, (G_D * WINDOW, span), 0) % WINDOW
    col = lax.broadcasted_iota(jnp.int32, (G_D * WINDOW, span), 1)
    for j in range(GQA_LAT_BLOCKS):
        n = pl.program_id(1) * GQA_LAT_BLOCKS + j
        rows = slice(j * WINDOW, (j + 1) * WINDOW)
        start = pl.multiple_of(jnp.clip((n - 1) * WINDOW, 0, DEC_SEQ - span), WINDOW)
        band = jnp.abs((start + col) - (n * WINDOW + row)) <= WINDOW
        for g in range(KV_D):
            tile = slice(LANE * g, LANE * (g + 1))
            kw = k_ref[pl.ds(start, span), tile].astype(BF16)
            vw = v_ref[pl.ds(start, span), tile].astype(BF16)
            kc = kc_ref[0, :, tile]
            vc = vc_ref[0, :, tile]

            def parts(q2, kw=kw, vw=vw, kc=kc, vc=vc, band=band):
                return [(jnp.where(band, _dot_nt(q2, kw), NEG), vw), (_dot_nt(q2, kc), vc)]

            _gqa_pair(q_ref, rows, g, parts, sink_ref, o_ref)


def _gqa_lat(q, k, v, kc, vc, sink, n_seq):
    nb = DEC_SEQ // (WINDOW * GQA_LAT_BLOCKS)
    wq, wkv = H_D * HD_D, KV_D * LANE
    qblk = pl.BlockSpec((WINDOW * GQA_LAT_BLOCKS, wq), lambda b, n: (b * nb + n, 0))
    return pl.pallas_call(
        _gqa_lat_kernel,
        grid=(n_seq, nb),
        in_specs=[qblk,
                  pl.BlockSpec((DEC_SEQ, wkv), lambda b, n: (b, 0)),
                  pl.BlockSpec((DEC_SEQ, wkv), lambda b, n: (b, 0)),
                  pl.BlockSpec((1, PAST_LEN, wkv), lambda b, n: (b, 0, 0)),
                  pl.BlockSpec((1, PAST_LEN, wkv), lambda b, n: (b, 0, 0)),
                  _resident((H_D, LANE))],
        out_specs=qblk,
        out_shape=jax.ShapeDtypeStruct((n_seq * DEC_SEQ, wq), BF16),
        compiler_params=_params("arbitrary", "arbitrary"),
        name="gqa_latent",
    )(q, k, v, kc, vc, sink)


def _conv_kernel(cur_ref, prev_ref, next_ref, w_ref, b_ref, o_ref, pad_ref, *, tiles_per_seq):
    i = pl.program_id(0)
    has_prev = (i % tiles_per_seq) != 0
    has_next = (i % tiles_per_seq) != tiles_per_seq - 1
    pad_ref[0:SUBLANE, :] = jnp.where(has_prev, prev_ref[...], 0.0)
    pad_ref[SUBLANE:SUBLANE + CONV_TILE, :] = cur_ref[...]
    pad_ref[SUBLANE + CONV_TILE:, :] = jnp.where(has_next, next_ref[...], 0.0)
    y = jnp.zeros((CONV_TILE, W_XBC), F32) + b_ref[...]
    for k in range(D_CONV):
        off = SUBLANE - D_CONV // 2 + k
        y = y + w_ref[k:k + 1, :] * pad_ref[off:off + CONV_TILE, :]
    o_ref[...] = _silu(y)


def _conv(xbc, seq_len, w, b):
    n = xbc.shape[0]
    per = CONV_TILE // SUBLANE
    last = n // SUBLANE - 1
    return pl.pallas_call(
        functools.partial(_conv_kernel, tiles_per_seq=seq_len // CONV_TILE),
        grid=(n // CONV_TILE,),
        in_specs=[pl.BlockSpec((CONV_TILE, W_XBC), lambda i: (i, 0)),
                  pl.BlockSpec((SUBLANE, W_XBC), lambda i: (jnp.maximum(i * per - 1, 0), 0)),
                  pl.BlockSpec((SUBLANE, W_XBC), lambda i: (jnp.minimum((i + 1) * per, last), 0)),
                  _resident((SUBLANE, W_XBC)), _resident((1, W_XBC))],
        out_specs=pl.BlockSpec((CONV_TILE, W_XBC), lambda i: (i, 0)),
        out_shape=jax.ShapeDtypeStruct((n, W_XBC), F32),
        scratch_shapes=[pltpu.VMEM((CONV_TILE + 2 * SUBLANE, W_XBC), F32)],
        compiler_params=_params("arbitrary"),
        name="ssd_conv",
    )(xbc, xbc, xbc, w, b)


def _ssd_kernel(*refs, has_s0, chunks, seqs):
    if has_s0:
        xf_ref, xb_ref, dtf_ref, dtb_ref, alog_ref, dsk_ref, s0_ref, yf_ref, yb_ref, st_ref, s_scr = refs
    else:
        xf_ref, xb_ref, dtf_ref, dtb_ref, alog_ref, dsk_ref, yf_ref, yb_ref, st_ref, s_scr = refs
    c = pl.program_id(1)
    q = SSD_CHUNK

    @pl.when(c == 0)
    def _():
        s_scr[...] = s0_ref[...] if has_s0 else jnp.zeros(s_scr.shape, F32)

    row = lax.broadcasted_iota(jnp.int32, (q, q), 0)
    col = lax.broadcasted_iota(jnp.int32, (q, q), 1)
    a_coef = -jnp.exp(alog_ref[...])
    low = _low_lanes()
    first_head_rows = lax.broadcasted_iota(jnp.int32, (HB, 1), 0) < P_B
    for sq, d in [(sq, d) for sq in range(seqs) for d in range(2)]:
        x_ref, dt_ref, y_ref = ((xf_ref, dtf_ref, yf_ref), (xb_ref, dtb_ref, yb_ref))[d]
        tri = (row >= col) if d == 0 else (row <= col)
        for ci in (range(chunks) if d == 0 else range(chunks - 1, -1, -1)):
            rows = slice((sq * chunks + ci) * q, (sq * chunks + ci + 1) * q)
            dt = dt_ref[rows, :]
            cum = _prefix_dot(tri, dt * a_coef)
            cum_t = cum.T
            dt_t = dt.T
            total = cum[q - 1:q, :] if d == 0 else cum[0:1, :]
            for g in range(G_B):
                half = low if g == 0 else jnp.logical_not(low)
                bg = jnp.where(half, x_ref[rows, 2 * LANE:3 * LANE], 0.0)
                cg = jnp.where(half, x_ref[rows, 3 * LANE:4 * LANE], 0.0)
                cb = _dot_nt(cg.astype(BF16), bg.astype(BF16))
                xp = x_ref[rows, LANE * g:LANE * (g + 1)]
                xp_b = xp.astype(BF16)
                xp_t = xp.T
                s_in = s_scr[sq, d, g]
                s_in_b = s_in.astype(BF16)
                ys, states = [], []
                for j in range(H_B // G_B):
                    k = d * H_B + g * (H_B // G_B) + j
                    cum_b = jnp.broadcast_to(cum[:, k:k + 1], (q, LANE))
                    seg = jnp.where(tri, jnp.exp(jnp.where(tri, cum_b - cum_t[k:k + 1, :], 0.0)), 0.0)
                    dt_row = dt_t[k:k + 1, :]
                    y = _dot((cb * seg * dt_row).astype(BF16), xp_b)
                    ys.append(y + _dot_nt((cg * jnp.exp(cum_b)).astype(BF16), s_in_b))
                    tot = total[:, k:k + 1]
                    bdec = bg * jnp.exp(tot - cum_b)
                    cs = _dot((xp_t * dt_row).astype(BF16), bdec.astype(BF16))
                    states.append(jnp.exp(tot) * s_in + cs)
                k0 = d * H_B + g * (H_B // G_B)
                d_skip = jnp.where(low, dsk_ref[k0:k0 + 1, :], dsk_ref[k0 + 1:k0 + 2, :])
                y_ref[rows, LANE * g:LANE * (g + 1)] = jnp.where(low, ys[0], ys[1]) + xp * d_skip
                s_scr[sq, d, g] = jnp.where(first_head_rows, states[0], states[1])

    @pl.when(c == pl.num_programs(1) - 1)
    def _():
        for sq in range(seqs):
            for d in range(2):
                for hh in range(H_B):
                    g, j = hh // (H_B // G_B), hh % (H_B // G_B)
                    st_ref[sq, d, hh] = s_scr[sq, d, g, P_B * j:P_B * (j + 1), N_B * g:N_B * (g + 1)]


def _ssd(xbc, dt, n_seq, seq_len, a_log, dskip, s0=None):
    chunks = min(SSD_STEP_CHUNKS, seq_len // SSD_CHUNK)
    nc = seq_len // (SSD_CHUNK * chunks)
    seqs = SSD_STEP_CHUNKS // chunks if nc == 1 else 1
    fwd = lambda w: pl.BlockSpec((SSD_CHUNK * chunks * seqs, w), lambda b, c: (b * nc + c, 0))
    bwd = lambda w: pl.BlockSpec((SSD_CHUNK * chunks * seqs, w), lambda b, c: (b * nc + nc - 1 - c, 0))
    state = pl.BlockSpec((seqs, 2, G_B, HB, HB), lambda b, c: (b, 0, 0, 0, 0))
    in_specs = [fwd(W_XBC), bwd(W_XBC), fwd(LANE), bwd(LANE), _resident((1, LANE)), _resident((2 * H_B, LANE))]
    args = [xbc, xbc, dt, dt, a_log, dskip]
    if s0 is not None:
        in_specs.append(state)
        args.append(s0)
    n = n_seq * seq_len
    return pl.pallas_call(
        functools.partial(_ssd_kernel, has_s0=s0 is not None, chunks=chunks, seqs=seqs),
        grid=(n_seq // seqs, nc),
        in_specs=in_specs,
        out_specs=[fwd(H_B * P_B), bwd(H_B * P_B),
                   pl.BlockSpec((seqs, 2, H_B, P_B, N_B), lambda b, c: (b, 0, 0, 0, 0))],
        out_shape=[jax.ShapeDtypeStruct((n, H_B * P_B), F32), jax.ShapeDtypeStruct((n, H_B * P_B), F32),
                   jax.ShapeDtypeStruct((n_seq, 2, H_B, P_B, N_B), F32)],
        scratch_shapes=[pltpu.VMEM((seqs, 2, G_B, HB, HB), F32)],
        compiler_params=_params("arbitrary", "arbitrary"),
        name="ssd_scan",
    )(*args)


def _gla_kernel(*refs, has_s0):
    if has_s0:
        (qf_ref, qb_ref, ff_ref, fb_ref, vf_ref, vb_ref, lb_ref, s0_ref, of_ref, ob_ref, st_ref, s_scr,
         o_scr) = refs
    else:
        (qf_ref, qb_ref, ff_ref, fb_ref, vf_ref, vb_ref, lb_ref, of_ref, ob_ref, st_ref, s_scr, o_scr) = refs
    c = pl.program_id(1)
    t = GLA_TILE
    ch = GLA_CHUNK
    nch = t // ch
    w = H_C * K_C

    @pl.when(c == 0)
    def _():
        s_scr[...] = s0_ref[0] if has_s0 else jnp.zeros(s_scr.shape, F32)

    row = lax.broadcasted_iota(jnp.int32, (t, t), 0)
    col = lax.broadcasted_iota(jnp.int32, (t, t), 1)
    same_chunk = (row // ch) == (col // ch)
    head_of_lane = lax.broadcasted_iota(jnp.int32, (1, w), 1) // K_C
    row_head = lax.broadcasted_iota(jnp.int32, (H_C * ch, 1), 0) // ch
    zero_b = jnp.zeros((), BF16)
    for d, (q_ref, f_ref, v_ref, o_ref) in enumerate(((qf_ref, ff_ref, vf_ref, of_ref),
                                                      (qb_ref, fb_ref, vb_ref, ob_ref))):
        qv = q_ref[...]
        fr = f_ref[...]
        vv = v_ref[...]
        lb = lb_ref[d:d + 1, :]
        f = lb + (1.0 - lb) * jax.nn.sigmoid(fr)
        log_f = jnp.log(jnp.maximum(f, F_MIN))
        key = (1.0 - lb) * jax.nn.sigmoid(-fr)
        tri = same_chunk & ((col <= row) if d == 0 else (col >= row))
        cum = _prefix_dot(tri, log_f)
        cum3 = cum.reshape(nch, ch, w)
        k3 = key.reshape(nch, ch, w)
        q_dec = (qv * jnp.exp(cum)).astype(BF16)
        q_heads = [jnp.where(head_of_lane == hh, q_dec, zero_b) for hh in range(H_C)]
        v_heads = [_head_alone(vv[:, LANE * (hh // 2):LANE * (hh // 2 + 1)], hh % 2) for hh in range(H_C)]

        k_inv = (key * jnp.exp(-cum)).astype(BF16)
        for hh in range(H_C):
            att = jnp.where(tri, _dot_nt(q_heads[hh], k_inv), 0.0).astype(BF16)
            o_scr[:, HB * hh:HB * (hh + 1)] = _dot(att, v_heads[hh].astype(BF16))

        @pl.when(jnp.min(cum) < -GLA_SAFE_LOG_DECAY)
        def _():
            wp = H_C * HB
            head_ones = ((lax.broadcasted_iota(jnp.int32, (w, wp), 0) // K_C)
                         == (lax.broadcasted_iota(jnp.int32, (w, wp), 1) // HB)).astype(F32)
            i_in_chunk = lax.broadcasted_iota(jnp.int32, (nch, ch, w), 1)
            q3 = qv.reshape(nch, ch, w)
            v3 = jnp.concatenate(v_heads, axis=1).reshape(nch, ch, wp)
            o3 = jnp.zeros((nch, ch, wp), F32)
            for j in range(ch):
                live = (i_in_chunk >= j) if d == 0 else (i_in_chunk <= j)
                e = jnp.exp(jnp.where(live, cum3 - cum3[:, j:j + 1, :], 0.0))
                term = jnp.where(live, q3 * e * k3[:, j:j + 1, :], 0.0)
                att = _dot(term.reshape(t, w), head_ones)
                o3 = o3 + att.reshape(nch, ch, wp) * v3[:, j:j + 1, :]
            o_scr[...] = o3.reshape(t, wp)

        edge = ch - 1 if d == 0 else 0
        last3 = jnp.broadcast_to(cum3[:, edge:edge + 1, :], (nch, ch, w))
        k_dec = (k3 * jnp.exp(last3 - cum3)).reshape(t, w).astype(BF16)
        order = range(nch) if d == 0 else range(nch - 1, -1, -1)
        for cc in order:
            rows = slice(cc * ch, (cc + 1) * ch)
            st = s_scr[d]
            q4 = jnp.concatenate([qh[rows, :] for qh in q_heads], axis=0)
            r = _dot_nt(q4, st.astype(BF16))
            for hh in range(H_C):
                blk = slice(HB * hh, HB * (hh + 1))
                o_scr[rows, blk] = o_scr[rows, blk] + r[hh * ch:(hh + 1) * ch, :]
            v4 = jnp.concatenate([vh[rows, :] for vh in v_heads], axis=0)
            k4 = jnp.where(row_head == head_of_lane, jnp.concatenate([k_dec[rows, :]] * H_C, axis=0), zero_b)
            decay = jnp.exp(cum[cc * ch + edge:cc * ch + edge + 1, :])
            s_scr[d] = decay * st + _dot(v4.T.astype(BF16), k4)

        for p in range(H_C // 2):
            o_ref[:, LANE * p:LANE * (p + 1)] = _two_heads(o_scr[:, HB * 2 * p:HB * (2 * p + 1)],
                                                           o_scr[:, HB * (2 * p + 1):HB * (2 * p + 2)])

    @pl.when(c == pl.num_programs(1) - 1)
    def _():
        k_idx = lax.broadcasted_iota(jnp.int32, (K_C, w), 0)
        lane_idx = lax.broadcasted_iota(jnp.int32, (K_C, w), 1)
        for d in range(2):
            rows = s_scr[d, 0:V_C, :]
            for hh in range(H_C):
                sel = (lane_idx == hh * K_C + k_idx).astype(F32)
                st_ref[0, d, hh] = lax.dot_general(sel, rows, (((1,), (1,)), ((), ())),
                                                   preferred_element_type=F32, precision=lax.Precision.HIGHEST)


def _gla(cq, cf, ci, n_seq, seq_len, lb, s0=None):
    nt = seq_len // GLA_TILE
    w = H_C * K_C
    fwd = lambda j: pl.BlockSpec((GLA_TILE, w), lambda b, c: (b * nt + c, j))
    bwd = lambda j: pl.BlockSpec((GLA_TILE, w), lambda b, c: (b * nt + nt - 1 - c, j))
    state = pl.BlockSpec((1, 2, HB, w), lambda b, c: (b, 0, 0, 0))
    in_specs = [fwd(0), bwd(0), fwd(0), bwd(1), fwd(0), bwd(0), _resident((2, w))]
    args = [cq, cq, cf, cf, ci, ci, lb]
    if s0 is not None:
        in_specs.append(state)
        args.append(s0)
    n = n_seq * seq_len
    return pl.pallas_call(
        functools.partial(_gla_kernel, has_s0=s0 is not None),
        grid=(n_seq, nt),
        in_specs=in_specs,
        out_specs=[fwd(0), bwd(0), pl.BlockSpec((1, 2, H_C, K_C, V_C), lambda b, c: (b, 0, 0, 0, 0))],
        out_shape=[jax.ShapeDtypeStruct((n, w), F32), jax.ShapeDtypeStruct((n, w), F32),
                   jax.ShapeDtypeStruct((n_seq, 2, H_C, K_C, V_C), F32)],
        scratch_shapes=[pltpu.VMEM((2, HB, w), F32), pltpu.VMEM((GLA_TILE, H_C * HB), F32)],
        compiler_params=_params("arbitrary", "arbitrary"),
        name="hgrn_scan",
    )(*args)


def _out_ffn_kernel(x_ref, mod_ref, oa_ref, yf_ref, yb_ref, bz_ref, of_ref, ob_ref, cg_ref, od_ref,
                    wo_ref, nb_ref, nc_ref, g1_ref, b1_ref, wg_ref, wu_ref, wd_ref, g2_ref, b2_ref, o_ref):
    x = x_ref[...]
    gate = mod_ref[0, 5:6, :]
    yb = _rms_heads((yf_ref[...] + yb_ref[...]) * _silu(bz_ref[...]), H_B * P_B) * nb_ref[...]
    oc = _rms_heads(of_ref[...] + ob_ref[...], H_C * V_C) * nc_ref[...] * _silu(cg_ref[...])
    mixed = jnp.concatenate([oa_ref[...], yb.astype(BF16), oc.astype(BF16), od_ref[...]], axis=-1)
    u = _dot(mixed, wo_ref[...])
    x = _layer_norm(ALPHA * x + gate * u, g1_ref[...], b1_ref[...])
    o_ref[...] = _ffn_sublayer(x, mod_ref, wg_ref, wu_ref, wd_ref, g2_ref, b2_ref, 2)


def _out_ffn(x, mod, seq_len, mix, wp, ln1_g, ln1_b, w_gu, w_down, layer_sub, ln2_g, ln2_b):
    n = x.shape[0]
    l, s = layer_sub
    row = lambda w: pl.BlockSpec((TM, w), lambda i: (i, 0))
    names = ("oa", "yf", "yb", "bz", "of", "ob", "cg", "od")
    vec = _resident((1, D_MODEL))
    return pl.pallas_call(
        _out_ffn_kernel,
        grid=(n // TM,),
        in_specs=[row(D_MODEL), _mod_spec(seq_len)] + [row(mix[k].shape[1]) for k in names]
                 + [_resident((D_MODEL, D_MODEL)), _resident((1, H_B * P_B)), _resident((1, H_C * V_C)), vec, vec,
                    _resident((None, None, D_MODEL, D_FF), (l, s, 0, 0)),
                    _resident((None, None, D_MODEL, D_FF), (l, s, 0, 1)),
                    _resident((None, None, D_FF, D_MODEL), (l, s, 0, 0)), vec, vec],
        out_specs=row(D_MODEL),
        out_shape=jax.ShapeDtypeStruct((n, D_MODEL), F32),
        compiler_params=_params("arbitrary"),
        name="out_proj_ffn",
    )(x, mod, *[mix[k] for k in names], wp["w_out"], wp["ssd_norm"], wp["hgrn_norm"], ln1_g, ln1_b,
      w_gu, w_gu, w_down, ln2_g, ln2_b)


def _prep_layer(l, w_in, w_out, mla_q_norm, mla_kv_norm, mla_w_uq, mla_w_ukv, ssd_conv_w, ssd_conv_b,
                ssd_a_log, ssd_dt_bias, ssd_d, ssd_norm, hgrn_lb, hgrn_norm, gqa_sink):
    return {
        "w_in": _gather_pad(w_in[l], _IDX_W_IN, 1).astype(BF16),
        "w_uq": _gather_pad(_gather_pad(mla_w_uq[l], _IDX_UQ_ROWS, 0), _IDX_UQ_COLS, 1).astype(BF16),
        "w_uk": _gather_pad(mla_w_ukv[l], _IDX_UKV_K, 1).astype(BF16),
        "w_uv": _gather_pad(mla_w_ukv[l], _IDX_UKV_V, 1).astype(BF16),
        "g_q": _gather_pad(mla_q_norm[l], _IDX_UQ_ROWS, 0).reshape(1, 256),
        "g_kv": mla_kv_norm[l].reshape(1, KV_RANK),
        "dt_bias": jnp.pad(ssd_dt_bias[l].reshape(1, 2 * H_B), ((0, 0), (0, LANE - 2 * H_B))),
        "conv_w": jnp.pad(ssd_conv_w[l], ((0, SUBLANE - D_CONV), (0, 0))),
        "conv_b": ssd_conv_b[l].reshape(1, W_XBC),
        "a_log": jnp.pad(ssd_a_log[l].reshape(1, 2 * H_B), ((0, 0), (0, LANE - 2 * H_B))),
        "d_skip": jnp.broadcast_to(ssd_d[l].reshape(2 * H_B, 1), (2 * H_B, LANE)),
        "ssd_norm": ssd_norm[l].reshape(1, H_B * P_B),
        "hgrn_lb": hgrn_lb[l],
        "hgrn_norm": hgrn_norm[l].reshape(1, H_C * V_C),
        "sink": jnp.broadcast_to(gqa_sink[l].reshape(H_D, 1), (H_D, LANE)),
        "w_out": w_out[l].astype(BF16),
    }


def _mixer(x, mod, group_len, wp, tabs, n_seq, seq_len, ctx):
    latent = ctx is not None
    p = _in_proj(x, mod, group_len, latent, wp, tabs)
    mix = {"bz": p["bz"], "cg": p["cg"]}
    cache = _mla_cache(ctx["ckv"], ctx["krope"], wp) if latent else None
    mix["oa"] = (_mla_lat if latent else _mla)(p["qa"], p["ka"], p["va"], n_seq, seq_len, cache)
    xbc = _conv(p["bxbc"], seq_len, wp["conv_w"], wp["conv_b"])
    mix["yf"], mix["yb"], st_b = _ssd(xbc, p["bdt"], n_seq, seq_len, wp["a_log"], wp["d_skip"],
                                      ctx["ssm"] if latent else None)
    mix["of"], mix["ob"], st_c = _gla(p["cq"], p["cf"], p["ci"], n_seq, seq_len, wp["hgrn_lb"],
                                      ctx["hgrn"] if latent else None)
    if latent:
        mix["od"] = _gqa_lat(p["dq"], p["dk"], p["dv"], ctx["dk"], ctx["dv"], wp["sink"], n_seq)
    else:
        mix["od"] = _gqa_ctx(p["dq"], p["dk"], p["dv"], wp["sink"], n_seq)
    state = None if latent else (p["ckv"], p["kr"], st_b, st_c, p["dkc"], p["dvc"])
    return mix, state


def _run_stream(x, mod, n_seq, seq_len, ctx, wp, ffn_w, lng, lnb, tabs):
    group_len = x.shape[0] if ctx is None else seq_len
    x = _ffn(x, mod, group_len, *ffn_w[0], lng[0], lnb[0], sub=0)
    mix, st = _mixer(x, mod, group_len, wp, tabs, n_seq, seq_len, ctx)
    x = _out_ffn(x, mod, group_len, mix, wp, lng[1], lnb[1], *ffn_w[1], lng[2], lnb[2])
    return x, st


def _layer_inputs(l, ctx_tensors, weights, hgrn_lb):
    (cache_a_ckv, cache_a_krope, state_b_ssm, state_c_hgrn, cache_d_k, cache_d_v) = ctx_tensors
    (ln_g, ln_b, ffn_w_gu, ffn_w_down, w_in, w_out, mla_q_norm, mla_kv_norm, mla_w_uq, mla_w_ukv, ssd_conv_w,
     ssd_conv_b, ssd_a_log, ssd_dt_bias, ssd_d, ssd_norm, hgrn_norm, gqa_sink) = weights
    wp = _prep_layer(l, w_in, w_out, mla_q_norm, mla_kv_norm, mla_w_uq, mla_w_ukv, ssd_conv_w, ssd_conv_b,
                     ssd_a_log, ssd_dt_bias, ssd_d, ssd_norm, hgrn_lb, hgrn_norm, gqa_sink)
    ffn_w = [(ffn_w_gu.astype(BF16), ffn_w_down.astype(BF16), (l, s)) for s in range(2)]
    lng = [ln_g[l, s].reshape(1, D_MODEL) for s in range(N_SUB)]
    lnb = [ln_b[l, s].reshape(1, D_MODEL) for s in range(N_SUB)]
    nb = cache_a_ckv.shape[0]
    ctx = {
        "ckv": cache_a_ckv[:, l],
        "krope": jnp.pad(cache_a_krope[:, l], ((0, 0), (0, 0), (NOPE_A, LANE - NOPE_A - ROPE_A))),
        "ssm": jnp.stack([jnp.pad(state_b_ssm[:, l, :, 2 * g:2 * g + 2].reshape(nb, 2, 2 * P_B, N_B),
                                  ((0, 0), (0, 0), (0, 0), (N_B * g, N_B * (G_B - 1 - g)))) for g in range(G_B)],
                         axis=2),
        "hgrn": jnp.pad(jnp.transpose(state_c_hgrn[:, l], (0, 1, 4, 2, 3)).reshape(nb, 2, V_C, H_C * K_C),
                        ((0, 0), (0, 0), (0, HB - V_C), (0, 0))),
        "dk": jnp.concatenate([cache_d_k[:, l]] * 2, axis=-1).reshape(nb, PAST_LEN, KV_D * LANE).astype(BF16),
        "dv": jnp.concatenate([cache_d_v[:, l]] * 2, axis=-1).reshape(nb, PAST_LEN, KV_D * LANE).astype(BF16),
    }
    return wp, ffn_w, lng, lnb, ctx


def kernel(x_prompt, x_sample, cache_a_ckv, cache_a_krope, state_b_ssm, state_c_hgrn, cache_d_k, cache_d_v,
           c, c_ctx, w_mod, b_mod, ln_g, ln_b, ffn_w_gu, ffn_w_down, w_in, w_out, mla_q_norm, mla_kv_norm,
           mla_w_uq, mla_w_ukv, ssd_conv_w, ssd_conv_b, ssd_a_log, ssd_dt_bias, ssd_d, ssd_norm,
           hgrn_lb_logits, hgrn_norm, gqa_sink):
    lb_p = jax.nn.softmax(hgrn_lb_logits.astype(F32), axis=0)
    hgrn_lb = jnp.cumsum(lb_p, axis=0) - lb_p[:1]

    cvec = jnp.concatenate([c_ctx[None], c, jnp.zeros((SUBLANE - 1 - DEC_BATCH, D_MODEL), F32)], axis=0)
    mod_all = _modulation(cvec, w_mod, b_mod)
    tabs = _rope_tables(8, (NOPE_A,)) + _rope_tables(16, (0, HD_D))
    ctx_tensors = (cache_a_ckv, cache_a_krope, state_b_ssm, state_c_hgrn, cache_d_k, cache_d_v)
    weights = (ln_g, ln_b, ffn_w_gu, ffn_w_down, w_in, w_out, mla_q_norm, mla_kv_norm, mla_w_uq, mla_w_ukv,
               ssd_conv_w, ssd_conv_b, ssd_a_log, ssd_dt_bias, ssd_d, ssd_norm, hgrn_norm, gqa_sink)

    y_p = x_prompt.reshape(BATCH * SEQ, D_MODEL)
    y_s = x_sample.reshape(DEC_BATCH * DEC_SEQ, D_MODEL)
    states = []
    for l in range(DEPTH):
        wp, ffn_w, lng, lnb, ctx = _layer_inputs(l, ctx_tensors, weights, hgrn_lb)
        mod_ctx = mod_all[l, 0:1].reshape(1, N_SUB * 3, D_MODEL)
        mod_lat = mod_all[l, 1:1 + DEC_BATCH].reshape(DEC_BATCH, N_SUB * 3, D_MODEL)
        y_p, st = _run_stream(y_p, mod_ctx, BATCH, SEQ, None, wp, ffn_w, lng, lnb, tabs)
        y_s, _ = _run_stream(y_s, mod_lat, DEC_BATCH, DEC_SEQ, ctx, wp, ffn_w, lng, lnb, tabs)
        states.append(st)

    def stack(i, f):
        return jnp.stack([f(s[i]) for s in states], axis=1)

    new_a_ckv = stack(0, lambda t: t.reshape(BATCH, SEQ, KV_RANK))
    new_a_krope = stack(1, lambda t: t.reshape(BATCH, SEQ, LANE)[..., :ROPE_A])
    new_b_ssm = stack(2, lambda t: t)
    new_c_hgrn = stack(3, lambda t: t)
    new_d_k = stack(4, lambda t: t.reshape(BATCH, SEQ, KV_D, HD_D))
    new_d_v = stack(5, lambda t: t.reshape(BATCH, SEQ, KV_D, HD_D))
    return (y_p.reshape(BATCH, SEQ, D_MODEL), y_s.reshape(DEC_BATCH, DEC_SEQ, D_MODEL),
            new_a_ckv, new_a_krope, new_b_ssm, new_c_hgrn, new_d_k, new_d_v)
```
